```python
import jax, jax.numpy as jnp
from jax import lax
import numpy as np

D_MODEL = 1024
BATCH = 4
SEQ = 4096
DEPTH = 1
DEC_BATCH = 32
DEC_SEQ = 1
PAST_LEN = 8192
PAGE_SIZE = 128

HEAD_DIM = 64
N_ATTN_HEADS = 8
N_RWKV_HEADS = 8
ATTN_WIDTH = N_ATTN_HEADS * HEAD_DIM
RWKV_WIDTH = N_RWKV_HEADS * HEAD_DIM
MIX_WIDTH = ATTN_WIDTH + RWKV_WIDTH
MOBA_BLOCK = 256
MOBA_TOP_K = 3
Q_CHUNK = 16
DECAY_LORA = 64
AAA_LORA = 64
GATE_LORA = 128
FFN_HIDDEN = 4 * D_MODEL
RMS_EPS = 1e-6
GN_EPS = 64e-5
NEG_INF = -1e30
ATTN_COLS = 3 * ATTN_WIDTH
RWKV_COLS = 3 * RWKV_WIDTH + DECAY_LORA + AAA_LORA + GATE_LORA
IN_COLS = ATTN_COLS + RWKV_COLS
RW_SPLITS = (RWKV_WIDTH, RWKV_WIDTH + DECAY_LORA, 2 * RWKV_WIDTH + DECAY_LORA,
             3 * RWKV_WIDTH + DECAY_LORA, 3 * RWKV_WIDTH + DECAY_LORA + AAA_LORA)

kernel_name = 'hymba_rwkv7_moba_alibi_decode_step'


def rms_norm(x, g):
    xf = x.astype(jnp.float32)
    y = xf * lax.rsqrt(jnp.mean(xf * xf, axis=-1, keepdims=True) + RMS_EPS)
    return (y * g.astype(jnp.float32)).astype(x.dtype)


def alibi_slopes():
    return jnp.exp2(-8.0 * jnp.arange(1, N_ATTN_HEADS + 1, dtype=jnp.float32) / N_ATTN_HEADS)


def moba_query_block(q, q_pos, k_blk, v_blk, k_mean, slopes, n_top):
    B, Qc, H, D = q.shape
    nb = k_mean.shape[2]
    q_blk = q_pos // MOBA_BLOCK
    gate = jnp.einsum('bqhd,bhnd->bhqn', q, k_mean.astype(q.dtype), preferred_element_type=jnp.float32)
    fully_past = jnp.arange(nb, dtype=jnp.int32)[None, :] < q_blk[:, None]
    gate = jnp.where(fully_past[None, None], gate, NEG_INF)
    _, top_idx = lax.top_k(gate, n_top)
    top_idx = top_idx.astype(jnp.int32)
    own = jnp.broadcast_to(q_blk[None, None, :, None], (B, H, Qc, 1))
    sel = jnp.concatenate([top_idx, own], axis=-1)
    sel_ok = jnp.concatenate([top_idx < q_blk[None, None, :, None],
                              jnp.ones((B, H, Qc, 1), dtype=bool)], axis=-1)
    bi = jnp.arange(B)[:, None, None, None]
    hi = jnp.arange(H)[None, :, None, None]
    k_sel = k_blk[bi, hi, sel]
    v_sel = v_blk[bi, hi, sel]
    key_pos = sel[..., None] * MOBA_BLOCK + jnp.arange(MOBA_BLOCK, dtype=jnp.int32)
    dist = q_pos[None, None, :, None, None] - key_pos
    s = jnp.einsum('bqhd,bhqskd->bhqsk', q, k_sel, preferred_element_type=jnp.float32) * (HEAD_DIM ** -0.5)
    s = s - slopes[None, :, None, None, None] * dist.astype(jnp.float32)
    s = jnp.where(sel_ok[..., None] & (dist >= 0), s, NEG_INF)
    n_sel = sel.shape[-1]
    p = jax.nn.softmax(s.reshape(B, H, Qc, n_sel * MOBA_BLOCK), axis=-1).reshape(s.shape)
    return jnp.einsum('bhqsk,bhqskd->bqhd', p.astype(v_sel.dtype), v_sel)


def moba_attention(q, k, v, q_pos):
    B, T, H, D = k.shape
    nb = -(-T // MOBA_BLOCK)
    pad = nb * MOBA_BLOCK - T
    k_blk = jnp.pad(k, ((0, 0), (0, pad), (0, 0), (0, 0))).reshape(B, nb, MOBA_BLOCK, H, D).transpose(0, 3, 1, 2, 4)
    v_blk = jnp.pad(v, ((0, 0), (0, pad), (0, 0), (0, 0))).reshape(B, nb, MOBA_BLOCK, H, D).transpose(0, 3, 1, 2, 4)
    k_mean = jnp.mean(k_blk.astype(jnp.float32), axis=3)
    slopes = alibi_slopes()
    n_top = min(MOBA_TOP_K, nb)
    Qn = q.shape[1]
    qc = Q_CHUNK if Qn % Q_CHUNK == 0 else Qn
    nc = Qn // qc
    q_c = q.reshape(B, nc, qc, H, D).transpose(1, 0, 2, 3, 4)
    pos_c = q_pos.reshape(nc, qc)
    out = lax.map(lambda a: moba_query_block(a[0], a[1], k_blk, v_blk, k_mean, slopes, n_top), (q_c, pos_c))
    return out.transpose(1, 0, 2, 3, 4).reshape(B, Qn, H, D)


def rwkv7_time_mix(p, p_prev, s0, mu_shift, decay_w0, decay_up, iclr_a0, iclr_up, gate_up,
                   k_k, k_a, r_k, ln_x_w, ln_x_b):
    B, T, _ = p.shape
    H, N = N_RWKV_HEADS, HEAD_DIM
    f32 = jnp.float32
    xs = (p + mu_shift * (p_prev - p)).astype(f32)
    r, xw, k, v, xa, xg = jnp.split(xs, RW_SPLITS, axis=-1)
    w = decay_w0.astype(f32) + jnp.tanh(xw) @ decay_up.astype(f32)
    w = -jax.nn.softplus(-w) - 0.5
    decay = jnp.exp(-jnp.exp(w))
    a = jax.nn.sigmoid(iclr_a0.astype(f32) + xa @ iclr_up.astype(f32))
    g = jax.nn.sigmoid(xg) @ gate_up.astype(f32)
    kk = (k * k_k.astype(f32)).reshape(B, T, H, N)
    kk = kk * lax.rsqrt(jnp.maximum(jnp.sum(kk * kk, axis=-1, keepdims=True), 1e-24))
    k = k * (1.0 + (a - 1.0) * k_a.astype(f32))
    heads = lambda t: t.reshape(B, T, H, N)
    r_h, k_h, v_h, w_h, a_h = heads(r), heads(k), heads(v), heads(decay), heads(a)

    def step(S, inp):
        r_t, w_t, k_t, v_t, kk_t, a_t = inp
        sa = jnp.einsum('bhij,bhj->bhi', S, -kk_t)
        S = S * w_t[:, :, None, :] + sa[..., None] * (kk_t * a_t)[:, :, None, :] + v_t[..., None] * k_t[:, :, None, :]
        return S, jnp.einsum('bhij,bhj->bhi', S, r_t)

    xs_t = tuple(jnp.moveaxis(t, 1, 0) for t in (r_h, w_h, k_h, v_h, kk, a_h))
    s_fin, y = lax.scan(step, s0.astype(f32), xs_t)
    y = jnp.moveaxis(y, 0, 1)
    mean = jnp.mean(y, axis=-1, keepdims=True)
    var = jnp.mean(jnp.square(y - mean), axis=-1, keepdims=True)
    y = (y - mean) * lax.rsqrt(var + GN_EPS) * ln_x_w.astype(f32).reshape(H, N) + ln_x_b.astype(f32).reshape(H, N)
    y = y + jnp.sum(r_h * k_h * r_k.astype(f32), axis=-1, keepdims=True) * v_h
    out = y.reshape(B, T, H * N) * g
    return out.astype(p.dtype), s_fin


def decoder_layer(x, k_past, v_past, wkv0, shift0, norm_mix_g, w_in, mu_shift, decay_w0, decay_up,
                  iclr_a0, iclr_up, gate_up, k_k, k_a, r_k, ln_x_w, ln_x_b, w_out,
                  norm_ffn_g, w_ffn_up, w_ffn_down):
    B, T, _ = x.shape
    P = k_past.shape[1]
    xn = rms_norm(x, norm_mix_g)
    proj = xn @ w_in
    p_attn, p_rw = proj[..., :ATTN_COLS], proj[..., ATTN_COLS:]
    q, k_new, v_new = [t.reshape(B, T, N_ATTN_HEADS, HEAD_DIM) for t in jnp.split(p_attn, 3, axis=-1)]
    k_all = jnp.concatenate([k_past.astype(k_new.dtype), k_new], axis=1)
    v_all = jnp.concatenate([v_past.astype(v_new.dtype), v_new], axis=1)
    q_pos = P + jnp.arange(T, dtype=jnp.int32)
    attn = moba_attention(q, k_all, v_all, q_pos).reshape(B, T, ATTN_WIDTH)
    p_prev = jnp.concatenate([shift0[:, None, :].astype(p_rw.dtype), p_rw[:, :-1]], axis=1)
    rw, wkv_fin = rwkv7_time_mix(p_rw, p_prev, wkv0, mu_shift, decay_w0, decay_up, iclr_a0, iclr_up,
                                 gate_up, k_k, k_a, r_k, ln_x_w, ln_x_b)
    h = x + jnp.concatenate([attn, rw], axis=-1) @ w_out
    u = jax.nn.relu(rms_norm(h, norm_ffn_g) @ w_ffn_up)
    h = h + (u * u) @ w_ffn_down
    return h, k_new, v_new, wkv_fin, p_rw[:, -1]


def setup_inputs(seed: int = 0) -> dict:
    key = jax.random.key(seed)
    ks = jax.random.split(key, 24)
    f32 = jnp.float32
    n_pages = PAST_LEN // PAGE_SIZE
    n_used = DEC_BATCH * n_pages
    n_pool = n_used + n_used // 4
    nrm = lambda k, shape, s: s * jax.random.normal(k, shape, f32)
    uni = lambda k, shape, lo, hi: jax.random.uniform(k, shape, f32, lo, hi)
    L = DEPTH
    return {
        'x_prompt': nrm(ks[0], (BATCH, SEQ, D_MODEL), 1.0),
        'x_sample': nrm(ks[1], (DEC_BATCH, DEC_SEQ, D_MODEL), 1.0),
        'cache_k': nrm(ks[2], (L, n_pool, PAGE_SIZE, N_ATTN_HEADS, HEAD_DIM), 1.0),
        'cache_v': nrm(ks[3], (L, n_pool, PAGE_SIZE, N_ATTN_HEADS, HEAD_DIM), 1.0),
        'page_table': jax.random.permutation(ks[4], n_pool)[:n_used].reshape(DEC_BATCH, n_pages).astype(jnp.int32),
        'state_wkv': nrm(ks[5], (L, DEC_BATCH, N_RWKV_HEADS, HEAD_DIM, HEAD_DIM), 0.5),
        'state_shift': nrm(ks[6], (L, DEC_BATCH, RWKV_COLS), 1.0),
        'norm_mix_g': 1.0 + nrm(ks[7], (L, D_MODEL), 0.02),
        'w_in': nrm(ks[8], (L, D_MODEL, IN_COLS), D_MODEL ** -0.5),
        'mu_shift': uni(ks[9], (L, RWKV_COLS), 0.1, 0.9),
        'decay_w0': uni(ks[10], (L, RWKV_WIDTH), -5.0, 0.0),
        'decay_up': nrm(ks[11], (L, DECAY_LORA, RWKV_WIDTH), 0.1),
        'iclr_a0': nrm(ks[12], (L, RWKV_WIDTH), 0.1),
        'iclr_up': nrm(ks[13], (L, AAA_LORA, RWKV_WIDTH), 0.1),
        'gate_up': nrm(ks[14], (L, GATE_LORA, RWKV_WIDTH), GATE_LORA ** -0.5),
        'k_k': 0.85 + nrm(ks[15], (L, RWKV_WIDTH), 0.05),
        'k_a': 1.0 + nrm(ks[16], (L, RWKV_WIDTH), 0.05),
        'r_k': nrm(ks[17], (L, N_RWKV_HEADS, HEAD_DIM), 0.1),
        'ln_x_w': 1.0 + nrm(ks[18], (L, RWKV_WIDTH), 0.02),
        'ln_x_b': nrm(ks[19], (L, RWKV_WIDTH), 0.02),
        'w_out': nrm(ks[20], (L, MIX_WIDTH, D_MODEL), MIX_WIDTH ** -0.5),
        'norm_ffn_g': 1.0 + nrm(ks[21], (L, D_MODEL), 0.02),
        'w_ffn_up': nrm(ks[22], (L, D_MODEL, FFN_HIDDEN), D_MODEL ** -0.5),
        'w_ffn_down': nrm(ks[23], (L, FFN_HIDDEN, D_MODEL), FFN_HIDDEN ** -0.5),
        'norm_final_g': 1.0 + nrm(jax.random.fold_in(key, 99), (D_MODEL,), 0.02),
    }


def reference(x_prompt, x_sample, cache_k, cache_v, page_table, state_wkv, state_shift,
              norm_mix_g, w_in, mu_shift, decay_w0, decay_up, iclr_a0, iclr_up, gate_up,
              k_k, k_a, r_k, ln_x_w, ln_x_b, w_out, norm_ffn_g, w_ffn_up, w_ffn_down, norm_final_g):
    n_seq, n_pages = page_table.shape
    b_p = x_prompt.shape[0]
    hp, hs = x_prompt, x_sample
    kp_l, vp_l, wp_l, sp_l, ks_l, vs_l, ws_l, ss_l = [], [], [], [], [], [], [], []
    for l in range(DEPTH):
        lw = (norm_mix_g[l], w_in[l], mu_shift[l], decay_w0[l], decay_up[l], iclr_a0[l], iclr_up[l],
              gate_up[l], k_k[l], k_a[l], r_k[l], ln_x_w[l], ln_x_b[l], w_out[l],
              norm_ffn_g[l], w_ffn_up[l], w_ffn_down[l])
        empty = jnp.zeros((b_p, 0, N_ATTN_HEADS, HEAD_DIM), hp.dtype)
        wkv0 = jnp.zeros((b_p, N_RWKV_HEADS, HEAD_DIM, HEAD_DIM), jnp.float32)
        sh0 = jnp.zeros((b_p, RWKV_COLS), hp.dtype)
        hp, kp, vp, wp, sp = decoder_layer(hp, empty, empty, wkv0, sh0, *lw)
        k_past = cache_k[l][page_table].reshape(n_seq, n_pages * PAGE_SIZE, N_ATTN_HEADS, HEAD_DIM)
        v_past = cache_v[l][page_table].reshape(n_seq, n_pages * PAGE_SIZE, N_ATTN_HEADS, HEAD_DIM)
        hs, kn, vn, wn, sn = decoder_layer(hs, k_past, v_past, state_wkv[l], state_shift[l], *lw)
        kp_l.append(kp); vp_l.append(vp); wp_l.append(wp); sp_l.append(sp)
        ks_l.append(kn); vs_l.append(vn); ws_l.append(wn); ss_l.append(sn)
    y_prompt = rms_norm(hp, norm_final_g)
    y_sample = rms_norm(hs, norm_final_g)
    return (y_prompt, y_sample,
            jnp.stack(kp_l), jnp.stack(vp_l), jnp.stack(wp_l), jnp.stack(sp_l),
            jnp.stack(ks_l), jnp.stack(vs_l), jnp.stack(ws_l), jnp.stack(ss_l))
```

```python
import functools

import jax
import jax.numpy as jnp
from jax import lax
from jax.experimental import pallas as pl
from jax.experimental.pallas import tpu as pltpu

F32 = jnp.float32
BF16 = jnp.bfloat16
HI = lax.Precision.HIGHEST

D_MODEL = 1024
HEAD_DIM = 64
N_HEADS = 8
WIDTH = N_HEADS * HEAD_DIM
MOBA_BLOCK = 256
MOBA_TOP_K = 3
DECAY_LORA = 64
AAA_LORA = 64
GATE_LORA = 128
ATTN_COLS = 3 * WIDTH
RWKV_COLS = 3 * WIDTH + DECAY_LORA + AAA_LORA + GATE_LORA
RMS_EPS = 1e-6
GN_EPS = 64e-5
NEG_INF = -1e30
PAGE_SIZE = 128
RWKV_CHUNK = 64
VMEM_LIMIT = 48 * 1024 * 1024

_O_R, _O_K, _O_V = 0, WIDTH, 2 * WIDTH
_O_XW = 3 * WIDTH
_O_XA = _O_XW + DECAY_LORA
_O_XG = _O_XA + AAA_LORA


def _to_internal(t):
    r, xw, kv, rest = (t[..., :WIDTH], t[..., WIDTH:WIDTH + DECAY_LORA],
                       t[..., WIDTH + DECAY_LORA:3 * WIDTH + DECAY_LORA], t[..., 3 * WIDTH + DECAY_LORA:])
    return jnp.concatenate([r, kv, xw, rest], axis=-1)


def _from_internal(t):
    r, kv, xw, rest = (t[..., :WIDTH], t[..., WIDTH:3 * WIDTH],
                       t[..., 3 * WIDTH:3 * WIDTH + DECAY_LORA], t[..., 3 * WIDTH + DECAY_LORA:])
    return jnp.concatenate([r, xw, kv, rest], axis=-1)


def _params(*sem):
    return pltpu.CompilerParams(dimension_semantics=sem, vmem_limit_bytes=VMEM_LIMIT)


def _rms(x, g):
    return x * lax.rsqrt(jnp.mean(x * x, axis=-1, keepdims=True) + RMS_EPS) * g


def _dot(a, b, **kw):
    return jnp.dot(a, b, preferred_element_type=F32, **kw)


def _dot_nt(a, b, **kw):
    return lax.dot_general(a, b, (((1,), (1,)), ((), ())), preferred_element_type=F32, **kw)


def _dot_tn(a, b, **kw):
    return lax.dot_general(a, b, (((0,), (0,)), ((), ())), preferred_element_type=F32, **kw)


def _inproj_kernel(x_ref, g_ref, w_ref, q_ref, k_ref, v_ref, qb_ref, kb_ref, vb_ref, prw_ref, km_ref):
    xn = _rms(x_ref[...], g_ref[...]).astype(BF16)
    proj = _dot(xn, w_ref[...])
    q = proj[:, 0:WIDTH]
    k = proj[:, WIDTH:2 * WIDTH]
    v = proj[:, 2 * WIDTH:3 * WIDTH]
    q_ref[...] = q
    k_ref[...] = k
    v_ref[...] = v
    qb_ref[...] = (q * (HEAD_DIM ** -0.5)).astype(BF16)
    kb_ref[...] = k.astype(BF16)
    vb_ref[...] = v.astype(BF16)
    prw_ref[...] = proj[:, ATTN_COLS:]
    tm = k.shape[0]
    km_ref[0] = jnp.mean(k.reshape(tm // MOBA_BLOCK, MOBA_BLOCK, WIDTH), axis=1)


def _inproj_prompt(x, g, w_bf16, tm=256):
    m = x.shape[0]
    ncol = w_bf16.shape[1]
    nblk = tm // MOBA_BLOCK
    row = lambda i: (i, 0)
    const = lambda i: (0, 0)
    outs = pl.pallas_call(
        _inproj_kernel,
        grid=(m // tm,),
        in_specs=[pl.BlockSpec((tm, D_MODEL), row),
                  pl.BlockSpec((1, D_MODEL), const),
                  pl.BlockSpec((D_MODEL, ncol), const)],
        out_specs=[pl.BlockSpec((tm, WIDTH), row)] * 6
        + [pl.BlockSpec((tm, RWKV_COLS), row),
           pl.BlockSpec((1, nblk, WIDTH), lambda i: (i, 0, 0))],
        out_shape=[jax.ShapeDtypeStruct((m, WIDTH), F32)] * 3
        + [jax.ShapeDtypeStruct((m, WIDTH), BF16)] * 3
        + [jax.ShapeDtypeStruct((m, RWKV_COLS), F32),
           jax.ShapeDtypeStruct((m // tm, nblk, WIDTH), F32)],
        compiler_params=_params("parallel"),
        name="inproj_prompt",
    )(x, g, w_bf16)
    return outs


def _inproj_small_kernel(x_ref, g_ref, w_ref, o_ref):
    xn = _rms(x_ref[...], g_ref[...]).astype(BF16)
    o_ref[...] = _dot(xn, w_ref[...])


def _inproj_small(x, g, w, tn=256):
    m = x.shape[0]
    ncol = w.shape[1]
    return pl.pallas_call(
        _inproj_small_kernel,
        grid=(ncol // tn,),
        in_specs=[pl.BlockSpec((m, D_MODEL), lambda j: (0, 0)),
                  pl.BlockSpec((1, D_MODEL), lambda j: (0, 0)),
                  pl.BlockSpec((D_MODEL, tn), lambda j: (0, j))],
        out_specs=pl.BlockSpec((m, tn), lambda j: (0, j)),
        out_shape=jax.ShapeDtypeStruct((m, ncol), F32),
        compiler_params=_params("parallel"),
        name="inproj_sample",
    )(x, g, w)


def _ffn_kernel(x_ref, attn_ref, rw_ref, woa_ref, wor_ref, gf_ref, wup_ref, wdn_ref, gfin_ref,
                y_ref, h_sc, hn_sc, acc_sc):
    j = pl.program_id(1)

    @pl.when(j == 0)
    def _():
        h = x_ref[...] + _dot(attn_ref[...], woa_ref[...]) + _dot(rw_ref[...], wor_ref[...])
        h_sc[...] = h
        hn_sc[...] = _rms(h, gf_ref[...]).astype(BF16)
        acc_sc[...] = jnp.zeros_like(acc_sc)

    u = jnp.maximum(_dot(hn_sc[...], wup_ref[...]), 0.0)
    acc_sc[...] += _dot((u * u).astype(BF16), wdn_ref[...])

    @pl.when(j == pl.num_programs(1) - 1)
    def _():
        y_ref[...] = _rms(h_sc[...] + acc_sc[...], gfin_ref[...])


def _out_ffn(x, attn, rw, woa, wor, gf, wup, wdn, gfin, tm, th=1024):
    m = x.shape[0]
    hid = wup.shape[1]
    row = lambda i, j: (i, 0)
    const = lambda i, j: (0, 0)
    return pl.pallas_call(
        _ffn_kernel,
        grid=(m // tm, hid // th),
        in_specs=[pl.BlockSpec((tm, D_MODEL), row),
                  pl.BlockSpec((tm, WIDTH), row),
                  pl.BlockSpec((tm, WIDTH), row),
                  pl.BlockSpec((WIDTH, D_MODEL), const),
                  pl.BlockSpec((WIDTH, D_MODEL), const),
                  pl.BlockSpec((1, D_MODEL), const),
                  pl.BlockSpec((D_MODEL, th), lambda i, j: (0, j)),
                  pl.BlockSpec((th, D_MODEL), lambda i, j: (j, 0)),
                  pl.BlockSpec((1, D_MODEL), const)],
        out_specs=pl.BlockSpec((tm, D_MODEL), row),
        out_shape=jax.ShapeDtypeStruct((m, D_MODEL), F32),
        scratch_shapes=[pltpu.VMEM((tm, D_MODEL), F32),
                        pltpu.VMEM((tm, D_MODEL), BF16),
                        pltpu.VMEM((tm, D_MODEL), F32)],
        compiler_params=_params("parallel", "arbitrary"),
        name="out_ffn",
    )(x, attn, rw, woa, wor, gf, wup, wdn, gfin)


def _block_rank(gm, axis):
    nb = gm.shape[axis]
    idx = lax.broadcasted_iota(jnp.int32, gm.shape, axis)
    rank = jnp.zeros(gm.shape, jnp.int32)
    for m in range(nb):
        gmm = lax.slice_in_dim(gm, m, m + 1, axis=axis)
        beats = (gmm > gm) | ((gmm == gm) & (m < idx))
        rank = rank + beats.astype(jnp.int32)
    return rank


def _moba_kernel(slopes_ref, qf_ref, qb_ref, kb_ref, vb_ref, km_ref, o_ref, m_sc, l_sc, acc_sc, *, nb):
    hp = pl.program_id(1)
    qi = pl.program_id(2)
    blk = MOBA_BLOCK
    n_top = min(MOBA_TOP_K, nb)
    row = lax.broadcasted_iota(jnp.int32, (blk, blk), 0)
    col = lax.broadcasted_iota(jnp.int32, (blk, blk), 1)
    causal = row >= col
    colf = lax.broadcasted_iota(jnp.int32, (1, blk), 1).astype(F32)
    nidx = lax.broadcasted_iota(jnp.int32, (blk, nb), 1)
    q0 = pl.multiple_of(qi * blk, blk)

    for hh in range(2):
        lanes = slice(hh * HEAD_DIM, (hh + 1) * HEAD_DIM)
        slope = slopes_ref[hp * 2 + hh]
        gate = _dot_nt(qf_ref[:, lanes].astype(BF16), km_ref[0, :, lanes].astype(BF16))
        valid = nidx < qi
        gm = jnp.where(valid, gate, NEG_INF)
        sel = ((_block_rank(gm, 1) < n_top) & valid).astype(F32)
        qb = qb_ref[:, lanes]

        s = _dot_nt(qb, kb_ref[pl.ds(q0, blk), lanes]) + slope * colf
        s = jnp.where(causal, s, NEG_INF)
        m0 = jnp.max(s, axis=-1, keepdims=True)
        p = jnp.exp(s - m0)
        m_sc[hh] = m0
        l_sc[hh] = jnp.sum(p, axis=-1, keepdims=True)
        acc_sc[hh] = _dot(p.astype(BF16), vb_ref[pl.ds(q0, blk), lanes])

        for j in range(nb - 1):
            @pl.when(j < qi)
            def _(j=j, hh=hh, lanes=lanes, slope=slope, sel=sel, qb=qb):
                bias = slope * (colf - (qi - j).astype(F32) * blk)
                s = _dot_nt(qb, kb_ref[j * blk:(j + 1) * blk, lanes]) + bias
                s = jnp.where(sel[:, j:j + 1] > 0.0, s, NEG_INF)
                m_old = m_sc[hh]
                m_new = jnp.maximum(m_old, jnp.max(s, axis=-1, keepdims=True))
                alpha = jnp.exp(m_old - m_new)
                p = jnp.exp(s - m_new)
                m_sc[hh] = m_new
                l_sc[hh] = alpha * l_sc[hh] + jnp.sum(p, axis=-1, keepdims=True)
                acc_sc[hh] = alpha * acc_sc[hh] + _dot(p.astype(BF16), vb_ref[j * blk:(j + 1) * blk, lanes])

    for hh in range(2):
        o_ref[:, hh * HEAD_DIM:(hh + 1) * HEAD_DIM] = (acc_sc[hh] / l_sc[hh]).astype(o_ref.dtype)


def _moba_prompt(slopes, qf, qb, kb, vb, kmean, batch, seq):
    nb = seq // MOBA_BLOCK
    blk = MOBA_BLOCK
    qmap = lambda b, hp, qi, s: (b * nb + qi, hp)
    kvmap = lambda b, hp, qi, s: (b, hp)
    grid_spec = pltpu.PrefetchScalarGridSpec(
        num_scalar_prefetch=1,
        grid=(batch, N_HEADS // 2, nb),
        in_specs=[pl.BlockSpec((blk, 128), qmap),
                  pl.BlockSpec((blk, 128), qmap),
                  pl.BlockSpec((seq, 128), kvmap),
                  pl.BlockSpec((seq, 128), kvmap),
                  pl.BlockSpec((1, nb, 128), lambda b, hp, qi, s: (b, 0, hp))],
        out_specs=pl.BlockSpec((blk, 128), qmap),
        scratch_shapes=[pltpu.VMEM((2, blk, 1), F32),
                        pltpu.VMEM((2, blk, 1), F32),
                        pltpu.VMEM((2, blk, HEAD_DIM), F32)],
    )
    return pl.pallas_call(
        functools.partial(_moba_kernel, nb=nb),
        grid_spec=grid_spec,
        out_shape=jax.ShapeDtypeStruct((batch * seq, WIDTH), BF16),
        compiler_params=_params("parallel", "parallel", "arbitrary"),
        name="moba_prompt",
    )(slopes, qf, qb, kb, vb, kmean)


def _rwkv_pointwise(p, pprev, mu, w0, decay_up, a0, iclr_up, gate_up, k_k, k_a):
    xs = p + mu * (pprev - p)
    r = xs[:, _O_R:_O_R + WIDTH]
    k = xs[:, _O_K:_O_K + WIDTH]
    v = xs[:, _O_V:_O_V + WIDTH]
    xw = xs[:, _O_XW:_O_XW + DECAY_LORA]
    xa = xs[:, _O_XA:_O_XA + AAA_LORA]
    xg = xs[:, _O_XG:_O_XG + GATE_LORA]
    w = w0 + _dot(jnp.tanh(xw), decay_up, precision=HI)
    w = -jax.nn.softplus(-w) - 0.5
    logdecay = -jnp.exp(w)
    a = jax.nn.sigmoid(a0 + _dot(xa, iclr_up, precision=HI))
    g = _dot(jax.nn.sigmoid(xg), gate_up, precision=HI)
    kk = k * k_k
    k2 = k * (1.0 + (a - 1.0) * k_a)
    return r, k2, v, kk, a, g, logdecay


def _head_norm(kk_h):
    return kk_h * lax.rsqrt(jnp.maximum(jnp.sum(kk_h * kk_h, axis=-1, keepdims=True), 1e-24))


def _group_norm_out(y_h, r_h, k_h, v_h, g_h, rk_h, lnw_h, lnb_h):
    mean = jnp.mean(y_h, axis=-1, keepdims=True)
    var = jnp.mean(jnp.square(y_h - mean), axis=-1, keepdims=True)
    yn = (y_h - mean) * lax.rsqrt(var + GN_EPS) * lnw_h + lnb_h
    yn = yn + jnp.sum(r_h * k_h * rk_h, axis=-1, keepdims=True) * v_h
    return yn * g_h


def _unit_lower_inverse(a):
    n = a.shape[0]
    row = lax.broadcasted_iota(jnp.int32, (n, n), 0)
    col = lax.broadcasted_iota(jnp.int32, (n, n), 1)
    eye = (row == col).astype(F32)
    size = 16
    same = (row // size) == (col // size)
    ad = jnp.where(same, a, 0.0)
    x = eye - ad
    pw = ad
    for _ in range(3):
        pw = _dot(pw, pw, precision=HI)
        x = x + _dot(x, pw, precision=HI)
    while size < n:
        size2 = size * 2
        same2 = (row // size2) == (col // size2)
        off = jnp.where(same2 & jnp.logical_not(same), a, 0.0)
        x = x - _dot(x, _dot(off, x, precision=HI), precision=HI)
        same = same2
        size = size2
    return x


def _rwkv_chunk_kernel(p_ref, mu_ref, w0_ref, dup_ref, a0_ref, iup_ref, gup_ref, kk_ref, ka_ref,
                       rk_ref, lnw_ref, lnb_ref, o_ref, s_out_ref, s_sc, last_sc):
    c = pl.program_id(1)
    L = RWKV_CHUNK

    @pl.when(c == 0)
    def _():
        s_sc[...] = jnp.zeros_like(s_sc)
        last_sc[...] = jnp.zeros_like(last_sc)

    p = p_ref[...]
    rowi = lax.broadcasted_iota(jnp.int32, p.shape, 0)
    pprev = jnp.where(rowi == 0, last_sc[0:1, :], pltpu.roll(p, 1, 0))
    last_sc[0:1, :] = p[L - 1:L, :]

    r, k2, v, kk, a, g, logdecay = _rwkv_pointwise(
        p, pprev, mu_ref[...], w0_ref[...], dup_ref[...], a0_ref[...], iup_ref[...], gup_ref[...],
        kk_ref[...], ka_ref[...])

    trow = lax.broadcasted_iota(jnp.int32, (L, L), 0)
    tcol = lax.broadcasted_iota(jnp.int32, (L, L), 1)
    lower_incl = trow >= tcol
    lower_strict = trow > tcol
    cum = _dot(lower_incl.astype(F32), logdecay, precision=HI)

    for h in range(N_HEADS):
        ln = slice(h * HEAD_DIM, (h + 1) * HEAD_DIM)
        cum_h = cum[:, ln]
        ld_h = logdecay[:, ln]
        cum_last = cum_h[L - 1:L, :]
        w_inc = jnp.exp(cum_h)
        w_exc = jnp.exp(cum_h - ld_h)
        w_inv = jnp.exp(-cum_h)
        w_tail = jnp.exp(cum_last - cum_h)
        kap = _head_norm(kk[:, ln])
        kap_hat = kap * w_exc
        k_hat = k2[:, ln] * w_inv
        b_hat = kap * a[:, ln] * w_inv
        r_hat = r[:, ln] * w_inc
        v_h = v[:, ln]
        s0 = s_sc[h]

        a_kk = jnp.where(lower_strict, _dot_nt(kap_hat, k_hat, precision=HI), 0.0)
        a_bk = jnp.where(lower_strict, _dot_nt(kap_hat, b_hat, precision=HI), 0.0)
        a_rk = jnp.where(lower_incl, _dot_nt(r_hat, k_hat, precision=HI), 0.0)
        a_rb = jnp.where(lower_incl, _dot_nt(r_hat, b_hat, precision=HI), 0.0)
        t_inv = _unit_lower_inverse(a_bk)

        x1 = _dot_nt(kap_hat, s0, precision=HI) + _dot(a_kk, v_h, precision=HI)
        u = -_dot(t_inv, x1, precision=HI)
        y = (_dot_nt(r_hat, s0, precision=HI) + _dot(a_rk, v_h, precision=HI)
             + _dot(a_rb, u, precision=HI))
        k_bar = k2[:, ln] * w_tail
        b_bar = kap * a[:, ln] * w_tail
        s_new = (s0 * jnp.exp(cum_last)
                 + _dot_tn(v_h, k_bar, precision=HI) + _dot_tn(u, b_bar, precision=HI))
        s_sc[h] = s_new
        o_ref[:, ln] = _group_norm_out(y, r[:, ln], k2[:, ln], v_h, g[:, ln], rk_ref[:, ln],
                                       lnw_ref[:, ln], lnb_ref[:, ln]).astype(o_ref.dtype)

    @pl.when(c == pl.num_programs(1) - 1)
    def _():
        s_out_ref[0] = s_sc[...]


def _rwkv_prompt(p_rw, weights, batch, seq):
    L = RWKV_CHUNK
    nc = seq // L
    const = lambda b, c: (0, 0)
    w_specs = [pl.BlockSpec(w.shape, const) for w in weights]
    return pl.pallas_call(
        _rwkv_chunk_kernel,
        grid=(batch, nc),
        in_specs=[pl.BlockSpec((L, RWKV_COLS), lambda b, c: (b * nc + c, 0))] + w_specs,
        out_specs=[pl.BlockSpec((L, WIDTH), lambda b, c: (b * nc + c, 0)),
                   pl.BlockSpec((1, N_HEADS, HEAD_DIM, HEAD_DIM), lambda b, c: (b, 0, 0, 0))],
        out_shape=[jax.ShapeDtypeStruct((batch * seq, WIDTH), BF16),
                   jax.ShapeDtypeStruct((batch, N_HEADS, HEAD_DIM, HEAD_DIM), F32)],
        scratch_shapes=[pltpu.VMEM((N_HEADS, HEAD_DIM, HEAD_DIM), F32),
                        pltpu.VMEM((8, RWKV_COLS), F32)],
        compiler_params=_params("parallel", "arbitrary"),
        name="rwkv_prompt",
    )(p_rw, *weights)


def _rwkv_step_kernel(p_ref, sh_ref, s_ref, mu_ref, w0_ref, dup_ref, a0_ref, iup_ref, gup_ref, kk_ref,
                      ka_ref, rk_ref, lnw_ref, lnb_ref, o_ref, s_out_ref):
    p = jnp.broadcast_to(p_ref[0], (8, RWKV_COLS))
    pprev = jnp.broadcast_to(sh_ref[0], (8, RWKV_COLS))
    r, k2, v, kk, a, g, logdecay = (t[0:1] for t in _rwkv_pointwise(
        p, pprev, mu_ref[...], w0_ref[...], dup_ref[...], a0_ref[...], iup_ref[...], gup_ref[...],
        kk_ref[...], ka_ref[...]))
    decay = jnp.exp(logdecay)
    n = HEAD_DIM
    eye = lax.broadcasted_iota(jnp.int32, (n, n), 0) == lax.broadcasted_iota(jnp.int32, (n, n), 1)

    def to_col(row_vec):
        return jnp.sum(jnp.where(eye, row_vec, 0.0), axis=-1, keepdims=True)

    def to_row(col_vec):
        return jnp.sum(jnp.where(eye, col_vec, 0.0), axis=0, keepdims=True)

    for h in range(N_HEADS):
        ln = slice(h * n, (h + 1) * n)
        s0 = s_ref[0, h]
        kap = _head_norm(kk[:, ln])
        sa = jnp.sum(s0 * (-kap), axis=-1, keepdims=True)
        s_new = s0 * decay[:, ln] + sa * (kap * a[:, ln]) + to_col(v[:, ln]) * k2[:, ln]
        s_out_ref[0, h] = s_new
        y = to_row(jnp.sum(s_new * r[:, ln], axis=-1, keepdims=True))
        o_ref[0, :, ln] = _group_norm_out(y, r[:, ln], k2[:, ln], v[:, ln], g[:, ln], rk_ref[:, ln],
                                          lnw_ref[:, ln], lnb_ref[:, ln]).astype(o_ref.dtype)


def _rwkv_sample(p_rw, shift, state, weights):
    n = p_rw.shape[0]
    const = lambda s: (0, 0)
    vec = pl.BlockSpec((1, 1, RWKV_COLS), lambda s: (s, 0, 0))
    st = pl.BlockSpec((1, N_HEADS, HEAD_DIM, HEAD_DIM), lambda s: (s, 0, 0, 0))
    rw, s_new = pl.pallas_call(
        _rwkv_step_kernel,
        grid=(n,),
        in_specs=[vec, vec, st] + [pl.BlockSpec(w.shape, const) for w in weights],
        out_specs=[pl.BlockSpec((1, 1, WIDTH), lambda s: (s, 0, 0)), st],
        out_shape=[jax.ShapeDtypeStruct((n, 1, WIDTH), BF16),
                   jax.ShapeDtypeStruct(state.shape, F32)],
        compiler_params=_params("parallel"),
        name="rwkv_sample",
    )(p_rw.reshape(n, 1, RWKV_COLS), shift.reshape(n, 1, RWKV_COLS), state, *weights)
    return rw.reshape(n, WIDTH), s_new


_PAGES_PER_BLOCK = MOBA_BLOCK // PAGE_SIZE


def _sample_attn_kernel(pt_ref, q_ref, kn_ref, vn_ref, slope_ref, *refs, past_len):
    ppb = _PAGES_PER_BLOCK
    k_refs = refs[:ppb]
    v_refs = refs[ppb:2 * ppb]
    o_ref = refs[2 * ppb]
    km_sc, m_sc, l_sc, acc_sc = refs[2 * ppb + 1:]
    n = pl.program_id(1)
    nb = pl.num_programs(1)
    scale = HEAD_DIM ** -0.5
    q = q_ref[0]
    slope = slope_ref[:, 0:1]
    tok = lax.broadcasted_iota(jnp.int32, (PAGE_SIZE, N_HEADS, 1), 0)

    ksum = jnp.zeros((N_HEADS, HEAD_DIM), F32)
    scores = []
    for i in range(ppb):
        kp = k_refs[i][0]
        ksum = ksum + jnp.sum(kp, axis=0)
        s = jnp.sum(kp * q[None], axis=-1, keepdims=True) * scale
        dist = (past_len - (n * MOBA_BLOCK + i * PAGE_SIZE) - tok).astype(F32)
        scores.append(s - slope[None] * dist)
    m = scores[0].max(axis=0)
    for s in scores[1:]:
        m = jnp.maximum(m, s.max(axis=0))
    lsum = jnp.zeros((N_HEADS, 1), F32)
    acc = jnp.zeros((N_HEADS, HEAD_DIM), F32)
    for i in range(ppb):
        p = jnp.exp(scores[i] - m[None])
        lsum = lsum + jnp.sum(p, axis=0)
        acc = acc + jnp.sum(p * v_refs[i][0], axis=0)
    km_sc[n] = ksum * (1.0 / MOBA_BLOCK)
    m_sc[n] = m
    l_sc[n] = lsum
    acc_sc[n] = acc

    @pl.when(n == nb - 1)
    def _():
        rnd = lambda t: t.astype(BF16).astype(F32)
        gate = jnp.sum(rnd(km_sc[...]) * rnd(q)[None], axis=-1, keepdims=True)
        sel = _block_rank(gate, 0) < MOBA_TOP_K
        m_all = m_sc[...]
        s_self = jnp.sum(q * kn_ref[0], axis=-1, keepdims=True) * scale
        m_tot = jnp.maximum(jnp.where(sel, m_all, NEG_INF).max(axis=0), s_self)
        w = jnp.where(sel, jnp.exp(m_all - m_tot[None]), 0.0)
        w_self = jnp.exp(s_self - m_tot)
        den = jnp.sum(w * l_sc[...], axis=0) + w_self
        num = jnp.sum(w * acc_sc[...], axis=0) + w_self * vn_ref[0]
        o_ref[0] = num / den


def _sample_attn(page_table, slopes, q, k_new, v_new, cache_k, cache_v):
    n, n_pages = page_table.shape
    ppb = _PAGES_PER_BLOCK
    nb = n_pages // ppb

    def page_spec(i):
        return pl.BlockSpec((1, PAGE_SIZE, N_HEADS, HEAD_DIM),
                            lambda s, b, pt, i=i: (pt[s * n_pages + b * ppb + i], 0, 0, 0))

    vec = pl.BlockSpec((1, N_HEADS, HEAD_DIM), lambda s, b, pt: (s, 0, 0))
    grid_spec = pltpu.PrefetchScalarGridSpec(
        num_scalar_prefetch=1,
        grid=(n, nb),
        in_specs=[vec, vec, vec, pl.BlockSpec((N_HEADS, HEAD_DIM), lambda s, b, pt: (0, 0))]
        + [page_spec(i) for i in range(ppb)] * 2,
        out_specs=vec,
        scratch_shapes=[pltpu.VMEM((nb, N_HEADS, HEAD_DIM), F32),
                        pltpu.VMEM((nb, N_HEADS, 1), F32),
                        pltpu.VMEM((nb, N_HEADS, 1), F32),
                        pltpu.VMEM((nb, N_HEADS, HEAD_DIM), F32)],
    )
    slope_tile = jnp.broadcast_to(slopes[:, None], (N_HEADS, HEAD_DIM))
    return pl.pallas_call(
        functools.partial(_sample_attn_kernel, past_len=n_pages * PAGE_SIZE),
        grid_spec=grid_spec,
        out_shape=jax.ShapeDtypeStruct((n, N_HEADS, HEAD_DIM), F32),
        compiler_params=_params("parallel", "arbitrary"),
        name="sample_attn",
    )(page_table.reshape(-1), q, k_new, v_new, slope_tile,
      *([cache_k] * ppb), *([cache_v] * ppb))


def kernel(x_prompt, x_sample, cache_k, cache_v, page_table, state_wkv, state_shift,
           norm_mix_g, w_in, mu_shift, decay_w0, decay_up, iclr_a0, iclr_up, gate_up,
           k_k, k_a, r_k, ln_x_w, ln_x_b, w_out, norm_ffn_g, w_ffn_up, w_ffn_down, norm_final_g):
    depth = w_in.shape[0]
    assert depth == 1
    batch, seq, _ = x_prompt.shape
    n_seq, n_pages = page_table.shape
    slopes = jnp.exp2(-8.0 * jnp.arange(1, N_HEADS + 1, dtype=F32) / N_HEADS)

    l = 0
    row = lambda t: t.reshape(1, -1)
    w_in_l = jnp.concatenate([w_in[l][:, :ATTN_COLS], _to_internal(w_in[l][:, ATTN_COLS:])], axis=1)
    g_mix = row(norm_mix_g[l])
    rw_weights = (row(_to_internal(mu_shift[l])), row(decay_w0[l]), decay_up[l], row(iclr_a0[l]), iclr_up[l],
                  gate_up[l], row(k_k[l]), row(k_a[l]), row(r_k[l]), row(ln_x_w[l]), row(ln_x_b[l]))
    wo = w_out[l].astype(BF16)
    ffn_weights = (wo[:WIDTH], wo[WIDTH:], row(norm_ffn_g[l]), w_ffn_up[l].astype(BF16),
                   w_ffn_down[l].astype(BF16), row(norm_final_g))

    xp = x_prompt.reshape(batch * seq, D_MODEL)
    w_in_b = w_in_l.astype(BF16)
    qf, kf, vf, qb, kb, vb, prw, kmean = _inproj_prompt(xp, g_mix, w_in_b)
    nb = seq // MOBA_BLOCK
    attn_p = _moba_prompt(slopes, qf, qb, kb, vb, kmean.reshape(batch, nb, WIDTH), batch, seq)
    rw_p, wkv_p = _rwkv_prompt(prw, rw_weights, batch, seq)
    y_prompt = _out_ffn(xp, attn_p, rw_p, *ffn_weights, tm=512).reshape(batch, seq, D_MODEL)
    shift_p = _from_internal(prw.reshape(batch, seq, RWKV_COLS)[:, -1])

    hd = (N_HEADS, HEAD_DIM)
    xs = x_sample.reshape(n_seq, D_MODEL)
    proj_s = _inproj_small(xs, g_mix, w_in_b)
    q_s, k_s, v_s = proj_s[:, :WIDTH], proj_s[:, WIDTH:2 * WIDTH], proj_s[:, 2 * WIDTH:ATTN_COLS]
    prw_s = proj_s[:, ATTN_COLS:]
    attn_s = _sample_attn(page_table, slopes, q_s.reshape(n_seq, *hd), k_s.reshape(n_seq, *hd),
                          v_s.reshape(n_seq, *hd), cache_k[l], cache_v[l]).reshape(n_seq, WIDTH)
    rw_s, wkv_s = _rwkv_sample(prw_s, _to_internal(state_shift[l]), state_wkv[l], rw_weights)
    y_sample = _out_ffn(xs, attn_s.astype(BF16), rw_s, *ffn_weights, tm=n_seq).reshape(n_seq, 1, D_MODEL)
    shift_s = _from_internal(prw_s)

    return (y_prompt, y_sample,
            kf.reshape(1, batch, seq, *hd), vf.reshape(1, batch, seq, *hd),
            wkv_p[None], shift_p[None],
            k_s.reshape(1, n_seq, 1, *hd), v_s.reshape(1, n_seq, 1, *hd),
            wkv_s[None], shift_s[None])
```

```python
import functools

import jax
import jax.numpy as jnp
from jax import lax
from jax.experimental import pallas as pl
from jax.experimental.pallas import tpu as pltpu

F32 = jnp.float32
BF16 = jnp.bfloat16

D_MODEL = 1024
HEAD_DIM = 64
N_HEADS = 8
WIDTH = N_HEADS * HEAD_DIM
MOBA_BLOCK = 256
MOBA_TOP_K = 3
DECAY_LORA = 64
AAA_LORA = 64
GATE_LORA = 128
ATTN_COLS = 3 * WIDTH
RWKV_COLS = 3 * WIDTH + DECAY_LORA + AAA_LORA + GATE_LORA
RMS_EPS = 1e-6
GN_EPS = 64e-5
NEG_INF = -1e30
PAGE_SIZE = 128
RWKV_CHUNK = 64
VMEM_LIMIT = 48 * 1024 * 1024

_O_R, _O_K, _O_V = 0, WIDTH, 2 * WIDTH
_O_XW = 3 * WIDTH
_O_XA = _O_XW + DECAY_LORA
_O_XG = _O_XA + AAA_LORA


def _to_internal(t):
    r, xw, kv, rest = (t[..., :WIDTH], t[..., WIDTH:WIDTH + DECAY_LORA],
                       t[..., WIDTH + DECAY_LORA:3 * WIDTH + DECAY_LORA], t[..., 3 * WIDTH + DECAY_LORA:])
    return jnp.concatenate([r, kv, xw, rest], axis=-1)


def _from_internal(t):
    r, kv, xw, rest = (t[..., :WIDTH], t[..., WIDTH:3 * WIDTH],
                       t[..., 3 * WIDTH:3 * WIDTH + DECAY_LORA], t[..., 3 * WIDTH + DECAY_LORA:])
    return jnp.concatenate([r, xw, kv, rest], axis=-1)


def _params(*sem):
    return pltpu.CompilerParams(dimension_semantics=sem, vmem_limit_bytes=VMEM_LIMIT)


def _rms(x, g):
    return x * lax.rsqrt(jnp.mean(x * x, axis=-1, keepdims=True) + RMS_EPS) * g


def _dot(a, b, **kw):
    return jnp.dot(a, b, preferred_element_type=F32, **kw)


def _dot_nt(a, b, **kw):
    return lax.dot_general(a, b, (((1,), (1,)), ((), ())), preferred_element_type=F32, **kw)


def _dot_tn(a, b, **kw):
    return lax.dot_general(a, b, (((0,), (0,)), ((), ())), preferred_element_type=F32, **kw)


def _inproj_kernel(x_ref, g_ref, w_ref, k_ref, v_ref, qb_ref, kb_ref, vb_ref, prw_ref, km_ref):
    xn = _rms(x_ref[...], g_ref[...]).astype(BF16)
    proj = _dot(xn, w_ref[...])
    q = proj[:, 0:WIDTH]
    k = proj[:, WIDTH:2 * WIDTH]
    v = proj[:, 2 * WIDTH:3 * WIDTH]
    k_ref[...] = k
    v_ref[...] = v
    qb_ref[...] = (q * (HEAD_DIM ** -0.5)).astype(BF16)
    kb_ref[...] = k.astype(BF16)
    vb_ref[...] = v.astype(BF16)
    prw_ref[...] = proj[:, ATTN_COLS:]
    tm = k.shape[0]
    km_ref[0] = jnp.mean(k.reshape(tm // MOBA_BLOCK, MOBA_BLOCK, WIDTH), axis=1)


def _inproj_prompt(x, g, w_bf16, tm=256):
    m = x.shape[0]
    ncol = w_bf16.shape[1]
    nblk = tm // MOBA_BLOCK
    row = lambda i: (i, 0)
    const = lambda i: (0, 0)
    outs = pl.pallas_call(
        _inproj_kernel,
        grid=(m // tm,),
        in_specs=[pl.BlockSpec((tm, D_MODEL), row),
                  pl.BlockSpec((1, D_MODEL), const),
                  pl.BlockSpec((D_MODEL, ncol), const)],
        out_specs=[pl.BlockSpec((tm, WIDTH), row)] * 5
        + [pl.BlockSpec((tm, RWKV_COLS), row),
           pl.BlockSpec((1, nblk, WIDTH), lambda i: (i, 0, 0))],
        out_shape=[jax.ShapeDtypeStruct((m, WIDTH), F32)] * 2
        + [jax.ShapeDtypeStruct((m, WIDTH), BF16)] * 3
        + [jax.ShapeDtypeStruct((m, RWKV_COLS), F32),
           jax.ShapeDtypeStruct((m // tm, nblk, WIDTH), F32)],
        compiler_params=_params("parallel"),
        name="inproj_prompt",
    )(x, g, w_bf16)
    return outs


def _inproj_small_kernel(x_ref, g_ref, w_ref, o_ref):
    xn = _rms(x_ref[...], g_ref[...]).astype(BF16)
    o_ref[...] = _dot(xn, w_ref[...])


def _inproj_small(x, g, w, tn=256):
    m = x.shape[0]
    ncol = w.shape[1]
    return pl.pallas_call(
        _inproj_small_kernel,
        grid=(ncol // tn,),
        in_specs=[pl.BlockSpec((m, D_MODEL), lambda j: (0, 0)),
                  pl.BlockSpec((1, D_MODEL), lambda j: (0, 0)),
                  pl.BlockSpec((D_MODEL, tn), lambda j: (0, j))],
        out_specs=pl.BlockSpec((m, tn), lambda j: (0, j)),
        out_shape=jax.ShapeDtypeStruct((m, ncol), F32),
        compiler_params=_params("parallel"),
        name="inproj_sample",
    )(x, g, w)


def _ffn_kernel(x_ref, attn_ref, rw_ref, woa_ref, wor_ref, gf_ref, wup_ref, wdn_ref, gfin_ref,
                y_ref, h_sc, hn_sc, acc_sc):
    j = pl.program_id(1)

    @pl.when(j == 0)
    def _():
        h = x_ref[...] + _dot(attn_ref[...], woa_ref[...]) + _dot(rw_ref[...], wor_ref[...])
        h_sc[...] = h
        hn_sc[...] = _rms(h, gf_ref[...]).astype(BF16)
        acc_sc[...] = jnp.zeros_like(acc_sc)

    u = jnp.maximum(_dot(hn_sc[...], wup_ref[...]), 0.0)
    acc_sc[...] += _dot((u * u).astype(BF16), wdn_ref[...])

    @pl.when(j == pl.num_programs(1) - 1)
    def _():
        y_ref[...] = _rms(h_sc[...] + acc_sc[...], gfin_ref[...])


def _out_ffn(x, attn, rw, woa, wor, gf, wup, wdn, gfin, tm, th=1024):
    m = x.shape[0]
    hid = wup.shape[1]
    row = lambda i, j: (i, 0)
    const = lambda i, j: (0, 0)
    return pl.pallas_call(
        _ffn_kernel,
        grid=(m // tm, hid // th),
        in_specs=[pl.BlockSpec((tm, D_MODEL), row),
                  pl.BlockSpec((tm, WIDTH), row),
                  pl.BlockSpec((tm, WIDTH), row),
                  pl.BlockSpec((WIDTH, D_MODEL), const),
                  pl.BlockSpec((WIDTH, D_MODEL), const),
                  pl.BlockSpec((1, D_MODEL), const),
                  pl.BlockSpec((D_MODEL, th), lambda i, j: (0, j)),
                  pl.BlockSpec((th, D_MODEL), lambda i, j: (j, 0)),
                  pl.BlockSpec((1, D_MODEL), const)],
        out_specs=pl.BlockSpec((tm, D_MODEL), row),
        out_shape=jax.ShapeDtypeStruct((m, D_MODEL), F32),
        scratch_shapes=[pltpu.VMEM((tm, D_MODEL), F32),
                        pltpu.VMEM((tm, D_MODEL), BF16),
                        pltpu.VMEM((tm, D_MODEL), F32)],
        compiler_params=_params("parallel", "arbitrary"),
        name="out_ffn",
    )(x, attn, rw, woa, wor, gf, wup, wdn, gfin)


def _block_rank(gm, axis):
    nb = gm.shape[axis]
    idx = lax.broadcasted_iota(jnp.int32, gm.shape, axis)
    rank = jnp.zeros(gm.shape, jnp.int32)
    for m in range(nb):
        gmm = lax.slice_in_dim(gm, m, m + 1, axis=axis)
        beats = (gmm > gm) | ((gmm == gm) & (m < idx))
        rank = rank + beats.astype(jnp.int32)
    return rank


MOBA_HEADS = 2
BIG = 1e30


def _moba_kernel(slopes_ref, qt_ref, ka_ref, vt_ref, km_ref, o_ref, sel_sc, m_sc, l_sc, acc_sc, *, nb):
    hg = pl.program_id(1)
    qi = pl.program_id(2)
    blk = MOBA_BLOCK
    n_top = min(MOBA_TOP_K, nb)
    keyi = lax.broadcasted_iota(jnp.int32, (blk, blk), 0)
    qryi = lax.broadcasted_iota(jnp.int32, (blk, blk), 1)
    causal = keyi <= qryi
    bidx = lax.broadcasted_iota(jnp.int32, (nb, blk), 0)
    q0 = pl.multiple_of(qi * blk, blk)

    for hh in range(MOBA_HEADS):
        qt = qt_ref[0, hh]
        gate = _dot(km_ref[0, hh], qt)
        valid = bidx < qi
        gm = jnp.where(valid, gate, NEG_INF)
        sel_sc[hh] = ((_block_rank(gm, 0) < n_top) & valid).astype(F32)

        s = _dot(ka_ref[0, hh, pl.ds(q0, blk), :], qt)
        s = jnp.where(causal, s, NEG_INF)
        m0 = jnp.max(s, axis=0, keepdims=True)
        p = jnp.exp(s - m0)
        m_sc[hh] = m0
        l_sc[hh] = jnp.sum(p, axis=0, keepdims=True)
        acc_sc[hh] = _dot(vt_ref[0, hh, qi], p.astype(BF16))

    def past_block(j, carry):
        k0 = pl.multiple_of(j * blk, blk)
        hs = range(MOBA_HEADS)
        s = [_dot(ka_ref[0, hh, pl.ds(k0, blk), :], qt_ref[0, hh]) for hh in hs]
        ps, alphas = [], []
        for hh in hs:
            cj = -slopes_ref[hg * MOBA_HEADS + hh] * ((qi - j) * blk).astype(F32)
            picked = sel_sc[hh, pl.ds(j, 1), :] > 0.0
            m_old = m_sc[hh]
            m_new = jnp.maximum(m_old, jnp.where(picked, jnp.max(s[hh], axis=0, keepdims=True) + cj, NEG_INF))
            alpha = jnp.exp(m_old - m_new)
            p = jnp.exp(s[hh] - jnp.where(picked, m_new - cj, BIG))
            m_sc[hh] = m_new
            l_sc[hh] = alpha * l_sc[hh] + jnp.sum(p, axis=0, keepdims=True)
            ps.append(p.astype(BF16))
            alphas.append(alpha)
        pv = [_dot(vt_ref[0, hh, j], ps[hh]) for hh in hs]
        for hh in hs:
            acc_sc[hh] = alphas[hh] * acc_sc[hh] + pv[hh]
        return carry

    lax.fori_loop(0, qi, past_block, 0)

    for hh in range(MOBA_HEADS):
        o_ref[0, hh] = (acc_sc[hh] / l_sc[hh]).astype(o_ref.dtype)


def _moba_prompt(slopes, qb, kb, vb, kmean, batch, seq):
    nb = seq // MOBA_BLOCK
    blk = MOBA_BLOCK
    hd = (N_HEADS, HEAD_DIM)
    pad = 128 - HEAD_DIM - 1
    qt = qb.reshape(batch, seq, *hd).transpose(0, 2, 3, 1)
    slope_row = jnp.broadcast_to(slopes.astype(BF16)[None, :, None, None], (batch, N_HEADS, 1, seq))
    qt = jnp.concatenate([qt, slope_row, jnp.zeros((batch, N_HEADS, pad, seq), BF16)], axis=2)
    ka = kb.reshape(batch, seq, *hd).transpose(0, 2, 1, 3)
    pos = (jnp.arange(seq, dtype=jnp.int32) % blk).astype(BF16)
    pos_col = jnp.broadcast_to(pos[None, None, :, None], (batch, N_HEADS, seq, 1))
    ka = jnp.concatenate([ka, pos_col, jnp.zeros((batch, N_HEADS, seq, pad), BF16)], axis=3)
    vt = vb.reshape(batch, nb, blk, *hd).transpose(0, 3, 1, 4, 2)
    km = kmean.reshape(batch, nb, *hd).transpose(0, 2, 1, 3).astype(BF16)
    km = jnp.concatenate([km, jnp.zeros((batch, N_HEADS, nb, 128 - HEAD_DIM), BF16)], axis=3)

    hgn = MOBA_HEADS
    grid_spec = pltpu.PrefetchScalarGridSpec(
        num_scalar_prefetch=1,
        grid=(batch, N_HEADS // hgn, nb),
        in_specs=[pl.BlockSpec((1, hgn, 128, blk), lambda b, g, qi, s: (b, g, 0, qi)),
                  pl.BlockSpec((1, hgn, seq, 128), lambda b, g, qi, s: (b, g, 0, 0)),
                  pl.BlockSpec((1, hgn, nb, HEAD_DIM, blk), lambda b, g, qi, s: (b, g, 0, 0, 0)),
                  pl.BlockSpec((1, hgn, nb, 128), lambda b, g, qi, s: (b, g, 0, 0))],
        out_specs=pl.BlockSpec((1, hgn, HEAD_DIM, blk), lambda b, g, qi, s: (b, g, 0, qi)),
        scratch_shapes=[pltpu.VMEM((hgn, nb, blk), F32),
                        pltpu.VMEM((hgn, 1, blk), F32),
                        pltpu.VMEM((hgn, 1, blk), F32),
                        pltpu.VMEM((hgn, HEAD_DIM, blk), F32)],
    )
    ot = pl.pallas_call(
        functools.partial(_moba_kernel, nb=nb),
        grid_spec=grid_spec,
        out_shape=jax.ShapeDtypeStruct((batch, N_HEADS, HEAD_DIM, seq), BF16),
        compiler_params=_params("parallel", "parallel", "arbitrary"),
        name="moba_prompt",
    )(slopes, qt, ka, vt, km)
    return ot.transpose(0, 3, 1, 2).reshape(batch * seq, WIDTH)


def _b(t):
    return t.astype(BF16)


def _rwkv_pointwise(p, pprev, mu, w0, decay_up, a0, iclr_up, gate_up, k_k, k_a):
    xs = p + mu * (pprev - p)
    r = xs[:, _O_R:_O_R + WIDTH]
    k = xs[:, _O_K:_O_K + WIDTH]
    v = xs[:, _O_V:_O_V + WIDTH]
    xw = xs[:, _O_XW:_O_XW + DECAY_LORA]
    xa = xs[:, _O_XA:_O_XA + AAA_LORA]
    xg = xs[:, _O_XG:_O_XG + GATE_LORA]
    w = w0 + _dot(_b(jnp.tanh(xw)), _b(decay_up))
    w = -jax.nn.softplus(-w) - 0.5
    logdecay = -jnp.exp(w)
    a = jax.nn.sigmoid(a0 + _dot(_b(xa), _b(iclr_up)))
    g = _dot(_b(jax.nn.sigmoid(xg)), _b(gate_up))
    kk = k * k_k
    k2 = k * (1.0 + (a - 1.0) * k_a)
    return r, k2, v, kk, a, g, logdecay


def _head_norm(kk_h):
    return kk_h * lax.rsqrt(jnp.maximum(jnp.sum(kk_h * kk_h, axis=-1, keepdims=True), 1e-24))


def _group_norm_out(y_h, r_h, k_h, v_h, g_h, rk_h, lnw_h, lnb_h):
    mean = jnp.mean(y_h, axis=-1, keepdims=True)
    var = jnp.mean(jnp.square(y_h - mean), axis=-1, keepdims=True)
    yn = (y_h - mean) * lax.rsqrt(var + GN_EPS) * lnw_h + lnb_h
    yn = yn + jnp.sum(r_h * k_h * rk_h, axis=-1, keepdims=True) * v_h
    return yn * g_h


def _unit_lower_inverse(mats):
    n = mats[0].shape[0]
    row = lax.broadcasted_iota(jnp.int32, (n, n), 0)
    col = lax.broadcasted_iota(jnp.int32, (n, n), 1)
    eye = (row == col).astype(F32)
    size = 16
    same = (row // size) == (col // size)
    pws = [jnp.where(same, a, 0.0) for a in mats]
    xs = [eye - pw for pw in pws]
    for _ in range(3):
        pwb = [_b(pw) for pw in pws]
        pws = [_dot(t, t) for t in pwb]
        xs = [x + _dot(_b(x), _b(pw)) for x, pw in zip(xs, pws)]
    while size < n:
        size2 = size * 2
        same2 = (row // size2) == (col // size2)
        keep = same2 & jnp.logical_not(same)
        xb = [_b(x) for x in xs]
        ox = [_b(_dot(_b(jnp.where(keep, a, 0.0)), t)) for a, t in zip(mats, xb)]
        xs = [x - _dot(t, o) for x, t, o in zip(xs, xb, ox)]
        same = same2
        size = size2
    return xs


def _cumsum_rows(x):
    n = x.shape[0]
    tri = (lax.broadcasted_iota(jnp.int32, (n, n), 0) >= lax.broadcasted_iota(jnp.int32, (n, n), 1)).astype(BF16)
    x1 = _b(x)
    r1 = x - x1.astype(F32)
    x2 = _b(r1)
    x3 = _b(r1 - x2.astype(F32))
    return _dot(tri, x1) + _dot(tri, x2) + _dot(tri, x3)


def _rwkv_chunk_kernel(p_ref, mu_ref, w0_ref, dup_ref, a0_ref, iup_ref, gup_ref, kk_ref, ka_ref,
                       rk_ref, lnw_ref, lnb_ref, o_ref, s_out_ref, s_sc, last_sc):
    c = pl.program_id(1)
    L = RWKV_CHUNK

    @pl.when(c == 0)
    def _():
        s_sc[...] = jnp.zeros_like(s_sc)
        last_sc[...] = jnp.zeros_like(last_sc)

    p = p_ref[...]
    rowi = lax.broadcasted_iota(jnp.int32, p.shape, 0)
    pprev = jnp.where(rowi == 0, last_sc[0:1, :], pltpu.roll(p, 1, 0))
    last_sc[0:1, :] = p[L - 1:L, :]

    r, k2, v, kk, a, g, logdecay = _rwkv_pointwise(
        p, pprev, mu_ref[...], w0_ref[...], dup_ref[...], a0_ref[...], iup_ref[...], gup_ref[...],
        kk_ref[...], ka_ref[...])

    trow = lax.broadcasted_iota(jnp.int32, (L, L), 0)
    tcol = lax.broadcasted_iota(jnp.int32, (L, L), 1)
    lower_incl = trow >= tcol
    lower_strict = trow > tcol
    cum = _cumsum_rows(logdecay)
    cum_end = cum[L - 1:L, :]
    w_inc = jnp.exp(cum)
    w_exc = jnp.exp(cum - logdecay)
    w_inv = jnp.exp(-cum)
    w_tail = jnp.exp(cum_end - cum)
    w_end = jnp.exp(cum_end)
    r_hat_all = r * w_inc
    k_hat_all = k2 * w_inv
    k_bar_all = k2 * w_tail

    heads = range(N_HEADS)
    lns = [slice(h * HEAD_DIM, (h + 1) * HEAD_DIM) for h in heads]
    kap = [_head_norm(kk[:, ln]) for ln in lns]
    kap_hat = [kap[h] * w_exc[:, lns[h]] for h in heads]
    bb = [kap[h] * a[:, lns[h]] for h in heads]
    b_hat = [_b(bb[h] * w_inv[:, lns[h]]) for h in heads]
    b_bar = [_b(bb[h] * w_tail[:, lns[h]]) for h in heads]
    r_hat = [r_hat_all[:, ln] for ln in lns]
    v_b = [_b(v[:, ln]) for ln in lns]
    lhs = [_b(jnp.concatenate([kap_hat[h], r_hat[h]], axis=0)) for h in heads]
    ak = [_dot_nt(lhs[h], _b(k_hat_all[:, lns[h]])) for h in heads]
    ab = [_dot_nt(lhs[h], b_hat[h]) for h in heads]
    a_kr = [_b(jnp.concatenate([jnp.where(lower_strict, t[:L], 0.0), jnp.where(lower_incl, t[L:], 0.0)], axis=0))
            for t in ak]
    a_rb = [_b(jnp.where(lower_incl, t[L:], 0.0)) for t in ab]
    t_inv = [_b(t) for t in _unit_lower_inverse([jnp.where(lower_strict, t[:L], 0.0) for t in ab])]
    av = [_dot(a_kr[h], v_b[h]) for h in heads]
    pm = [_b(_dot(t_inv[h], _b(kap_hat[h]))) for h in heads]
    qm = [_b(_dot(t_inv[h], _b(av[h][:L]))) for h in heads]
    r_eff = [_b(r_hat[h] - _dot(a_rb[h], pm[h])) for h in heads]
    y0 = [av[h][L:] - _dot(a_rb[h], qm[h]) for h in heads]
    ptb = [_b(_dot_tn(pm[h], b_bar[h])) for h in heads]
    cm = [_dot_tn(v_b[h], _b(k_bar_all[:, lns[h]])) - _dot_tn(qm[h], b_bar[h]) for h in heads]
    s0 = [s_sc[h] for h in heads]
    s0b = [_b(t) for t in s0]
    y = [_dot_nt(r_eff[h], s0b[h]) + y0[h] for h in heads]
    s1 = [s0[h] * w_end[:, lns[h]] - _dot(s0b[h], ptb[h]) + cm[h] for h in heads]
    for h in heads:
        ln = lns[h]
        s_sc[h] = s1[h]
        o_ref[:, ln] = _group_norm_out(y[h], r[:, ln], k2[:, ln], v[:, ln], g[:, ln], rk_ref[:, ln],
                                       lnw_ref[:, ln], lnb_ref[:, ln]).astype(o_ref.dtype)

    @pl.when(c == pl.num_programs(1) - 1)
    def _():
        s_out_ref[0] = s_sc[...]


def _rwkv_prompt(p_rw, weights, batch, seq):
    L = RWKV_CHUNK
    nc = seq // L
    const = lambda b, c: (0, 0)
    w_specs = [pl.BlockSpec(w.shape, const) for w in weights]
    return pl.pallas_call(
        _rwkv_chunk_kernel,
        grid=(batch, nc),
        in_specs=[pl.BlockSpec((L, RWKV_COLS), lambda b, c: (b * nc + c, 0))] + w_specs,
        out_specs=[pl.BlockSpec((L, WIDTH), lambda b, c: (b * nc + c, 0)),
                   pl.BlockSpec((1, N_HEADS, HEAD_DIM, HEAD_DIM), lambda b, c: (b, 0, 0, 0))],
        out_shape=[jax.ShapeDtypeStruct((batch * seq, WIDTH), BF16),
                   jax.ShapeDtypeStruct((batch, N_HEADS, HEAD_DIM, HEAD_DIM), F32)],
        scratch_shapes=[pltpu.VMEM((N_HEADS, HEAD_DIM, HEAD_DIM), F32),
                        pltpu.VMEM((8, RWKV_COLS), F32)],
        compiler_params=_params("parallel", "arbitrary"),
        name="rwkv_prompt",
    )(p_rw, *weights)


def _rwkv_step_kernel(p_ref, sh_ref, s_ref, mu_ref, w0_ref, dup_ref, a0_ref, iup_ref, gup_ref, kk_ref,
                      ka_ref, rk_ref, lnw_ref, lnb_ref, o_ref, s_out_ref):
    p = jnp.broadcast_to(p_ref[0], (8, RWKV_COLS))
    pprev = jnp.broadcast_to(sh_ref[0], (8, RWKV_COLS))
    r, k2, v, kk, a, g, logdecay = (t[0:1] for t in _rwkv_pointwise(
        p, pprev, mu_ref[...], w0_ref[...], dup_ref[...], a0_ref[...], iup_ref[...], gup_ref[...],
        kk_ref[...], ka_ref[...]))
    decay = jnp.exp(logdecay)
    n = HEAD_DIM
    eye = lax.broadcasted_iota(jnp.int32, (n, n), 0) == lax.broadcasted_iota(jnp.int32, (n, n), 1)

    def to_col(row_vec):
        return jnp.sum(jnp.where(eye, row_vec, 0.0), axis=-1, keepdims=True)

    def to_row(col_vec):
        return jnp.sum(jnp.where(eye, col_vec, 0.0), axis=0, keepdims=True)

    for h in range(N_HEADS):
        ln = slice(h * n, (h + 1) * n)
        s0 = s_ref[0, h]
        kap = _head_norm(kk[:, ln])
        sa = jnp.sum(s0 * (-kap), axis=-1, keepdims=True)
        s_new = s0 * decay[:, ln] + sa * (kap * a[:, ln]) + to_col(v[:, ln]) * k2[:, ln]
        s_out_ref[0, h] = s_new
        y = to_row(jnp.sum(s_new * r[:, ln], axis=-1, keepdims=True))
        o_ref[0, :, ln] = _group_norm_out(y, r[:, ln], k2[:, ln], v[:, ln], g[:, ln], rk_ref[:, ln],
                                          lnw_ref[:, ln], lnb_ref[:, ln]).astype(o_ref.dtype)


def _rwkv_sample(p_rw, shift, state, weights):
    n = p_rw.shape[0]
    const = lambda s: (0, 0)
    vec = pl.BlockSpec((1, 1, RWKV_COLS), lambda s: (s, 0, 0))
    st = pl.BlockSpec((1, N_HEADS, HEAD_DIM, HEAD_DIM), lambda s: (s, 0, 0, 0))
    rw, s_new = pl.pallas_call(
        _rwkv_step_kernel,
        grid=(n,),
        in_specs=[vec, vec, st] + [pl.BlockSpec(w.shape, const) for w in weights],
        out_specs=[pl.BlockSpec((1, 1, WIDTH), lambda s: (s, 0, 0)), st],
        out_shape=[jax.ShapeDtypeStruct((n, 1, WIDTH), BF16),
                   jax.ShapeDtypeStruct(state.shape, F32)],
        compiler_params=_params("parallel"),
        name="rwkv_sample",
    )(p_rw.reshape(n, 1, RWKV_COLS), shift.reshape(n, 1, RWKV_COLS), state, *weights)
    return rw.reshape(n, WIDTH), s_new


_PAGES_PER_BLOCK = MOBA_BLOCK // PAGE_SIZE


def _sample_attn_kernel(pt_ref, q_ref, kn_ref, vn_ref, slope_ref, *refs, past_len):
    ppb = _PAGES_PER_BLOCK
    k_refs = refs[:ppb]
    v_refs = refs[ppb:2 * ppb]
    o_ref = refs[2 * ppb]
    km_sc, m_sc, l_sc, acc_sc = refs[2 * ppb + 1:]
    n = pl.program_id(1)
    nb = pl.num_programs(1)
    scale = HEAD_DIM ** -0.5
    q = q_ref[0]
    slope = slope_ref[:, 0:1]
    tok = lax.broadcasted_iota(jnp.int32, (PAGE_SIZE, N_HEADS, 1), 0)

    ksum = jnp.zeros((N_HEADS, HEAD_DIM), F32)
    scores = []
    for i in range(ppb):
        kp = k_refs[i][0]
        ksum = ksum + jnp.sum(kp, axis=0)
        s = jnp.sum(kp * q[None], axis=-1, keepdims=True) * scale
        dist = (past_len - (n * MOBA_BLOCK + i * PAGE_SIZE) - tok).astype(F32)
        scores.append(s - slope[None] * dist)
    m = scores[0].max(axis=0)
    for s in scores[1:]:
        m = jnp.maximum(m, s.max(axis=0))
    lsum = jnp.zeros((N_HEADS, 1), F32)
    acc = jnp.zeros((N_HEADS, HEAD_DIM), F32)
    for i in range(ppb):
        p = jnp.exp(scores[i] - m[None])
        lsum = lsum + jnp.sum(p, axis=0)
        acc = acc + jnp.sum(p * v_refs[i][0], axis=0)
    km_sc[n] = ksum * (1.0 / MOBA_BLOCK)
    m_sc[n] = m
    l_sc[n] = lsum
    acc_sc[n] = acc

    @pl.when(n == nb - 1)
    def _():
        rnd = lambda t: t.astype(BF16).astype(F32)
        gate = jnp.sum(rnd(km_sc[...]) * rnd(q)[None], axis=-1, keepdims=True)
        sel = _block_rank(gate, 0) < MOBA_TOP_K
        m_all = m_sc[...]
        s_self = jnp.sum(q * kn_ref[0], axis=-1, keepdims=True) * scale
        m_tot = jnp.maximum(jnp.where(sel, m_all, NEG_INF).max(axis=0), s_self)
        w = jnp.where(sel, jnp.exp(m_all - m_tot[None]), 0.0)
        w_self = jnp.exp(s_self - m_tot)
        den = jnp.sum(w * l_sc[...], axis=0) + w_self
        num = jnp.sum(w * acc_sc[...], axis=0) + w_self * vn_ref[0]
        o_ref[0] = num / den


def _sample_attn(page_table, slopes, q, k_new, v_new, cache_k, cache_v):
    n, n_pages = page_table.shape
    ppb = _PAGES_PER_BLOCK
    nb = n_pages // ppb

    def page_spec(i):
        return pl.BlockSpec((1, PAGE_SIZE, N_HEADS, HEAD_DIM),
                            lambda s, b, pt, i=i: (pt[s * n_pages + b * ppb + i], 0, 0, 0))

    vec = pl.BlockSpec((1, N_HEADS, HEAD_DIM), lambda s, b, pt: (s, 0, 0))
    grid_spec = pltpu.PrefetchScalarGridSpec(
        num_scalar_prefetch=1,
        grid=(n, nb),
        in_specs=[vec, vec, vec, pl.BlockSpec((N_HEADS, HEAD_DIM), lambda s, b, pt: (0, 0))]
        + [page_spec(i) for i in range(ppb)] * 2,
        out_specs=vec,
        scratch_shapes=[pltpu.VMEM((nb, N_HEADS, HEAD_DIM), F32),
                        pltpu.VMEM((nb, N_HEADS, 1), F32),
                        pltpu.VMEM((nb, N_HEADS, 1), F32),
                        pltpu.VMEM((nb, N_HEADS, HEAD_DIM), F32)],
    )
    slope_tile = jnp.broadcast_to(slopes[:, None], (N_HEADS, HEAD_DIM))
    return pl.pallas_call(
        functools.partial(_sample_attn_kernel, past_len=n_pages * PAGE_SIZE),
        grid_spec=grid_spec,
        out_shape=jax.ShapeDtypeStruct((n, N_HEADS, HEAD_DIM), F32),
        compiler_params=_params("parallel", "arbitrary"),
        name="sample_attn",
    )(page_table.reshape(-1), q, k_new, v_new, slope_tile,
      *([cache_k] * ppb), *([cache_v] * ppb))


def kernel(x_prompt, x_sample, cache_k, cache_v, page_table, state_wkv, state_shift,
           norm_mix_g, w_in, mu_shift, decay_w0, decay_up, iclr_a0, iclr_up, gate_up,
           k_k, k_a, r_k, ln_x_w, ln_x_b, w_out, norm_ffn_g, w_ffn_up, w_ffn_down, norm_final_g):
    depth = w_in.shape[0]
    assert depth == 1
    batch, seq, _ = x_prompt.shape
    n_seq, n_pages = page_table.shape
    slopes = jnp.exp2(-8.0 * jnp.arange(1, N_HEADS + 1, dtype=F32) / N_HEADS)

    l = 0
    row = lambda t: t.reshape(1, -1)
    w_in_l = jnp.concatenate([w_in[l][:, :ATTN_COLS], _to_internal(w_in[l][:, ATTN_COLS:])], axis=1)
    g_mix = row(norm_mix_g[l])
    rw_weights = (row(_to_internal(mu_shift[l])), row(decay_w0[l]), decay_up[l], row(iclr_a0[l]), iclr_up[l],
                  gate_up[l], row(k_k[l]), row(k_a[l]), row(r_k[l]), row(ln_x_w[l]), row(ln_x_b[l]))
    wo = w_out[l].astype(BF16)
    ffn_weights = (wo[:WIDTH], wo[WIDTH:], row(norm_ffn_g[l]), w_ffn_up[l].astype(BF16),
                   w_ffn_down[l].astype(BF16), row(norm_final_g))

    xp = x_prompt.reshape(batch * seq, D_MODEL)
    w_in_b = w_in_l.astype(BF16)
    kf, vf, qb, kb, vb, prw, kmean = _inproj_prompt(xp, g_mix, w_in_b)
    nb = seq // MOBA_BLOCK
    attn_p = _moba_prompt(slopes, qb, kb, vb, kmean.reshape(batch, nb, WIDTH), batch, seq)
    rw_p, wkv_p = _rwkv_prompt(prw, rw_weights, batch, seq)
    y_prompt = _out_ffn(xp, attn_p, rw_p, *ffn_weights, tm=512).reshape(batch, seq, D_MODEL)
    shift_p = _from_internal(prw.reshape(batch, seq, RWKV_COLS)[:, -1])

    hd = (N_HEADS, HEAD_DIM)
    xs = x_sample.reshape(n_seq, D_MODEL)
    proj_s = _inproj_small(xs, g_mix, w_in_b)
    q_s, k_s, v_s = proj_s[:, :WIDTH], proj_s[:, WIDTH:2 * WIDTH], proj_s[:, 2 * WIDTH:ATTN_COLS]
    prw_s = proj_s[:, ATTN_COLS:]
    attn_s = _sample_attn(page_table, slopes, q_s.reshape(n_seq, *hd), k_s.reshape(n_seq, *hd),
                          v_s.reshape(n_seq, *hd), cache_k.reshape(cache_k.shape[1:]),
                          cache_v.reshape(cache_v.shape[1:])).reshape(n_seq, WIDTH)
    rw_s, wkv_s = _rwkv_sample(prw_s, _to_internal(state_shift[l]), state_wkv[l], rw_weights)
    y_sample = _out_ffn(xs, attn_s.astype(BF16), rw_s, *ffn_weights, tm=n_seq).reshape(n_seq, 1, D_MODEL)
    shift_s = _from_internal(prw_s)

    return (y_prompt, y_sample,
            kf.reshape(1, batch, seq, *hd), vf.reshape(1, batch, seq, *hd),
            wkv_p[None], shift_p[None],
            k_s.reshape(1, n_seq, 1, *hd), v_s.reshape(1, n_seq, 1, *hd),
            wkv_s[None], shift_s[None])
```

```python
import functools

import jax
import jax.numpy as jnp
from jax import lax
from jax.experimental import pallas as pl
from jax.experimental.pallas import tpu as pltpu

F32 = jnp.float32
BF16 = jnp.bfloat16

D_MODEL = 1024
HEAD_DIM = 64
N_HEADS = 8
WIDTH = N_HEADS * HEAD_DIM
MOBA_BLOCK = 256
MOBA_TOP_K = 3
DECAY_LORA = 64
AAA_LORA = 64
GATE_LORA = 128
ATTN_COLS = 3 * WIDTH
RWKV_COLS = 3 * WIDTH + DECAY_LORA + AAA_LORA + GATE_LORA
RMS_EPS = 1e-6
GN_EPS = 64e-5
NEG_INF = -1e30
PAGE_SIZE = 128
RWKV_CHUNK = 64
VMEM_LIMIT = 48 * 1024 * 1024

_O_R, _O_K, _O_V = 0, WIDTH, 2 * WIDTH
_O_XW = 3 * WIDTH
_O_XA = _O_XW + DECAY_LORA
_O_XG = _O_XA + AAA_LORA


def _to_internal(t):
    r, xw, kv, rest = (t[..., :WIDTH], t[..., WIDTH:WIDTH + DECAY_LORA],
                       t[..., WIDTH + DECAY_LORA:3 * WIDTH + DECAY_LORA], t[..., 3 * WIDTH + DECAY_LORA:])
    return jnp.concatenate([r, kv, xw, rest], axis=-1)


def _from_internal(t):
    r, kv, xw, rest = (t[..., :WIDTH], t[..., WIDTH:3 * WIDTH],
                       t[..., 3 * WIDTH:3 * WIDTH + DECAY_LORA], t[..., 3 * WIDTH + DECAY_LORA:])
    return jnp.concatenate([r, xw, kv, rest], axis=-1)


def _params(*sem):
    return pltpu.CompilerParams(dimension_semantics=sem, vmem_limit_bytes=VMEM_LIMIT)


def _rms(x, g):
    return x * lax.rsqrt(jnp.mean(x * x, axis=-1, keepdims=True) + RMS_EPS) * g


def _dot(a, b, **kw):
    return jnp.dot(a, b, preferred_element_type=F32, **kw)


def _dot_nt(a, b, **kw):
    return lax.dot_general(a, b, (((1,), (1,)), ((), ())), preferred_element_type=F32, **kw)


def _dot_tn(a, b, **kw):
    return lax.dot_general(a, b, (((0,), (0,)), ((), ())), preferred_element_type=F32, **kw)


def _inproj_kernel(x_ref, g_ref, w_ref, k_ref, v_ref, qb_ref, kb_ref, vb_ref, prw_ref, km_ref):
    xn = _rms(x_ref[...], g_ref[...]).astype(BF16)
    proj = _dot(xn, w_ref[...])
    q = proj[:, 0:WIDTH]
    k = proj[:, WIDTH:2 * WIDTH]
    v = proj[:, 2 * WIDTH:3 * WIDTH]
    k_ref[...] = k
    v_ref[...] = v
    qb_ref[...] = (q * (HEAD_DIM ** -0.5)).astype(BF16)
    kb_ref[...] = k.astype(BF16)
    vb_ref[...] = v.astype(BF16)
    prw_ref[...] = proj[:, ATTN_COLS:]
    tm = k.shape[0]
    km_ref[0] = jnp.mean(k.reshape(tm // MOBA_BLOCK, MOBA_BLOCK, WIDTH), axis=1)


def _inproj_prompt(x, g, w_bf16, tm=256):
    m = x.shape[0]
    ncol = w_bf16.shape[1]
    nblk = tm // MOBA_BLOCK
    row = lambda i: (i, 0)
    const = lambda i: (0, 0)
    outs = pl.pallas_call(
        _inproj_kernel,
        grid=(m // tm,),
        in_specs=[pl.BlockSpec((tm, D_MODEL), row),
                  pl.BlockSpec((1, D_MODEL), const),
                  pl.BlockSpec((D_MODEL, ncol), const)],
        out_specs=[pl.BlockSpec((tm, WIDTH), row)] * 5
        + [pl.BlockSpec((tm, RWKV_COLS), row),
           pl.BlockSpec((1, nblk, WIDTH), lambda i: (i, 0, 0))],
        out_shape=[jax.ShapeDtypeStruct((m, WIDTH), F32)] * 2
        + [jax.ShapeDtypeStruct((m, WIDTH), BF16)] * 3
        + [jax.ShapeDtypeStruct((m, RWKV_COLS), F32),
           jax.ShapeDtypeStruct((m // tm, nblk, WIDTH), F32)],
        compiler_params=_params("parallel"),
        name="inproj_prompt",
    )(x, g, w_bf16)
    return outs


def _inproj_small_kernel(x_ref, g_ref, w_ref, o_ref):
    xn = _rms(x_ref[...], g_ref[...]).astype(BF16)
    o_ref[...] = _dot(xn, w_ref[...])


def _inproj_small(x, g, w, tn=256):
    m = x.shape[0]
    ncol = w.shape[1]
    return pl.pallas_call(
        _inproj_small_kernel,
        grid=(ncol // tn,),
        in_specs=[pl.BlockSpec((m, D_MODEL), lambda j: (0, 0)),
                  pl.BlockSpec((1, D_MODEL), lambda j: (0, 0)),
                  pl.BlockSpec((D_MODEL, tn), lambda j: (0, j))],
        out_specs=pl.BlockSpec((m, tn), lambda j: (0, j)),
        out_shape=jax.ShapeDtypeStruct((m, ncol), F32),
        compiler_params=_params("parallel"),
        name="inproj_sample",
    )(x, g, w)


def _ffn_kernel(x_ref, attn_ref, rw_ref, woa_ref, wor_ref, gf_ref, wup_ref, wdn_ref, gfin_ref,
                y_ref, h_sc, hn_sc, acc_sc):
    j = pl.program_id(1)

    @pl.when(j == 0)
    def _():
        h = x_ref[...] + _dot(attn_ref[...], woa_ref[...]) + _dot(rw_ref[...], wor_ref[...])
        h_sc[...] = h
        hn_sc[...] = _rms(h, gf_ref[...]).astype(BF16)
        acc_sc[...] = jnp.zeros_like(acc_sc)

    u = jnp.maximum(_dot(hn_sc[...], wup_ref[...]), 0.0)
    acc_sc[...] += _dot((u * u).astype(BF16), wdn_ref[...])

    @pl.when(j == pl.num_programs(1) - 1)
    def _():
        y_ref[...] = _rms(h_sc[...] + acc_sc[...], gfin_ref[...])


def _out_ffn(x, attn, rw, woa, wor, gf, wup, wdn, gfin, tm, th=1024):
    m = x.shape[0]
    hid = wup.shape[1]
    row = lambda i, j: (i, 0)
    const = lambda i, j: (0, 0)
    return pl.pallas_call(
        _ffn_kernel,
        grid=(m // tm, hid // th),
        in_specs=[pl.BlockSpec((tm, D_MODEL), row),
                  pl.BlockSpec((tm, WIDTH), row),
                  pl.BlockSpec((tm, WIDTH), row),
                  pl.BlockSpec((WIDTH, D_MODEL), const),
                  pl.BlockSpec((WIDTH, D_MODEL), const),
                  pl.BlockSpec((1, D_MODEL), const),
                  pl.BlockSpec((D_MODEL, th), lambda i, j: (0, j)),
                  pl.BlockSpec((th, D_MODEL), lambda i, j: (j, 0)),
                  pl.BlockSpec((1, D_MODEL), const)],
        out_specs=pl.BlockSpec((tm, D_MODEL), row),
        out_shape=jax.ShapeDtypeStruct((m, D_MODEL), F32),
        scratch_shapes=[pltpu.VMEM((tm, D_MODEL), F32),
                        pltpu.VMEM((tm, D_MODEL), BF16),
                        pltpu.VMEM((tm, D_MODEL), F32)],
        compiler_params=_params("parallel", "arbitrary"),
        name="out_ffn",
    )(x, attn, rw, woa, wor, gf, wup, wdn, gfin)


def _block_rank(gm, axis):
    nb = gm.shape[axis]
    idx = lax.broadcasted_iota(jnp.int32, gm.shape, axis)
    rank = jnp.zeros(gm.shape, jnp.int32)
    for m in range(nb):
        gmm = lax.slice_in_dim(gm, m, m + 1, axis=axis)
        beats = (gmm > gm) | ((gmm == gm) & (m < idx))
        rank = rank + beats.astype(jnp.int32)
    return rank


MOBA_HEADS = 2
BIG = 1e30


def _moba_kernel(slopes_ref, qt_ref, ka_ref, vt_ref, km_ref, o_ref, sel_sc, m_sc, l_sc, acc_sc, *, nb):
    hg = pl.program_id(1)
    qi = pl.program_id(2)
    blk = MOBA_BLOCK
    n_top = min(MOBA_TOP_K, nb)
    keyi = lax.broadcasted_iota(jnp.int32, (blk, blk), 0)
    qryi = lax.broadcasted_iota(jnp.int32, (blk, blk), 1)
    causal = keyi <= qryi
    bidx = lax.broadcasted_iota(jnp.int32, (nb, blk), 0)
    q0 = pl.multiple_of(qi * blk, blk)

    for hh in range(MOBA_HEADS):
        qt = qt_ref[0, hh]
        gate = _dot(km_ref[0, hh], qt)
        valid = bidx < qi
        gm = jnp.where(valid, gate, NEG_INF)
        sel_sc[hh] = ((_block_rank(gm, 0) < n_top) & valid).astype(F32)

        s = _dot(ka_ref[0, hh, pl.ds(q0, blk), :], qt)
        s = jnp.where(causal, s, NEG_INF)
        m0 = jnp.max(s, axis=0, keepdims=True)
        p = jnp.exp(s - m0)
        m_sc[hh] = m0
        l_sc[hh] = jnp.sum(p, axis=0, keepdims=True)
        acc_sc[hh] = _dot(vt_ref[0, hh, qi], p.astype(BF16))

    def past_block(j, carry):
        k0 = pl.multiple_of(j * blk, blk)
        hs = range(MOBA_HEADS)
        s = [_dot(ka_ref[0, hh, pl.ds(k0, blk), :], qt_ref[0, hh]) for hh in hs]
        ps, alphas = [], []
        for hh in hs:
            cj = -slopes_ref[hg * MOBA_HEADS + hh] * ((qi - j) * blk).astype(F32)
            picked = sel_sc[hh, pl.ds(j, 1), :] > 0.0
            m_old = m_sc[hh]
            m_new = jnp.maximum(m_old, jnp.where(picked, jnp.max(s[hh], axis=0, keepdims=True) + cj, NEG_INF))
            alpha = jnp.exp(m_old - m_new)
            p = jnp.exp(s[hh] - jnp.where(picked, m_new - cj, BIG))
            m_sc[hh] = m_new
            l_sc[hh] = alpha * l_sc[hh] + jnp.sum(p, axis=0, keepdims=True)
            ps.append(p.astype(BF16))
            alphas.append(alpha)
        pv = [_dot(vt_ref[0, hh, j], ps[hh]) for hh in hs]
        for hh in hs:
            acc_sc[hh] = alphas[hh] * acc_sc[hh] + pv[hh]
        return carry

    lax.fori_loop(0, qi, past_block, 0)

    for hh in range(MOBA_HEADS):
        o_ref[0, hh] = (acc_sc[hh] / l_sc[hh]).astype(o_ref.dtype)


def _moba_prompt(slopes, qb, kb, vb, kmean, batch, seq):
    nb = seq // MOBA_BLOCK
    blk = MOBA_BLOCK
    hd = (N_HEADS, HEAD_DIM)
    pad = 128 - HEAD_DIM - 1
    qt = qb.reshape(batch, seq, *hd).transpose(0, 2, 3, 1)
    slope_row = jnp.broadcast_to(slopes.astype(BF16)[None, :, None, None], (batch, N_HEADS, 1, seq))
    qt = jnp.concatenate([qt, slope_row, jnp.zeros((batch, N_HEADS, pad, seq), BF16)], axis=2)
    ka = kb.reshape(batch, seq, *hd).transpose(0, 2, 1, 3)
    pos = (jnp.arange(seq, dtype=jnp.int32) % blk).astype(BF16)
    pos_col = jnp.broadcast_to(pos[None, None, :, None], (batch, N_HEADS, seq, 1))
    ka = jnp.concatenate([ka, pos_col, jnp.zeros((batch, N_HEADS, seq, pad), BF16)], axis=3)
    vt = vb.reshape(batch, nb, blk, *hd).transpose(0, 3, 1, 4, 2)
    km = kmean.reshape(batch, nb, *hd).transpose(0, 2, 1, 3).astype(BF16)
    km = jnp.concatenate([km, jnp.zeros((batch, N_HEADS, nb, 128 - HEAD_DIM), BF16)], axis=3)

    hgn = MOBA_HEADS
    grid_spec = pltpu.PrefetchScalarGridSpec(
        num_scalar_prefetch=1,
        grid=(batch, N_HEADS // hgn, nb),
        in_specs=[pl.BlockSpec((1, hgn, 128, blk), lambda b, g, qi, s: (b, g, 0, qi)),
                  pl.BlockSpec((1, hgn, seq, 128), lambda b, g, qi, s: (b, g, 0, 0)),
                  pl.BlockSpec((1, hgn, nb, HEAD_DIM, blk), lambda b, g, qi, s: (b, g, 0, 0, 0)),
                  pl.BlockSpec((1, hgn, nb, 128), lambda b, g, qi, s: (b, g, 0, 0))],
        out_specs=pl.BlockSpec((1, hgn, HEAD_DIM, blk), lambda b, g, qi, s: (b, g, 0, qi)),
        scratch_shapes=[pltpu.VMEM((hgn, nb, blk), F32),
                        pltpu.VMEM((hgn, 1, blk), F32),
                        pltpu.VMEM((hgn, 1, blk), F32),
                        pltpu.VMEM((hgn, HEAD_DIM, blk), F32)],
    )
    ot = pl.pallas_call(
        functools.partial(_moba_kernel, nb=nb),
        grid_spec=grid_spec,
        out_shape=jax.ShapeDtypeStruct((batch, N_HEADS, HEAD_DIM, seq), BF16),
        compiler_params=_params("parallel", "parallel", "arbitrary"),
        name="moba_prompt",
    )(slopes, qt, ka, vt, km)
    return ot.transpose(0, 3, 1, 2).reshape(batch * seq, WIDTH)


def _b(t):
    return t.astype(BF16)


def _rwkv_pointwise(p, pprev, mu, w0, decay_up, a0, iclr_up, gate_up, k_k, k_a):
    xs = p + mu * (pprev - p)
    r = xs[:, _O_R:_O_R + WIDTH]
    k = xs[:, _O_K:_O_K + WIDTH]
    v = xs[:, _O_V:_O_V + WIDTH]
    xw = xs[:, _O_XW:_O_XW + DECAY_LORA]
    xa = xs[:, _O_XA:_O_XA + AAA_LORA]
    xg = xs[:, _O_XG:_O_XG + GATE_LORA]
    w = w0 + _dot(_b(jnp.tanh(xw)), _b(decay_up))
    w = -jax.nn.softplus(-w) - 0.5
    logdecay = -jnp.exp(w)
    a = jax.nn.sigmoid(a0 + _dot(_b(xa), _b(iclr_up)))
    g = _dot(_b(jax.nn.sigmoid(xg)), _b(gate_up))
    kk = k * k_k
    k2 = k * (1.0 + (a - 1.0) * k_a)
    return r, k2, v, kk, a, g, logdecay


def _head_norm(kk_h):
    return kk_h * lax.rsqrt(jnp.maximum(jnp.sum(kk_h * kk_h, axis=-1, keepdims=True), 1e-24))


def _group_norm_out(y_h, r_h, k_h, v_h, g_h, rk_h, lnw_h, lnb_h):
    mean = jnp.mean(y_h, axis=-1, keepdims=True)
    var = jnp.mean(jnp.square(y_h - mean), axis=-1, keepdims=True)
    yn = (y_h - mean) * lax.rsqrt(var + GN_EPS) * lnw_h + lnb_h
    yn = yn + jnp.sum(r_h * k_h * rk_h, axis=-1, keepdims=True) * v_h
    return yn * g_h


def _unit_lower_inverse(mats):
    n = mats[0].shape[0]
    row = lax.broadcasted_iota(jnp.int32, (n, n), 0)
    col = lax.broadcasted_iota(jnp.int32, (n, n), 1)
    eye = (row == col).astype(F32)
    size = 16
    same = (row // size) == (col // size)
    pws = [jnp.where(same, a, 0.0) for a in mats]
    xs = [eye - pw for pw in pws]
    for _ in range(3):
        pwb = [_b(pw) for pw in pws]
        pws = [_dot(t, t) for t in pwb]
        xs = [x + _dot(_b(x), _b(pw)) for x, pw in zip(xs, pws)]
    while size < n:
        size2 = size * 2
        same2 = (row // size2) == (col // size2)
        keep = same2 & jnp.logical_not(same)
        xb = [_b(x) for x in xs]
        ox = [_b(_dot(_b(jnp.where(keep, a, 0.0)), t)) for a, t in zip(mats, xb)]
        xs = [x - _dot(t, o) for x, t, o in zip(xs, xb, ox)]
        same = same2
        size = size2
    return xs


def _cumsum_rows(x):
    n = x.shape[0]
    tri = (lax.broadcasted_iota(jnp.int32, (n, n), 0) >= lax.broadcasted_iota(jnp.int32, (n, n), 1)).astype(BF16)
    x1 = _b(x)
    r1 = x - x1.astype(F32)
    x2 = _b(r1)
    x3 = _b(r1 - x2.astype(F32))
    return _dot(tri, x1) + _dot(tri, x2) + _dot(tri, x3)


def _rwkv_chunk_kernel(p_ref, mu_ref, w0_ref, dup_ref, a0_ref, iup_ref, gup_ref, kk_ref, ka_ref,
                       rk_ref, lnw_ref, lnb_ref, o_ref, s_out_ref, s_sc, last_sc):
    c = pl.program_id(1)
    L = RWKV_CHUNK

    @pl.when(c == 0)
    def _():
        s_sc[...] = jnp.zeros_like(s_sc)
        last_sc[...] = jnp.zeros_like(last_sc)

    p = p_ref[...]
    rowi = lax.broadcasted_iota(jnp.int32, p.shape, 0)
    pprev = jnp.where(rowi == 0, last_sc[0:1, :], pltpu.roll(p, 1, 0))
    last_sc[0:1, :] = p[L - 1:L, :]

    r, k2, v, kk, a, g, logdecay = _rwkv_pointwise(
        p, pprev, mu_ref[...], w0_ref[...], dup_ref[...], a0_ref[...], iup_ref[...], gup_ref[...],
        kk_ref[...], ka_ref[...])

    trow = lax.broadcasted_iota(jnp.int32, (L, L), 0)
    tcol = lax.broadcasted_iota(jnp.int32, (L, L), 1)
    lower_incl = trow >= tcol
    lower_strict = trow > tcol
    cum = _cumsum_rows(logdecay)
    cum_end = cum[L - 1:L, :]
    w_inc = jnp.exp(cum)
    w_exc = jnp.exp(cum - logdecay)
    w_inv = jnp.exp(-cum)
    w_tail = jnp.exp(cum_end - cum)
    w_end = jnp.exp(cum_end)
    r_hat_all = r * w_inc
    k_hat_all = k2 * w_inv
    k_bar_all = k2 * w_tail

    heads = range(N_HEADS)
    lns = [slice(h * HEAD_DIM, (h + 1) * HEAD_DIM) for h in heads]
    kap = [_head_norm(kk[:, ln]) for ln in lns]
    kap_hat = [kap[h] * w_exc[:, lns[h]] for h in heads]
    bb = [kap[h] * a[:, lns[h]] for h in heads]
    b_hat = [_b(bb[h] * w_inv[:, lns[h]]) for h in heads]
    b_bar = [_b(bb[h] * w_tail[:, lns[h]]) for h in heads]
    r_hat = [r_hat_all[:, ln] for ln in lns]
    v_b = [_b(v[:, ln]) for ln in lns]
    lhs = [_b(jnp.concatenate([kap_hat[h], r_hat[h]], axis=0)) for h in heads]
    ak = [_dot_nt(lhs[h], _b(k_hat_all[:, lns[h]])) for h in heads]
    ab = [_dot_nt(lhs[h], b_hat[h]) for h in heads]
    a_kr = [_b(jnp.concatenate([jnp.where(lower_strict, t[:L], 0.0), jnp.where(lower_incl, t[L:], 0.0)], axis=0))
            for t in ak]
    a_rb = [_b(jnp.where(lower_incl, t[L:], 0.0)) for t in ab]
    t_inv = [_b(t) for t in _unit_lower_inverse([jnp.where(lower_strict, t[:L], 0.0) for t in ab])]
    av = [_dot(a_kr[h], v_b[h]) for h in heads]
    pm = [_b(_dot(t_inv[h], _b(kap_hat[h]))) for h in heads]
    qm = [_b(_dot(t_inv[h], _b(av[h][:L]))) for h in heads]
    r_eff = [_b(r_hat[h] - _dot(a_rb[h], pm[h])) for h in heads]
    y0 = [av[h][L:] - _dot(a_rb[h], qm[h]) for h in heads]
    ptb = [_b(_dot_tn(pm[h], b_bar[h])) for h in heads]
    cm = [_dot_tn(v_b[h], _b(k_bar_all[:, lns[h]])) - _dot_tn(qm[h], b_bar[h]) for h in heads]
    s0 = [s_sc[h] for h in heads]
    s0b = [_b(t) for t in s0]
    y = [_dot_nt(r_eff[h], s0b[h]) + y0[h] for h in heads]
    s1 = [s0[h] * w_end[:, lns[h]] - _dot(s0b[h], ptb[h]) + cm[h] for h in heads]
    for h in heads:
        ln = lns[h]
        s_sc[h] = s1[h]
        o_ref[:, ln] = _group_norm_out(y[h], r[:, ln], k2[:, ln], v[:, ln], g[:, ln], rk_ref[:, ln],
                                       lnw_ref[:, ln], lnb_ref[:, ln]).astype(o_ref.dtype)

    @pl.when(c == pl.num_programs(1) - 1)
    def _():
        s_out_ref[0] = s_sc[...]


def _rwkv_prompt(p_rw, weights, batch, seq):
    L = RWKV_CHUNK
    nc = seq // L
    const = lambda b, c: (0, 0)
    w_specs = [pl.BlockSpec(w.shape, const) for w in weights]
    return pl.pallas_call(
        _rwkv_chunk_kernel,
        grid=(batch, nc),
        in_specs=[pl.BlockSpec((L, RWKV_COLS), lambda b, c: (b * nc + c, 0))] + w_specs,
        out_specs=[pl.BlockSpec((L, WIDTH), lambda b, c: (b * nc + c, 0)),
                   pl.BlockSpec((1, N_HEADS, HEAD_DIM, HEAD_DIM), lambda b, c: (b, 0, 0, 0))],
        out_shape=[jax.ShapeDtypeStruct((batch * seq, WIDTH), BF16),
                   jax.ShapeDtypeStruct((batch, N_HEADS, HEAD_DIM, HEAD_DIM), F32)],
        scratch_shapes=[pltpu.VMEM((N_HEADS, HEAD_DIM, HEAD_DIM), F32),
                        pltpu.VMEM((8, RWKV_COLS), F32)],
        compiler_params=_params("parallel", "arbitrary"),
        name="rwkv_prompt",
    )(p_rw, *weights)


def _rwkv_step_kernel(p_ref, sh_ref, s_ref, mu_ref, w0_ref, dup_ref, a0_ref, iup_ref, gup_ref, kk_ref,
                      ka_ref, rk_ref, lnw_ref, lnb_ref, o_ref, s_out_ref):
    p = jnp.broadcast_to(p_ref[0], (8, RWKV_COLS))
    pprev = jnp.broadcast_to(sh_ref[0], (8, RWKV_COLS))
    r, k2, v, kk, a, g, logdecay = (t[0:1] for t in _rwkv_pointwise(
        p, pprev, mu_ref[...], w0_ref[...], dup_ref[...], a0_ref[...], iup_ref[...], gup_ref[...],
        kk_ref[...], ka_ref[...]))
    decay = jnp.exp(logdecay)
    n = HEAD_DIM
    eye = lax.broadcasted_iota(jnp.int32, (n, n), 0) == lax.broadcasted_iota(jnp.int32, (n, n), 1)

    def to_col(row_vec):
        return jnp.sum(jnp.where(eye, row_vec, 0.0), axis=-1, keepdims=True)

    def to_row(col_vec):
        return jnp.sum(jnp.where(eye, col_vec, 0.0), axis=0, keepdims=True)

    for h in range(N_HEADS):
        ln = slice(h * n, (h + 1) * n)
        s0 = s_ref[0, h]
        kap = _head_norm(kk[:, ln])
        sa = jnp.sum(s0 * (-kap), axis=-1, keepdims=True)
        s_new = s0 * decay[:, ln] + sa * (kap * a[:, ln]) + to_col(v[:, ln]) * k2[:, ln]
        s_out_ref[0, h] = s_new
        y = to_row(jnp.sum(s_new * r[:, ln], axis=-1, keepdims=True))
        o_ref[0, :, ln] = _group_norm_out(y, r[:, ln], k2[:, ln], v[:, ln], g[:, ln], rk_ref[:, ln],
                                          lnw_ref[:, ln], lnb_ref[:, ln]).astype(o_ref.dtype)


def _rwkv_sample(p_rw, shift, state, weights):
    n = p_rw.shape[0]
    const = lambda s: (0, 0)
    vec = pl.BlockSpec((1, 1, RWKV_COLS), lambda s: (s, 0, 0))
    st = pl.BlockSpec((1, N_HEADS, HEAD_DIM, HEAD_DIM), lambda s: (s, 0, 0, 0))
    rw, s_new = pl.pallas_call(
        _rwkv_step_kernel,
        grid=(n,),
        in_specs=[vec, vec, st] + [pl.BlockSpec(w.shape, const) for w in weights],
        out_specs=[pl.BlockSpec((1, 1, WIDTH), lambda s: (s, 0, 0)), st],
        out_shape=[jax.ShapeDtypeStruct((n, 1, WIDTH), BF16),
                   jax.ShapeDtypeStruct(state.shape, F32)],
        compiler_params=_params("parallel"),
        name="rwkv_sample",
    )(p_rw.reshape(n, 1, RWKV_COLS), shift.reshape(n, 1, RWKV_COLS), state, *weights)
    return rw.reshape(n, WIDTH), s_new


_PAGES_PER_BLOCK = MOBA_BLOCK // PAGE_SIZE
_SELECT_PAGES = 8


def _split3(x):
    x1 = _b(x)
    r1 = x - x1.astype(F32)
    x2 = _b(r1)
    return x1, x2, _b(r1 - x2.astype(F32))


def _sample_select_kernel(pt_ref, q_ref, *refs):
    page_refs = refs[:_SELECT_PAGES]
    idx_ref = refs[_SELECT_PAGES]
    km_sc = refs[_SELECT_PAGES + 1]
    g = pl.program_id(1)
    ppb = _PAGES_PER_BLOCK
    bps = _SELECT_PAGES // ppb
    ones = jnp.ones((8, PAGE_SIZE), BF16)
    for b in range(bps):
        tot = page_refs[b * ppb][0]
        for i in range(1, ppb):
            tot = tot + page_refs[b * ppb + i][0]
        pieces = _split3(tot.reshape(WIDTH, PAGE_SIZE))
        ksum = _dot_nt(ones, pieces[0]) + _dot_nt(ones, pieces[1]) + _dot_nt(ones, pieces[2])
        km_sc[pl.ds(g * bps + b, 1), :] = ksum[0:1] * (1.0 / MOBA_BLOCK)

    @pl.when(g == pl.num_programs(1) - 1)
    def _():
        rnd = lambda t: t.astype(BF16).astype(F32)
        prod = rnd(km_sc[...]) * rnd(q_ref[0])
        lane = lax.broadcasted_iota(jnp.int32, (WIDTH, 128), 0)
        hcol = lax.broadcasted_iota(jnp.int32, (WIDTH, 128), 1)
        head_sum = ((lane // HEAD_DIM) == hcol).astype(BF16)
        pieces = _split3(prod)
        gate = _dot(pieces[0], head_sum) + _dot(pieces[1], head_sum) + _dot(pieces[2], head_sum)
        rank = _block_rank(gate, 0)
        bidx = lax.broadcasted_iota(jnp.int32, gate.shape, 0)
        rows = [jnp.sum(jnp.where(rank == i, bidx, 0), axis=0, keepdims=True) for i in range(MOBA_TOP_K)]
        rows += [jnp.zeros((1, 128), jnp.int32)] * (8 - MOBA_TOP_K)
        idx_ref[0] = jnp.concatenate(rows, axis=0)


def _sample_select(page_table, q, cache_kt):
    n, n_pages = page_table.shape
    nb = n_pages // _PAGES_PER_BLOCK
    steps = n_pages // _SELECT_PAGES

    def page_spec(i):
        return pl.BlockSpec((1, N_HEADS, HEAD_DIM, PAGE_SIZE),
                            lambda s, g, pt, i=i: (pt[s * n_pages + g * _SELECT_PAGES + i], 0, 0, 0))

    grid_spec = pltpu.PrefetchScalarGridSpec(
        num_scalar_prefetch=1,
        grid=(n, steps),
        in_specs=[pl.BlockSpec((1, 1, WIDTH), lambda s, g, pt: (s, 0, 0))]
        + [page_spec(i) for i in range(_SELECT_PAGES)],
        out_specs=pl.BlockSpec((1, 8, 128), lambda s, g, pt: (s, 0, 0)),
        scratch_shapes=[pltpu.VMEM((nb, WIDTH), F32)],
    )
    return pl.pallas_call(
        _sample_select_kernel,
        grid_spec=grid_spec,
        out_shape=jax.ShapeDtypeStruct((n, 8, 128), jnp.int32),
        compiler_params=_params("parallel", "arbitrary"),
        name="sample_select",
    )(page_table.reshape(-1), q.reshape(n, 1, WIDTH), *([cache_kt] * _SELECT_PAGES))


def _sample_attn_kernel(pt_ref, top_ref, slopes_ref, q_ref, kn_ref, vn_ref, *refs, past_len):
    ppb = _PAGES_PER_BLOCK
    npg = MOBA_TOP_K * ppb
    k_refs = refs[:npg]
    v_refs = refs[npg:2 * npg]
    o_ref = refs[2 * npg]
    s_idx = pl.program_id(0)
    h = pl.program_id(1)
    slope = slopes_ref[h]
    q = q_ref[0, 0] * (HEAD_DIM ** -0.5)
    lane = lax.broadcasted_iota(jnp.int32, (1, PAGE_SIZE), 1)

    scores = []
    for i in range(npg):
        blk_id = top_ref[(s_idx * N_HEADS + h) * MOBA_TOP_K + i // ppb]
        dist = (past_len - blk_id * MOBA_BLOCK - (i % ppb) * PAGE_SIZE - lane).astype(F32)
        s = jnp.sum(k_refs[i][0, 0] * q, axis=0, keepdims=True)
        scores.append(s - slope * dist)
    s_self = jnp.sum(q * kn_ref[0, 0], axis=0, keepdims=True)
    m = s_self
    for s in scores:
        m = jnp.maximum(m, jnp.max(s, axis=1, keepdims=True))
    p_self = jnp.exp(s_self - m)
    den = p_self
    accv = jnp.zeros((HEAD_DIM, PAGE_SIZE), F32)
    for i, s in enumerate(scores):
        p = jnp.exp(s - m)
        den = den + jnp.sum(p, axis=1, keepdims=True)
        accv = accv + v_refs[i][0, 0] * p
    acc = jnp.sum(accv, axis=1, keepdims=True) + p_self * vn_ref[0, 0]
    o_ref[0, 0] = acc / den


def _sample_attn(page_table, top_idx, slopes, q, k_new, v_new, cache_kt, cache_vt):
    n, n_pages = page_table.shape
    ppb = _PAGES_PER_BLOCK
    npg = MOBA_TOP_K * ppb

    def slab_spec(i):
        def imap(s, h, pt, top, sl, i=i):
            blk_id = top[(s * N_HEADS + h) * MOBA_TOP_K + i // ppb]
            return (pt[s * n_pages + blk_id * ppb + i % ppb], h, 0, 0)
        return pl.BlockSpec((1, 1, HEAD_DIM, PAGE_SIZE), imap)

    vec = pl.BlockSpec((1, 1, HEAD_DIM, 1), lambda s, h, pt, top, sl: (s, h, 0, 0))
    grid_spec = pltpu.PrefetchScalarGridSpec(
        num_scalar_prefetch=3,
        grid=(n, N_HEADS),
        in_specs=[vec, vec, vec] + [slab_spec(i) for i in range(npg)] * 2,
        out_specs=vec,
    )
    return pl.pallas_call(
        functools.partial(_sample_attn_kernel, past_len=n_pages * PAGE_SIZE),
        grid_spec=grid_spec,
        out_shape=jax.ShapeDtypeStruct((n, N_HEADS, HEAD_DIM, 1), F32),
        compiler_params=_params("parallel", "parallel"),
        name="sample_attn",
    )(page_table.reshape(-1), top_idx, slopes, q, k_new, v_new,
      *([cache_kt] * npg), *([cache_vt] * npg))


def kernel(x_prompt, x_sample, cache_k, cache_v, page_table, state_wkv, state_shift,
           norm_mix_g, w_in, mu_shift, decay_w0, decay_up, iclr_a0, iclr_up, gate_up,
           k_k, k_a, r_k, ln_x_w, ln_x_b, w_out, norm_ffn_g, w_ffn_up, w_ffn_down, norm_final_g):
    depth = w_in.shape[0]
    assert depth == 1
    batch, seq, _ = x_prompt.shape
    n_seq, n_pages = page_table.shape
    slopes = jnp.exp2(-8.0 * jnp.arange(1, N_HEADS + 1, dtype=F32) / N_HEADS)

    l = 0
    row = lambda t: t.reshape(1, -1)
    w_in_l = jnp.concatenate([w_in[l][:, :ATTN_COLS], _to_internal(w_in[l][:, ATTN_COLS:])], axis=1)
    g_mix = row(norm_mix_g[l])
    rw_weights = (row(_to_internal(mu_shift[l])), row(decay_w0[l]), decay_up[l], row(iclr_a0[l]), iclr_up[l],
                  gate_up[l], row(k_k[l]), row(k_a[l]), row(r_k[l]), row(ln_x_w[l]), row(ln_x_b[l]))
    wo = w_out[l].astype(BF16)
    ffn_weights = (wo[:WIDTH], wo[WIDTH:], row(norm_ffn_g[l]), w_ffn_up[l].astype(BF16),
                   w_ffn_down[l].astype(BF16), row(norm_final_g))

    xp = x_prompt.reshape(batch * seq, D_MODEL)
    w_in_b = w_in_l.astype(BF16)
    kf, vf, qb, kb, vb, prw, kmean = _inproj_prompt(xp, g_mix, w_in_b)
    nb = seq // MOBA_BLOCK
    attn_p = _moba_prompt(slopes, qb, kb, vb, kmean.reshape(batch, nb, WIDTH), batch, seq)
    rw_p, wkv_p = _rwkv_prompt(prw, rw_weights, batch, seq)
    y_prompt = _out_ffn(xp, attn_p, rw_p, *ffn_weights, tm=512).reshape(batch, seq, D_MODEL)
    shift_p = _from_internal(prw.reshape(batch, seq, RWKV_COLS)[:, -1])

    hd = (N_HEADS, HEAD_DIM)
    xs = x_sample.reshape(n_seq, D_MODEL)
    proj_s = _inproj_small(xs, g_mix, w_in_b)
    q_s, k_s, v_s = proj_s[:, :WIDTH], proj_s[:, WIDTH:2 * WIDTH], proj_s[:, 2 * WIDTH:ATTN_COLS]
    prw_s = proj_s[:, ATTN_COLS:]
    ckt = cache_k.transpose(0, 1, 3, 4, 2).reshape(-1, N_HEADS, HEAD_DIM, PAGE_SIZE)
    cvt = cache_v.transpose(0, 1, 3, 4, 2).reshape(-1, N_HEADS, HEAD_DIM, PAGE_SIZE)
    top = _sample_select(page_table, q_s, ckt)
    top_idx = top[:, :MOBA_TOP_K, :N_HEADS].transpose(0, 2, 1).reshape(-1)
    col = lambda t: t.reshape(n_seq, N_HEADS, HEAD_DIM, 1)
    attn_s = _sample_attn(page_table, top_idx, slopes, col(q_s), col(k_s), col(v_s), ckt, cvt).reshape(n_seq, WIDTH)
    rw_s, wkv_s = _rwkv_sample(prw_s, _to_internal(state_shift[l]), state_wkv[l], rw_weights)
    y_sample = _out_ffn(xs, attn_s.astype(BF16), rw_s, *ffn_weights, tm=n_seq).reshape(n_seq, 1, D_MODEL)
    shift_s = _from_internal(prw_s)

    return (y_prompt, y_sample,
            kf.reshape(1, batch, seq, *hd), vf.reshape(1, batch, seq, *hd),
            wkv_p[None], shift_p[None],
            k_s.reshape(1, n_seq, 1, *hd), v_s.reshape(1, n_seq, 1, *hd),
            wkv_s[None], shift_s[None])
```

```python
import functools

import jax
import jax.numpy as jnp
from jax import lax
from jax.experimental import pallas as pl
from jax.experimental.pallas import tpu as pltpu

F32 = jnp.float32
BF16 = jnp.bfloat16

D_MODEL = 1024
HEAD_DIM = 64
N_HEADS = 8
WIDTH = N_HEADS * HEAD_DIM
MOBA_BLOCK = 256
MOBA_TOP_K = 3
DECAY_LORA = 64
AAA_LORA = 64
GATE_LORA = 128
ATTN_COLS = 3 * WIDTH
RWKV_COLS = 3 * WIDTH + DECAY_LORA + AAA_LORA + GATE_LORA
RMS_EPS = 1e-6
GN_EPS = 64e-5
NEG_INF = -1e30
PAGE_SIZE = 128
RWKV_CHUNK = 64
VMEM_LIMIT = 48 * 1024 * 1024

_O_R, _O_K, _O_V = 0, WIDTH, 2 * WIDTH
_O_XW = 3 * WIDTH
_O_XA = _O_XW + DECAY_LORA
_O_XG = _O_XA + AAA_LORA


def _to_internal(t):
    r, xw, kv, rest = (t[..., :WIDTH], t[..., WIDTH:WIDTH + DECAY_LORA],
                       t[..., WIDTH + DECAY_LORA:3 * WIDTH + DECAY_LORA], t[..., 3 * WIDTH + DECAY_LORA:])
    return jnp.concatenate([r, kv, xw, rest], axis=-1)


def _from_internal(t):
    r, kv, xw, rest = (t[..., :WIDTH], t[..., WIDTH:3 * WIDTH],
                       t[..., 3 * WIDTH:3 * WIDTH + DECAY_LORA], t[..., 3 * WIDTH + DECAY_LORA:])
    return jnp.concatenate([r, xw, kv, rest], axis=-1)


def _params(*sem):
    return pltpu.CompilerParams(dimension_semantics=sem, vmem_limit_bytes=VMEM_LIMIT)


def _rms(x, g):
    return x * lax.rsqrt(jnp.mean(x * x, axis=-1, keepdims=True) + RMS_EPS) * g


def _dot(a, b, **kw):
    return jnp.dot(a, b, preferred_element_type=F32, **kw)


def _dot_nt(a, b, **kw):
    return lax.dot_general(a, b, (((1,), (1,)), ((), ())), preferred_element_type=F32, **kw)


def _dot_tn(a, b, **kw):
    return lax.dot_general(a, b, (((0,), (0,)), ((), ())), preferred_element_type=F32, **kw)


def _inproj_kernel(x_ref, g_ref, w_ref, srow_ref, kt_ref, vt_ref, qt_ref, ka_ref, vtb_ref, prw_ref, km_ref):
    tm = x_ref.shape[0]
    hd = (N_HEADS, HEAD_DIM, tm)
    xn = _rms(x_ref[...], g_ref[...]).astype(BF16)
    proj = _dot(xn, w_ref[...])
    q = proj[:, 0:WIDTH] * (HEAD_DIM ** -0.5)
    k = proj[:, WIDTH:2 * WIDTH]
    v = proj[:, 2 * WIDTH:3 * WIDTH]
    prw_ref[...] = proj[:, ATTN_COLS:]
    km_ref[0] = jnp.mean(k, axis=0, keepdims=True)
    k_t = k.T.reshape(hd)
    v_t = v.T.reshape(hd)
    kt_ref[0] = k_t
    vt_ref[0] = v_t
    vtb_ref[0, :, 0] = v_t.astype(BF16)
    qt_ref[0, :, 0:HEAD_DIM, :] = q.T.reshape(hd).astype(BF16)
    qt_ref[0, :, HEAD_DIM:, :] = jnp.broadcast_to(srow_ref[...], hd).astype(BF16)
    lane = lax.broadcasted_iota(jnp.int32, (tm, 128 - HEAD_DIM), 1)
    pos = lax.broadcasted_iota(jnp.int32, (tm, 128 - HEAD_DIM), 0)
    pos_cols = jnp.where(lane == 0, pos, 0).astype(F32).astype(BF16)
    for h in range(N_HEADS):
        ka_ref[0, h, :, 0:HEAD_DIM] = k[:, h * HEAD_DIM:(h + 1) * HEAD_DIM].astype(BF16)
        ka_ref[0, h, :, HEAD_DIM:] = pos_cols


def _inproj_prompt(x, g, w_bf16, slopes, batch, seq):
    tm = MOBA_BLOCK
    m = x.shape[0]
    nb = seq // tm
    ncol = w_bf16.shape[1]
    row = lambda i: (i, 0)
    const = lambda i: (0, 0)
    tok = lambda i: (i // nb, 0, 0, i % nb)
    srow = jnp.zeros((N_HEADS, HEAD_DIM, 1), F32).at[:, 0, 0].set(slopes)
    return pl.pallas_call(
        _inproj_kernel,
        grid=(m // tm,),
        in_specs=[pl.BlockSpec((tm, D_MODEL), row),
                  pl.BlockSpec((1, D_MODEL), const),
                  pl.BlockSpec((D_MODEL, ncol), const),
                  pl.BlockSpec((N_HEADS, HEAD_DIM, 1), lambda i: (0, 0, 0))],
        out_specs=[pl.BlockSpec((1, N_HEADS, HEAD_DIM, tm), tok),
                   pl.BlockSpec((1, N_HEADS, HEAD_DIM, tm), tok),
                   pl.BlockSpec((1, N_HEADS, 128, tm), tok),
                   pl.BlockSpec((1, N_HEADS, tm, 128), lambda i: (i // nb, 0, i % nb, 0)),
                   pl.BlockSpec((1, N_HEADS, 1, HEAD_DIM, tm), lambda i: (i // nb, 0, i % nb, 0, 0)),
                   pl.BlockSpec((tm, RWKV_COLS), row),
                   pl.BlockSpec((1, 1, WIDTH), lambda i: (i, 0, 0))],
        out_shape=[jax.ShapeDtypeStruct((batch, N_HEADS, HEAD_DIM, seq), F32),
                   jax.ShapeDtypeStruct((batch, N_HEADS, HEAD_DIM, seq), F32),
                   jax.ShapeDtypeStruct((batch, N_HEADS, 128, seq), BF16),
                   jax.ShapeDtypeStruct((batch, N_HEADS, seq, 128), BF16),
                   jax.ShapeDtypeStruct((batch, N_HEADS, nb, HEAD_DIM, tm), BF16),
                   jax.ShapeDtypeStruct((m, RWKV_COLS), F32),
                   jax.ShapeDtypeStruct((m // tm, 1, WIDTH), F32)],
        compiler_params=_params("parallel"),
        name="inproj_prompt",
    )(x, g, w_bf16, srow)


def _inproj_small_kernel(x_ref, g_ref, w_ref, o_ref):
    xn = _rms(x_ref[...], g_ref[...]).astype(BF16)
    o_ref[...] = _dot(xn, w_ref[...])


def _inproj_small(x, g, w, tn=256):
    m = x.shape[0]
    ncol = w.shape[1]
    return pl.pallas_call(
        _inproj_small_kernel,
        grid=(ncol // tn,),
        in_specs=[pl.BlockSpec((m, D_MODEL), lambda j: (0, 0)),
                  pl.BlockSpec((1, D_MODEL), lambda j: (0, 0)),
                  pl.BlockSpec((D_MODEL, tn), lambda j: (0, j))],
        out_specs=pl.BlockSpec((m, tn), lambda j: (0, j)),
        out_shape=jax.ShapeDtypeStruct((m, ncol), F32),
        compiler_params=_params("parallel"),
        name="inproj_sample",
    )(x, g, w)


def _ffn_kernel(x_ref, attn_ref, rw_ref, woa_ref, wor_ref, gf_ref, wup_ref, wdn_ref, gfin_ref,
                y_ref, h_sc, hn_sc, acc_sc):
    j = pl.program_id(1)

    @pl.when(j == 0)
    def _():
        h = x_ref[...] + _dot(attn_ref[...], woa_ref[...]) + _dot(rw_ref[...], wor_ref[...])
        h_sc[...] = h
        hn_sc[...] = _rms(h, gf_ref[...]).astype(BF16)
        acc_sc[...] = jnp.zeros_like(acc_sc)

    u = jnp.maximum(_dot(hn_sc[...], wup_ref[...]), 0.0)
    acc_sc[...] += _dot((u * u).astype(BF16), wdn_ref[...])

    @pl.when(j == pl.num_programs(1) - 1)
    def _():
        y_ref[...] = _rms(h_sc[...] + acc_sc[...], gfin_ref[...])


def _out_ffn(x, attn, rw, woa, wor, gf, wup, wdn, gfin, tm, th=1024):
    m = x.shape[0]
    hid = wup.shape[1]
    row = lambda i, j: (i, 0)
    const = lambda i, j: (0, 0)
    return pl.pallas_call(
        _ffn_kernel,
        grid=(m // tm, hid // th),
        in_specs=[pl.BlockSpec((tm, D_MODEL), row),
                  pl.BlockSpec((tm, WIDTH), row),
                  pl.BlockSpec((tm, WIDTH), row),
                  pl.BlockSpec((WIDTH, D_MODEL), const),
                  pl.BlockSpec((WIDTH, D_MODEL), const),
                  pl.BlockSpec((1, D_MODEL), const),
                  pl.BlockSpec((D_MODEL, th), lambda i, j: (0, j)),
                  pl.BlockSpec((th, D_MODEL), lambda i, j: (j, 0)),
                  pl.BlockSpec((1, D_MODEL), const)],
        out_specs=pl.BlockSpec((tm, D_MODEL), row),
        out_shape=jax.ShapeDtypeStruct((m, D_MODEL), F32),
        scratch_shapes=[pltpu.VMEM((tm, D_MODEL), F32),
                        pltpu.VMEM((tm, D_MODEL), BF16),
                        pltpu.VMEM((tm, D_MODEL), F32)],
        compiler_params=_params("parallel", "arbitrary"),
        name="out_ffn",
    )(x, attn, rw, woa, wor, gf, wup, wdn, gfin)


def _block_rank(gm, axis):
    nb = gm.shape[axis]
    idx = lax.broadcasted_iota(jnp.int32, gm.shape, axis)
    beats = []
    for m in range(nb):
        gmm = lax.slice_in_dim(gm, m, m + 1, axis=axis)
        beats.append(((gmm > gm) | ((gmm == gm) & (m < idx))).astype(jnp.int32))
    while len(beats) > 1:
        beats = [a + b for a, b in zip(beats[0::2], beats[1::2])] + ([beats[-1]] if len(beats) % 2 else [])
    return beats[0]


MOBA_HEADS = 8
BIG = 1e30


def _moba_kernel(slopes_ref, qt_ref, ka_ref, vt_ref, km_ref, o_ref, sel_sc, m_sc, l_sc, acc_sc, *, nb):
    hg = pl.program_id(1)
    qi = pl.program_id(2)
    blk = MOBA_BLOCK
    n_top = min(MOBA_TOP_K, nb)
    keyi = lax.broadcasted_iota(jnp.int32, (blk, blk), 0)
    qryi = lax.broadcasted_iota(jnp.int32, (blk, blk), 1)
    causal = keyi <= qryi
    bidx = lax.broadcasted_iota(jnp.int32, (nb, blk), 0)
    q0 = pl.multiple_of(qi * blk, blk)

    hs = range(MOBA_HEADS)
    qts = [qt_ref[0, hh] for hh in hs]
    gates = [_dot(km_ref[0, hh], qts[hh]) for hh in hs]
    valid = bidx < qi
    ranks = [_block_rank(jnp.where(valid, g, NEG_INF), 0) for g in gates]
    for hh in hs:
        sel_sc[hh] = ((ranks[hh] < n_top) & valid).astype(F32)
    s0 = [jnp.where(causal, _dot(ka_ref[0, hh, pl.ds(q0, blk), :], qts[hh]), NEG_INF) for hh in hs]
    m0 = [jnp.max(t, axis=0, keepdims=True) for t in s0]
    p0 = [jnp.exp(s0[hh] - m0[hh]) for hh in hs]
    for hh in hs:
        m_sc[hh] = m0[hh]
        l_sc[hh] = jnp.sum(p0[hh], axis=0, keepdims=True)
    pv0 = [_dot(vt_ref[0, hh, qi], p0[hh].astype(BF16)) for hh in hs]
    for hh in hs:
        acc_sc[hh] = pv0[hh]

    def past_block(j, carry):
        k0 = pl.multiple_of(j * blk, blk)
        s = [_dot(ka_ref[0, hh, pl.ds(k0, blk), :], qt_ref[0, hh]) for hh in hs]
        ps, alphas = [], []
        for hh in hs:
            cj = -slopes_ref[hg * MOBA_HEADS + hh] * ((qi - j) * blk).astype(F32)
            picked = sel_sc[hh, pl.ds(j, 1), :] > 0.0
            m_old = m_sc[hh]
            m_new = jnp.maximum(m_old, jnp.where(picked, jnp.max(s[hh], axis=0, keepdims=True) + cj, NEG_INF))
            alpha = jnp.exp(m_old - m_new)
            p = jnp.exp(s[hh] - jnp.where(picked, m_new - cj, BIG))
            m_sc[hh] = m_new
            l_sc[hh] = alpha * l_sc[hh] + jnp.sum(p, axis=0, keepdims=True)
            ps.append(p.astype(BF16))
            alphas.append(alpha)
        pv = [_dot(vt_ref[0, hh, j], ps[hh]) for hh in hs]
        for hh in hs:
            acc_sc[hh] = alphas[hh] * acc_sc[hh] + pv[hh]
        return carry

    lax.fori_loop(0, qi, past_block, 0)

    o_ref[...] = jnp.concatenate([(acc_sc[hh] / l_sc[hh]).T for hh in hs], axis=1).astype(o_ref.dtype)


def _moba_prompt(slopes, qt, ka, vt, kmean, batch, seq):
    nb = seq // MOBA_BLOCK
    blk = MOBA_BLOCK
    km = kmean.reshape(batch, nb, N_HEADS, HEAD_DIM).transpose(0, 2, 1, 3).astype(BF16)
    km = jnp.concatenate([km, jnp.zeros((batch, N_HEADS, nb, 128 - HEAD_DIM), BF16)], axis=3)

    hgn = MOBA_HEADS
    grid_spec = pltpu.PrefetchScalarGridSpec(
        num_scalar_prefetch=1,
        grid=(batch, N_HEADS // hgn, nb),
        in_specs=[pl.BlockSpec((1, hgn, 128, blk), lambda b, g, qi, s: (b, g, 0, qi)),
                  pl.BlockSpec((1, hgn, seq, 128), lambda b, g, qi, s: (b, g, 0, 0)),
                  pl.BlockSpec((1, hgn, nb, HEAD_DIM, blk), lambda b, g, qi, s: (b, g, 0, 0, 0)),
                  pl.BlockSpec((1, hgn, nb, 128), lambda b, g, qi, s: (b, g, 0, 0))],
        out_specs=pl.BlockSpec((blk, hgn * HEAD_DIM), lambda b, g, qi, s: (b * nb + qi, g)),
        scratch_shapes=[pltpu.VMEM((hgn, nb, blk), F32),
                        pltpu.VMEM((hgn, 1, blk), F32),
                        pltpu.VMEM((hgn, 1, blk), F32),
                        pltpu.VMEM((hgn, HEAD_DIM, blk), F32)],
    )
    return pl.pallas_call(
        functools.partial(_moba_kernel, nb=nb),
        grid_spec=grid_spec,
        out_shape=jax.ShapeDtypeStruct((batch * seq, WIDTH), BF16),
        compiler_params=_params("parallel", "parallel", "arbitrary"),
        name="moba_prompt",
    )(slopes, qt, ka, vt, km)


def _b(t):
    return t.astype(BF16)


def _split3(x):
    x1 = _b(x)
    r1 = x - x1.astype(F32)
    x2 = _b(r1)
    return x1, x2, _b(r1 - x2.astype(F32))


def _rwkv_pointwise(p, pprev, mu, w0, decay_up, a0, iclr_up, gate_up, k_k, k_a):
    xs = p + mu * (pprev - p)
    r = xs[:, _O_R:_O_R + WIDTH]
    k = xs[:, _O_K:_O_K + WIDTH]
    v = xs[:, _O_V:_O_V + WIDTH]
    xw = xs[:, _O_XW:_O_XW + DECAY_LORA]
    xa = xs[:, _O_XA:_O_XA + AAA_LORA]
    xg = xs[:, _O_XG:_O_XG + GATE_LORA]
    w = w0 + _dot(_b(jnp.tanh(xw)), _b(decay_up))
    w = -jax.nn.softplus(-w) - 0.5
    logdecay = -jnp.exp(w)
    a = jax.nn.sigmoid(a0 + _dot(_b(xa), _b(iclr_up)))
    g = _dot(_b(jax.nn.sigmoid(xg)), _b(gate_up))
    kk = k * k_k
    k2 = k * (1.0 + (a - 1.0) * k_a)
    return r, k2, v, kk, a, g, logdecay


def _head_norm(kk_h):
    return kk_h * lax.rsqrt(jnp.maximum(jnp.sum(kk_h * kk_h, axis=-1, keepdims=True), 1e-24))


def _group_norm_out(y_h, r_h, k_h, v_h, g_h, rk_h, lnw_h, lnb_h):
    mean = jnp.mean(y_h, axis=-1, keepdims=True)
    var = jnp.mean(jnp.square(y_h - mean), axis=-1, keepdims=True)
    yn = (y_h - mean) * lax.rsqrt(var + GN_EPS) * lnw_h + lnb_h
    yn = yn + jnp.sum(r_h * k_h * rk_h, axis=-1, keepdims=True) * v_h
    return yn * g_h


def _unit_lower_inverse(mats):
    n = mats[0].shape[0]
    row = lax.broadcasted_iota(jnp.int32, (n, n), 0)
    col = lax.broadcasted_iota(jnp.int32, (n, n), 1)
    eye = (row == col).astype(F32)
    size = 16
    same = (row // size) == (col // size)
    pws = [jnp.where(same, a, 0.0) for a in mats]
    xs = [eye - pw for pw in pws]
    for _ in range(3):
        pwb = [_b(pw) for pw in pws]
        pws = [_dot(t, t) for t in pwb]
        xs = [x + _dot(_b(x), _b(pw)) for x, pw in zip(xs, pws)]
    while size < n:
        size2 = size * 2
        same2 = (row // size2) == (col // size2)
        keep = same2 & jnp.logical_not(same)
        xb = [_b(x) for x in xs]
        ox = [_b(_dot(_b(jnp.where(keep, a, 0.0)), t)) for a, t in zip(mats, xb)]
        xs = [x - _dot(t, o) for x, t, o in zip(xs, xb, ox)]
        same = same2
        size = size2
    return xs


def _cumsum_rows(x, seg):
    n = x.shape[0]
    row = lax.broadcasted_iota(jnp.int32, (n, n), 0)
    col = lax.broadcasted_iota(jnp.int32, (n, n), 1)
    tri = ((row >= col) & ((row // seg) == (col // seg))).astype(BF16)
    x1, x2, x3 = _split3(x)
    return _dot(tri, x1) + _dot(tri, x2) + _dot(tri, x3)


RWKV_STEP_CHUNKS = 2


def _rwkv_chunk_kernel(p_ref, mu_ref, w0_ref, dup_ref, a0_ref, iup_ref, gup_ref, kk_ref, ka_ref,
                       rk_ref, lnw_ref, lnb_ref, o_ref, s_out_ref, s_sc, last_sc):
    step = pl.program_id(1)
    L = RWKV_CHUNK
    nck = RWKV_STEP_CHUNKS
    ts = nck * L

    @pl.when(step == 0)
    def _():
        s_sc[...] = jnp.zeros_like(s_sc)
        last_sc[...] = jnp.zeros_like(last_sc)

    p = p_ref[...]
    rowi = lax.broadcasted_iota(jnp.int32, p.shape, 0)
    pprev = jnp.where(rowi == 0, last_sc[0:1, :], pltpu.roll(p, 1, 0))
    last_sc[0:1, :] = p[ts - 1:ts, :]

    r, k2, v, kk, a, g, logdecay = _rwkv_pointwise(
        p, pprev, mu_ref[...], w0_ref[...], dup_ref[...], a0_ref[...], iup_ref[...], gup_ref[...],
        kk_ref[...], ka_ref[...])

    trow = lax.broadcasted_iota(jnp.int32, (L, L), 0)
    tcol = lax.broadcasted_iota(jnp.int32, (L, L), 1)
    lower_incl = trow >= tcol
    lower_strict = trow > tcol
    cum = _cumsum_rows(logdecay, L)
    ends = [cum[c * L + L - 1:c * L + L, :] for c in range(nck)]
    cum_end = jnp.concatenate([jnp.broadcast_to(e, (L, WIDTH)) for e in ends], axis=0)
    w_inc = jnp.exp(cum)
    w_exc = jnp.exp(cum - logdecay)
    w_inv = jnp.exp(-cum)
    w_tail = jnp.exp(cum_end - cum)
    w_end = [jnp.exp(e) for e in ends]
    r_hat_all = r * w_inc
    k_hat_all = k2 * w_inv
    k_bar_all = k2 * w_tail

    heads = range(N_HEADS)
    items = [(slice(c * L, (c + 1) * L), slice(h * HEAD_DIM, (h + 1) * HEAD_DIM))
             for c in range(nck) for h in heads]
    n = range(len(items))
    kap = [_head_norm(kk[it]) for it in items]
    kap_hat = [kap[i] * w_exc[items[i]] for i in n]
    bb = [kap[i] * a[items[i]] for i in n]
    b_hat = [_b(bb[i] * w_inv[items[i]]) for i in n]
    b_bar = [_b(bb[i] * w_tail[items[i]]) for i in n]
    r_hat = [r_hat_all[it] for it in items]
    v_b = [_b(v[it]) for it in items]
    lhs = [_b(jnp.concatenate([kap_hat[i], r_hat[i]], axis=0)) for i in n]
    ak = [_dot_nt(lhs[i], _b(k_hat_all[items[i]])) for i in n]
    ab = [_dot_nt(lhs[i], b_hat[i]) for i in n]
    a_kr = [_b(jnp.concatenate([jnp.where(lower_strict, t[:L], 0.0), jnp.where(lower_incl, t[L:], 0.0)], axis=0))
            for t in ak]
    a_rb = [_b(jnp.where(lower_incl, t[L:], 0.0)) for t in ab]
    t_inv = [_b(t) for t in _unit_lower_inverse([jnp.where(lower_strict, t[:L], 0.0) for t in ab])]
    av = [_dot(a_kr[i], v_b[i]) for i in n]
    pm = [_b(_dot(t_inv[i], _b(kap_hat[i]))) for i in n]
    qm = [_b(_dot(t_inv[i], _b(av[i][:L]))) for i in n]
    r_eff = [_b(r_hat[i] - _dot(a_rb[i], pm[i])) for i in n]
    y0 = [av[i][L:] - _dot(a_rb[i], qm[i]) for i in n]
    ptb = [_b(_dot_tn(pm[i], b_bar[i])) for i in n]
    cm = [_dot_tn(v_b[i], _b(k_bar_all[items[i]])) - _dot_tn(qm[i], b_bar[i]) for i in n]

    state = [s_sc[h] for h in heads]
    for c in range(nck):
        sb = [_b(t) for t in state]
        ys = [_dot_nt(r_eff[c * N_HEADS + h], sb[h]) + y0[c * N_HEADS + h] for h in heads]
        state = [state[h] * w_end[c][:, items[h][1]] - _dot(sb[h], ptb[c * N_HEADS + h]) + cm[c * N_HEADS + h]
                 for h in heads]
        for h in heads:
            rows, ln = items[c * N_HEADS + h]
            o_ref[rows, ln] = _group_norm_out(ys[h], r[rows, ln], k2[rows, ln], v[rows, ln], g[rows, ln],
                                              rk_ref[:, ln], lnw_ref[:, ln], lnb_ref[:, ln]).astype(o_ref.dtype)
    for h in heads:
        s_sc[h] = state[h]

    @pl.when(step == pl.num_programs(1) - 1)
    def _():
        s_out_ref[0] = s_sc[...]


def _rwkv_prompt(p_rw, weights, batch, seq):
    ts = RWKV_CHUNK * RWKV_STEP_CHUNKS
    ns = seq // ts
    const = lambda b, c: (0, 0)
    w_specs = [pl.BlockSpec(w.shape, const) for w in weights]
    return pl.pallas_call(
        _rwkv_chunk_kernel,
        grid=(batch, ns),
        in_specs=[pl.BlockSpec((ts, RWKV_COLS), lambda b, c: (b * ns + c, 0))] + w_specs,
        out_specs=[pl.BlockSpec((ts, WIDTH), lambda b, c: (b * ns + c, 0)),
                   pl.BlockSpec((1, N_HEADS, HEAD_DIM, HEAD_DIM), lambda b, c: (b, 0, 0, 0))],
        out_shape=[jax.ShapeDtypeStruct((batch * seq, WIDTH), BF16),
                   jax.ShapeDtypeStruct((batch, N_HEADS, HEAD_DIM, HEAD_DIM), F32)],
        scratch_shapes=[pltpu.VMEM((N_HEADS, HEAD_DIM, HEAD_DIM), F32),
                        pltpu.VMEM((8, RWKV_COLS), F32)],
        compiler_params=_params("parallel", "arbitrary"),
        name="rwkv_prompt",
    )(p_rw, *weights)


def _rwkv_step_kernel(p_ref, sh_ref, s_ref, mu_ref, w0_ref, dup_ref, a0_ref, iup_ref, gup_ref, kk_ref,
                      ka_ref, rk_ref, lnw_ref, lnb_ref, o_ref, s_out_ref):
    p = jnp.broadcast_to(p_ref[0], (8, RWKV_COLS))
    pprev = jnp.broadcast_to(sh_ref[0], (8, RWKV_COLS))
    r, k2, v, kk, a, g, logdecay = (t[0:1] for t in _rwkv_pointwise(
        p, pprev, mu_ref[...], w0_ref[...], dup_ref[...], a0_ref[...], iup_ref[...], gup_ref[...],
        kk_ref[...], ka_ref[...]))
    decay = jnp.exp(logdecay)
    n = HEAD_DIM
    eye = lax.broadcasted_iota(jnp.int32, (n, n), 0) == lax.broadcasted_iota(jnp.int32, (n, n), 1)

    def to_col(row_vec):
        return jnp.sum(jnp.where(eye, row_vec, 0.0), axis=-1, keepdims=True)

    def to_row(col_vec):
        return jnp.sum(jnp.where(eye, col_vec, 0.0), axis=0, keepdims=True)

    for h in range(N_HEADS):
        ln = slice(h * n, (h + 1) * n)
        s0 = s_ref[0, h]
        kap = _head_norm(kk[:, ln])
        sa = jnp.sum(s0 * (-kap), axis=-1, keepdims=True)
        s_new = s0 * decay[:, ln] + sa * (kap * a[:, ln]) + to_col(v[:, ln]) * k2[:, ln]
        s_out_ref[0, h] = s_new
        y = to_row(jnp.sum(s_new * r[:, ln], axis=-1, keepdims=True))
        o_ref[0, :, ln] = _group_norm_out(y, r[:, ln], k2[:, ln], v[:, ln], g[:, ln], rk_ref[:, ln],
                                          lnw_ref[:, ln], lnb_ref[:, ln]).astype(o_ref.dtype)


def _rwkv_sample(p_rw, shift, state, weights):
    n = p_rw.shape[0]
    const = lambda s: (0, 0)
    vec = pl.BlockSpec((1, 1, RWKV_COLS), lambda s: (s, 0, 0))
    st = pl.BlockSpec((1, N_HEADS, HEAD_DIM, HEAD_DIM), lambda s: (s, 0, 0, 0))
    rw, s_new = pl.pallas_call(
        _rwkv_step_kernel,
        grid=(n,),
        in_specs=[vec, vec, st] + [pl.BlockSpec(w.shape, const) for w in weights],
        out_specs=[pl.BlockSpec((1, 1, WIDTH), lambda s: (s, 0, 0)), st],
        out_shape=[jax.ShapeDtypeStruct((n, 1, WIDTH), BF16),
                   jax.ShapeDtypeStruct(state.shape, F32)],
        compiler_params=_params("parallel"),
        name="rwkv_sample",
    )(p_rw.reshape(n, 1, RWKV_COLS), shift.reshape(n, 1, RWKV_COLS), state, *weights)
    return rw.reshape(n, WIDTH), s_new


_PAGES_PER_BLOCK = MOBA_BLOCK // PAGE_SIZE
_SELECT_PAGES = 8


def _sample_select_kernel(pt_ref, q_ref, *refs):
    page_refs = refs[:_SELECT_PAGES]
    idx_ref = refs[_SELECT_PAGES]
    km_sc = refs[_SELECT_PAGES + 1]
    g = pl.program_id(1)
    ppb = _PAGES_PER_BLOCK
    bps = _SELECT_PAGES // ppb
    ones = jnp.ones((8, PAGE_SIZE), BF16)
    for b in range(bps):
        tot = page_refs[b * ppb][0]
        for i in range(1, ppb):
            tot = tot + page_refs[b * ppb + i][0]
        pieces = _split3(tot.reshape(WIDTH, PAGE_SIZE))
        ksum = _dot_nt(ones, pieces[0]) + _dot_nt(ones, pieces[1]) + _dot_nt(ones, pieces[2])
        km_sc[pl.ds(g * bps + b, 1), :] = ksum[0:1] * (1.0 / MOBA_BLOCK)

    @pl.when(g == pl.num_programs(1) - 1)
    def _():
        rnd = lambda t: t.astype(BF16).astype(F32)
        prod = rnd(km_sc[...]) * rnd(q_ref[0])
        lane = lax.broadcasted_iota(jnp.int32, (WIDTH, 128), 0)
        hcol = lax.broadcasted_iota(jnp.int32, (WIDTH, 128), 1)
        head_sum = ((lane // HEAD_DIM) == hcol).astype(BF16)
        pieces = _split3(prod)
        gate = _dot(pieces[0], head_sum) + _dot(pieces[1], head_sum) + _dot(pieces[2], head_sum)
        rank = _block_rank(gate, 0)
        bidx = lax.broadcasted_iota(jnp.int32, gate.shape, 0)
        rows = [jnp.sum(jnp.where(rank == i, bidx, 0), axis=0, keepdims=True) for i in range(MOBA_TOP_K)]
        rows += [jnp.zeros((1, 128), jnp.int32)] * (8 - MOBA_TOP_K)
        idx_ref[0] = jnp.concatenate(rows, axis=0)


def _sample_select(page_table, q, cache_kt):
    n, n_pages = page_table.shape
    nb = n_pages // _PAGES_PER_BLOCK
    steps = n_pages // _SELECT_PAGES

    def page_spec(i):
        return pl.BlockSpec((1, N_HEADS, HEAD_DIM, PAGE_SIZE),
                            lambda s, g, pt, i=i: (pt[s * n_pages + g * _SELECT_PAGES + i], 0, 0, 0))

    grid_spec = pltpu.PrefetchScalarGridSpec(
        num_scalar_prefetch=1,
        grid=(n, steps),
        in_specs=[pl.BlockSpec((1, 1, WIDTH), lambda s, g, pt: (s, 0, 0))]
        + [page_spec(i) for i in range(_SELECT_PAGES)],
        out_specs=pl.BlockSpec((1, 8, 128), lambda s, g, pt: (s, 0, 0)),
        scratch_shapes=[pltpu.VMEM((nb, WIDTH), F32)],
    )
    return pl.pallas_call(
        _sample_select_kernel,
        grid_spec=grid_spec,
        out_shape=jax.ShapeDtypeStruct((n, 8, 128), jnp.int32),
        compiler_params=_params("parallel", "arbitrary"),
        name="sample_select",
    )(page_table.reshape(-1), q.reshape(n, 1, WIDTH), *([cache_kt] * _SELECT_PAGES))


def _sample_attn_kernel(pt_ref, top_ref, slopes_ref, q_ref, kn_ref, vn_ref, *refs, past_len):
    ppb = _PAGES_PER_BLOCK
    npg = MOBA_TOP_K * ppb
    k_refs = refs[:npg]
    v_refs = refs[npg:2 * npg]
    o_ref = refs[2 * npg]
    s_idx = pl.program_id(0)
    h = pl.program_id(1)
    slope = slopes_ref[h]
    q = q_ref[0, 0] * (HEAD_DIM ** -0.5)
    lane = lax.broadcasted_iota(jnp.int32, (1, PAGE_SIZE), 1)

    scores = []
    for i in range(npg):
        blk_id = top_ref[(s_idx * N_HEADS + h) * MOBA_TOP_K + i // ppb]
        dist = (past_len - blk_id * MOBA_BLOCK - (i % ppb) * PAGE_SIZE - lane).astype(F32)
        s = jnp.sum(k_refs[i][0, 0] * q, axis=0, keepdims=True)
        scores.append(s - slope * dist)
    s_self = jnp.sum(q * kn_ref[0, 0], axis=0, keepdims=True)
    m = s_self
    for s in scores:
        m = jnp.maximum(m, jnp.max(s, axis=1, keepdims=True))
    p_self = jnp.exp(s_self - m)
    den = p_self
    accv = jnp.zeros((HEAD_DIM, PAGE_SIZE), F32)
    for i, s in enumerate(scores):
        p = jnp.exp(s - m)
        den = den + jnp.sum(p, axis=1, keepdims=True)
        accv = accv + v_refs[i][0, 0] * p
    acc = jnp.sum(accv, axis=1, keepdims=True) + p_self * vn_ref[0, 0]
    o_ref[0, 0] = acc / den


def _sample_attn(page_table, top_idx, slopes, q, k_new, v_new, cache_kt, cache_vt):
    n, n_pages = page_table.shape
    ppb = _PAGES_PER_BLOCK
    npg = MOBA_TOP_K * ppb

    def slab_spec(i):
        def imap(s, h, pt, top, sl, i=i):
            blk_id = top[(s * N_HEADS + h) * MOBA_TOP_K + i // ppb]
            return (pt[s * n_pages + blk_id * ppb + i % ppb], h, 0, 0)
        return pl.BlockSpec((1, 1, HEAD_DIM, PAGE_SIZE), imap)

    vec = pl.BlockSpec((1, 1, HEAD_DIM, 1), lambda s, h, pt, top, sl: (s, h, 0, 0))
    grid_spec = pltpu.PrefetchScalarGridSpec(
        num_scalar_prefetch=3,
        grid=(n, N_HEADS),
        in_specs=[vec, vec, vec] + [slab_spec(i) for i in range(npg)] * 2,
        out_specs=vec,
    )
    return pl.pallas_call(
        functools.partial(_sample_attn_kernel, past_len=n_pages * PAGE_SIZE),
        grid_spec=grid_spec,
        out_shape=jax.ShapeDtypeStruct((n, N_HEADS, HEAD_DIM, 1), F32),
        compiler_params=_params("parallel", "parallel"),
        name="sample_attn",
    )(page_table.reshape(-1), top_idx, slopes, q, k_new, v_new,
      *([cache_kt] * npg), *([cache_vt] * npg))


def kernel(x_prompt, x_sample, cache_k, cache_v, page_table, state_wkv, state_shift,
           norm_mix_g, w_in, mu_shift, decay_w0, decay_up, iclr_a0, iclr_up, gate_up,
           k_k, k_a, r_k, ln_x_w, ln_x_b, w_out, norm_ffn_g, w_ffn_up, w_ffn_down, norm_final_g):
    depth = w_in.shape[0]
    assert depth == 1
    batch, seq, _ = x_prompt.shape
    n_seq, n_pages = page_table.shape
    slopes = jnp.exp2(-8.0 * jnp.arange(1, N_HEADS + 1, dtype=F32) / N_HEADS)

    l = 0
    row = lambda t: t.reshape(1, -1)
    w_in_l = jnp.concatenate([w_in[l][:, :ATTN_COLS], _to_internal(w_in[l][:, ATTN_COLS:])], axis=1)
    g_mix = row(norm_mix_g[l])
    rw_weights = (row(_to_internal(mu_shift[l])), row(decay_w0[l]), decay_up[l], row(iclr_a0[l]), iclr_up[l],
                  gate_up[l], row(k_k[l]), row(k_a[l]), row(r_k[l]), row(ln_x_w[l]), row(ln_x_b[l]))
    wo = w_out[l].astype(BF16)
    ffn_weights = (wo[:WIDTH], wo[WIDTH:], row(norm_ffn_g[l]), w_ffn_up[l].astype(BF16),
                   w_ffn_down[l].astype(BF16), row(norm_final_g))

    xp = x_prompt.reshape(batch * seq, D_MODEL)
    w_in_b = w_in_l.astype(BF16)
    kt, vt, qt, ka, vtb, prw, kmean = _inproj_prompt(xp, g_mix, w_in_b, slopes, batch, seq)
    nb = seq // MOBA_BLOCK
    attn_p = _moba_prompt(slopes, qt, ka, vtb, kmean.reshape(batch, nb, WIDTH), batch, seq)
    rw_p, wkv_p = _rwkv_prompt(prw, rw_weights, batch, seq)
    y_prompt = _out_ffn(xp, attn_p, rw_p, *ffn_weights, tm=512).reshape(batch, seq, D_MODEL)
    shift_p = _from_internal(prw.reshape(batch, seq, RWKV_COLS)[:, -1])

    hd = (N_HEADS, HEAD_DIM)
    xs = x_sample.reshape(n_seq, D_MODEL)
    proj_s = _inproj_small(xs, g_mix, w_in_b)
    q_s, k_s, v_s = proj_s[:, :WIDTH], proj_s[:, WIDTH:2 * WIDTH], proj_s[:, 2 * WIDTH:ATTN_COLS]
    prw_s = proj_s[:, ATTN_COLS:]
    ckt = cache_k.transpose(0, 1, 3, 4, 2).reshape(-1, N_HEADS, HEAD_DIM, PAGE_SIZE)
    cvt = cache_v.transpose(0, 1, 3, 4, 2).reshape(-1, N_HEADS, HEAD_DIM, PAGE_SIZE)
    top = _sample_select(page_table, q_s, ckt)
    top_idx = top[:, :MOBA_TOP_K, :N_HEADS].transpose(0, 2, 1).reshape(-1)
    col = lambda t: t.reshape(n_seq, N_HEADS, HEAD_DIM, 1)
    attn_s = _sample_attn(page_table, top_idx, slopes, col(q_s), col(k_s), col(v_s), ckt, cvt).reshape(n_seq, WIDTH)
    rw_s, wkv_s = _rwkv_sample(prw_s, _to_internal(state_shift[l]), state_wkv[l], rw_weights)
    y_sample = _out_ffn(xs, attn_s.astype(BF16), rw_s, *ffn_weights, tm=n_seq).reshape(n_seq, 1, D_MODEL)
    shift_s = _from_internal(prw_s)

    return (y_prompt, y_sample,
            kt.transpose(0, 3, 1, 2)[None], vt.transpose(0, 3, 1, 2)[None],
            wkv_p[None], shift_p[None],
            k_s.reshape(1, n_seq, 1, *hd), v_s.reshape(1, n_seq, 1, *hd),
            wkv_s[None], shift_s[None])
```

```python
import functools

import jax
import jax.numpy as jnp
from jax import lax
from jax.experimental import pallas as pl
from jax.experimental.pallas import tpu as pltpu

F32 = jnp.float32
BF16 = jnp.bfloat16

D_MODEL = 1024
HEAD_DIM = 64
N_HEADS = 8
WIDTH = N_HEADS * HEAD_DIM
MOBA_BLOCK = 256
MOBA_TOP_K = 3
DECAY_LORA = 64
AAA_LORA = 64
GATE_LORA = 128
ATTN_COLS = 3 * WIDTH
RWKV_COLS = 3 * WIDTH + DECAY_LORA + AAA_LORA + GATE_LORA
RMS_EPS = 1e-6
GN_EPS = 64e-5
NEG_INF = -1e30
PAGE_SIZE = 128
RWKV_CHUNK = 64
VMEM_LIMIT = 48 * 1024 * 1024

_O_R, _O_K, _O_V = 0, WIDTH, 2 * WIDTH
_O_XW = 3 * WIDTH
_O_XA = _O_XW + DECAY_LORA
_O_XG = _O_XA + AAA_LORA


def _to_internal(t):
    r, xw, kv, rest = (t[..., :WIDTH], t[..., WIDTH:WIDTH + DECAY_LORA],
                       t[..., WIDTH + DECAY_LORA:3 * WIDTH + DECAY_LORA], t[..., 3 * WIDTH + DECAY_LORA:])
    return jnp.concatenate([r, kv, xw, rest], axis=-1)


def _from_internal(t):
    r, kv, xw, rest = (t[..., :WIDTH], t[..., WIDTH:3 * WIDTH],
                       t[..., 3 * WIDTH:3 * WIDTH + DECAY_LORA], t[..., 3 * WIDTH + DECAY_LORA:])
    return jnp.concatenate([r, xw, kv, rest], axis=-1)


def _params(*sem):
    return pltpu.CompilerParams(dimension_semantics=sem, vmem_limit_bytes=VMEM_LIMIT)


def _rms(x, g):
    return x * lax.rsqrt(jnp.mean(x * x, axis=-1, keepdims=True) + RMS_EPS) * g


def _dot(a, b, **kw):
    return jnp.dot(a, b, preferred_element_type=F32, **kw)


def _dot_nt(a, b, **kw):
    return lax.dot_general(a, b, (((1,), (1,)), ((), ())), preferred_element_type=F32, **kw)


def _dot_tn(a, b, **kw):
    return lax.dot_general(a, b, (((0,), (0,)), ((), ())), preferred_element_type=F32, **kw)


def _inproj_kernel(x_ref, g_ref, w_ref, srow_ref, kt_ref, vt_ref, qt_ref, ka_ref, vtb_ref, prw_ref, km_ref):
    tm = x_ref.shape[0]
    hd = (N_HEADS, HEAD_DIM, tm)
    xn = _rms(x_ref[...], g_ref[...]).astype(BF16)
    proj = _dot(xn, w_ref[...])
    q = proj[:, 0:WIDTH] * (HEAD_DIM ** -0.5)
    k = proj[:, WIDTH:2 * WIDTH]
    v = proj[:, 2 * WIDTH:3 * WIDTH]
    prw_ref[...] = proj[:, ATTN_COLS:]
    km_ref[0] = jnp.mean(k, axis=0, keepdims=True)
    k_t = k.T.reshape(hd)
    v_t = v.T.reshape(hd)
    kt_ref[0] = k_t
    vt_ref[0] = v_t
    vtb_ref[0, :, 0] = v_t.astype(BF16)
    qt_ref[0, :, 0:HEAD_DIM, :] = q.T.reshape(hd).astype(BF16)
    qt_ref[0, :, HEAD_DIM:, :] = jnp.broadcast_to(srow_ref[...], hd).astype(BF16)
    lane = lax.broadcasted_iota(jnp.int32, (tm, 128 - HEAD_DIM), 1)
    pos = lax.broadcasted_iota(jnp.int32, (tm, 128 - HEAD_DIM), 0)
    pos_cols = jnp.where(lane == 0, pos, 0).astype(F32).astype(BF16)
    for h in range(N_HEADS):
        ka_ref[0, h, :, 0:HEAD_DIM] = k[:, h * HEAD_DIM:(h + 1) * HEAD_DIM].astype(BF16)
        ka_ref[0, h, :, HEAD_DIM:] = pos_cols


def _inproj_prompt(x, g, w_bf16, slopes, batch, seq):
    tm = MOBA_BLOCK
    m = x.shape[0]
    nb = seq // tm
    ncol = w_bf16.shape[1]
    row = lambda i: (i, 0)
    const = lambda i: (0, 0)
    tok = lambda i: (i // nb, 0, 0, i % nb)
    srow = jnp.zeros((N_HEADS, HEAD_DIM, 1), F32).at[:, 0, 0].set(slopes)
    return pl.pallas_call(
        _inproj_kernel,
        grid=(m // tm,),
        in_specs=[pl.BlockSpec((tm, D_MODEL), row),
                  pl.BlockSpec((1, D_MODEL), const),
                  pl.BlockSpec((D_MODEL, ncol), const),
                  pl.BlockSpec((N_HEADS, HEAD_DIM, 1), lambda i: (0, 0, 0))],
        out_specs=[pl.BlockSpec((1, N_HEADS, HEAD_DIM, tm), tok),
                   pl.BlockSpec((1, N_HEADS, HEAD_DIM, tm), tok),
                   pl.BlockSpec((1, N_HEADS, 128, tm), tok),
                   pl.BlockSpec((1, N_HEADS, tm, 128), lambda i: (i // nb, 0, i % nb, 0)),
                   pl.BlockSpec((1, N_HEADS, 1, HEAD_DIM, tm), lambda i: (i // nb, 0, i % nb, 0, 0)),
                   pl.BlockSpec((tm, RWKV_COLS), row),
                   pl.BlockSpec((1, 1, WIDTH), lambda i: (i, 0, 0))],
        out_shape=[jax.ShapeDtypeStruct((batch, N_HEADS, HEAD_DIM, seq), F32),
                   jax.ShapeDtypeStruct((batch, N_HEADS, HEAD_DIM, seq), F32),
                   jax.ShapeDtypeStruct((batch, N_HEADS, 128, seq), BF16),
                   jax.ShapeDtypeStruct((batch, N_HEADS, seq, 128), BF16),
                   jax.ShapeDtypeStruct((batch, N_HEADS, nb, HEAD_DIM, tm), BF16),
                   jax.ShapeDtypeStruct((m, RWKV_COLS), F32),
                   jax.ShapeDtypeStruct((m // tm, 1, WIDTH), F32)],
        compiler_params=_params("parallel"),
        name="inproj_prompt",
    )(x, g, w_bf16, srow)


def _inproj_small_kernel(x_ref, g_ref, w_ref, o_ref):
    xn = _rms(x_ref[...], g_ref[...]).astype(BF16)
    o_ref[...] = _dot(xn, w_ref[...])


def _inproj_small(x, g, w, tn=256):
    m = x.shape[0]
    ncol = w.shape[1]
    return pl.pallas_call(
        _inproj_small_kernel,
        grid=(ncol // tn,),
        in_specs=[pl.BlockSpec((m, D_MODEL), lambda j: (0, 0)),
                  pl.BlockSpec((1, D_MODEL), lambda j: (0, 0)),
                  pl.BlockSpec((D_MODEL, tn), lambda j: (0, j))],
        out_specs=pl.BlockSpec((m, tn), lambda j: (0, j)),
        out_shape=jax.ShapeDtypeStruct((m, ncol), F32),
        compiler_params=_params("parallel"),
        name="inproj_sample",
    )(x, g, w)


def _ffn_kernel(x_ref, attn_ref, rw_ref, woa_ref, wor_ref, gf_ref, wup_ref, wdn_ref, gfin_ref,
                y_ref, h_sc, hn_sc, acc_sc):
    j = pl.program_id(1)

    @pl.when(j == 0)
    def _():
        h = x_ref[...] + _dot(attn_ref[...], woa_ref[...]) + _dot(rw_ref[...], wor_ref[...])
        h_sc[...] = h
        hn_sc[...] = _rms(h, gf_ref[...]).astype(BF16)
        acc_sc[...] = jnp.zeros_like(acc_sc)

    u = jnp.maximum(_dot(hn_sc[...], wup_ref[...]), 0.0)
    acc_sc[...] += _dot((u * u).astype(BF16), wdn_ref[...])

    @pl.when(j == pl.num_programs(1) - 1)
    def _():
        y_ref[...] = _rms(h_sc[...] + acc_sc[...], gfin_ref[...])


def _out_ffn(x, attn, rw, woa, wor, gf, wup, wdn, gfin, tm, th=1024):
    m = x.shape[0]
    hid = wup.shape[1]
    row = lambda i, j: (i, 0)
    const = lambda i, j: (0, 0)
    return pl.pallas_call(
        _ffn_kernel,
        grid=(m // tm, hid // th),
        in_specs=[pl.BlockSpec((tm, D_MODEL), row),
                  pl.BlockSpec((tm, WIDTH), row),
                  pl.BlockSpec((tm, WIDTH), row),
                  pl.BlockSpec((WIDTH, D_MODEL), const),
                  pl.BlockSpec((WIDTH, D_MODEL), const),
                  pl.BlockSpec((1, D_MODEL), const),
                  pl.BlockSpec((D_MODEL, th), lambda i, j: (0, j)),
                  pl.BlockSpec((th, D_MODEL), lambda i, j: (j, 0)),
                  pl.BlockSpec((1, D_MODEL), const)],
        out_specs=pl.BlockSpec((tm, D_MODEL), row),
        out_shape=jax.ShapeDtypeStruct((m, D_MODEL), F32),
        scratch_shapes=[pltpu.VMEM((tm, D_MODEL), F32),
                        pltpu.VMEM((tm, D_MODEL), BF16),
                        pltpu.VMEM((tm, D_MODEL), F32)],
        compiler_params=_params("parallel", "arbitrary"),
        name="out_ffn",
    )(x, attn, rw, woa, wor, gf, wup, wdn, gfin)


def _block_rank(gm, axis):
    nb = gm.shape[axis]
    idx = lax.broadcasted_iota(jnp.int32, gm.shape, axis)
    beats = []
    for m in range(nb):
        gmm = lax.slice_in_dim(gm, m, m + 1, axis=axis)
        beats.append(((gmm > gm) | ((gmm == gm) & (m < idx))).astype(jnp.int32))
    while len(beats) > 1:
        beats = [a + b for a, b in zip(beats[0::2], beats[1::2])] + ([beats[-1]] if len(beats) % 2 else [])
    return beats[0]


MOBA_HEADS = 8
BIG = 1e30


def _moba_kernel(slopes_ref, qt_ref, ka_ref, vt_ref, km_ref, o_ref, sel_sc, m_sc, l_sc, acc_sc, *, nb):
    hg = pl.program_id(1)
    qi = pl.program_id(2)
    blk = MOBA_BLOCK
    n_top = min(MOBA_TOP_K, nb)
    keyi = lax.broadcasted_iota(jnp.int32, (blk, blk), 0)
    qryi = lax.broadcasted_iota(jnp.int32, (blk, blk), 1)
    causal = keyi <= qryi
    bidx = lax.broadcasted_iota(jnp.int32, (nb, blk), 0)
    q0 = pl.multiple_of(qi * blk, blk)

    hs = range(MOBA_HEADS)
    qts = [qt_ref[0, hh] for hh in hs]
    gates = [_dot(km_ref[0, hh], qts[hh]) for hh in hs]
    valid = bidx < qi
    ranks = [_block_rank(jnp.where(valid, g, NEG_INF), 0) for g in gates]
    for hh in hs:
        sel_sc[hh] = ((ranks[hh] < n_top) & valid).astype(F32)
    s0 = [jnp.where(causal, _dot(ka_ref[0, hh, pl.ds(q0, blk), :], qts[hh]), NEG_INF) for hh in hs]
    m0 = [jnp.max(t, axis=0, keepdims=True) for t in s0]
    p0 = [jnp.exp(s0[hh] - m0[hh]) for hh in hs]
    for hh in hs:
        m_sc[hh] = m0[hh]
        l_sc[hh] = jnp.sum(p0[hh], axis=0, keepdims=True)
    pv0 = [_dot(vt_ref[0, hh, qi], p0[hh].astype(BF16)) for hh in hs]
    for hh in hs:
        acc_sc[hh] = pv0[hh]

    def past_block(j, carry):
        k0 = pl.multiple_of(j * blk, blk)
        s = [_dot(ka_ref[0, hh, pl.ds(k0, blk), :], qt_ref[0, hh]) for hh in hs]
        ps, alphas = [], []
        for hh in hs:
            cj = -slopes_ref[hg * MOBA_HEADS + hh] * ((qi - j) * blk).astype(F32)
            picked = sel_sc[hh, pl.ds(j, 1), :] > 0.0
            m_old = m_sc[hh]
            m_new = jnp.maximum(m_old, jnp.where(picked, jnp.max(s[hh], axis=0, keepdims=True) + cj, NEG_INF))
            alpha = jnp.exp(m_old - m_new)
            p = jnp.exp(s[hh] - jnp.where(picked, m_new - cj, BIG))
            m_sc[hh] = m_new
            l_sc[hh] = alpha * l_sc[hh] + jnp.sum(p, axis=0, keepdims=True)
            ps.append(p.astype(BF16))
            alphas.append(alpha)
        pv = [_dot(vt_ref[0, hh, j], ps[hh]) for hh in hs]
        for hh in hs:
            acc_sc[hh] = alphas[hh] * acc_sc[hh] + pv[hh]
        return carry

    lax.fori_loop(0, qi, past_block, 0)

    o_ref[...] = jnp.concatenate([(acc_sc[hh] / l_sc[hh]).T for hh in hs], axis=1).astype(o_ref.dtype)


def _moba_prompt(slopes, qt, ka, vt, kmean, batch, seq):
    nb = seq // MOBA_BLOCK
    blk = MOBA_BLOCK
    km = kmean.reshape(batch, nb, N_HEADS, HEAD_DIM).transpose(0, 2, 1, 3).astype(BF16)
    km = jnp.concatenate([km, jnp.zeros((batch, N_HEADS, nb, 128 - HEAD_DIM), BF16)], axis=3)

    hgn = MOBA_HEADS
    grid_spec = pltpu.PrefetchScalarGridSpec(
        num_scalar_prefetch=1,
        grid=(batch, N_HEADS // hgn, nb),
        in_specs=[pl.BlockSpec((1, hgn, 128, blk), lambda b, g, qi, s: (b, g, 0, qi)),
                  pl.BlockSpec((1, hgn, seq, 128), lambda b, g, qi, s: (b, g, 0, 0)),
                  pl.BlockSpec((1, hgn, nb, HEAD_DIM, blk), lambda b, g, qi, s: (b, g, 0, 0, 0)),
                  pl.BlockSpec((1, hgn, nb, 128), lambda b, g, qi, s: (b, g, 0, 0))],
        out_specs=pl.BlockSpec((blk, hgn * HEAD_DIM), lambda b, g, qi, s: (b * nb + qi, g)),
        scratch_shapes=[pltpu.VMEM((hgn, nb, blk), F32),
                        pltpu.VMEM((hgn, 1, blk), F32),
                        pltpu.VMEM((hgn, 1, blk), F32),
                        pltpu.VMEM((hgn, HEAD_DIM, blk), F32)],
    )
    return pl.pallas_call(
        functools.partial(_moba_kernel, nb=nb),
        grid_spec=grid_spec,
        out_shape=jax.ShapeDtypeStruct((batch * seq, WIDTH), BF16),
        compiler_params=_params("parallel", "parallel", "arbitrary"),
        name="moba_prompt",
    )(slopes, qt, ka, vt, km)


def _b(t):
    return t.astype(BF16)


def _split3(x):
    x1 = _b(x)
    r1 = x - x1.astype(F32)
    x2 = _b(r1)
    return x1, x2, _b(r1 - x2.astype(F32))


def _rwkv_pointwise(p, pprev, mu, w0, decay_up, a0, iclr_up, gate_up, k_k, k_a):
    xs = p + mu * (pprev - p)
    r = xs[:, _O_R:_O_R + WIDTH]
    k = xs[:, _O_K:_O_K + WIDTH]
    v = xs[:, _O_V:_O_V + WIDTH]
    xw = xs[:, _O_XW:_O_XW + DECAY_LORA]
    xa = xs[:, _O_XA:_O_XA + AAA_LORA]
    xg = xs[:, _O_XG:_O_XG + GATE_LORA]
    w = w0 + _dot(_b(jnp.tanh(xw)), _b(decay_up))
    w = -jax.nn.softplus(-w) - 0.5
    logdecay = -jnp.exp(w)
    a = jax.nn.sigmoid(a0 + _dot(_b(xa), _b(iclr_up)))
    g = _dot(_b(jax.nn.sigmoid(xg)), _b(gate_up))
    kk = k * k_k
    k2 = k * (1.0 + (a - 1.0) * k_a)
    return r, k2, v, kk, a, g, logdecay


def _head_sum(x):
    row = lax.broadcasted_iota(jnp.int32, (128, 128), 0)
    col = lax.broadcasted_iota(jnp.int32, (128, 128), 1)
    seg = ((row // HEAD_DIM) == (col // HEAD_DIM)).astype(BF16)
    hi = _b(x)
    lo = _b(x - hi.astype(F32))
    cols = [slice(g * 128, (g + 1) * 128) for g in range(x.shape[1] // 128)]
    return jnp.concatenate([_dot(hi[:, c], seg) + _dot(lo[:, c], seg) for c in cols], axis=1)


def _head_norm(kk_h):
    return kk_h * lax.rsqrt(jnp.maximum(jnp.sum(kk_h * kk_h, axis=-1, keepdims=True), 1e-24))


def _group_norm_out(y_h, r_h, k_h, v_h, g_h, rk_h, lnw_h, lnb_h):
    mean = jnp.mean(y_h, axis=-1, keepdims=True)
    var = jnp.mean(jnp.square(y_h - mean), axis=-1, keepdims=True)
    yn = (y_h - mean) * lax.rsqrt(var + GN_EPS) * lnw_h + lnb_h
    yn = yn + jnp.sum(r_h * k_h * rk_h, axis=-1, keepdims=True) * v_h
    return yn * g_h


def _unit_lower_inverse(mats):
    n = mats[0].shape[0]
    row = lax.broadcasted_iota(jnp.int32, (n, n), 0)
    col = lax.broadcasted_iota(jnp.int32, (n, n), 1)
    eye = (row == col).astype(F32)
    size = 16
    same = (row // size) == (col // size)
    pws = [jnp.where(same, a, 0.0) for a in mats]
    xs = [eye - pw for pw in pws]
    for _ in range(3):
        pwb = [_b(pw) for pw in pws]
        pws = [_dot(t, t) for t in pwb]
        xs = [x + _dot(_b(x), _b(pw)) for x, pw in zip(xs, pws)]
        yield
    while size < n:
        size2 = size * 2
        same2 = (row // size2) == (col // size2)
        keep = same2 & jnp.logical_not(same)
        xb = [_b(x) for x in xs]
        ox = [_b(_dot(_b(jnp.where(keep, a, 0.0)), t)) for a, t in zip(mats, xb)]
        xs = [x - _dot(t, o) for x, t, o in zip(xs, xb, ox)]
        same = same2
        size = size2
        yield
    return xs


def _cumsum_rows(x, seg):
    n = x.shape[0]
    row = lax.broadcasted_iota(jnp.int32, (n, n), 0)
    col = lax.broadcasted_iota(jnp.int32, (n, n), 1)
    tri = ((row >= col) & ((row // seg) == (col // seg))).astype(BF16)
    x1, x2, x3 = _split3(x)
    return _dot(tri, x1) + _dot(tri, x2) + _dot(tri, x3)


RWKV_STEP_CHUNKS = 2
_RWKV_BLOCK = RWKV_CHUNK * RWKV_STEP_CHUNKS
_KAPH, _RHAT, _KHAT, _KBAR, _BHAT, _BBAR, _VB = range(7)
_RHAT32, _BONUS, _GATE = range(3)


def _interleave(*gens):
    live = list(gens)
    while live:
        for gen in list(live):
            try:
                next(gen)
            except StopIteration:
                live.remove(gen)


def _rwkv_pointwise_stage(p, last_sc, w, ob_sc, of_sc, we_sc):
    mu, w0, dup, a0, iup, gup, k_k, k_a, r_k = w
    L = RWKV_CHUNK
    nck = RWKV_STEP_CHUNKS
    ts = _RWKV_BLOCK
    rowi = lax.broadcasted_iota(jnp.int32, p.shape, 0)
    pprev = jnp.where(rowi == 0, last_sc[0:1, :], pltpu.roll(p, 1, 0))
    last_sc[0:1, :] = p[ts - 1:ts, :]
    yield
    r, k2, v, kk, a, g, logdecay = _rwkv_pointwise(p, pprev, mu, w0, dup, a0, iup, gup, k_k, k_a)
    yield
    cum = _cumsum_rows(logdecay, L)
    ends = [cum[c * L + L - 1:c * L + L, :] for c in range(nck)]
    cum_end = jnp.concatenate([jnp.broadcast_to(e, (L, WIDTH)) for e in ends], axis=0)
    w_inc = jnp.exp(cum)
    w_exc = jnp.exp(cum - logdecay)
    w_inv = jnp.exp(-cum)
    w_tail = jnp.exp(cum_end - cum)
    for c in range(nck):
        we_sc[c:c + 1, :] = jnp.exp(ends[c])
    yield
    kap_all = kk * lax.rsqrt(jnp.maximum(_head_sum(kk * kk), 1e-24))
    bb_all = kap_all * a
    r_hat_all = r * w_inc
    ob_sc[_KAPH] = _b(kap_all * w_exc)
    ob_sc[_RHAT] = _b(r_hat_all)
    ob_sc[_KHAT] = _b(k2 * w_inv)
    ob_sc[_KBAR] = _b(k2 * w_tail)
    yield
    ob_sc[_BHAT] = _b(bb_all * w_inv)
    ob_sc[_BBAR] = _b(bb_all * w_tail)
    ob_sc[_VB] = _b(v)
    of_sc[_RHAT32] = r_hat_all
    of_sc[_BONUS] = _head_sum(r * k2 * r_k) * v
    of_sc[_GATE] = g


def _rwkv_matmul_stage(ob_sc, of_sc, we_sc, s_sc, y_sc, lnw, lnb, o_ref, rows_out):
    L = RWKV_CHUNK
    nck = RWKV_STEP_CHUNKS
    trow = lax.broadcasted_iota(jnp.int32, (L, L), 0)
    tcol = lax.broadcasted_iota(jnp.int32, (L, L), 1)
    lower_incl = trow >= tcol
    lower_strict = trow > tcol
    heads = range(N_HEADS)
    items = [(slice(c * L, (c + 1) * L), slice(h * HEAD_DIM, (h + 1) * HEAD_DIM))
             for c in range(nck) for h in heads]
    n = range(len(items))
    ld = lambda slot, it: ob_sc[slot, it[0], it[1]]
    lhs = [jnp.concatenate([ld(_KAPH, it), ld(_RHAT, it)], axis=0) for it in items]
    ak = [_dot_nt(lhs[i], ld(_KHAT, items[i])) for i in n]
    ab = [_dot_nt(lhs[i], ld(_BHAT, items[i])) for i in n]
    yield
    a_kr = [_b(jnp.concatenate([jnp.where(lower_strict, t[:L], 0.0), jnp.where(lower_incl, t[L:], 0.0)], axis=0))
            for t in ak]
    a_rb = [_b(jnp.where(lower_incl, t[L:], 0.0)) for t in ab]
    t_inv = yield from _unit_lower_inverse([jnp.where(lower_strict, t[:L], 0.0) for t in ab])
    t_inv = [_b(t) for t in t_inv]
    av = [_dot(a_kr[i], ld(_VB, items[i])) for i in n]
    pm = [_b(_dot(t_inv[i], ld(_KAPH, items[i]))) for i in n]
    yield
    qm = [_b(_dot(t_inv[i], _b(av[i][:L]))) for i in n]
    r_eff = [_b(of_sc[_RHAT32, items[i][0], items[i][1]] - _dot(a_rb[i], pm[i])) for i in n]
    ptb = [_b(_dot_tn(pm[i], ld(_BBAR, items[i]))) for i in n]
    yield
    y0 = [av[i][L:] - _dot(a_rb[i], qm[i]) for i in n]
    cm = [_dot_tn(ld(_VB, items[i]), ld(_KBAR, items[i])) - _dot_tn(qm[i], ld(_BBAR, items[i])) for i in n]
    yield
    state = [s_sc[h] for h in heads]
    for c in range(nck):
        w_end = we_sc[c:c + 1, :]
        sb = [_b(t) for t in state]
        ys = [_dot_nt(r_eff[c * N_HEADS + h], sb[h]) + y0[c * N_HEADS + h] for h in heads]
        state = [state[h] * w_end[:, items[h][1]] - _dot(sb[h], ptb[c * N_HEADS + h]) + cm[c * N_HEADS + h]
                 for h in heads]
        for h in heads:
            y_sc[items[c * N_HEADS + h]] = ys[h]
    for h in heads:
        s_sc[h] = state[h]
    yield
    y = y_sc[...]
    dev = y - _head_sum(y) * (1.0 / HEAD_DIM)
    var = _head_sum(dev * dev) * (1.0 / HEAD_DIM)
    yn = dev * lax.rsqrt(var + GN_EPS) * lnw + lnb
    o_ref[0, rows_out, :] = ((yn + of_sc[_BONUS]) * of_sc[_GATE]).astype(o_ref.dtype)


def _rwkv_chunk_kernel(p_ref, mu_ref, w0_ref, dup_ref, a0_ref, iup_ref, gup_ref, kk_ref, ka_ref,
                       rk_ref, lnw_ref, lnb_ref, o_ref, s_out_ref,
                       s_sc, last_sc, y_sc, xb_sc, xf_sc, xw_sc, yb_sc, yf_sc, yw_sc):
    step = pl.program_id(1)
    last = pl.num_programs(1) - 1
    ts = _RWKV_BLOCK

    @pl.when(step == 0)
    def _():
        s_sc[...] = jnp.zeros_like(s_sc)
        last_sc[...] = jnp.zeros_like(last_sc)
        yb_sc[...] = jnp.zeros_like(yb_sc)
        yf_sc[...] = jnp.zeros_like(yf_sc)
        yw_sc[...] = jnp.zeros_like(yw_sc)

    w = (mu_ref[...], w0_ref[...], dup_ref[...], a0_ref[...], iup_ref[...], gup_ref[...],
         kk_ref[...], ka_ref[...], rk_ref[...])
    lnw, lnb = lnw_ref[...], lnb_ref[...]
    _interleave(_rwkv_matmul_stage(yb_sc, yf_sc, yw_sc, s_sc, y_sc, lnw, lnb, o_ref, slice(0, ts)),
                _rwkv_pointwise_stage(p_ref[0:ts, :], last_sc, w, xb_sc, xf_sc, xw_sc))

    @pl.when(step == last)
    def _():
        s_out_ref[0] = s_sc[...]

    _interleave(_rwkv_matmul_stage(xb_sc, xf_sc, xw_sc, s_sc, y_sc, lnw, lnb, o_ref, slice(ts, 2 * ts)),
                _rwkv_pointwise_stage(p_ref[ts:2 * ts, :], last_sc, w, yb_sc, yf_sc, yw_sc))


def _rwkv_prompt(p_rw, weights, batch, seq):
    ts = _RWKV_BLOCK
    ns = seq // (2 * ts)
    const = lambda b, c: (0, 0)
    w_specs = [pl.BlockSpec(w.shape, const) for w in weights]
    operand_scratch = [pltpu.VMEM((7, ts, WIDTH), BF16), pltpu.VMEM((3, ts, WIDTH), F32), pltpu.VMEM((8, WIDTH), F32)]
    out, state = pl.pallas_call(
        _rwkv_chunk_kernel,
        grid=(batch, ns + 1),
        in_specs=[pl.BlockSpec((2 * ts, RWKV_COLS), lambda b, c: (b * ns + jnp.minimum(c, ns - 1), 0))] + w_specs,
        out_specs=[pl.BlockSpec((1, 2 * ts, WIDTH), lambda b, c: (b, c, 0)),
                   pl.BlockSpec((1, N_HEADS, HEAD_DIM, HEAD_DIM), lambda b, c: (b, 0, 0, 0))],
        out_shape=[jax.ShapeDtypeStruct((batch, seq + 2 * ts, WIDTH), BF16),
                   jax.ShapeDtypeStruct((batch, N_HEADS, HEAD_DIM, HEAD_DIM), F32)],
        scratch_shapes=[pltpu.VMEM((N_HEADS, HEAD_DIM, HEAD_DIM), F32),
                        pltpu.VMEM((8, RWKV_COLS), F32),
                        pltpu.VMEM((ts, WIDTH), F32)] + operand_scratch + operand_scratch,
        compiler_params=_params("parallel", "arbitrary"),
        name="rwkv_prompt",
    )(p_rw, *weights)
    return out[:, ts:ts + seq].reshape(batch * seq, WIDTH), state


def _rwkv_step_kernel(p_ref, sh_ref, s_ref, mu_ref, w0_ref, dup_ref, a0_ref, iup_ref, gup_ref, kk_ref,
                      ka_ref, rk_ref, lnw_ref, lnb_ref, o_ref, s_out_ref):
    p = jnp.broadcast_to(p_ref[0], (8, RWKV_COLS))
    pprev = jnp.broadcast_to(sh_ref[0], (8, RWKV_COLS))
    r, k2, v, kk, a, g, logdecay = (t[0:1] for t in _rwkv_pointwise(
        p, pprev, mu_ref[...], w0_ref[...], dup_ref[...], a0_ref[...], iup_ref[...], gup_ref[...],
        kk_ref[...], ka_ref[...]))
    decay = jnp.exp(logdecay)
    n = HEAD_DIM
    eye = lax.broadcasted_iota(jnp.int32, (n, n), 0) == lax.broadcasted_iota(jnp.int32, (n, n), 1)

    def to_col(row_vec):
        return jnp.sum(jnp.where(eye, row_vec, 0.0), axis=-1, keepdims=True)

    def to_row(col_vec):
        return jnp.sum(jnp.where(eye, col_vec, 0.0), axis=0, keepdims=True)

    for h in range(N_HEADS):
        ln = slice(h * n, (h + 1) * n)
        s0 = s_ref[0, h]
        kap = _head_norm(kk[:, ln])
        sa = jnp.sum(s0 * (-kap), axis=-1, keepdims=True)
        s_new = s0 * decay[:, ln] + sa * (kap * a[:, ln]) + to_col(v[:, ln]) * k2[:, ln]
        s_out_ref[0, h] = s_new
        y = to_row(jnp.sum(s_new * r[:, ln], axis=-1, keepdims=True))
        o_ref[0, :, ln] = _group_norm_out(y, r[:, ln], k2[:, ln], v[:, ln], g[:, ln], rk_ref[:, ln],
                                          lnw_ref[:, ln], lnb_ref[:, ln]).astype(o_ref.dtype)


def _rwkv_sample(p_rw, shift, state, weights):
    n = p_rw.shape[0]
    const = lambda s: (0, 0)
    vec = pl.BlockSpec((1, 1, RWKV_COLS), lambda s: (s, 0, 0))
    st = pl.BlockSpec((1, N_HEADS, HEAD_DIM, HEAD_DIM), lambda s: (s, 0, 0, 0))
    rw, s_new = pl.pallas_call(
        _rwkv_step_kernel,
        grid=(n,),
        in_specs=[vec, vec, st] + [pl.BlockSpec(w.shape, const) for w in weights],
        out_specs=[pl.BlockSpec((1, 1, WIDTH), lambda s: (s, 0, 0)), st],
        out_shape=[jax.ShapeDtypeStruct((n, 1, WIDTH), BF16),
                   jax.ShapeDtypeStruct(state.shape, F32)],
        compiler_params=_params("parallel"),
        name="rwkv_sample",
    )(p_rw.reshape(n, 1, RWKV_COLS), shift.reshape(n, 1, RWKV_COLS), state, *weights)
    return rw.reshape(n, WIDTH), s_new


_PAGES_PER_BLOCK = MOBA_BLOCK // PAGE_SIZE
_SELECT_PAGES = 16


def _sample_select_kernel(pt_ref, q_ref, *refs):
    page_refs = refs[:_SELECT_PAGES]
    idx_ref = refs[_SELECT_PAGES]
    km_sc = refs[_SELECT_PAGES + 1]
    g = pl.program_id(1)
    ppb = _PAGES_PER_BLOCK
    bps = _SELECT_PAGES // ppb
    ones = jnp.ones((8, PAGE_SIZE), BF16)
    tots = []
    for b in range(bps):
        tot = page_refs[b * ppb][0]
        for i in range(1, ppb):
            tot = tot + page_refs[b * ppb + i][0]
        tots.append(tot.reshape(WIDTH, PAGE_SIZE))
    pieces = [_split3(t) for t in tots]
    ksums = [_dot_nt(ones, pc[0]) + _dot_nt(ones, pc[1]) + _dot_nt(ones, pc[2]) for pc in pieces]
    for b in range(bps):
        km_sc[pl.ds(g * bps + b, 1), :] = ksums[b][0:1] * (1.0 / MOBA_BLOCK)

    @pl.when(g == pl.num_programs(1) - 1)
    def _():
        rnd = lambda t: t.astype(BF16).astype(F32)
        prod = rnd(km_sc[...]) * rnd(q_ref[0])
        lane = lax.broadcasted_iota(jnp.int32, (WIDTH, 128), 0)
        hcol = lax.broadcasted_iota(jnp.int32, (WIDTH, 128), 1)
        head_sum = ((lane // HEAD_DIM) == hcol).astype(BF16)
        pieces = _split3(prod)
        gate = _dot(pieces[0], head_sum) + _dot(pieces[1], head_sum) + _dot(pieces[2], head_sum)
        rank = _block_rank(gate, 0)
        bidx = lax.broadcasted_iota(jnp.int32, gate.shape, 0)
        rows = [jnp.sum(jnp.where(rank == i, bidx, 0), axis=0, keepdims=True) for i in range(MOBA_TOP_K)]
        rows += [jnp.zeros((1, 128), jnp.int32)] * (8 - MOBA_TOP_K)
        idx_ref[0] = jnp.concatenate(rows, axis=0)


def _sample_select(page_table, q, cache_kt):
    n, n_pages = page_table.shape
    nb = n_pages // _PAGES_PER_BLOCK
    steps = n_pages // _SELECT_PAGES

    def page_spec(i):
        return pl.BlockSpec((1, N_HEADS, HEAD_DIM, PAGE_SIZE),
                            lambda s, g, pt, i=i: (pt[s * n_pages + g * _SELECT_PAGES + i], 0, 0, 0))

    grid_spec = pltpu.PrefetchScalarGridSpec(
        num_scalar_prefetch=1,
        grid=(n, steps),
        in_specs=[pl.BlockSpec((1, 1, WIDTH), lambda s, g, pt: (s, 0, 0))]
        + [page_spec(i) for i in range(_SELECT_PAGES)],
        out_specs=pl.BlockSpec((1, 8, 128), lambda s, g, pt: (s, 0, 0)),
        scratch_shapes=[pltpu.VMEM((nb, WIDTH), F32)],
    )
    return pl.pallas_call(
        _sample_select_kernel,
        grid_spec=grid_spec,
        out_shape=jax.ShapeDtypeStruct((n, 8, 128), jnp.int32),
        compiler_params=_params("parallel", "arbitrary"),
        name="sample_select",
    )(page_table.reshape(-1), q.reshape(n, 1, WIDTH), *([cache_kt] * _SELECT_PAGES))


def _sample_attn_kernel(pt_ref, top_ref, slopes_ref, q_ref, kn_ref, vn_ref, *refs, past_len):
    ppb = _PAGES_PER_BLOCK
    npg = MOBA_TOP_K * ppb
    k_refs = refs[:npg]
    v_refs = refs[npg:2 * npg]
    o_ref = refs[2 * npg]
    s_idx = pl.program_id(0)
    h = pl.program_id(1)
    slope = slopes_ref[h]
    q = q_ref[0, 0] * (HEAD_DIM ** -0.5)
    lane = lax.broadcasted_iota(jnp.int32, (1, PAGE_SIZE), 1)

    scores = []
    for i in range(npg):
        blk_id = top_ref[(s_idx * N_HEADS + h) * MOBA_TOP_K + i // ppb]
        dist = (past_len - blk_id * MOBA_BLOCK - (i % ppb) * PAGE_SIZE - lane).astype(F32)
        s = jnp.sum(k_refs[i][0, 0] * q, axis=0, keepdims=True)
        scores.append(s - slope * dist)
    s_self = jnp.sum(q * kn_ref[0, 0], axis=0, keepdims=True)
    m = s_self
    for s in scores:
        m = jnp.maximum(m, jnp.max(s, axis=1, keepdims=True))
    p_self = jnp.exp(s_self - m)
    den = p_self
    accv = jnp.zeros((HEAD_DIM, PAGE_SIZE), F32)
    for i, s in enumerate(scores):
        p = jnp.exp(s - m)
        den = den + jnp.sum(p, axis=1, keepdims=True)
        accv = accv + v_refs[i][0, 0] * p
    acc = jnp.sum(accv, axis=1, keepdims=True) + p_self * vn_ref[0, 0]
    o_ref[0, 0] = acc / den


def _sample_attn(page_table, top_idx, slopes, q, k_new, v_new, cache_kt, cache_vt):
    n, n_pages = page_table.shape
    ppb = _PAGES_PER_BLOCK
    npg = MOBA_TOP_K * ppb

    def slab_spec(i):
        def imap(s, h, pt, top, sl, i=i):
            blk_id = top[(s * N_HEADS + h) * MOBA_TOP_K + i // ppb]
            return (pt[s * n_pages + blk_id * ppb + i % ppb], h, 0, 0)
        return pl.BlockSpec((1, 1, HEAD_DIM, PAGE_SIZE), imap)

    vec = pl.BlockSpec((1, 1, HEAD_DIM, 1), lambda s, h, pt, top, sl: (s, h, 0, 0))
    grid_spec = pltpu.PrefetchScalarGridSpec(
        num_scalar_prefetch=3,
        grid=(n, N_HEADS),
        in_specs=[vec, vec, vec] + [slab_spec(i) for i in range(npg)] * 2,
        out_specs=vec,
    )
    return pl.pallas_call(
        functools.partial(_sample_attn_kernel, past_len=n_pages * PAGE_SIZE),
        grid_spec=grid_spec,
        out_shape=jax.ShapeDtypeStruct((n, N_HEADS, HEAD_DIM, 1), F32),
        compiler_params=_params("parallel", "parallel"),
        name="sample_attn",
    )(page_table.reshape(-1), top_idx, slopes, q, k_new, v_new,
      *([cache_kt] * npg), *([cache_vt] * npg))


def kernel(x_prompt, x_sample, cache_k, cache_v, page_table, state_wkv, state_shift,
           norm_mix_g, w_in, mu_shift, decay_w0, decay_up, iclr_a0, iclr_up, gate_up,
           k_k, k_a, r_k, ln_x_w, ln_x_b, w_out, norm_ffn_g, w_ffn_up, w_ffn_down, norm_final_g):
    depth = w_in.shape[0]
    assert depth == 1
    batch, seq, _ = x_prompt.shape
    n_seq, n_pages = page_table.shape
    slopes = jnp.exp2(-8.0 * jnp.arange(1, N_HEADS + 1, dtype=F32) / N_HEADS)

    l = 0
    row = lambda t: t.reshape(1, -1)
    w_in_l = jnp.concatenate([w_in[l][:, :ATTN_COLS], _to_internal(w_in[l][:, ATTN_COLS:])], axis=1)
    g_mix = row(norm_mix_g[l])
    rw_weights = (row(_to_internal(mu_shift[l])), row(decay_w0[l]), decay_up[l], row(iclr_a0[l]), iclr_up[l],
                  gate_up[l], row(k_k[l]), row(k_a[l]), row(r_k[l]), row(ln_x_w[l]), row(ln_x_b[l]))
    wo = w_out[l].astype(BF16)
    ffn_weights = (wo[:WIDTH], wo[WIDTH:], row(norm_ffn_g[l]), w_ffn_up[l].astype(BF16),
                   w_ffn_down[l].astype(BF16), row(norm_final_g))

    xp = x_prompt.reshape(batch * seq, D_MODEL)
    w_in_b = w_in_l.astype(BF16)
    kt, vt, qt, ka, vtb, prw, kmean = _inproj_prompt(xp, g_mix, w_in_b, slopes, batch, seq)
    nb = seq // MOBA_BLOCK
    attn_p = _moba_prompt(slopes, qt, ka, vtb, kmean.reshape(batch, nb, WIDTH), batch, seq)
    rw_p, wkv_p = _rwkv_prompt(prw, rw_weights, batch, seq)
    y_prompt = _out_ffn(xp, attn_p, rw_p, *ffn_weights, tm=512).reshape(batch, seq, D_MODEL)
    shift_p = _from_internal(prw.reshape(batch, seq, RWKV_COLS)[:, -1])

    hd = (N_HEADS, HEAD_DIM)
    xs = x_sample.reshape(n_seq, D_MODEL)
    proj_s = _inproj_small(xs, g_mix, w_in_b)
    q_s, k_s, v_s = proj_s[:, :WIDTH], proj_s[:, WIDTH:2 * WIDTH], proj_s[:, 2 * WIDTH:ATTN_COLS]
    prw_s = proj_s[:, ATTN_COLS:]
    ckt = cache_k.transpose(0, 1, 3, 4, 2).reshape(-1, N_HEADS, HEAD_DIM, PAGE_SIZE)
    cvt = cache_v.transpose(0, 1, 3, 4, 2).reshape(-1, N_HEADS, HEAD_DIM, PAGE_SIZE)
    top = _sample_select(page_table, q_s, ckt)
    top_idx = top[:, :MOBA_TOP_K, :N_HEADS].transpose(0, 2, 1).reshape(-1)
    col = lambda t: t.reshape(n_seq, N_HEADS, HEAD_DIM, 1)
    attn_s = _sample_attn(page_table, top_idx, slopes, col(q_s), col(k_s), col(v_s), ckt, cvt).reshape(n_seq, WIDTH)
    rw_s, wkv_s = _rwkv_sample(prw_s, _to_internal(state_shift[l]), state_wkv[l], rw_weights)
    y_sample = _out_ffn(xs, attn_s.astype(BF16), rw_s, *ffn_weights, tm=n_seq).reshape(n_seq, 1, D_MODEL)
    shift_s = _from_internal(prw_s)

    return (y_prompt, y_sample,
            kt.transpose(0, 3, 1, 2)[None], vt.transpose(0, 3, 1, 2)[None],
            wkv_p[None], shift_p[None],
            k_s.reshape(1, n_seq, 1, *hd), v_s.reshape(1, n_seq, 1, *hd),
            wkv_s[None], shift_s[None])
```

```python
import functools

import jax
import jax.numpy as jnp
from jax import lax
from jax.experimental import pallas as pl
from jax.experimental.pallas import tpu as pltpu

F32 = jnp.float32
BF16 = jnp.bfloat16

D_MODEL = 1024
HEAD_DIM = 64
N_HEADS = 8
WIDTH = N_HEADS * HEAD_DIM
MOBA_BLOCK = 256
MOBA_TOP_K = 3
DECAY_LORA = 64
AAA_LORA = 64
GATE_LORA = 128
ATTN_COLS = 3 * WIDTH
RWKV_COLS = 3 * WIDTH + DECAY_LORA + AAA_LORA + GATE_LORA
RMS_EPS = 1e-6
GN_EPS = 64e-5
NEG_INF = -1e30
PAGE_SIZE = 128
RWKV_CHUNK = 64
VMEM_LIMIT = 48 * 1024 * 1024

_O_R, _O_K, _O_V = 0, WIDTH, 2 * WIDTH
_O_XW = 3 * WIDTH
_O_XA = _O_XW + DECAY_LORA
_O_XG = _O_XA + AAA_LORA


def _to_internal(t):
    r, xw, kv, rest = (t[..., :WIDTH], t[..., WIDTH:WIDTH + DECAY_LORA],
                       t[..., WIDTH + DECAY_LORA:3 * WIDTH + DECAY_LORA], t[..., 3 * WIDTH + DECAY_LORA:])
    return jnp.concatenate([r, kv, xw, rest], axis=-1)


def _from_internal(t):
    r, kv, xw, rest = (t[..., :WIDTH], t[..., WIDTH:3 * WIDTH],
                       t[..., 3 * WIDTH:3 * WIDTH + DECAY_LORA], t[..., 3 * WIDTH + DECAY_LORA:])
    return jnp.concatenate([r, xw, kv, rest], axis=-1)


def _params(*sem):
    return pltpu.CompilerParams(dimension_semantics=sem, vmem_limit_bytes=VMEM_LIMIT)


def _rms(x, g):
    return x * lax.rsqrt(jnp.mean(x * x, axis=-1, keepdims=True) + RMS_EPS) * g


def _dot(a, b, **kw):
    return jnp.dot(a, b, preferred_element_type=F32, **kw)


def _dot_nt(a, b, **kw):
    return lax.dot_general(a, b, (((1,), (1,)), ((), ())), preferred_element_type=F32, **kw)


def _dot_tn(a, b, **kw):
    return lax.dot_general(a, b, (((0,), (0,)), ((), ())), preferred_element_type=F32, **kw)


def _inproj_kernel(x_ref, g_ref, w_ref, srow_ref, kt_ref, vt_ref, qt_ref, ka_ref, vtb_ref, prw_ref, km_ref):
    tm = x_ref.shape[0]
    hd = (N_HEADS, HEAD_DIM, tm)
    xn = _rms(x_ref[...], g_ref[...]).astype(BF16)
    proj = _dot(xn, w_ref[...])
    q = proj[:, 0:WIDTH] * (HEAD_DIM ** -0.5)
    k = proj[:, WIDTH:2 * WIDTH]
    v = proj[:, 2 * WIDTH:3 * WIDTH]
    prw_ref[...] = proj[:, ATTN_COLS:]
    km_ref[0] = jnp.mean(k, axis=0, keepdims=True)
    k_t = k.T.reshape(hd)
    v_t = v.T.reshape(hd)
    kt_ref[0] = k_t
    vt_ref[0] = v_t
    vtb_ref[0, :, 0] = v_t.astype(BF16)
    qt_ref[0, :, 0:HEAD_DIM, :] = q.T.reshape(hd).astype(BF16)
    qt_ref[0, :, HEAD_DIM:, :] = jnp.broadcast_to(srow_ref[...], hd).astype(BF16)
    lane = lax.broadcasted_iota(jnp.int32, (tm, 128 - HEAD_DIM), 1)
    pos = lax.broadcasted_iota(jnp.int32, (tm, 128 - HEAD_DIM), 0)
    pos_cols = jnp.where(lane == 0, pos, 0).astype(F32).astype(BF16)
    for h in range(N_HEADS):
        ka_ref[0, h, :, 0:HEAD_DIM] = k[:, h * HEAD_DIM:(h + 1) * HEAD_DIM].astype(BF16)
        ka_ref[0, h, :, HEAD_DIM:] = pos_cols


def _inproj_prompt(x, g, w_bf16, slopes, batch, seq):
    tm = MOBA_BLOCK
    m = x.shape[0]
    nb = seq // tm
    ncol = w_bf16.shape[1]
    row = lambda i: (i, 0)
    const = lambda i: (0, 0)
    tok = lambda i: (i // nb, 0, 0, i % nb)
    srow = jnp.zeros((N_HEADS, HEAD_DIM, 1), F32).at[:, 0, 0].set(slopes)
    return pl.pallas_call(
        _inproj_kernel,
        grid=(m // tm,),
        in_specs=[pl.BlockSpec((tm, D_MODEL), row),
                  pl.BlockSpec((1, D_MODEL), const),
                  pl.BlockSpec((D_MODEL, ncol), const),
                  pl.BlockSpec((N_HEADS, HEAD_DIM, 1), lambda i: (0, 0, 0))],
        out_specs=[pl.BlockSpec((1, N_HEADS, HEAD_DIM, tm), tok),
                   pl.BlockSpec((1, N_HEADS, HEAD_DIM, tm), tok),
                   pl.BlockSpec((1, N_HEADS, 128, tm), tok),
                   pl.BlockSpec((1, N_HEADS, tm, 128), lambda i: (i // nb, 0, i % nb, 0)),
                   pl.BlockSpec((1, N_HEADS, 1, HEAD_DIM, tm), lambda i: (i // nb, 0, i % nb, 0, 0)),
                   pl.BlockSpec((tm, RWKV_COLS), row),
                   pl.BlockSpec((1, 1, WIDTH), lambda i: (i, 0, 0))],
        out_shape=[jax.ShapeDtypeStruct((batch, N_HEADS, HEAD_DIM, seq), F32),
                   jax.ShapeDtypeStruct((batch, N_HEADS, HEAD_DIM, seq), F32),
                   jax.ShapeDtypeStruct((batch, N_HEADS, 128, seq), BF16),
                   jax.ShapeDtypeStruct((batch, N_HEADS, seq, 128), BF16),
                   jax.ShapeDtypeStruct((batch, N_HEADS, nb, HEAD_DIM, tm), BF16),
                   jax.ShapeDtypeStruct((m, RWKV_COLS), F32),
                   jax.ShapeDtypeStruct((m // tm, 1, WIDTH), F32)],
        compiler_params=_params("parallel"),
        name="inproj_prompt",
    )(x, g, w_bf16, srow)


def _inproj_small_kernel(x_ref, g_ref, w_ref, o_ref):
    xn = _rms(x_ref[...], g_ref[...]).astype(BF16)
    o_ref[...] = _dot(xn, w_ref[...])


def _inproj_small(x, g, w, tn=256):
    m = x.shape[0]
    ncol = w.shape[1]
    return pl.pallas_call(
        _inproj_small_kernel,
        grid=(ncol // tn,),
        in_specs=[pl.BlockSpec((m, D_MODEL), lambda j: (0, 0)),
                  pl.BlockSpec((1, D_MODEL), lambda j: (0, 0)),
                  pl.BlockSpec((D_MODEL, tn), lambda j: (0, j))],
        out_specs=pl.BlockSpec((m, tn), lambda j: (0, j)),
        out_shape=jax.ShapeDtypeStruct((m, ncol), F32),
        compiler_params=_params("parallel"),
        name="inproj_sample",
    )(x, g, w)


def _ffn_kernel(x_ref, attn_ref, rw_ref, woa_ref, wor_ref, gf_ref, wup_ref, wdn_ref, gfin_ref,
                y_ref, h_sc, hn_sc, acc_sc):
    j = pl.program_id(1)

    @pl.when(j == 0)
    def _():
        h = x_ref[...] + _dot(attn_ref[...], woa_ref[...]) + _dot(rw_ref[...], wor_ref[...])
        h_sc[...] = h
        hn_sc[...] = _rms(h, gf_ref[...]).astype(BF16)
        acc_sc[...] = jnp.zeros_like(acc_sc)

    u = jnp.maximum(_dot(hn_sc[...], wup_ref[...]), 0.0)
    acc_sc[...] += _dot((u * u).astype(BF16), wdn_ref[...])

    @pl.when(j == pl.num_programs(1) - 1)
    def _():
        y_ref[...] = _rms(h_sc[...] + acc_sc[...], gfin_ref[...])


def _out_ffn(x, attn, rw, woa, wor, gf, wup, wdn, gfin, tm, th=1024):
    m = x.shape[0]
    hid = wup.shape[1]
    row = lambda i, j: (i, 0)
    const = lambda i, j: (0, 0)
    return pl.pallas_call(
        _ffn_kernel,
        grid=(m // tm, hid // th),
        in_specs=[pl.BlockSpec((tm, D_MODEL), row),
                  pl.BlockSpec((tm, WIDTH), row),
                  pl.BlockSpec((tm, WIDTH), row),
                  pl.BlockSpec((WIDTH, D_MODEL), const),
                  pl.BlockSpec((WIDTH, D_MODEL), const),
                  pl.BlockSpec((1, D_MODEL), const),
                  pl.BlockSpec((D_MODEL, th), lambda i, j: (0, j)),
                  pl.BlockSpec((th, D_MODEL), lambda i, j: (j, 0)),
                  pl.BlockSpec((1, D_MODEL), const)],
        out_specs=pl.BlockSpec((tm, D_MODEL), row),
        out_shape=jax.ShapeDtypeStruct((m, D_MODEL), F32),
        scratch_shapes=[pltpu.VMEM((tm, D_MODEL), F32),
                        pltpu.VMEM((tm, D_MODEL), BF16),
                        pltpu.VMEM((tm, D_MODEL), F32)],
        compiler_params=_params("parallel", "arbitrary"),
        name="out_ffn",
    )(x, attn, rw, woa, wor, gf, wup, wdn, gfin)


def _block_rank(gm, axis):
    nb = gm.shape[axis]
    idx = lax.broadcasted_iota(jnp.int32, gm.shape, axis)
    beats = []
    for m in range(nb):
        gmm = lax.slice_in_dim(gm, m, m + 1, axis=axis)
        beats.append(((gmm > gm) | ((gmm == gm) & (m < idx))).astype(jnp.int32))
    while len(beats) > 1:
        beats = [a + b for a, b in zip(beats[0::2], beats[1::2])] + ([beats[-1]] if len(beats) % 2 else [])
    return beats[0]


MOBA_HEADS = 8
BIG = 1e30


def _moba_kernel(slopes_ref, qt_ref, ka_ref, vt_ref, km_ref, o_ref, sel_sc, m_sc, l_sc, acc_sc, *, nb):
    hg = pl.program_id(1)
    qi = pl.program_id(2)
    blk = MOBA_BLOCK
    n_top = min(MOBA_TOP_K, nb)
    keyi = lax.broadcasted_iota(jnp.int32, (blk, blk), 0)
    qryi = lax.broadcasted_iota(jnp.int32, (blk, blk), 1)
    causal = keyi <= qryi
    bidx = lax.broadcasted_iota(jnp.int32, (nb, blk), 0)
    q0 = pl.multiple_of(qi * blk, blk)

    hs = range(MOBA_HEADS)
    qts = [qt_ref[0, hh] for hh in hs]
    gates = [_dot(km_ref[0, hh], qts[hh]) for hh in hs]
    valid = bidx < qi
    ranks = [_block_rank(jnp.where(valid, g, NEG_INF), 0) for g in gates]
    for hh in hs:
        sel_sc[hh] = ((ranks[hh] < n_top) & valid).astype(F32)
    s0 = [jnp.where(causal, _dot(ka_ref[0, hh, pl.ds(q0, blk), :], qts[hh]), NEG_INF) for hh in hs]
    m0 = [jnp.max(t, axis=0, keepdims=True) for t in s0]
    p0 = [jnp.exp(s0[hh] - m0[hh]) for hh in hs]
    for hh in hs:
        m_sc[hh] = m0[hh]
        l_sc[hh] = jnp.sum(p0[hh], axis=0, keepdims=True)
    pv0 = [_dot(vt_ref[0, hh, qi], p0[hh].astype(BF16)) for hh in hs]
    for hh in hs:
        acc_sc[hh] = pv0[hh]

    def past_block(j, carry):
        k0 = pl.multiple_of(j * blk, blk)
        s = [_dot(ka_ref[0, hh, pl.ds(k0, blk), :], qt_ref[0, hh]) for hh in hs]
        ps, alphas = [], []
        for hh in hs:
            cj = -slopes_ref[hg * MOBA_HEADS + hh] * ((qi - j) * blk).astype(F32)
            picked = sel_sc[hh, pl.ds(j, 1), :] > 0.0
            m_old = m_sc[hh]
            m_new = jnp.maximum(m_old, jnp.where(picked, jnp.max(s[hh], axis=0, keepdims=True) + cj, NEG_INF))
            alpha = jnp.exp(m_old - m_new)
            p = jnp.exp(s[hh] - jnp.where(picked, m_new - cj, BIG))
            m_sc[hh] = m_new
            l_sc[hh] = alpha * l_sc[hh] + jnp.sum(p, axis=0, keepdims=True)
            ps.append(p.astype(BF16))
            alphas.append(alpha)
        pv = [_dot(vt_ref[0, hh, j], ps[hh]) for hh in hs]
        for hh in hs:
            acc_sc[hh] = alphas[hh] * acc_sc[hh] + pv[hh]
        return carry

    lax.fori_loop(0, qi, past_block, 0)

    o_ref[...] = jnp.concatenate([(acc_sc[hh] / l_sc[hh]).T for hh in hs], axis=1).astype(o_ref.dtype)


def _moba_prompt(slopes, qt, ka, vt, kmean, batch, seq):
    nb = seq // MOBA_BLOCK
    blk = MOBA_BLOCK
    km = kmean.reshape(batch, nb, N_HEADS, HEAD_DIM).transpose(0, 2, 1, 3).astype(BF16)
    km = jnp.concatenate([km, jnp.zeros((batch, N_HEADS, nb, 128 - HEAD_DIM), BF16)], axis=3)

    hgn = MOBA_HEADS
    grid_spec = pltpu.PrefetchScalarGridSpec(
        num_scalar_prefetch=1,
        grid=(batch, N_HEADS // hgn, nb),
        in_specs=[pl.BlockSpec((1, hgn, 128, blk), lambda b, g, qi, s: (b, g, 0, qi)),
                  pl.BlockSpec((1, hgn, seq, 128), lambda b, g, qi, s: (b, g, 0, 0)),
                  pl.BlockSpec((1, hgn, nb, HEAD_DIM, blk), lambda b, g, qi, s: (b, g, 0, 0, 0)),
                  pl.BlockSpec((1, hgn, nb, 128), lambda b, g, qi, s: (b, g, 0, 0))],
        out_specs=pl.BlockSpec((blk, hgn * HEAD_DIM), lambda b, g, qi, s: (b * nb + qi, g)),
        scratch_shapes=[pltpu.VMEM((hgn, nb, blk), F32),
                        pltpu.VMEM((hgn, 1, blk), F32),
                        pltpu.VMEM((hgn, 1, blk), F32),
                        pltpu.VMEM((hgn, HEAD_DIM, blk), F32)],
    )
    return pl.pallas_call(
        functools.partial(_moba_kernel, nb=nb),
        grid_spec=grid_spec,
        out_shape=jax.ShapeDtypeStruct((batch * seq, WIDTH), BF16),
        compiler_params=_params("parallel", "parallel", "arbitrary"),
        name="moba_prompt",
    )(slopes, qt, ka, vt, km)


def _b(t):
    return t.astype(BF16)


def _split3(x):
    x1 = _b(x)
    r1 = x - x1.astype(F32)
    x2 = _b(r1)
    return x1, x2, _b(r1 - x2.astype(F32))


def _rwkv_pointwise(p, pprev, mu, w0, decay_up, a0, iclr_up, gate_up, k_k, k_a):
    xs = p + mu * (pprev - p)
    r = xs[:, _O_R:_O_R + WIDTH]
    k = xs[:, _O_K:_O_K + WIDTH]
    v = xs[:, _O_V:_O_V + WIDTH]
    xw = xs[:, _O_XW:_O_XW + DECAY_LORA]
    xa = xs[:, _O_XA:_O_XA + AAA_LORA]
    xg = xs[:, _O_XG:_O_XG + GATE_LORA]
    w = w0 + _dot(_b(jnp.tanh(xw)), _b(decay_up))
    w = -jax.nn.softplus(-w) - 0.5
    logdecay = -jnp.exp(w)
    a = jax.nn.sigmoid(a0 + _dot(_b(xa), _b(iclr_up)))
    g = _dot(_b(jax.nn.sigmoid(xg)), _b(gate_up))
    kk = k * k_k
    k2 = k * (1.0 + (a - 1.0) * k_a)
    return r, k2, v, kk, a, g, logdecay


def _head_sum(x):
    row = lax.broadcasted_iota(jnp.int32, (128, 128), 0)
    col = lax.broadcasted_iota(jnp.int32, (128, 128), 1)
    seg = ((row // HEAD_DIM) == (col // HEAD_DIM)).astype(BF16)
    hi = _b(x)
    lo = _b(x - hi.astype(F32))
    cols = [slice(g * 128, (g + 1) * 128) for g in range(x.shape[1] // 128)]
    return jnp.concatenate([_dot(hi[:, c], seg) + _dot(lo[:, c], seg) for c in cols], axis=1)


def _head_norm(kk_h):
    return kk_h * lax.rsqrt(jnp.maximum(jnp.sum(kk_h * kk_h, axis=-1, keepdims=True), 1e-24))


def _group_norm_out(y_h, r_h, k_h, v_h, g_h, rk_h, lnw_h, lnb_h):
    mean = jnp.mean(y_h, axis=-1, keepdims=True)
    var = jnp.mean(jnp.square(y_h - mean), axis=-1, keepdims=True)
    yn = (y_h - mean) * lax.rsqrt(var + GN_EPS) * lnw_h + lnb_h
    yn = yn + jnp.sum(r_h * k_h * rk_h, axis=-1, keepdims=True) * v_h
    return yn * g_h


def _unit_lower_inverse(mats):
    n = mats[0].shape[0]
    row = lax.broadcasted_iota(jnp.int32, (n, n), 0)
    col = lax.broadcasted_iota(jnp.int32, (n, n), 1)
    eye = (row == col).astype(F32)
    size = 16
    same = (row // size) == (col // size)
    pws = [jnp.where(same, a, 0.0) for a in mats]
    xs = [eye - pw for pw in pws]
    for _ in range(3):
        pwb = [_b(pw) for pw in pws]
        pws = [_dot(t, t) for t in pwb]
        xs = [x + _dot(_b(x), _b(pw)) for x, pw in zip(xs, pws)]
        yield
    while size < n:
        size2 = size * 2
        same2 = (row // size2) == (col // size2)
        keep = same2 & jnp.logical_not(same)
        xb = [_b(x) for x in xs]
        ox = [_b(_dot(_b(jnp.where(keep, a, 0.0)), t)) for a, t in zip(mats, xb)]
        xs = [x - _dot(t, o) for x, t, o in zip(xs, xb, ox)]
        same = same2
        size = size2
        yield
    return xs


def _cumsum_rows(x, seg):
    n = x.shape[0]
    row = lax.broadcasted_iota(jnp.int32, (n, n), 0)
    col = lax.broadcasted_iota(jnp.int32, (n, n), 1)
    tri = ((row >= col) & ((row // seg) == (col // seg))).astype(BF16)
    x1, x2, x3 = _split3(x)
    return _dot(tri, x1) + _dot(tri, x2) + _dot(tri, x3)


RWKV_STEP_CHUNKS = 2
_RWKV_BLOCK = RWKV_CHUNK * RWKV_STEP_CHUNKS
_KAPH, _RHAT, _KHAT, _KBAR, _BHAT, _BBAR, _VB = range(7)
_RHAT32, _BONUS, _GATE = range(3)


def _interleave(*gens):
    live = list(gens)
    while live:
        for gen in list(live):
            try:
                next(gen)
            except StopIteration:
                live.remove(gen)


def _rwkv_pointwise_stage(p, last_sc, w, ob_sc, of_sc, we_sc):
    mu, w0, dup, a0, iup, gup, k_k, k_a, r_k = w
    L = RWKV_CHUNK
    nck = RWKV_STEP_CHUNKS
    ts = _RWKV_BLOCK
    rowi = lax.broadcasted_iota(jnp.int32, p.shape, 0)
    pprev = jnp.where(rowi == 0, last_sc[0:1, :], pltpu.roll(p, 1, 0))
    last_sc[0:1, :] = p[ts - 1:ts, :]
    yield
    r, k2, v, kk, a, g, logdecay = _rwkv_pointwise(p, pprev, mu, w0, dup, a0, iup, gup, k_k, k_a)
    yield
    cum = _cumsum_rows(logdecay, L)
    ends = [cum[c * L + L - 1:c * L + L, :] for c in range(nck)]
    cum_end = jnp.concatenate([jnp.broadcast_to(e, (L, WIDTH)) for e in ends], axis=0)
    w_inc = jnp.exp(cum)
    w_exc = jnp.exp(cum - logdecay)
    w_inv = jnp.exp(-cum)
    w_tail = jnp.exp(cum_end - cum)
    for c in range(nck):
        we_sc[c:c + 1, :] = jnp.exp(ends[c])
    yield
    kap_all = kk * lax.rsqrt(jnp.maximum(_head_sum(kk * kk), 1e-24))
    bb_all = kap_all * a
    r_hat_all = r * w_inc
    ob_sc[_KAPH] = _b(kap_all * w_exc)
    ob_sc[_RHAT] = _b(r_hat_all)
    ob_sc[_KHAT] = _b(k2 * w_inv)
    ob_sc[_KBAR] = _b(k2 * w_tail)
    yield
    ob_sc[_BHAT] = _b(bb_all * w_inv)
    ob_sc[_BBAR] = _b(bb_all * w_tail)
    ob_sc[_VB] = _b(v)
    of_sc[_RHAT32] = r_hat_all
    of_sc[_BONUS] = _head_sum(r * k2 * r_k) * v
    of_sc[_GATE] = g


def _rwkv_matmul_stage(ob_sc, of_sc, we_sc, s_sc, y_sc, lnw, lnb, o_ref, rows_out):
    L = RWKV_CHUNK
    nck = RWKV_STEP_CHUNKS
    trow = lax.broadcasted_iota(jnp.int32, (L, L), 0)
    tcol = lax.broadcasted_iota(jnp.int32, (L, L), 1)
    lower_incl = trow >= tcol
    lower_strict = trow > tcol
    heads = range(N_HEADS)
    items = [(slice(c * L, (c + 1) * L), slice(h * HEAD_DIM, (h + 1) * HEAD_DIM))
             for c in range(nck) for h in heads]
    n = range(len(items))
    ld = lambda slot, it: ob_sc[slot, it[0], it[1]]
    lhs = [jnp.concatenate([ld(_KAPH, it), ld(_RHAT, it)], axis=0) for it in items]
    ak = [_dot_nt(lhs[i], ld(_KHAT, items[i])) for i in n]
    ab = [_dot_nt(lhs[i], ld(_BHAT, items[i])) for i in n]
    yield
    a_kr = [_b(jnp.concatenate([jnp.where(lower_strict, t[:L], 0.0), jnp.where(lower_incl, t[L:], 0.0)], axis=0))
            for t in ak]
    a_rb = [_b(jnp.where(lower_incl, t[L:], 0.0)) for t in ab]
    t_inv = yield from _unit_lower_inverse([jnp.where(lower_strict, t[:L], 0.0) for t in ab])
    t_inv = [_b(t) for t in t_inv]
    av = [_dot(a_kr[i], ld(_VB, items[i])) for i in n]
    pm = [_b(_dot(t_inv[i], ld(_KAPH, items[i]))) for i in n]
    yield
    qm = [_b(_dot(t_inv[i], _b(av[i][:L]))) for i in n]
    r_eff = [_b(of_sc[_RHAT32, items[i][0], items[i][1]] - _dot(a_rb[i], pm[i])) for i in n]
    ptb = [_b(_dot_tn(pm[i], ld(_BBAR, items[i]))) for i in n]
    yield
    y0 = [av[i][L:] - _dot(a_rb[i], qm[i]) for i in n]
    cm = [_dot_tn(ld(_VB, items[i]), ld(_KBAR, items[i])) - _dot_tn(qm[i], ld(_BBAR, items[i])) for i in n]
    yield
    state = [s_sc[h] for h in heads]
    for c in range(nck):
        w_end = we_sc[c:c + 1, :]
        sb = [_b(t) for t in state]
        ys = [_dot_nt(r_eff[c * N_HEADS + h], sb[h]) + y0[c * N_HEADS + h] for h in heads]
        state = [state[h] * w_end[:, items[h][1]] - _dot(sb[h], ptb[c * N_HEADS + h]) + cm[c * N_HEADS + h]
                 for h in heads]
        for h in heads:
            y_sc[items[c * N_HEADS + h]] = ys[h]
    for h in heads:
        s_sc[h] = state[h]
    yield
    y = y_sc[...]
    dev = y - _head_sum(y) * (1.0 / HEAD_DIM)
    var = _head_sum(dev * dev) * (1.0 / HEAD_DIM)
    yn = dev * lax.rsqrt(var + GN_EPS) * lnw + lnb
    o_ref[0, rows_out, :] = ((yn + of_sc[_BONUS]) * of_sc[_GATE]).astype(o_ref.dtype)


def _rwkv_chunk_kernel(p_ref, mu_ref, w0_ref, dup_ref, a0_ref, iup_ref, gup_ref, kk_ref, ka_ref,
                       rk_ref, lnw_ref, lnb_ref, o_ref, s_out_ref,
                       s_sc, last_sc, y_sc, xb_sc, xf_sc, xw_sc, yb_sc, yf_sc, yw_sc):
    step = pl.program_id(1)
    last = pl.num_programs(1) - 1
    ts = _RWKV_BLOCK

    @pl.when(step == 0)
    def _():
        s_sc[...] = jnp.zeros_like(s_sc)
        last_sc[...] = jnp.zeros_like(last_sc)
        yb_sc[...] = jnp.zeros_like(yb_sc)
        yf_sc[...] = jnp.zeros_like(yf_sc)
        yw_sc[...] = jnp.zeros_like(yw_sc)

    w = (mu_ref[...], w0_ref[...], dup_ref[...], a0_ref[...], iup_ref[...], gup_ref[...],
         kk_ref[...], ka_ref[...], rk_ref[...])
    lnw, lnb = lnw_ref[...], lnb_ref[...]
    _interleave(_rwkv_matmul_stage(yb_sc, yf_sc, yw_sc, s_sc, y_sc, lnw, lnb, o_ref, slice(0, ts)),
                _rwkv_pointwise_stage(p_ref[0:ts, :], last_sc, w, xb_sc, xf_sc, xw_sc))

    @pl.when(step == last)
    def _():
        s_out_ref[0] = s_sc[...]

    _interleave(_rwkv_matmul_stage(xb_sc, xf_sc, xw_sc, s_sc, y_sc, lnw, lnb, o_ref, slice(ts, 2 * ts)),
                _rwkv_pointwise_stage(p_ref[ts:2 * ts, :], last_sc, w, yb_sc, yf_sc, yw_sc))


def _rwkv_prompt(p_rw, weights, batch, seq):
    ts = _RWKV_BLOCK
    ns = seq // (2 * ts)
    const = lambda b, c: (0, 0)
    w_specs = [pl.BlockSpec(w.shape, const) for w in weights]
    operand_scratch = [pltpu.VMEM((7, ts, WIDTH), BF16), pltpu.VMEM((3, ts, WIDTH), F32), pltpu.VMEM((8, WIDTH), F32)]
    out, state = pl.pallas_call(
        _rwkv_chunk_kernel,
        grid=(batch, ns + 1),
        in_specs=[pl.BlockSpec((2 * ts, RWKV_COLS), lambda b, c: (b * ns + jnp.minimum(c, ns - 1), 0))] + w_specs,
        out_specs=[pl.BlockSpec((1, 2 * ts, WIDTH), lambda b, c: (b, c, 0)),
                   pl.BlockSpec((1, N_HEADS, HEAD_DIM, HEAD_DIM), lambda b, c: (b, 0, 0, 0))],
        out_shape=[jax.ShapeDtypeStruct((batch, seq + 2 * ts, WIDTH), BF16),
                   jax.ShapeDtypeStruct((batch, N_HEADS, HEAD_DIM, HEAD_DIM), F32)],
        scratch_shapes=[pltpu.VMEM((N_HEADS, HEAD_DIM, HEAD_DIM), F32),
                        pltpu.VMEM((8, RWKV_COLS), F32),
                        pltpu.VMEM((ts, WIDTH), F32)] + operand_scratch + operand_scratch,
        compiler_params=_params("parallel", "arbitrary"),
        name="rwkv_prompt",
    )(p_rw, *weights)
    return out[:, ts:ts + seq].reshape(batch * seq, WIDTH), state


def _rwkv_step_kernel(p_ref, sh_ref, s_ref, mu_ref, w0_ref, dup_ref, a0_ref, iup_ref, gup_ref, kk_ref,
                      ka_ref, rk_ref, lnw_ref, lnb_ref, o_ref, s_out_ref):
    p = jnp.broadcast_to(p_ref[0], (8, RWKV_COLS))
    pprev = jnp.broadcast_to(sh_ref[0], (8, RWKV_COLS))
    r, k2, v, kk, a, g, logdecay = (t[0:1] for t in _rwkv_pointwise(
        p, pprev, mu_ref[...], w0_ref[...], dup_ref[...], a0_ref[...], iup_ref[...], gup_ref[...],
        kk_ref[...], ka_ref[...]))
    decay = jnp.exp(logdecay)
    n = HEAD_DIM
    eye = lax.broadcasted_iota(jnp.int32, (n, n), 0) == lax.broadcasted_iota(jnp.int32, (n, n), 1)

    def to_col(row_vec):
        return jnp.sum(jnp.where(eye, row_vec, 0.0), axis=-1, keepdims=True)

    def to_row(col_vec):
        return jnp.sum(jnp.where(eye, col_vec, 0.0), axis=0, keepdims=True)

    for h in range(N_HEADS):
        ln = slice(h * n, (h + 1) * n)
        s0 = s_ref[0, h]
        kap = _head_norm(kk[:, ln])
        sa = jnp.sum(s0 * (-kap), axis=-1, keepdims=True)
        s_new = s0 * decay[:, ln] + sa * (kap * a[:, ln]) + to_col(v[:, ln]) * k2[:, ln]
        s_out_ref[0, h] = s_new
        y = to_row(jnp.sum(s_new * r[:, ln], axis=-1, keepdims=True))
        o_ref[0, :, ln] = _group_norm_out(y, r[:, ln], k2[:, ln], v[:, ln], g[:, ln], rk_ref[:, ln],
                                          lnw_ref[:, ln], lnb_ref[:, ln]).astype(o_ref.dtype)


def _rwkv_sample(p_rw, shift, state, weights):
    n = p_rw.shape[0]
    const = lambda s: (0, 0)
    vec = pl.BlockSpec((1, 1, RWKV_COLS), lambda s: (s, 0, 0))
    st = pl.BlockSpec((1, N_HEADS, HEAD_DIM, HEAD_DIM), lambda s: (s, 0, 0, 0))
    rw, s_new = pl.pallas_call(
        _rwkv_step_kernel,
        grid=(n,),
        in_specs=[vec, vec, st] + [pl.BlockSpec(w.shape, const) for w in weights],
        out_specs=[pl.BlockSpec((1, 1, WIDTH), lambda s: (s, 0, 0)), st],
        out_shape=[jax.ShapeDtypeStruct((n, 1, WIDTH), BF16),
                   jax.ShapeDtypeStruct(state.shape, F32)],
        compiler_params=_params("parallel"),
        name="rwkv_sample",
    )(p_rw.reshape(n, 1, RWKV_COLS), shift.reshape(n, 1, RWKV_COLS), state, *weights)
    return rw.reshape(n, WIDTH), s_new


_PAGES_PER_BLOCK = MOBA_BLOCK // PAGE_SIZE
_SELECT_PAGES = 16


def _sample_select_kernel(pt_ref, q_ref, *refs):
    page_refs = refs[:_SELECT_PAGES]
    idx_ref = refs[_SELECT_PAGES]
    km_sc = refs[_SELECT_PAGES + 1]
    g = pl.program_id(1)
    ppb = _PAGES_PER_BLOCK
    bps = _SELECT_PAGES // ppb
    ones = jnp.ones((8, PAGE_SIZE), BF16)
    tots = []
    for b in range(bps):
        tot = page_refs[b * ppb][0]
        for i in range(1, ppb):
            tot = tot + page_refs[b * ppb + i][0]
        tots.append(tot.reshape(WIDTH, PAGE_SIZE))
    pieces = [_split3(t) for t in tots]
    ksums = [_dot_nt(ones, pc[0]) + _dot_nt(ones, pc[1]) + _dot_nt(ones, pc[2]) for pc in pieces]
    for b in range(bps):
        km_sc[pl.ds(g * bps + b, 1), :] = ksums[b][0:1] * (1.0 / MOBA_BLOCK)

    @pl.when(g == pl.num_programs(1) - 1)
    def _():
        rnd = lambda t: t.astype(BF16).astype(F32)
        prod = rnd(km_sc[...]) * rnd(q_ref[0])
        lane = lax.broadcasted_iota(jnp.int32, (WIDTH, 128), 0)
        hcol = lax.broadcasted_iota(jnp.int32, (WIDTH, 128), 1)
        head_sum = ((lane // HEAD_DIM) == hcol).astype(BF16)
        pieces = _split3(prod)
        gate = _dot(pieces[0], head_sum) + _dot(pieces[1], head_sum) + _dot(pieces[2], head_sum)
        rank = _block_rank(gate, 0)
        bidx = lax.broadcasted_iota(jnp.int32, gate.shape, 0)
        rows = [jnp.sum(jnp.where(rank == i, bidx, 0), axis=0, keepdims=True) for i in range(MOBA_TOP_K)]
        rows += [jnp.zeros((1, 128), jnp.int32)] * (8 - MOBA_TOP_K)
        idx_ref[0] = jnp.concatenate(rows, axis=0)


def _sample_select(page_table, q, cache_kt):
    n, n_pages = page_table.shape
    nb = n_pages // _PAGES_PER_BLOCK
    steps = n_pages // _SELECT_PAGES

    def page_spec(i):
        return pl.BlockSpec((1, N_HEADS, HEAD_DIM, PAGE_SIZE),
                            lambda s, g, pt, i=i: (pt[s * n_pages + g * _SELECT_PAGES + i], 0, 0, 0))

    grid_spec = pltpu.PrefetchScalarGridSpec(
        num_scalar_prefetch=1,
        grid=(n, steps),
        in_specs=[pl.BlockSpec((1, 1, WIDTH), lambda s, g, pt: (s, 0, 0))]
        + [page_spec(i) for i in range(_SELECT_PAGES)],
        out_specs=pl.BlockSpec((1, 8, 128), lambda s, g, pt: (s, 0, 0)),
        scratch_shapes=[pltpu.VMEM((nb, WIDTH), F32)],
    )
    return pl.pallas_call(
        _sample_select_kernel,
        grid_spec=grid_spec,
        out_shape=jax.ShapeDtypeStruct((n, 8, 128), jnp.int32),
        compiler_params=_params("parallel", "arbitrary"),
        name="sample_select",
    )(page_table.reshape(-1), q.reshape(n, 1, WIDTH), *([cache_kt] * _SELECT_PAGES))


def _sample_attn_kernel(pt_ref, top_ref, q_ref, kn_ref, vn_ref, topv_ref, slope_ref, ck_hbm, cv_hbm, o_ref,
                        kbuf, vbuf, sems, *, past_len, n_pages):
    ppb = _PAGES_PER_BLOCK
    npg = MOBA_TOP_K * ppb
    seq = pl.program_id(0)
    nseq = pl.num_programs(0)
    slot = seq % 2

    def slab_copies(sq, sl, h, i):
        blk_id = top_ref[(sq * N_HEADS + h) * MOBA_TOP_K + i // ppb]
        page = pt_ref[sq * n_pages + blk_id * ppb + i % ppb]
        return (pltpu.make_async_copy(ck_hbm.at[page, h], kbuf.at[sl, h, i], sems.at[sl]),
                pltpu.make_async_copy(cv_hbm.at[page, h], vbuf.at[sl, h, i], sems.at[sl]))

    def start_all(sq, sl):
        def per_head(h, carry):
            for i in range(npg):
                for cp in slab_copies(sq, sl, h, i):
                    cp.start()
            return carry
        lax.fori_loop(0, N_HEADS, per_head, 0)

    @pl.when(seq == 0)
    def _():
        start_all(0, 0)

    @pl.when(seq + 1 < nseq)
    def _():
        start_all(seq + 1, 1 - slot)

    def wait_head(h, carry):
        for i in range(npg):
            for cp in slab_copies(seq, slot, h, i):
                cp.wait()
        return carry
    lax.fori_loop(0, N_HEADS, wait_head, 0)

    q = q_ref[0] * (HEAD_DIM ** -0.5)
    slope = slope_ref[...]
    lane = lax.broadcasted_iota(jnp.int32, (1, 1, PAGE_SIZE), 2)
    scores = []
    for i in range(npg):
        blk_id = topv_ref[0, i // ppb][:, :, None]
        dist = (past_len - blk_id * MOBA_BLOCK - (i % ppb) * PAGE_SIZE - lane).astype(F32)
        s = jnp.sum(kbuf[slot, :, i] * q, axis=1, keepdims=True)
        scores.append(s - slope * dist)
    s_self = jnp.sum(q * kn_ref[0], axis=1, keepdims=True)
    m = s_self
    for s in scores:
        m = jnp.maximum(m, jnp.max(s, axis=2, keepdims=True))
    p_self = jnp.exp(s_self - m)
    den = p_self
    accv = jnp.zeros((N_HEADS, HEAD_DIM, PAGE_SIZE), F32)
    for i, s in enumerate(scores):
        p = jnp.exp(s - m)
        den = den + jnp.sum(p, axis=2, keepdims=True)
        accv = accv + vbuf[slot, :, i] * p
    acc = jnp.sum(accv, axis=2, keepdims=True) + p_self * vn_ref[0]
    o_ref[0] = acc / den


def _sample_attn(page_table, top, slopes, q, k_new, v_new, cache_kt, cache_vt):
    n, n_pages = page_table.shape
    npg = MOBA_TOP_K * _PAGES_PER_BLOCK
    top_flat = top.transpose(0, 2, 1).reshape(-1)
    vec = pl.BlockSpec((1, N_HEADS, HEAD_DIM, 1), lambda s, pt, tp: (s, 0, 0, 0))
    grid_spec = pltpu.PrefetchScalarGridSpec(
        num_scalar_prefetch=2,
        grid=(n,),
        in_specs=[vec, vec, vec,
                  pl.BlockSpec((1, MOBA_TOP_K, N_HEADS, 1), lambda s, pt, tp: (s, 0, 0, 0)),
                  pl.BlockSpec((N_HEADS, 1, 1), lambda s, pt, tp: (0, 0, 0)),
                  pl.BlockSpec(memory_space=pl.ANY),
                  pl.BlockSpec(memory_space=pl.ANY)],
        out_specs=vec,
        scratch_shapes=[pltpu.VMEM((2, N_HEADS, npg, HEAD_DIM, PAGE_SIZE), F32),
                        pltpu.VMEM((2, N_HEADS, npg, HEAD_DIM, PAGE_SIZE), F32),
                        pltpu.SemaphoreType.DMA((2,))],
    )
    return pl.pallas_call(
        functools.partial(_sample_attn_kernel, past_len=n_pages * PAGE_SIZE, n_pages=n_pages),
        grid_spec=grid_spec,
        out_shape=jax.ShapeDtypeStruct((n, N_HEADS, HEAD_DIM, 1), F32),
        compiler_params=_params("arbitrary"),
        name="sample_attn",
    )(page_table.reshape(-1), top_flat, q, k_new, v_new, top[..., None], slopes.reshape(N_HEADS, 1, 1),
      cache_kt, cache_vt)


def kernel(x_prompt, x_sample, cache_k, cache_v, page_table, state_wkv, state_shift,
           norm_mix_g, w_in, mu_shift, decay_w0, decay_up, iclr_a0, iclr_up, gate_up,
           k_k, k_a, r_k, ln_x_w, ln_x_b, w_out, norm_ffn_g, w_ffn_up, w_ffn_down, norm_final_g):
    depth = w_in.shape[0]
    assert depth == 1
    batch, seq, _ = x_prompt.shape
    n_seq, n_pages = page_table.shape
    slopes = jnp.exp2(-8.0 * jnp.arange(1, N_HEADS + 1, dtype=F32) / N_HEADS)

    l = 0
    row = lambda t: t.reshape(1, -1)
    w_in_l = jnp.concatenate([w_in[l][:, :ATTN_COLS], _to_internal(w_in[l][:, ATTN_COLS:])], axis=1)
    g_mix = row(norm_mix_g[l])
    rw_weights = (row(_to_internal(mu_shift[l])), row(decay_w0[l]), decay_up[l], row(iclr_a0[l]), iclr_up[l],
                  gate_up[l], row(k_k[l]), row(k_a[l]), row(r_k[l]), row(ln_x_w[l]), row(ln_x_b[l]))
    wo = w_out[l].astype(BF16)
    ffn_weights = (wo[:WIDTH], wo[WIDTH:], row(norm_ffn_g[l]), w_ffn_up[l].astype(BF16),
                   w_ffn_down[l].astype(BF16), row(norm_final_g))

    xp = x_prompt.reshape(batch * seq, D_MODEL)
    w_in_b = w_in_l.astype(BF16)
    kt, vt, qt, ka, vtb, prw, kmean = _inproj_prompt(xp, g_mix, w_in_b, slopes, batch, seq)
    nb = seq // MOBA_BLOCK
    attn_p = _moba_prompt(slopes, qt, ka, vtb, kmean.reshape(batch, nb, WIDTH), batch, seq)
    rw_p, wkv_p = _rwkv_prompt(prw, rw_weights, batch, seq)
    y_prompt = _out_ffn(xp, attn_p, rw_p, *ffn_weights, tm=512).reshape(batch, seq, D_MODEL)
    shift_p = _from_internal(prw.reshape(batch, seq, RWKV_COLS)[:, -1])

    hd = (N_HEADS, HEAD_DIM)
    xs = x_sample.reshape(n_seq, D_MODEL)
    proj_s = _inproj_small(xs, g_mix, w_in_b)
    q_s, k_s, v_s = proj_s[:, :WIDTH], proj_s[:, WIDTH:2 * WIDTH], proj_s[:, 2 * WIDTH:ATTN_COLS]
    prw_s = proj_s[:, ATTN_COLS:]
    ckt = cache_k.transpose(0, 1, 3, 4, 2).reshape(-1, N_HEADS, HEAD_DIM, PAGE_SIZE)
    cvt = cache_v.transpose(0, 1, 3, 4, 2).reshape(-1, N_HEADS, HEAD_DIM, PAGE_SIZE)
    top = _sample_select(page_table, q_s, ckt)[:, :MOBA_TOP_K, :N_HEADS]
    col = lambda t: t.reshape(n_seq, N_HEADS, HEAD_DIM, 1)
    attn_s = _sample_attn(page_table, top, slopes, col(q_s), col(k_s), col(v_s), ckt, cvt).reshape(n_seq, WIDTH)
    rw_s, wkv_s = _rwkv_sample(prw_s, _to_internal(state_shift[l]), state_wkv[l], rw_weights)
    y_sample = _out_ffn(xs, attn_s.astype(BF16), rw_s, *ffn_weights, tm=n_seq).reshape(n_seq, 1, D_MODEL)
    shift_s = _from_internal(prw_s)

    return (y_prompt, y_sample,
            kt.transpose(0, 3, 1, 2)[None], vt.transpose(0, 3, 1, 2)[None],
            wkv_p[None], shift_p[None],
            k_s.reshape(1, n_seq, 1, *hd), v_s.reshape(1, n_seq, 1, *hd),
            wkv_s[None], shift_s[None])
```

```python
import functools

import jax
import jax.numpy as jnp
from jax import lax
from jax.experimental import pallas as pl
from jax.experimental.pallas import tpu as pltpu

F32 = jnp.float32
BF16 = jnp.bfloat16

D_MODEL = 1024
HEAD_DIM = 64
N_HEADS = 8
WIDTH = N_HEADS * HEAD_DIM
MOBA_BLOCK = 256
MOBA_TOP_K = 3
DECAY_LORA = 64
AAA_LORA = 64
GATE_LORA = 128
ATTN_COLS = 3 * WIDTH
RWKV_COLS = 3 * WIDTH + DECAY_LORA + AAA_LORA + GATE_LORA
RMS_EPS = 1e-6
GN_EPS = 64e-5
NEG_INF = -1e30
PAGE_SIZE = 128
RWKV_CHUNK = 64
VMEM_LIMIT = 48 * 1024 * 1024

_O_R, _O_K, _O_V = 0, WIDTH, 2 * WIDTH
_O_XW = 3 * WIDTH
_O_XA = _O_XW + DECAY_LORA
_O_XG = _O_XA + AAA_LORA


def _to_internal(t):
    r, xw, kv, rest = (t[..., :WIDTH], t[..., WIDTH:WIDTH + DECAY_LORA],
                       t[..., WIDTH + DECAY_LORA:3 * WIDTH + DECAY_LORA], t[..., 3 * WIDTH + DECAY_LORA:])
    return jnp.concatenate([r, kv, xw, rest], axis=-1)


def _from_internal(t):
    r, kv, xw, rest = (t[..., :WIDTH], t[..., WIDTH:3 * WIDTH],
                       t[..., 3 * WIDTH:3 * WIDTH + DECAY_LORA], t[..., 3 * WIDTH + DECAY_LORA:])
    return jnp.concatenate([r, xw, kv, rest], axis=-1)


def _params(*sem):
    return pltpu.CompilerParams(dimension_semantics=sem, vmem_limit_bytes=VMEM_LIMIT)


def _rms(x, g):
    return x * lax.rsqrt(jnp.mean(x * x, axis=-1, keepdims=True) + RMS_EPS) * g


def _dot(a, b, **kw):
    return jnp.dot(a, b, preferred_element_type=F32, **kw)


def _dot_nt(a, b, **kw):
    return lax.dot_general(a, b, (((1,), (1,)), ((), ())), preferred_element_type=F32, **kw)


def _dot_tn(a, b, **kw):
    return lax.dot_general(a, b, (((0,), (0,)), ((), ())), preferred_element_type=F32, **kw)


def _inproj_kernel(x_ref, g_ref, w_ref, srow_ref, kt_ref, vt_ref, qt_ref, ka_ref, vtb_ref, prw_ref, km_ref):
    tm = x_ref.shape[0]
    hd = (N_HEADS, HEAD_DIM, tm)
    xn = _rms(x_ref[...], g_ref[...]).astype(BF16)
    proj = _dot(xn, w_ref[...])
    q = proj[:, 0:WIDTH] * (HEAD_DIM ** -0.5)
    k = proj[:, WIDTH:2 * WIDTH]
    v = proj[:, 2 * WIDTH:3 * WIDTH]
    prw_ref[...] = proj[:, ATTN_COLS:]
    km_ref[0] = jnp.mean(k, axis=0, keepdims=True)
    k_t = k.T.reshape(hd)
    v_t = v.T.reshape(hd)
    kt_ref[0] = k_t
    vt_ref[0] = v_t
    vtb_ref[0, :, 0] = v_t.astype(BF16)
    qt_ref[0, :, 0:HEAD_DIM, :] = q.T.reshape(hd).astype(BF16)
    qt_ref[0, :, HEAD_DIM:, :] = jnp.broadcast_to(srow_ref[...], hd).astype(BF16)
    lane = lax.broadcasted_iota(jnp.int32, (tm, 128 - HEAD_DIM), 1)
    pos = lax.broadcasted_iota(jnp.int32, (tm, 128 - HEAD_DIM), 0)
    pos_cols = jnp.where(lane == 0, pos, 0).astype(F32).astype(BF16)
    for h in range(N_HEADS):
        ka_ref[0, h, :, 0:HEAD_DIM] = k[:, h * HEAD_DIM:(h + 1) * HEAD_DIM].astype(BF16)
        ka_ref[0, h, :, HEAD_DIM:] = pos_cols


def _inproj_prompt(x, g, w_bf16, slopes, batch, seq):
    tm = MOBA_BLOCK
    m = x.shape[0]
    nb = seq // tm
    ncol = w_bf16.shape[1]
    row = lambda i: (i, 0)
    const = lambda i: (0, 0)
    tok = lambda i: (i // nb, 0, 0, i % nb)
    srow = jnp.zeros((N_HEADS, HEAD_DIM, 1), F32).at[:, 0, 0].set(slopes)
    return pl.pallas_call(
        _inproj_kernel,
        grid=(m // tm,),
        in_specs=[pl.BlockSpec((tm, D_MODEL), row),
                  pl.BlockSpec((1, D_MODEL), const),
                  pl.BlockSpec((D_MODEL, ncol), const),
                  pl.BlockSpec((N_HEADS, HEAD_DIM, 1), lambda i: (0, 0, 0))],
        out_specs=[pl.BlockSpec((1, N_HEADS, HEAD_DIM, tm), tok),
                   pl.BlockSpec((1, N_HEADS, HEAD_DIM, tm), tok),
                   pl.BlockSpec((1, N_HEADS, 128, tm), tok),
                   pl.BlockSpec((1, N_HEADS, tm, 128), lambda i: (i // nb, 0, i % nb, 0)),
                   pl.BlockSpec((1, N_HEADS, 1, HEAD_DIM, tm), lambda i: (i // nb, 0, i % nb, 0, 0)),
                   pl.BlockSpec((tm, RWKV_COLS), row),
                   pl.BlockSpec((1, 1, WIDTH), lambda i: (i, 0, 0))],
        out_shape=[jax.ShapeDtypeStruct((batch, N_HEADS, HEAD_DIM, seq), F32),
                   jax.ShapeDtypeStruct((batch, N_HEADS, HEAD_DIM, seq), F32),
                   jax.ShapeDtypeStruct((batch, N_HEADS, 128, seq), BF16),
                   jax.ShapeDtypeStruct((batch, N_HEADS, seq, 128), BF16),
                   jax.ShapeDtypeStruct((batch, N_HEADS, nb, HEAD_DIM, tm), BF16),
                   jax.ShapeDtypeStruct((m, RWKV_COLS), F32),
                   jax.ShapeDtypeStruct((m // tm, 1, WIDTH), F32)],
        compiler_params=_params("parallel"),
        name="inproj_prompt",
    )(x, g, w_bf16, srow)


def _inproj_small_kernel(x_ref, g_ref, w_ref, o_ref):
    xn = _rms(x_ref[...], g_ref[...]).astype(BF16)
    o_ref[...] = _dot(xn, w_ref[...])


def _inproj_small(x, g, w, tn=256):
    m = x.shape[0]
    ncol = w.shape[1]
    return pl.pallas_call(
        _inproj_small_kernel,
        grid=(ncol // tn,),
        in_specs=[pl.BlockSpec((m, D_MODEL), lambda j: (0, 0)),
                  pl.BlockSpec((1, D_MODEL), lambda j: (0, 0)),
                  pl.BlockSpec((D_MODEL, tn), lambda j: (0, j))],
        out_specs=pl.BlockSpec((m, tn), lambda j: (0, j)),
        out_shape=jax.ShapeDtypeStruct((m, ncol), F32),
        compiler_params=_params("parallel"),
        name="inproj_sample",
    )(x, g, w)


_STREAM_PAGES = 16
_PAGES_PER_BLOCK = MOBA_BLOCK // PAGE_SIZE


def _ffn_kernel(pt_ref, x_ref, attn_ref, rw_ref, woa_ref, wor_ref, gf_ref, wup_ref, wdn_ref, gfin_ref, *rest,
                n_stream):
    page_refs = rest[:n_stream]
    y_ref = rest[n_stream]
    km_ref = rest[n_stream + 1] if n_stream else None
    h_sc, hn_sc, acc_sc = rest[-3:]
    j = pl.program_id(1)

    @pl.when(j == 0)
    def _():
        h = x_ref[...] + _dot(attn_ref[...], woa_ref[...]) + _dot(rw_ref[...], wor_ref[...])
        h_sc[...] = h
        hn_sc[...] = _rms(h, gf_ref[...]).astype(BF16)
        acc_sc[...] = jnp.zeros_like(acc_sc)

    u = jnp.maximum(_dot(hn_sc[...], wup_ref[...]), 0.0)
    acc_sc[...] += _dot((u * u).astype(BF16), wdn_ref[...])

    if n_stream:
        ppb = _PAGES_PER_BLOCK
        bps = n_stream // ppb
        nb = km_ref.shape[1]
        g = (pl.program_id(0) * pl.num_programs(1) + j) % (nb // bps)
        for b in range(bps):
            tot = page_refs[b * ppb][0]
            for i in range(1, ppb):
                tot = tot + page_refs[b * ppb + i][0]
            tot_t = tot.reshape(WIDTH, PAGE_SIZE).T
            km_ref[0, pl.ds(g * bps + b, 1), :] = jnp.sum(tot_t, axis=0, keepdims=True) * (1.0 / MOBA_BLOCK)

    @pl.when(j == pl.num_programs(1) - 1)
    def _():
        y_ref[...] = _rms(h_sc[...] + acc_sc[...], gfin_ref[...])


def _out_ffn(x, attn, rw, woa, wor, gf, wup, wdn, gfin, tm, th=1024, stream=None):
    m = x.shape[0]
    hid = wup.shape[1]
    nj = hid // th
    row = lambda i, j, pt: (i, 0)
    const = lambda i, j, pt: (0, 0)
    in_specs = [pl.BlockSpec((tm, D_MODEL), row),
                pl.BlockSpec((tm, WIDTH), row),
                pl.BlockSpec((tm, WIDTH), row),
                pl.BlockSpec((WIDTH, D_MODEL), const),
                pl.BlockSpec((WIDTH, D_MODEL), const),
                pl.BlockSpec((1, D_MODEL), const),
                pl.BlockSpec((D_MODEL, th), lambda i, j, pt: (0, j)),
                pl.BlockSpec((th, D_MODEL), lambda i, j, pt: (j, 0)),
                pl.BlockSpec((1, D_MODEL), const)]
    out_specs = [pl.BlockSpec((tm, D_MODEL), row)]
    out_shape = [jax.ShapeDtypeStruct((m, D_MODEL), F32)]
    operands = [x, attn, rw, woa, wor, gf, wup, wdn, gfin]
    n_stream = 0
    pt_flat = jnp.zeros((1,), jnp.int32)
    if stream is not None:
        page_table, cache_kt = stream
        n_seq, n_pages = page_table.shape
        n_stream = _STREAM_PAGES
        spp = n_pages // n_stream
        assert (m // tm) * nj == n_seq * spp, "page streaming needs one grid step per 16 pages"
        nb = n_pages // _PAGES_PER_BLOCK
        pt_flat = page_table.reshape(-1)

        def page_spec(k):
            def imap(i, j, pt, k=k):
                t = i * nj + j
                return (pt[(t // spp) * n_pages + (t % spp) * n_stream + k], 0, 0, 0)
            return pl.BlockSpec((1, N_HEADS, HEAD_DIM, PAGE_SIZE), imap)

        in_specs += [page_spec(k) for k in range(n_stream)]
        operands += [cache_kt] * n_stream
        out_specs.append(pl.BlockSpec((1, nb, WIDTH), lambda i, j, pt: ((i * nj + j) // spp, 0, 0)))
        out_shape.append(jax.ShapeDtypeStruct((n_seq, nb, WIDTH), F32))
    grid_spec = pltpu.PrefetchScalarGridSpec(
        num_scalar_prefetch=1,
        grid=(m // tm, nj),
        in_specs=in_specs,
        out_specs=out_specs,
        scratch_shapes=[pltpu.VMEM((tm, D_MODEL), F32),
                        pltpu.VMEM((tm, D_MODEL), BF16),
                        pltpu.VMEM((tm, D_MODEL), F32)],
    )
    outs = pl.pallas_call(
        functools.partial(_ffn_kernel, n_stream=n_stream),
        grid_spec=grid_spec,
        out_shape=out_shape,
        compiler_params=_params("arbitrary", "arbitrary"),
        name="out_ffn",
    )(pt_flat, *operands)
    return outs if stream is not None else outs[0]


def _block_rank(gm, axis):
    nb = gm.shape[axis]
    idx = lax.broadcasted_iota(jnp.int32, gm.shape, axis)
    beats = []
    for m in range(nb):
        gmm = lax.slice_in_dim(gm, m, m + 1, axis=axis)
        beats.append(((gmm > gm) | ((gmm == gm) & (m < idx))).astype(jnp.int32))
    while len(beats) > 1:
        beats = [a + b for a, b in zip(beats[0::2], beats[1::2])] + ([beats[-1]] if len(beats) % 2 else [])
    return beats[0]


MOBA_HEADS = 8
BIG = 1e30


def _moba_kernel(slopes_ref, qt_ref, ka_ref, vt_ref, km_ref, o_ref, sel_sc, m_sc, l_sc, acc_sc, *, nb):
    hg = pl.program_id(1)
    qi = pl.program_id(2)
    blk = MOBA_BLOCK
    n_top = min(MOBA_TOP_K, nb)
    keyi = lax.broadcasted_iota(jnp.int32, (blk, blk), 0)
    qryi = lax.broadcasted_iota(jnp.int32, (blk, blk), 1)
    causal = keyi <= qryi
    bidx = lax.broadcasted_iota(jnp.int32, (nb, blk), 0)
    q0 = pl.multiple_of(qi * blk, blk)

    hs = range(MOBA_HEADS)
    qts = [qt_ref[0, hh] for hh in hs]
    gates = [_dot(km_ref[0, hh], qts[hh]) for hh in hs]
    valid = bidx < qi
    ranks = [_block_rank(jnp.where(valid, g, NEG_INF), 0) for g in gates]
    for hh in hs:
        sel_sc[hh] = ((ranks[hh] < n_top) & valid).astype(F32)
    s0 = [jnp.where(causal, _dot(ka_ref[0, hh, pl.ds(q0, blk), :], qts[hh]), NEG_INF) for hh in hs]
    m0 = [jnp.max(t, axis=0, keepdims=True) for t in s0]
    p0 = [jnp.exp(s0[hh] - m0[hh]) for hh in hs]
    for hh in hs:
        m_sc[hh] = m0[hh]
        l_sc[hh] = jnp.sum(p0[hh], axis=0, keepdims=True)
    pv0 = [_dot(vt_ref[0, hh, qi], p0[hh].astype(BF16)) for hh in hs]
    for hh in hs:
        acc_sc[hh] = pv0[hh]

    def past_block(j, carry):
        k0 = pl.multiple_of(j * blk, blk)
        s = [_dot(ka_ref[0, hh, pl.ds(k0, blk), :], qt_ref[0, hh]) for hh in hs]
        ps, alphas = [], []
        for hh in hs:
            cj = -slopes_ref[hg * MOBA_HEADS + hh] * ((qi - j) * blk).astype(F32)
            picked = sel_sc[hh, pl.ds(j, 1), :] > 0.0
            m_old = m_sc[hh]
            m_new = jnp.maximum(m_old, jnp.where(picked, jnp.max(s[hh], axis=0, keepdims=True) + cj, NEG_INF))
            alpha = jnp.exp(m_old - m_new)
            p = jnp.exp(s[hh] - jnp.where(picked, m_new - cj, BIG))
            m_sc[hh] = m_new
            l_sc[hh] = alpha * l_sc[hh] + jnp.sum(p, axis=0, keepdims=True)
            ps.append(p.astype(BF16))
            alphas.append(alpha)
        pv = [_dot(vt_ref[0, hh, j], ps[hh]) for hh in hs]
        for hh in hs:
            acc_sc[hh] = alphas[hh] * acc_sc[hh] + pv[hh]
        return carry

    lax.fori_loop(0, qi, past_block, 0)

    o_ref[...] = jnp.concatenate([(acc_sc[hh] / l_sc[hh]).T for hh in hs], axis=1).astype(o_ref.dtype)


def _moba_prompt(slopes, qt, ka, vt, kmean, batch, seq):
    nb = seq // MOBA_BLOCK
    blk = MOBA_BLOCK
    km = kmean.reshape(batch, nb, N_HEADS, HEAD_DIM).transpose(0, 2, 1, 3).astype(BF16)
    km = jnp.concatenate([km, jnp.zeros((batch, N_HEADS, nb, 128 - HEAD_DIM), BF16)], axis=3)

    hgn = MOBA_HEADS
    grid_spec = pltpu.PrefetchScalarGridSpec(
        num_scalar_prefetch=1,
        grid=(batch, N_HEADS // hgn, nb),
        in_specs=[pl.BlockSpec((1, hgn, 128, blk), lambda b, g, qi, s: (b, g, 0, qi)),
                  pl.BlockSpec((1, hgn, seq, 128), lambda b, g, qi, s: (b, g, 0, 0)),
                  pl.BlockSpec((1, hgn, nb, HEAD_DIM, blk), lambda b, g, qi, s: (b, g, 0, 0, 0)),
                  pl.BlockSpec((1, hgn, nb, 128), lambda b, g, qi, s: (b, g, 0, 0))],
        out_specs=pl.BlockSpec((blk, hgn * HEAD_DIM), lambda b, g, qi, s: (b * nb + qi, g)),
        scratch_shapes=[pltpu.VMEM((hgn, nb, blk), F32),
                        pltpu.VMEM((hgn, 1, blk), F32),
                        pltpu.VMEM((hgn, 1, blk), F32),
                        pltpu.VMEM((hgn, HEAD_DIM, blk), F32)],
    )
    return pl.pallas_call(
        functools.partial(_moba_kernel, nb=nb),
        grid_spec=grid_spec,
        out_shape=jax.ShapeDtypeStruct((batch * seq, WIDTH), BF16),
        compiler_params=_params("parallel", "parallel", "arbitrary"),
        name="moba_prompt",
    )(slopes, qt, ka, vt, km)


def _b(t):
    return t.astype(BF16)


def _split3(x):
    x1 = _b(x)
    r1 = x - x1.astype(F32)
    x2 = _b(r1)
    return x1, x2, _b(r1 - x2.astype(F32))


def _rwkv_pointwise(p, pprev, mu, w0, decay_up, a0, iclr_up, gate_up, k_k, k_a):
    xs = p + mu * (pprev - p)
    r = xs[:, _O_R:_O_R + WIDTH]
    k = xs[:, _O_K:_O_K + WIDTH]
    v = xs[:, _O_V:_O_V + WIDTH]
    xw = xs[:, _O_XW:_O_XW + DECAY_LORA]
    xa = xs[:, _O_XA:_O_XA + AAA_LORA]
    xg = xs[:, _O_XG:_O_XG + GATE_LORA]
    w = w0 + _dot(_b(jnp.tanh(xw)), _b(decay_up))
    w = -jax.nn.softplus(-w) - 0.5
    logdecay = -jnp.exp(w)
    a = jax.nn.sigmoid(a0 + _dot(_b(xa), _b(iclr_up)))
    g = _dot(_b(jax.nn.sigmoid(xg)), _b(gate_up))
    kk = k * k_k
    k2 = k * (1.0 + (a - 1.0) * k_a)
    return r, k2, v, kk, a, g, logdecay


def _head_sum(x):
    row = lax.broadcasted_iota(jnp.int32, (128, 128), 0)
    col = lax.broadcasted_iota(jnp.int32, (128, 128), 1)
    seg = ((row // HEAD_DIM) == (col // HEAD_DIM)).astype(BF16)
    hi = _b(x)
    lo = _b(x - hi.astype(F32))
    cols = [slice(g * 128, (g + 1) * 128) for g in range(x.shape[1] // 128)]
    return jnp.concatenate([_dot(hi[:, c], seg) + _dot(lo[:, c], seg) for c in cols], axis=1)


def _head_norm(kk_h):
    return kk_h * lax.rsqrt(jnp.maximum(jnp.sum(kk_h * kk_h, axis=-1, keepdims=True), 1e-24))


def _group_norm_out(y_h, r_h, k_h, v_h, g_h, rk_h, lnw_h, lnb_h):
    mean = jnp.mean(y_h, axis=-1, keepdims=True)
    var = jnp.mean(jnp.square(y_h - mean), axis=-1, keepdims=True)
    yn = (y_h - mean) * lax.rsqrt(var + GN_EPS) * lnw_h + lnb_h
    yn = yn + jnp.sum(r_h * k_h * rk_h, axis=-1, keepdims=True) * v_h
    return yn * g_h


def _unit_lower_inverse(mats):
    n = mats[0].shape[0]
    row = lax.broadcasted_iota(jnp.int32, (n, n), 0)
    col = lax.broadcasted_iota(jnp.int32, (n, n), 1)
    eye = (row == col).astype(F32)
    size = 16
    same = (row // size) == (col // size)
    pws = [jnp.where(same, a, 0.0) for a in mats]
    xs = [eye - pw for pw in pws]
    for _ in range(3):
        pwb = [_b(pw) for pw in pws]
        pws = [_dot(t, t) for t in pwb]
        xs = [x + _dot(_b(x), _b(pw)) for x, pw in zip(xs, pws)]
        yield
    while size < n:
        size2 = size * 2
        same2 = (row // size2) == (col // size2)
        keep = same2 & jnp.logical_not(same)
        xb = [_b(x) for x in xs]
        ox = [_b(_dot(_b(jnp.where(keep, a, 0.0)), t)) for a, t in zip(mats, xb)]
        xs = [x - _dot(t, o) for x, t, o in zip(xs, xb, ox)]
        same = same2
        size = size2
        yield
    return xs


def _cumsum_rows(x, seg):
    n = x.shape[0]
    row = lax.broadcasted_iota(jnp.int32, (n, n), 0)
    col = lax.broadcasted_iota(jnp.int32, (n, n), 1)
    tri = ((row >= col) & ((row // seg) == (col // seg))).astype(BF16)
    x1, x2, x3 = _split3(x)
    return _dot(tri, x1) + _dot(tri, x2) + _dot(tri, x3)


RWKV_STEP_CHUNKS = 2
_RWKV_BLOCK = RWKV_CHUNK * RWKV_STEP_CHUNKS
_KAPH, _RHAT, _KHAT, _KBAR, _BHAT, _BBAR, _VB = range(7)
_RHAT32, _BONUS, _GATE = range(3)


def _interleave(*gens):
    live = list(gens)
    while live:
        for gen in list(live):
            try:
                next(gen)
            except StopIteration:
                live.remove(gen)


def _rwkv_pointwise_stage(p, last_sc, w, ob_sc, of_sc, we_sc):
    mu, w0, dup, a0, iup, gup, k_k, k_a, r_k = w
    L = RWKV_CHUNK
    nck = RWKV_STEP_CHUNKS
    ts = _RWKV_BLOCK
    rowi = lax.broadcasted_iota(jnp.int32, p.shape, 0)
    pprev = jnp.where(rowi == 0, last_sc[0:1, :], pltpu.roll(p, 1, 0))
    last_sc[0:1, :] = p[ts - 1:ts, :]
    yield
    r, k2, v, kk, a, g, logdecay = _rwkv_pointwise(p, pprev, mu, w0, dup, a0, iup, gup, k_k, k_a)
    yield
    cum = _cumsum_rows(logdecay, L)
    ends = [cum[c * L + L - 1:c * L + L, :] for c in range(nck)]
    cum_end = jnp.concatenate([jnp.broadcast_to(e, (L, WIDTH)) for e in ends], axis=0)
    w_inc = jnp.exp(cum)
    w_exc = jnp.exp(cum - logdecay)
    w_inv = jnp.exp(-cum)
    w_tail = jnp.exp(cum_end - cum)
    for c in range(nck):
        we_sc[c:c + 1, :] = jnp.exp(ends[c])
    yield
    kap_all = kk * lax.rsqrt(jnp.maximum(_head_sum(kk * kk), 1e-24))
    bb_all = kap_all * a
    r_hat_all = r * w_inc
    ob_sc[_KAPH] = _b(kap_all * w_exc)
    ob_sc[_RHAT] = _b(r_hat_all)
    ob_sc[_KHAT] = _b(k2 * w_inv)
    ob_sc[_KBAR] = _b(k2 * w_tail)
    yield
    ob_sc[_BHAT] = _b(bb_all * w_inv)
    ob_sc[_BBAR] = _b(bb_all * w_tail)
    ob_sc[_VB] = _b(v)
    of_sc[_RHAT32] = r_hat_all
    of_sc[_BONUS] = _head_sum(r * k2 * r_k) * v
    of_sc[_GATE] = g


def _rwkv_matmul_stage(ob_sc, of_sc, we_sc, s_sc, y_sc, lnw, lnb, o_ref, rows_out):
    L = RWKV_CHUNK
    nck = RWKV_STEP_CHUNKS
    trow = lax.broadcasted_iota(jnp.int32, (L, L), 0)
    tcol = lax.broadcasted_iota(jnp.int32, (L, L), 1)
    lower_incl = trow >= tcol
    lower_strict = trow > tcol
    heads = range(N_HEADS)
    items = [(slice(c * L, (c + 1) * L), slice(h * HEAD_DIM, (h + 1) * HEAD_DIM))
             for c in range(nck) for h in heads]
    n = range(len(items))
    ld = lambda slot, it: ob_sc[slot, it[0], it[1]]
    lhs = [jnp.concatenate([ld(_KAPH, it), ld(_RHAT, it)], axis=0) for it in items]
    ak = [_dot_nt(lhs[i], ld(_KHAT, items[i])) for i in n]
    ab = [_dot_nt(lhs[i], ld(_BHAT, items[i])) for i in n]
    yield
    a_kr = [_b(jnp.concatenate([jnp.where(lower_strict, t[:L], 0.0), jnp.where(lower_incl, t[L:], 0.0)], axis=0))
            for t in ak]
    a_rb = [_b(jnp.where(lower_incl, t[L:], 0.0)) for t in ab]
    t_inv = yield from _unit_lower_inverse([jnp.where(lower_strict, t[:L], 0.0) for t in ab])
    t_inv = [_b(t) for t in t_inv]
    av = [_dot(a_kr[i], ld(_VB, items[i])) for i in n]
    pm = [_b(_dot(t_inv[i], ld(_KAPH, items[i]))) for i in n]
    yield
    qm = [_b(_dot(t_inv[i], _b(av[i][:L]))) for i in n]
    r_eff = [_b(of_sc[_RHAT32, items[i][0], items[i][1]] - _dot(a_rb[i], pm[i])) for i in n]
    ptb = [_b(_dot_tn(pm[i], ld(_BBAR, items[i]))) for i in n]
    yield
    y0 = [av[i][L:] - _dot(a_rb[i], qm[i]) for i in n]
    cm = [_dot_tn(ld(_VB, items[i]), ld(_KBAR, items[i])) - _dot_tn(qm[i], ld(_BBAR, items[i])) for i in n]
    yield
    state = [s_sc[h] for h in heads]
    for c in range(nck):
        w_end = we_sc[c:c + 1, :]
        sb = [_b(t) for t in state]
        ys = [_dot_nt(r_eff[c * N_HEADS + h], sb[h]) + y0[c * N_HEADS + h] for h in heads]
        state = [state[h] * w_end[:, items[h][1]] - _dot(sb[h], ptb[c * N_HEADS + h]) + cm[c * N_HEADS + h]
                 for h in heads]
        for h in heads:
            y_sc[items[c * N_HEADS + h]] = ys[h]
    for h in heads:
        s_sc[h] = state[h]
    yield
    y = y_sc[...]
    dev = y - _head_sum(y) * (1.0 / HEAD_DIM)
    var = _head_sum(dev * dev) * (1.0 / HEAD_DIM)
    yn = dev * lax.rsqrt(var + GN_EPS) * lnw + lnb
    o_ref[0, rows_out, :] = ((yn + of_sc[_BONUS]) * of_sc[_GATE]).astype(o_ref.dtype)


def _rwkv_chunk_kernel(p_ref, mu_ref, w0_ref, dup_ref, a0_ref, iup_ref, gup_ref, kk_ref, ka_ref,
                       rk_ref, lnw_ref, lnb_ref, o_ref, s_out_ref,
                       s_sc, last_sc, y_sc, xb_sc, xf_sc, xw_sc, yb_sc, yf_sc, yw_sc):
    step = pl.program_id(1)
    last = pl.num_programs(1) - 1
    ts = _RWKV_BLOCK

    @pl.when(step == 0)
    def _():
        s_sc[...] = jnp.zeros_like(s_sc)
        last_sc[...] = jnp.zeros_like(last_sc)
        yb_sc[...] = jnp.zeros_like(yb_sc)
        yf_sc[...] = jnp.zeros_like(yf_sc)
        yw_sc[...] = jnp.zeros_like(yw_sc)

    w = (mu_ref[...], w0_ref[...], dup_ref[...], a0_ref[...], iup_ref[...], gup_ref[...],
         kk_ref[...], ka_ref[...], rk_ref[...])
    lnw, lnb = lnw_ref[...], lnb_ref[...]
    _interleave(_rwkv_matmul_stage(yb_sc, yf_sc, yw_sc, s_sc, y_sc, lnw, lnb, o_ref, slice(0, ts)),
                _rwkv_pointwise_stage(p_ref[0:ts, :], last_sc, w, xb_sc, xf_sc, xw_sc))

    @pl.when(step == last)
    def _():
        s_out_ref[0] = s_sc[...]

    _interleave(_rwkv_matmul_stage(xb_sc, xf_sc, xw_sc, s_sc, y_sc, lnw, lnb, o_ref, slice(ts, 2 * ts)),
                _rwkv_pointwise_stage(p_ref[ts:2 * ts, :], last_sc, w, yb_sc, yf_sc, yw_sc))


def _rwkv_prompt(p_rw, weights, batch, seq):
    ts = _RWKV_BLOCK
    ns = seq // (2 * ts)
    const = lambda b, c: (0, 0)
    w_specs = [pl.BlockSpec(w.shape, const) for w in weights]
    operand_scratch = [pltpu.VMEM((7, ts, WIDTH), BF16), pltpu.VMEM((3, ts, WIDTH), F32), pltpu.VMEM((8, WIDTH), F32)]
    out, state = pl.pallas_call(
        _rwkv_chunk_kernel,
        grid=(batch, ns + 1),
        in_specs=[pl.BlockSpec((2 * ts, RWKV_COLS), lambda b, c: (b * ns + jnp.minimum(c, ns - 1), 0))] + w_specs,
        out_specs=[pl.BlockSpec((1, 2 * ts, WIDTH), lambda b, c: (b, c, 0)),
                   pl.BlockSpec((1, N_HEADS, HEAD_DIM, HEAD_DIM), lambda b, c: (b, 0, 0, 0))],
        out_shape=[jax.ShapeDtypeStruct((batch, seq + 2 * ts, WIDTH), BF16),
                   jax.ShapeDtypeStruct((batch, N_HEADS, HEAD_DIM, HEAD_DIM), F32)],
        scratch_shapes=[pltpu.VMEM((N_HEADS, HEAD_DIM, HEAD_DIM), F32),
                        pltpu.VMEM((8, RWKV_COLS), F32),
                        pltpu.VMEM((ts, WIDTH), F32)] + operand_scratch + operand_scratch,
        compiler_params=_params("parallel", "arbitrary"),
        name="rwkv_prompt",
    )(p_rw, *weights)
    return out[:, ts:ts + seq].reshape(batch * seq, WIDTH), state


def _rwkv_step_kernel(p_ref, sh_ref, s_ref, mu_ref, w0_ref, dup_ref, a0_ref, iup_ref, gup_ref, kk_ref,
                      ka_ref, rk_ref, lnw_ref, lnb_ref, o_ref, s_out_ref):
    p = jnp.broadcast_to(p_ref[0], (8, RWKV_COLS))
    pprev = jnp.broadcast_to(sh_ref[0], (8, RWKV_COLS))
    r, k2, v, kk, a, g, logdecay = (t[0:1] for t in _rwkv_pointwise(
        p, pprev, mu_ref[...], w0_ref[...], dup_ref[...], a0_ref[...], iup_ref[...], gup_ref[...],
        kk_ref[...], ka_ref[...]))
    decay = jnp.exp(logdecay)
    n = HEAD_DIM
    eye = lax.broadcasted_iota(jnp.int32, (n, n), 0) == lax.broadcasted_iota(jnp.int32, (n, n), 1)

    def to_col(row_vec):
        return jnp.sum(jnp.where(eye, row_vec, 0.0), axis=-1, keepdims=True)

    def to_row(col_vec):
        return jnp.sum(jnp.where(eye, col_vec, 0.0), axis=0, keepdims=True)

    for h in range(N_HEADS):
        ln = slice(h * n, (h + 1) * n)
        s0 = s_ref[0, h]
        kap = _head_norm(kk[:, ln])
        sa = jnp.sum(s0 * (-kap), axis=-1, keepdims=True)
        s_new = s0 * decay[:, ln] + sa * (kap * a[:, ln]) + to_col(v[:, ln]) * k2[:, ln]
        s_out_ref[0, h] = s_new
        y = to_row(jnp.sum(s_new * r[:, ln], axis=-1, keepdims=True))
        o_ref[0, :, ln] = _group_norm_out(y, r[:, ln], k2[:, ln], v[:, ln], g[:, ln], rk_ref[:, ln],
                                          lnw_ref[:, ln], lnb_ref[:, ln]).astype(o_ref.dtype)


def _rwkv_sample(p_rw, shift, state, weights):
    n = p_rw.shape[0]
    const = lambda s: (0, 0)
    vec = pl.BlockSpec((1, 1, RWKV_COLS), lambda s: (s, 0, 0))
    st = pl.BlockSpec((1, N_HEADS, HEAD_DIM, HEAD_DIM), lambda s: (s, 0, 0, 0))
    rw, s_new = pl.pallas_call(
        _rwkv_step_kernel,
        grid=(n,),
        in_specs=[vec, vec, st] + [pl.BlockSpec(w.shape, const) for w in weights],
        out_specs=[pl.BlockSpec((1, 1, WIDTH), lambda s: (s, 0, 0)), st],
        out_shape=[jax.ShapeDtypeStruct((n, 1, WIDTH), BF16),
                   jax.ShapeDtypeStruct(state.shape, F32)],
        compiler_params=_params("parallel"),
        name="rwkv_sample",
    )(p_rw.reshape(n, 1, RWKV_COLS), shift.reshape(n, 1, RWKV_COLS), state, *weights)
    return rw.reshape(n, WIDTH), s_new


def _sample_gate_kernel(q_ref, km_ref, idx_ref):
    rnd = lambda t: t.astype(BF16).astype(F32)
    prod = rnd(km_ref[0]) * rnd(q_ref[0])
    lane = lax.broadcasted_iota(jnp.int32, (WIDTH, 128), 0)
    hcol = lax.broadcasted_iota(jnp.int32, (WIDTH, 128), 1)
    head_sum = ((lane // HEAD_DIM) == hcol).astype(BF16)
    pieces = _split3(prod)
    gate = _dot(pieces[0], head_sum) + _dot(pieces[1], head_sum) + _dot(pieces[2], head_sum)
    rank = _block_rank(gate, 0)
    bidx = lax.broadcasted_iota(jnp.int32, gate.shape, 0)
    rows = [jnp.sum(jnp.where(rank == i, bidx, 0), axis=0, keepdims=True) for i in range(MOBA_TOP_K)]
    rows += [jnp.zeros((1, 128), jnp.int32)] * (8 - MOBA_TOP_K)
    idx_ref[0] = jnp.concatenate(rows, axis=0)


def _sample_gate(q, kmean):
    n, nb, _ = kmean.shape
    return pl.pallas_call(
        _sample_gate_kernel,
        grid=(n,),
        in_specs=[pl.BlockSpec((1, 1, WIDTH), lambda s: (s, 0, 0)),
                  pl.BlockSpec((1, nb, WIDTH), lambda s: (s, 0, 0))],
        out_specs=pl.BlockSpec((1, 8, 128), lambda s: (s, 0, 0)),
        out_shape=jax.ShapeDtypeStruct((n, 8, 128), jnp.int32),
        compiler_params=_params("parallel"),
        name="sample_gate",
    )(q.reshape(n, 1, WIDTH), kmean)


def _sample_attn_kernel(pt_ref, top_ref, q_ref, kn_ref, vn_ref, topv_ref, slope_ref, ck_hbm, cv_hbm, o_ref,
                        kbuf, vbuf, sems, *, past_len, n_pages):
    ppb = _PAGES_PER_BLOCK
    npg = MOBA_TOP_K * ppb
    seq = pl.program_id(0)
    nseq = pl.num_programs(0)
    slot = seq % 2

    def slab_copies(sq, sl, h, i):
        blk_id = top_ref[(sq * N_HEADS + h) * MOBA_TOP_K + i // ppb]
        page = pt_ref[sq * n_pages + blk_id * ppb + i % ppb]
        return (pltpu.make_async_copy(ck_hbm.at[page, h], kbuf.at[sl, h, i], sems.at[sl]),
                pltpu.make_async_copy(cv_hbm.at[page, h], vbuf.at[sl, h, i], sems.at[sl]))

    def start_all(sq, sl):
        def per_head(h, carry):
            for i in range(npg):
                for cp in slab_copies(sq, sl, h, i):
                    cp.start()
            return carry
        lax.fori_loop(0, N_HEADS, per_head, 0)

    @pl.when(seq == 0)
    def _():
        start_all(0, 0)

    @pl.when(seq + 1 < nseq)
    def _():
        start_all(seq + 1, 1 - slot)

    def wait_head(h, carry):
        for i in range(npg):
            for cp in slab_copies(seq, slot, h, i):
                cp.wait()
        return carry
    lax.fori_loop(0, N_HEADS, wait_head, 0)

    q = q_ref[0] * (HEAD_DIM ** -0.5)
    slope = slope_ref[...]
    lane = lax.broadcasted_iota(jnp.int32, (1, 1, PAGE_SIZE), 2)
    scores = []
    for i in range(npg):
        blk_id = topv_ref[0, i // ppb][:, :, None]
        dist = (past_len - blk_id * MOBA_BLOCK - (i % ppb) * PAGE_SIZE - lane).astype(F32)
        s = jnp.sum(kbuf[slot, :, i] * q, axis=1, keepdims=True)
        scores.append(s - slope * dist)
    s_self = jnp.sum(q * kn_ref[0], axis=1, keepdims=True)
    m = s_self
    for s in scores:
        m = jnp.maximum(m, jnp.max(s, axis=2, keepdims=True))
    p_self = jnp.exp(s_self - m)
    den = p_self
    accv = jnp.zeros((N_HEADS, HEAD_DIM, PAGE_SIZE), F32)
    for i, s in enumerate(scores):
        p = jnp.exp(s - m)
        den = den + jnp.sum(p, axis=2, keepdims=True)
        accv = accv + vbuf[slot, :, i] * p
    acc = jnp.sum(accv, axis=2, keepdims=True) + p_self * vn_ref[0]
    o_ref[0] = acc / den


def _sample_attn(page_table, top, slopes, q, k_new, v_new, cache_kt, cache_vt):
    n, n_pages = page_table.shape
    npg = MOBA_TOP_K * _PAGES_PER_BLOCK
    top_flat = top.transpose(0, 2, 1).reshape(-1)
    vec = pl.BlockSpec((1, N_HEADS, HEAD_DIM, 1), lambda s, pt, tp: (s, 0, 0, 0))
    grid_spec = pltpu.PrefetchScalarGridSpec(
        num_scalar_prefetch=2,
        grid=(n,),
        in_specs=[vec, vec, vec,
                  pl.BlockSpec((1, MOBA_TOP_K, N_HEADS, 1), lambda s, pt, tp: (s, 0, 0, 0)),
                  pl.BlockSpec((N_HEADS, 1, 1), lambda s, pt, tp: (0, 0, 0)),
                  pl.BlockSpec(memory_space=pl.ANY),
                  pl.BlockSpec(memory_space=pl.ANY)],
        out_specs=vec,
        scratch_shapes=[pltpu.VMEM((2, N_HEADS, npg, HEAD_DIM, PAGE_SIZE), F32),
                        pltpu.VMEM((2, N_HEADS, npg, HEAD_DIM, PAGE_SIZE), F32),
                        pltpu.SemaphoreType.DMA((2,))],
    )
    return pl.pallas_call(
        functools.partial(_sample_attn_kernel, past_len=n_pages * PAGE_SIZE, n_pages=n_pages),
        grid_spec=grid_spec,
        out_shape=jax.ShapeDtypeStruct((n, N_HEADS, HEAD_DIM, 1), F32),
        compiler_params=_params("arbitrary"),
        name="sample_attn",
    )(page_table.reshape(-1), top_flat, q, k_new, v_new, top[..., None], slopes.reshape(N_HEADS, 1, 1),
      cache_kt, cache_vt)


def kernel(x_prompt, x_sample, cache_k, cache_v, page_table, state_wkv, state_shift,
           norm_mix_g, w_in, mu_shift, decay_w0, decay_up, iclr_a0, iclr_up, gate_up,
           k_k, k_a, r_k, ln_x_w, ln_x_b, w_out, norm_ffn_g, w_ffn_up, w_ffn_down, norm_final_g):
    depth = w_in.shape[0]
    assert depth == 1
    batch, seq, _ = x_prompt.shape
    n_seq, n_pages = page_table.shape
    slopes = jnp.exp2(-8.0 * jnp.arange(1, N_HEADS + 1, dtype=F32) / N_HEADS)

    l = 0
    row = lambda t: t.reshape(1, -1)
    w_in_l = jnp.concatenate([w_in[l][:, :ATTN_COLS], _to_internal(w_in[l][:, ATTN_COLS:])], axis=1)
    g_mix = row(norm_mix_g[l])
    rw_weights = (row(_to_internal(mu_shift[l])), row(decay_w0[l]), decay_up[l], row(iclr_a0[l]), iclr_up[l],
                  gate_up[l], row(k_k[l]), row(k_a[l]), row(r_k[l]), row(ln_x_w[l]), row(ln_x_b[l]))
    wo = w_out[l].astype(BF16)
    ffn_weights = (wo[:WIDTH], wo[WIDTH:], row(norm_ffn_g[l]), w_ffn_up[l].astype(BF16),
                   w_ffn_down[l].astype(BF16), row(norm_final_g))

    xp = x_prompt.reshape(batch * seq, D_MODEL)
    w_in_b = w_in_l.astype(BF16)
    kt, vt, qt, ka, vtb, prw, kmean = _inproj_prompt(xp, g_mix, w_in_b, slopes, batch, seq)
    nb = seq // MOBA_BLOCK
    attn_p = _moba_prompt(slopes, qt, ka, vtb, kmean.reshape(batch, nb, WIDTH), batch, seq)
    rw_p, wkv_p = _rwkv_prompt(prw, rw_weights, batch, seq)
    ckt = cache_k.transpose(0, 1, 3, 4, 2).reshape(-1, N_HEADS, HEAD_DIM, PAGE_SIZE)
    cvt = cache_v.transpose(0, 1, 3, 4, 2).reshape(-1, N_HEADS, HEAD_DIM, PAGE_SIZE)
    y_prompt, kmean_s = _out_ffn(xp, attn_p, rw_p, *ffn_weights, tm=512, stream=(page_table, ckt))
    y_prompt = y_prompt.reshape(batch, seq, D_MODEL)
    shift_p = _from_internal(prw.reshape(batch, seq, RWKV_COLS)[:, -1])

    hd = (N_HEADS, HEAD_DIM)
    xs = x_sample.reshape(n_seq, D_MODEL)
    proj_s = _inproj_small(xs, g_mix, w_in_b)
    q_s, k_s, v_s = proj_s[:, :WIDTH], proj_s[:, WIDTH:2 * WIDTH], proj_s[:, 2 * WIDTH:ATTN_COLS]
    prw_s = proj_s[:, ATTN_COLS:]
    top = _sample_gate(q_s, kmean_s)[:, :MOBA_TOP_K, :N_HEADS]
    col = lambda t: t.reshape(n_seq, N_HEADS, HEAD_DIM, 1)
    attn_s = _sample_attn(page_table, top, slopes, col(q_s), col(k_s), col(v_s), ckt, cvt).reshape(n_seq, WIDTH)
    rw_s, wkv_s = _rwkv_sample(prw_s, _to_internal(state_shift[l]), state_wkv[l], rw_weights)
    y_sample = _out_ffn(xs, attn_s.astype(BF16), rw_s, *ffn_weights, tm=n_seq).reshape(n_seq, 1, D_MODEL)
    shift_s = _from_internal(prw_s)

    return (y_prompt, y_sample,
            kt.transpose(0, 3, 1, 2)[None], vt.transpose(0, 3, 1, 2)[None],
            wkv_p[None], shift_p[None],
            k_s.reshape(1, n_seq, 1, *hd), v_s.reshape(1, n_seq, 1, *hd),
            wkv_s[None], shift_s[None])
```

```python
import functools

import jax
import jax.numpy as jnp
from jax import lax
from jax.experimental import pallas as pl
from jax.experimental.pallas import tpu as pltpu

F32 = jnp.float32
BF16 = jnp.bfloat16

D_MODEL = 1024
HEAD_DIM = 64
N_HEADS = 8
WIDTH = N_HEADS * HEAD_DIM
MOBA_BLOCK = 256
MOBA_TOP_K = 3
DECAY_LORA = 64
AAA_LORA = 64
GATE_LORA = 128
ATTN_COLS = 3 * WIDTH
RWKV_COLS = 3 * WIDTH + DECAY_LORA + AAA_LORA + GATE_LORA
RMS_EPS = 1e-6
GN_EPS = 64e-5
NEG_INF = -1e30
PAGE_SIZE = 128
RWKV_CHUNK = 64
VMEM_LIMIT = 48 * 1024 * 1024

_O_R, _O_K, _O_V = 0, WIDTH, 2 * WIDTH
_O_XW = 3 * WIDTH
_O_XA = _O_XW + DECAY_LORA
_O_XG = _O_XA + AAA_LORA


def _to_internal(t):
    r, xw, kv, rest = (t[..., :WIDTH], t[..., WIDTH:WIDTH + DECAY_LORA],
                       t[..., WIDTH + DECAY_LORA:3 * WIDTH + DECAY_LORA], t[..., 3 * WIDTH + DECAY_LORA:])
    return jnp.concatenate([r, kv, xw, rest], axis=-1)


def _from_internal(t):
    r, kv, xw, rest = (t[..., :WIDTH], t[..., WIDTH:3 * WIDTH],
                       t[..., 3 * WIDTH:3 * WIDTH + DECAY_LORA], t[..., 3 * WIDTH + DECAY_LORA:])
    return jnp.concatenate([r, xw, kv, rest], axis=-1)


def _params(*sem):
    return pltpu.CompilerParams(dimension_semantics=sem, vmem_limit_bytes=VMEM_LIMIT)


def _rms(x, g):
    return x * lax.rsqrt(jnp.mean(x * x, axis=-1, keepdims=True) + RMS_EPS) * g


def _dot(a, b, **kw):
    return jnp.dot(a, b, preferred_element_type=F32, **kw)


def _dot_nt(a, b, **kw):
    return lax.dot_general(a, b, (((1,), (1,)), ((), ())), preferred_element_type=F32, **kw)


def _dot_tn(a, b, **kw):
    return lax.dot_general(a, b, (((0,), (0,)), ((), ())), preferred_element_type=F32, **kw)


def _inproj_kernel(x_ref, g_ref, w_ref, srow_ref, kt_ref, vt_ref, qt_ref, ka_ref, vtb_ref, prw_ref, km_ref):
    tm = x_ref.shape[0]
    hd = (N_HEADS, HEAD_DIM, tm)
    xn = _rms(x_ref[...], g_ref[...]).astype(BF16)
    proj = _dot(xn, w_ref[...])
    q = proj[:, 0:WIDTH] * (HEAD_DIM ** -0.5)
    k = proj[:, WIDTH:2 * WIDTH]
    v = proj[:, 2 * WIDTH:3 * WIDTH]
    prw_ref[...] = proj[:, ATTN_COLS:]
    km_ref[0] = jnp.mean(k, axis=0, keepdims=True)
    k_t = k.T.reshape(hd)
    v_t = v.T.reshape(hd)
    kt_ref[0] = k_t
    vt_ref[0] = v_t
    vtb_ref[0, :, 0] = v_t.astype(BF16)
    qt_ref[0, :, 0:HEAD_DIM, :] = q.T.reshape(hd).astype(BF16)
    qt_ref[0, :, HEAD_DIM:, :] = jnp.broadcast_to(srow_ref[...], hd).astype(BF16)
    lane = lax.broadcasted_iota(jnp.int32, (tm, 128 - HEAD_DIM), 1)
    pos = lax.broadcasted_iota(jnp.int32, (tm, 128 - HEAD_DIM), 0)
    pos_cols = jnp.where(lane == 0, pos, 0).astype(F32).astype(BF16)
    for h in range(N_HEADS):
        ka_ref[0, h, :, 0:HEAD_DIM] = k[:, h * HEAD_DIM:(h + 1) * HEAD_DIM].astype(BF16)
        ka_ref[0, h, :, HEAD_DIM:] = pos_cols


def _inproj_prompt(x, g, w_bf16, slopes, batch, seq):
    tm = MOBA_BLOCK
    m = x.shape[0]
    nb = seq // tm
    ncol = w_bf16.shape[1]
    row = lambda i: (i, 0)
    const = lambda i: (0, 0)
    tok = lambda i: (i // nb, 0, 0, i % nb)
    srow = jnp.zeros((N_HEADS, HEAD_DIM, 1), F32).at[:, 0, 0].set(slopes)
    return pl.pallas_call(
        _inproj_kernel,
        grid=(m // tm,),
        in_specs=[pl.BlockSpec((tm, D_MODEL), row),
                  pl.BlockSpec((1, D_MODEL), const),
                  pl.BlockSpec((D_MODEL, ncol), const),
                  pl.BlockSpec((N_HEADS, HEAD_DIM, 1), lambda i: (0, 0, 0))],
        out_specs=[pl.BlockSpec((1, N_HEADS, HEAD_DIM, tm), tok),
                   pl.BlockSpec((1, N_HEADS, HEAD_DIM, tm), tok),
                   pl.BlockSpec((1, N_HEADS, 128, tm), tok),
                   pl.BlockSpec((1, N_HEADS, tm, 128), lambda i: (i // nb, 0, i % nb, 0)),
                   pl.BlockSpec((1, N_HEADS, 1, HEAD_DIM, tm), lambda i: (i // nb, 0, i % nb, 0, 0)),
                   pl.BlockSpec((tm, RWKV_COLS), row),
                   pl.BlockSpec((1, 1, WIDTH), lambda i: (i, 0, 0))],
        out_shape=[jax.ShapeDtypeStruct((batch, N_HEADS, HEAD_DIM, seq), F32),
                   jax.ShapeDtypeStruct((batch, N_HEADS, HEAD_DIM, seq), F32),
                   jax.ShapeDtypeStruct((batch, N_HEADS, 128, seq), BF16),
                   jax.ShapeDtypeStruct((batch, N_HEADS, seq, 128), BF16),
                   jax.ShapeDtypeStruct((batch, N_HEADS, nb, HEAD_DIM, tm), BF16),
                   jax.ShapeDtypeStruct((m, RWKV_COLS), F32),
                   jax.ShapeDtypeStruct((m // tm, 1, WIDTH), F32)],
        compiler_params=_params("parallel"),
        name="inproj_prompt",
    )(x, g, w_bf16, srow)


def _inproj_small_kernel(x_ref, g_ref, w_ref, o_ref):
    xn = _rms(x_ref[...], g_ref[...]).astype(BF16)
    o_ref[...] = _dot(xn, w_ref[...])


def _inproj_small(x, g, w, tn=256):
    m = x.shape[0]
    ncol = w.shape[1]
    return pl.pallas_call(
        _inproj_small_kernel,
        grid=(ncol // tn,),
        in_specs=[pl.BlockSpec((m, D_MODEL), lambda j: (0, 0)),
                  pl.BlockSpec((1, D_MODEL), lambda j: (0, 0)),
                  pl.BlockSpec((D_MODEL, tn), lambda j: (0, j))],
        out_specs=pl.BlockSpec((m, tn), lambda j: (0, j)),
        out_shape=jax.ShapeDtypeStruct((m, ncol), F32),
        compiler_params=_params("parallel"),
        name="inproj_sample",
    )(x, g, w)


_STREAM_PAGES = 16
_PAGES_PER_BLOCK = MOBA_BLOCK // PAGE_SIZE


def _ffn_kernel(pt_ref, x_ref, attn_ref, rw_ref, woa_ref, wor_ref, gf_ref, wup_ref, wdn_ref, gfin_ref, *rest,
                n_stream):
    page_refs = rest[:n_stream]
    y_ref = rest[n_stream]
    km_ref = rest[n_stream + 1] if n_stream else None
    h_sc, hn_sc, acc_sc = rest[-3:]
    j = pl.program_id(1)

    @pl.when(j == 0)
    def _():
        h = x_ref[...] + _dot(attn_ref[...], woa_ref[...]) + _dot(rw_ref[...], wor_ref[...])
        h_sc[...] = h
        hn_sc[...] = _rms(h, gf_ref[...]).astype(BF16)
        acc_sc[...] = jnp.zeros_like(acc_sc)

    u = jnp.maximum(_dot(hn_sc[...], wup_ref[...]), 0.0)
    acc_sc[...] += _dot((u * u).astype(BF16), wdn_ref[...])

    if n_stream:
        ppb = _PAGES_PER_BLOCK
        bps = n_stream // ppb
        nb = km_ref.shape[1]
        g = (pl.program_id(0) * pl.num_programs(1) + j) % (nb // bps)
        for b in range(bps):
            tot = page_refs[b * ppb][0]
            for i in range(1, ppb):
                tot = tot + page_refs[b * ppb + i][0]
            tot_t = tot.reshape(WIDTH, PAGE_SIZE).T
            km_ref[0, pl.ds(g * bps + b, 1), :] = jnp.sum(tot_t, axis=0, keepdims=True) * (1.0 / MOBA_BLOCK)

    @pl.when(j == pl.num_programs(1) - 1)
    def _():
        y_ref[...] = _rms(h_sc[...] + acc_sc[...], gfin_ref[...])


def _out_ffn(x, attn, rw, woa, wor, gf, wup, wdn, gfin, tm, th=1024, stream=None):
    m = x.shape[0]
    hid = wup.shape[1]
    nj = hid // th
    row = lambda i, j, pt: (i, 0)
    const = lambda i, j, pt: (0, 0)
    in_specs = [pl.BlockSpec((tm, D_MODEL), row),
                pl.BlockSpec((tm, WIDTH), row),
                pl.BlockSpec((tm, WIDTH), row),
                pl.BlockSpec((WIDTH, D_MODEL), const),
                pl.BlockSpec((WIDTH, D_MODEL), const),
                pl.BlockSpec((1, D_MODEL), const),
                pl.BlockSpec((D_MODEL, th), lambda i, j, pt: (0, j)),
                pl.BlockSpec((th, D_MODEL), lambda i, j, pt: (j, 0)),
                pl.BlockSpec((1, D_MODEL), const)]
    out_specs = [pl.BlockSpec((tm, D_MODEL), row)]
    out_shape = [jax.ShapeDtypeStruct((m, D_MODEL), F32)]
    operands = [x, attn, rw, woa, wor, gf, wup, wdn, gfin]
    n_stream = 0
    pt_flat = jnp.zeros((1,), jnp.int32)
    if stream is not None:
        page_table, cache_kt = stream
        n_seq, n_pages = page_table.shape
        n_stream = _STREAM_PAGES
        spp = n_pages // n_stream
        assert (m // tm) * nj == n_seq * spp, "page streaming needs one grid step per 16 pages"
        nb = n_pages // _PAGES_PER_BLOCK
        pt_flat = page_table.reshape(-1)

        def page_spec(k):
            def imap(i, j, pt, k=k):
                t = i * nj + j
                return (pt[(t // spp) * n_pages + (t % spp) * n_stream + k], 0, 0, 0)
            return pl.BlockSpec((1, N_HEADS, HEAD_DIM, PAGE_SIZE), imap)

        in_specs += [page_spec(k) for k in range(n_stream)]
        operands += [cache_kt] * n_stream
        out_specs.append(pl.BlockSpec((1, nb, WIDTH), lambda i, j, pt: ((i * nj + j) // spp, 0, 0)))
        out_shape.append(jax.ShapeDtypeStruct((n_seq, nb, WIDTH), F32))
    grid_spec = pltpu.PrefetchScalarGridSpec(
        num_scalar_prefetch=1,
        grid=(m // tm, nj),
        in_specs=in_specs,
        out_specs=out_specs,
        scratch_shapes=[pltpu.VMEM((tm, D_MODEL), F32),
                        pltpu.VMEM((tm, D_MODEL), BF16),
                        pltpu.VMEM((tm, D_MODEL), F32)],
    )
    outs = pl.pallas_call(
        functools.partial(_ffn_kernel, n_stream=n_stream),
        grid_spec=grid_spec,
        out_shape=out_shape,
        compiler_params=_params("arbitrary", "arbitrary"),
        name="out_ffn",
    )(pt_flat, *operands)
    return outs if stream is not None else outs[0]


def _block_rank(gm, axis):
    nb = gm.shape[axis]
    idx = lax.broadcasted_iota(jnp.int32, gm.shape, axis)
    beats = []
    for m in range(nb):
        gmm = lax.slice_in_dim(gm, m, m + 1, axis=axis)
        beats.append(((gmm > gm) | ((gmm == gm) & (m < idx))).astype(jnp.int32))
    while len(beats) > 1:
        beats = [a + b for a, b in zip(beats[0::2], beats[1::2])] + ([beats[-1]] if len(beats) % 2 else [])
    return beats[0]


MOBA_HEADS = 8
BIG = 1e30


def _moba_kernel(slopes_ref, qt_ref, ka_ref, vt_ref, km_ref, o_ref, sel_sc, m_sc, l_sc, acc_sc, *, nb):
    hg = pl.program_id(1)
    qi = pl.program_id(2)
    blk = MOBA_BLOCK
    n_top = min(MOBA_TOP_K, nb)
    keyi = lax.broadcasted_iota(jnp.int32, (blk, blk), 0)
    qryi = lax.broadcasted_iota(jnp.int32, (blk, blk), 1)
    causal = keyi <= qryi
    bidx = lax.broadcasted_iota(jnp.int32, (nb, blk), 0)
    q0 = pl.multiple_of(qi * blk, blk)

    hs = range(MOBA_HEADS)
    qts = [qt_ref[0, hh] for hh in hs]
    gates = [_dot(km_ref[0, hh], qts[hh]) for hh in hs]
    valid = bidx < qi
    ranks = [_block_rank(jnp.where(valid, g, NEG_INF), 0) for g in gates]
    for hh in hs:
        sel_sc[hh] = ((ranks[hh] < n_top) & valid).astype(F32)
    s0 = [jnp.where(causal, _dot(ka_ref[0, hh, pl.ds(q0, blk), :], qts[hh]), NEG_INF) for hh in hs]
    m0 = [jnp.max(t, axis=0, keepdims=True) for t in s0]
    p0 = [jnp.exp(s0[hh] - m0[hh]) for hh in hs]
    for hh in hs:
        m_sc[hh] = m0[hh]
        l_sc[hh] = jnp.sum(p0[hh], axis=0, keepdims=True)
    pv0 = [_dot(vt_ref[0, hh, qi], p0[hh].astype(BF16)) for hh in hs]
    for hh in hs:
        acc_sc[hh] = pv0[hh]

    def past_block(j, carry):
        k0 = pl.multiple_of(j * blk, blk)
        s = [_dot(ka_ref[0, hh, pl.ds(k0, blk), :], qt_ref[0, hh]) for hh in hs]
        ps, alphas = [], []
        for hh in hs:
            cj = -slopes_ref[hg * MOBA_HEADS + hh] * ((qi - j) * blk).astype(F32)
            picked = sel_sc[hh, pl.ds(j, 1), :] > 0.0
            m_old = m_sc[hh]
            m_new = jnp.maximum(m_old, jnp.where(picked, jnp.max(s[hh], axis=0, keepdims=True) + cj, NEG_INF))
            alpha = jnp.exp(m_old - m_new)
            p = jnp.exp(s[hh] - jnp.where(picked, m_new - cj, BIG))
            m_sc[hh] = m_new
            l_sc[hh] = alpha * l_sc[hh] + jnp.sum(p, axis=0, keepdims=True)
            ps.append(p.astype(BF16))
            alphas.append(alpha)
        pv = [_dot(vt_ref[0, hh, j], ps[hh]) for hh in hs]
        for hh in hs:
            acc_sc[hh] = alphas[hh] * acc_sc[hh] + pv[hh]
        return carry

    lax.fori_loop(0, qi, past_block, 0)

    o_ref[...] = jnp.concatenate([(acc_sc[hh] / l_sc[hh]).T for hh in hs], axis=1).astype(o_ref.dtype)


def _moba_prompt(slopes, qt, ka, vt, kmean, batch, seq):
    nb = seq // MOBA_BLOCK
    blk = MOBA_BLOCK
    km = kmean.reshape(batch, nb, N_HEADS, HEAD_DIM).transpose(0, 2, 1, 3).astype(BF16)
    km = jnp.concatenate([km, jnp.zeros((batch, N_HEADS, nb, 128 - HEAD_DIM), BF16)], axis=3)

    hgn = MOBA_HEADS
    grid_spec = pltpu.PrefetchScalarGridSpec(
        num_scalar_prefetch=1,
        grid=(batch, N_HEADS // hgn, nb),
        in_specs=[pl.BlockSpec((1, hgn, 128, blk), lambda b, g, qi, s: (b, g, 0, qi)),
                  pl.BlockSpec((1, hgn, seq, 128), lambda b, g, qi, s: (b, g, 0, 0)),
                  pl.BlockSpec((1, hgn, nb, HEAD_DIM, blk), lambda b, g, qi, s: (b, g, 0, 0, 0)),
                  pl.BlockSpec((1, hgn, nb, 128), lambda b, g, qi, s: (b, g, 0, 0))],
        out_specs=pl.BlockSpec((blk, hgn * HEAD_DIM), lambda b, g, qi, s: (b * nb + qi, g)),
        scratch_shapes=[pltpu.VMEM((hgn, nb, blk), F32),
                        pltpu.VMEM((hgn, 1, blk), F32),
                        pltpu.VMEM((hgn, 1, blk), F32),
                        pltpu.VMEM((hgn, HEAD_DIM, blk), F32)],
    )
    return pl.pallas_call(
        functools.partial(_moba_kernel, nb=nb),
        grid_spec=grid_spec,
        out_shape=jax.ShapeDtypeStruct((batch * seq, WIDTH), BF16),
        compiler_params=_params("parallel", "parallel", "arbitrary"),
        name="moba_prompt",
    )(slopes, qt, ka, vt, km)


def _b(t):
    return t.astype(BF16)


def _split3(x):
    x1 = _b(x)
    r1 = x - x1.astype(F32)
    x2 = _b(r1)
    return x1, x2, _b(r1 - x2.astype(F32))


def _rwkv_pointwise(p, pprev, mu, w0, decay_up, a0, iclr_up, gate_up, k_k, k_a):
    xs = p + mu * (pprev - p)
    r = xs[:, _O_R:_O_R + WIDTH]
    k = xs[:, _O_K:_O_K + WIDTH]
    v = xs[:, _O_V:_O_V + WIDTH]
    xw = xs[:, _O_XW:_O_XW + DECAY_LORA]
    xa = xs[:, _O_XA:_O_XA + AAA_LORA]
    xg = xs[:, _O_XG:_O_XG + GATE_LORA]
    w = w0 + _dot(_b(jnp.tanh(xw)), _b(decay_up))
    w = -jax.nn.softplus(-w) - 0.5
    logdecay = -jnp.exp(w)
    a = jax.nn.sigmoid(a0 + _dot(_b(xa), _b(iclr_up)))
    g = _dot(_b(jax.nn.sigmoid(xg)), _b(gate_up))
    kk = k * k_k
    k2 = k * (1.0 + (a - 1.0) * k_a)
    return r, k2, v, kk, a, g, logdecay


def _head_sum(x):
    row = lax.broadcasted_iota(jnp.int32, (128, 128), 0)
    col = lax.broadcasted_iota(jnp.int32, (128, 128), 1)
    seg = ((row // HEAD_DIM) == (col // HEAD_DIM)).astype(BF16)
    hi = _b(x)
    lo = _b(x - hi.astype(F32))
    cols = [slice(g * 128, (g + 1) * 128) for g in range(x.shape[1] // 128)]
    return jnp.concatenate([_dot(hi[:, c], seg) + _dot(lo[:, c], seg) for c in cols], axis=1)


def _head_norm(kk_h):
    return kk_h * lax.rsqrt(jnp.maximum(jnp.sum(kk_h * kk_h, axis=-1, keepdims=True), 1e-24))


def _group_norm_out(y_h, r_h, k_h, v_h, g_h, rk_h, lnw_h, lnb_h):
    mean = jnp.mean(y_h, axis=-1, keepdims=True)
    var = jnp.mean(jnp.square(y_h - mean), axis=-1, keepdims=True)
    yn = (y_h - mean) * lax.rsqrt(var + GN_EPS) * lnw_h + lnb_h
    yn = yn + jnp.sum(r_h * k_h * rk_h, axis=-1, keepdims=True) * v_h
    return yn * g_h


def _unit_lower_inverse(mats):
    n = mats[0].shape[0]
    row = lax.broadcasted_iota(jnp.int32, (n, n), 0)
    col = lax.broadcasted_iota(jnp.int32, (n, n), 1)
    eye = (row == col).astype(F32)
    size = 16
    same = (row // size) == (col // size)
    pws = [jnp.where(same, a, 0.0) for a in mats]
    xs = [eye - pw for pw in pws]
    for _ in range(3):
        pwb = [_b(pw) for pw in pws]
        pws = [_dot(t, t) for t in pwb]
        xs = [x + _dot(_b(x), _b(pw)) for x, pw in zip(xs, pws)]
        yield
    while size < n:
        size2 = size * 2
        same2 = (row // size2) == (col // size2)
        keep = same2 & jnp.logical_not(same)
        xb = [_b(x) for x in xs]
        ox = [_b(_dot(_b(jnp.where(keep, a, 0.0)), t)) for a, t in zip(mats, xb)]
        xs = [x - _dot(t, o) for x, t, o in zip(xs, xb, ox)]
        same = same2
        size = size2
        yield
    return xs


def _cumsum_rows(x, seg):
    n = x.shape[0]
    row = lax.broadcasted_iota(jnp.int32, (n, n), 0)
    col = lax.broadcasted_iota(jnp.int32, (n, n), 1)
    tri = ((row >= col) & ((row // seg) == (col // seg))).astype(BF16)
    x1, x2, x3 = _split3(x)
    return _dot(tri, x1) + _dot(tri, x2) + _dot(tri, x3)


RWKV_STEP_CHUNKS = 2
_RWKV_BLOCK = RWKV_CHUNK * RWKV_STEP_CHUNKS
_KAPH, _RHAT, _KHAT, _KBAR, _BHAT, _BBAR, _VB = range(7)
_RHAT32, _BONUS, _GATE = range(3)


def _interleave(*gens):
    live = list(gens)
    while live:
        for gen in list(live):
            try:
                next(gen)
            except StopIteration:
                live.remove(gen)


def _rwkv_pointwise_stage(p, last_sc, w, ob_sc, of_sc, we_sc):
    mu, w0, dup, a0, iup, gup, k_k, k_a, r_k = w
    L = RWKV_CHUNK
    nck = RWKV_STEP_CHUNKS
    ts = _RWKV_BLOCK
    rowi = lax.broadcasted_iota(jnp.int32, p.shape, 0)
    pprev = jnp.where(rowi == 0, last_sc[0:1, :], pltpu.roll(p, 1, 0))
    last_sc[0:1, :] = p[ts - 1:ts, :]
    yield
    r, k2, v, kk, a, g, logdecay = _rwkv_pointwise(p, pprev, mu, w0, dup, a0, iup, gup, k_k, k_a)
    yield
    cum = _cumsum_rows(logdecay, L)
    ends = [cum[c * L + L - 1:c * L + L, :] for c in range(nck)]
    cum_end = jnp.concatenate([jnp.broadcast_to(e, (L, WIDTH)) for e in ends], axis=0)
    w_inc = jnp.exp(cum)
    w_exc = jnp.exp(cum - logdecay)
    w_inv = jnp.exp(-cum)
    w_tail = jnp.exp(cum_end - cum)
    for c in range(nck):
        we_sc[c:c + 1, :] = jnp.exp(ends[c])
    yield
    kap_all = kk * lax.rsqrt(jnp.maximum(_head_sum(kk * kk), 1e-24))
    bb_all = kap_all * a
    r_hat_all = r * w_inc
    ob_sc[_KAPH] = _b(kap_all * w_exc)
    ob_sc[_RHAT] = _b(r_hat_all)
    ob_sc[_KHAT] = _b(k2 * w_inv)
    ob_sc[_KBAR] = _b(k2 * w_tail)
    yield
    ob_sc[_BHAT] = _b(bb_all * w_inv)
    ob_sc[_BBAR] = _b(bb_all * w_tail)
    ob_sc[_VB] = _b(v)
    of_sc[_RHAT32] = r_hat_all
    of_sc[_BONUS] = _head_sum(r * k2 * r_k) * v
    of_sc[_GATE] = g


def _rwkv_matmul_stage(ob_sc, of_sc, we_sc, s_sc, y_sc, lnw, lnb, o_ref, rows_out):
    L = RWKV_CHUNK
    nck = RWKV_STEP_CHUNKS
    trow = lax.broadcasted_iota(jnp.int32, (L, L), 0)
    tcol = lax.broadcasted_iota(jnp.int32, (L, L), 1)
    lower_incl = trow >= tcol
    lower_strict = trow > tcol
    heads = range(N_HEADS)
    items = [(slice(c * L, (c + 1) * L), slice(h * HEAD_DIM, (h + 1) * HEAD_DIM))
             for c in range(nck) for h in heads]
    n = range(len(items))
    ld = lambda slot, it: ob_sc[slot, it[0], it[1]]
    lhs = [jnp.concatenate([ld(_KAPH, it), ld(_RHAT, it)], axis=0) for it in items]
    ak = [_dot_nt(lhs[i], ld(_KHAT, items[i])) for i in n]
    ab = [_dot_nt(lhs[i], ld(_BHAT, items[i])) for i in n]
    yield
    a_kr = [_b(jnp.concatenate([jnp.where(lower_strict, t[:L], 0.0), jnp.where(lower_incl, t[L:], 0.0)], axis=0))
            for t in ak]
    a_rb = [_b(jnp.where(lower_incl, t[L:], 0.0)) for t in ab]
    t_inv = yield from _unit_lower_inverse([jnp.where(lower_strict, t[:L], 0.0) for t in ab])
    t_inv = [_b(t) for t in t_inv]
    av = [_dot(a_kr[i], ld(_VB, items[i])) for i in n]
    pm = [_b(_dot(t_inv[i], ld(_KAPH, items[i]))) for i in n]
    yield
    qm = [_b(_dot(t_inv[i], _b(av[i][:L]))) for i in n]
    r_eff = [_b(of_sc[_RHAT32, items[i][0], items[i][1]] - _dot(a_rb[i], pm[i])) for i in n]
    ptb = [_b(_dot_tn(pm[i], ld(_BBAR, items[i]))) for i in n]
    yield
    y0 = [av[i][L:] - _dot(a_rb[i], qm[i]) for i in n]
    cm = [_dot_tn(ld(_VB, items[i]), ld(_KBAR, items[i])) - _dot_tn(qm[i], ld(_BBAR, items[i])) for i in n]
    yield
    state = [s_sc[h] for h in heads]
    for c in range(nck):
        w_end = we_sc[c:c + 1, :]
        sb = [_b(t) for t in state]
        ys = [_dot_nt(r_eff[c * N_HEADS + h], sb[h]) + y0[c * N_HEADS + h] for h in heads]
        state = [state[h] * w_end[:, items[h][1]] - _dot(sb[h], ptb[c * N_HEADS + h]) + cm[c * N_HEADS + h]
                 for h in heads]
        for h in heads:
            y_sc[items[c * N_HEADS + h]] = ys[h]
    for h in heads:
        s_sc[h] = state[h]
    yield
    y = y_sc[...]
    dev = y - _head_sum(y) * (1.0 / HEAD_DIM)
    var = _head_sum(dev * dev) * (1.0 / HEAD_DIM)
    yn = dev * lax.rsqrt(var + GN_EPS) * lnw + lnb
    o_ref[0, rows_out, :] = ((yn + of_sc[_BONUS]) * of_sc[_GATE]).astype(o_ref.dtype)


def _rwkv_chunk_kernel(p_ref, mu_ref, w0_ref, dup_ref, a0_ref, iup_ref, gup_ref, kk_ref, ka_ref,
                       rk_ref, lnw_ref, lnb_ref, o_ref, s_out_ref,
                       s_sc, last_sc, y_sc, xb_sc, xf_sc, xw_sc, yb_sc, yf_sc, yw_sc):
    step = pl.program_id(1)
    last = pl.num_programs(1) - 1
    ts = _RWKV_BLOCK

    @pl.when(step == 0)
    def _():
        s_sc[...] = jnp.zeros_like(s_sc)
        last_sc[...] = jnp.zeros_like(last_sc)
        yb_sc[...] = jnp.zeros_like(yb_sc)
        yf_sc[...] = jnp.zeros_like(yf_sc)
        yw_sc[...] = jnp.zeros_like(yw_sc)

    w = (mu_ref[...], w0_ref[...], dup_ref[...], a0_ref[...], iup_ref[...], gup_ref[...],
         kk_ref[...], ka_ref[...], rk_ref[...])
    lnw, lnb = lnw_ref[...], lnb_ref[...]
    _interleave(_rwkv_matmul_stage(yb_sc, yf_sc, yw_sc, s_sc, y_sc, lnw, lnb, o_ref, slice(0, ts)),
                _rwkv_pointwise_stage(p_ref[0:ts, :], last_sc, w, xb_sc, xf_sc, xw_sc))

    @pl.when(step == last)
    def _():
        s_out_ref[0] = s_sc[...]

    _interleave(_rwkv_matmul_stage(xb_sc, xf_sc, xw_sc, s_sc, y_sc, lnw, lnb, o_ref, slice(ts, 2 * ts)),
                _rwkv_pointwise_stage(p_ref[ts:2 * ts, :], last_sc, w, yb_sc, yf_sc, yw_sc))


def _rwkv_prompt(p_rw, weights, batch, seq):
    ts = _RWKV_BLOCK
    ns = seq // (2 * ts)
    const = lambda b, c: (0, 0)
    w_specs = [pl.BlockSpec(w.shape, const) for w in weights]
    operand_scratch = [pltpu.VMEM((7, ts, WIDTH), BF16), pltpu.VMEM((3, ts, WIDTH), F32), pltpu.VMEM((8, WIDTH), F32)]
    out, state = pl.pallas_call(
        _rwkv_chunk_kernel,
        grid=(batch, ns + 1),
        in_specs=[pl.BlockSpec((2 * ts, RWKV_COLS), lambda b, c: (b * ns + jnp.minimum(c, ns - 1), 0))] + w_specs,
        out_specs=[pl.BlockSpec((1, 2 * ts, WIDTH), lambda b, c: (b, c, 0)),
                   pl.BlockSpec((1, N_HEADS, HEAD_DIM, HEAD_DIM), lambda b, c: (b, 0, 0, 0))],
        out_shape=[jax.ShapeDtypeStruct((batch, seq + 2 * ts, WIDTH), BF16),
                   jax.ShapeDtypeStruct((batch, N_HEADS, HEAD_DIM, HEAD_DIM), F32)],
        scratch_shapes=[pltpu.VMEM((N_HEADS, HEAD_DIM, HEAD_DIM), F32),
                        pltpu.VMEM((8, RWKV_COLS), F32),
                        pltpu.VMEM((ts, WIDTH), F32)] + operand_scratch + operand_scratch,
        compiler_params=_params("parallel", "arbitrary"),
        name="rwkv_prompt",
    )(p_rw, *weights)
    return out[:, ts:ts + seq].reshape(batch * seq, WIDTH), state


def _rwkv_step_kernel(p_ref, sh_ref, s_ref, mu_ref, w0_ref, dup_ref, a0_ref, iup_ref, gup_ref, kk_ref,
                      ka_ref, rk_ref, lnw_ref, lnb_ref, o_ref, s_out_ref):
    p = jnp.broadcast_to(p_ref[0], (8, RWKV_COLS))
    pprev = jnp.broadcast_to(sh_ref[0], (8, RWKV_COLS))
    r, k2, v, kk, a, g, logdecay = (t[0:1] for t in _rwkv_pointwise(
        p, pprev, mu_ref[...], w0_ref[...], dup_ref[...], a0_ref[...], iup_ref[...], gup_ref[...],
        kk_ref[...], ka_ref[...]))
    decay = jnp.exp(logdecay)
    n = HEAD_DIM
    eye = lax.broadcasted_iota(jnp.int32, (n, n), 0) == lax.broadcasted_iota(jnp.int32, (n, n), 1)

    def to_col(row_vec):
        return jnp.sum(jnp.where(eye, row_vec, 0.0), axis=-1, keepdims=True)

    def to_row(col_vec):
        return jnp.sum(jnp.where(eye, col_vec, 0.0), axis=0, keepdims=True)

    for h in range(N_HEADS):
        ln = slice(h * n, (h + 1) * n)
        s0 = s_ref[0, h]
        kap = _head_norm(kk[:, ln])
        sa = jnp.sum(s0 * (-kap), axis=-1, keepdims=True)
        s_new = s0 * decay[:, ln] + sa * (kap * a[:, ln]) + to_col(v[:, ln]) * k2[:, ln]
        s_out_ref[0, h] = s_new
        y = to_row(jnp.sum(s_new * r[:, ln], axis=-1, keepdims=True))
        o_ref[0, :, ln] = _group_norm_out(y, r[:, ln], k2[:, ln], v[:, ln], g[:, ln], rk_ref[:, ln],
                                          lnw_ref[:, ln], lnb_ref[:, ln]).astype(o_ref.dtype)


def _rwkv_sample(p_rw, shift, state, weights):
    n = p_rw.shape[0]
    const = lambda s: (0, 0)
    vec = pl.BlockSpec((1, 1, RWKV_COLS), lambda s: (s, 0, 0))
    st = pl.BlockSpec((1, N_HEADS, HEAD_DIM, HEAD_DIM), lambda s: (s, 0, 0, 0))
    rw, s_new = pl.pallas_call(
        _rwkv_step_kernel,
        grid=(n,),
        in_specs=[vec, vec, st] + [pl.BlockSpec(w.shape, const) for w in weights],
        out_specs=[pl.BlockSpec((1, 1, WIDTH), lambda s: (s, 0, 0)), st],
        out_shape=[jax.ShapeDtypeStruct((n, 1, WIDTH), BF16),
                   jax.ShapeDtypeStruct(state.shape, F32)],
        compiler_params=_params("parallel"),
        name="rwkv_sample",
    )(p_rw.reshape(n, 1, RWKV_COLS), shift.reshape(n, 1, RWKV_COLS), state, *weights)
    return rw.reshape(n, WIDTH), s_new


def _sample_gate_kernel(q_ref, km_ref, idx_ref):
    rnd = lambda t: t.astype(BF16).astype(F32)
    prod = rnd(km_ref[0]) * rnd(q_ref[0])
    lane = lax.broadcasted_iota(jnp.int32, (WIDTH, 128), 0)
    hcol = lax.broadcasted_iota(jnp.int32, (WIDTH, 128), 1)
    head_sum = ((lane // HEAD_DIM) == hcol).astype(BF16)
    pieces = _split3(prod)
    gate = _dot(pieces[0], head_sum) + _dot(pieces[1], head_sum) + _dot(pieces[2], head_sum)
    rank = _block_rank(gate, 0)
    bidx = lax.broadcasted_iota(jnp.int32, gate.shape, 0)
    rows = [jnp.sum(jnp.where(rank == i, bidx, 0), axis=0, keepdims=True) for i in range(MOBA_TOP_K)]
    rows += [jnp.zeros((1, 128), jnp.int32)] * (8 - MOBA_TOP_K)
    idx_ref[0] = jnp.concatenate(rows, axis=0)


def _sample_gate(q, kmean):
    n, nb, _ = kmean.shape
    return pl.pallas_call(
        _sample_gate_kernel,
        grid=(n,),
        in_specs=[pl.BlockSpec((1, 1, WIDTH), lambda s: (s, 0, 0)),
                  pl.BlockSpec((1, nb, WIDTH), lambda s: (s, 0, 0))],
        out_specs=pl.BlockSpec((1, 8, 128), lambda s: (s, 0, 0)),
        out_shape=jax.ShapeDtypeStruct((n, 8, 128), jnp.int32),
        compiler_params=_params("parallel"),
        name="sample_gate",
    )(q.reshape(n, 1, WIDTH), kmean)


def _sample_attn_kernel(pt_ref, top_ref, q_ref, kn_ref, vn_ref, topv_ref, slope_ref, ck_hbm, cv_hbm, o_ref,
                        kbuf, vbuf, sems, *, past_len, n_pages):
    ppb = _PAGES_PER_BLOCK
    npg = MOBA_TOP_K * ppb
    seq = pl.program_id(0)
    nseq = pl.num_programs(0)
    slot = seq % 2

    def slab_copies(sq, sl, h, i):
        blk_id = top_ref[(sq * N_HEADS + h) * MOBA_TOP_K + i // ppb]
        page = pt_ref[sq * n_pages + blk_id * ppb + i % ppb]
        return (pltpu.make_async_copy(ck_hbm.at[page, h], kbuf.at[sl, h, i], sems.at[sl]),
                pltpu.make_async_copy(cv_hbm.at[page, h], vbuf.at[sl, h, i], sems.at[sl]))

    def start_all(sq, sl):
        def per_head(h, carry):
            for i in range(npg):
                for cp in slab_copies(sq, sl, h, i):
                    cp.start()
            return carry
        lax.fori_loop(0, N_HEADS, per_head, 0)

    @pl.when(seq == 0)
    def _():
        start_all(0, 0)

    @pl.when(seq + 1 < nseq)
    def _():
        start_all(seq + 1, 1 - slot)

    def wait_head(h, carry):
        for i in range(npg):
            for cp in slab_copies(seq, slot, h, i):
                cp.wait()
        return carry
    lax.fori_loop(0, N_HEADS, wait_head, 0)

    q = q_ref[0] * (HEAD_DIM ** -0.5)
    slope = slope_ref[...]
    lane = lax.broadcasted_iota(jnp.int32, (1, 1, PAGE_SIZE), 2)
    scores = []
    for i in range(npg):
        blk_id = topv_ref[0, i // ppb][:, :, None]
        dist = (past_len - blk_id * MOBA_BLOCK - (i % ppb) * PAGE_SIZE - lane).astype(F32)
        s = jnp.sum(kbuf[slot, :, i] * q, axis=1, keepdims=True)
        scores.append(s - slope * dist)
    s_self = jnp.sum(q * kn_ref[0], axis=1, keepdims=True)
    m = s_self
    for s in scores:
        m = jnp.maximum(m, jnp.max(s, axis=2, keepdims=True))
    p_self = jnp.exp(s_self - m)
    den = p_self
    accv = jnp.zeros((N_HEADS, HEAD_DIM, PAGE_SIZE), F32)
    for i, s in enumerate(scores):
        p = jnp.exp(s - m)
        den = den + jnp.sum(p, axis=2, keepdims=True)
        accv = accv + vbuf[slot, :, i] * p
    acc = jnp.sum(accv, axis=2, keepdims=True) + p_self * vn_ref[0]
    o_ref[0] = acc / den


def _sample_attn(page_table, top, slopes, q, k_new, v_new, cache_kt, cache_vt):
    n, n_pages = page_table.shape
    npg = MOBA_TOP_K * _PAGES_PER_BLOCK
    top_flat = top.transpose(0, 2, 1).reshape(-1)
    vec = pl.BlockSpec((1, N_HEADS, HEAD_DIM, 1), lambda s, pt, tp: (s, 0, 0, 0))
    grid_spec = pltpu.PrefetchScalarGridSpec(
        num_scalar_prefetch=2,
        grid=(n,),
        in_specs=[vec, vec, vec,
                  pl.BlockSpec((1, MOBA_TOP_K, N_HEADS, 1), lambda s, pt, tp: (s, 0, 0, 0)),
                  pl.BlockSpec((N_HEADS, 1, 1), lambda s, pt, tp: (0, 0, 0)),
                  pl.BlockSpec(memory_space=pl.ANY),
                  pl.BlockSpec(memory_space=pl.ANY)],
        out_specs=vec,
        scratch_shapes=[pltpu.VMEM((2, N_HEADS, npg, HEAD_DIM, PAGE_SIZE), F32),
                        pltpu.VMEM((2, N_HEADS, npg, HEAD_DIM, PAGE_SIZE), F32),
                        pltpu.SemaphoreType.DMA((2,))],
    )
    return pl.pallas_call(
        functools.partial(_sample_attn_kernel, past_len=n_pages * PAGE_SIZE, n_pages=n_pages),
        grid_spec=grid_spec,
        out_shape=jax.ShapeDtypeStruct((n, N_HEADS, HEAD_DIM, 1), F32),
        compiler_params=_params("arbitrary"),
        name="sample_attn",
    )(page_table.reshape(-1), top_flat, q, k_new, v_new, top[..., None], slopes.reshape(N_HEADS, 1, 1),
      cache_kt, cache_vt)


def kernel(x_prompt, x_sample, cache_k, cache_v, page_table, state_wkv, state_shift,
           norm_mix_g, w_in, mu_shift, decay_w0, decay_up, iclr_a0, iclr_up, gate_up,
           k_k, k_a, r_k, ln_x_w, ln_x_b, w_out, norm_ffn_g, w_ffn_up, w_ffn_down, norm_final_g):
    depth = w_in.shape[0]
    assert depth == 1
    batch, seq, _ = x_prompt.shape
    n_seq, n_pages = page_table.shape
    slopes = jnp.exp2(-8.0 * jnp.arange(1, N_HEADS + 1, dtype=F32) / N_HEADS)

    l = 0
    row = lambda t: t.reshape(1, -1)
    w_in_l = jnp.concatenate([w_in[l][:, :ATTN_COLS], _to_internal(w_in[l][:, ATTN_COLS:])], axis=1)
    g_mix = row(norm_mix_g[l])
    rw_weights = (row(_to_internal(mu_shift[l])), row(decay_w0[l]), decay_up[l], row(iclr_a0[l]), iclr_up[l],
                  gate_up[l], row(k_k[l]), row(k_a[l]), row(r_k[l]), row(ln_x_w[l]), row(ln_x_b[l]))
    wo = w_out[l].astype(BF16)
    ffn_weights = (wo[:WIDTH], wo[WIDTH:], row(norm_ffn_g[l]), w_ffn_up[l].astype(BF16),
                   w_ffn_down[l].astype(BF16), row(norm_final_g))

    xp = x_prompt.reshape(batch * seq, D_MODEL)
    w_in_b = w_in_l.astype(BF16)
    kt, vt, qt, ka, vtb, prw, kmean = _inproj_prompt(xp, g_mix, w_in_b, slopes, batch, seq)
    nb = seq // MOBA_BLOCK
    attn_p = _moba_prompt(slopes, qt, ka, vtb, kmean.reshape(batch, nb, WIDTH), batch, seq)
    rw_p, wkv_p = _rwkv_prompt(prw, rw_weights, batch, seq)
    ckt = cache_k.transpose(0, 1, 3, 4, 2).reshape(-1, N_HEADS, HEAD_DIM, PAGE_SIZE)
    cvt = cache_v.transpose(0, 1, 3, 4, 2).reshape(-1, N_HEADS, HEAD_DIM, PAGE_SIZE)
    y_prompt, kmean_s = _out_ffn(xp, attn_p, rw_p, *ffn_weights, tm=1024, th=512, stream=(page_table, ckt))
    y_prompt = y_prompt.reshape(batch, seq, D_MODEL)
    shift_p = _from_internal(prw.reshape(batch, seq, RWKV_COLS)[:, -1])

    hd = (N_HEADS, HEAD_DIM)
    xs = x_sample.reshape(n_seq, D_MODEL)
    proj_s = _inproj_small(xs, g_mix, w_in_b)
    q_s, k_s, v_s = proj_s[:, :WIDTH], proj_s[:, WIDTH:2 * WIDTH], proj_s[:, 2 * WIDTH:ATTN_COLS]
    prw_s = proj_s[:, ATTN_COLS:]
    top = _sample_gate(q_s, kmean_s)[:, :MOBA_TOP_K, :N_HEADS]
    col = lambda t: t.reshape(n_seq, N_HEADS, HEAD_DIM, 1)
    attn_s = _sample_attn(page_table, top, slopes, col(q_s), col(k_s), col(v_s), ckt, cvt).reshape(n_seq, WIDTH)
    rw_s, wkv_s = _rwkv_sample(prw_s, _to_internal(state_shift[l]), state_wkv[l], rw_weights)
    y_sample = _out_ffn(xs, attn_s.astype(BF16), rw_s, *ffn_weights, tm=n_seq).reshape(n_seq, 1, D_MODEL)
    shift_s = _from_internal(prw_s)

    return (y_prompt, y_sample,
            kt.transpose(0, 3, 1, 2)[None], vt.transpose(0, 3, 1, 2)[None],
            wkv_p[None], shift_p[None],
            k_s.reshape(1, n_seq, 1, *hd), v_s.reshape(1, n_seq, 1, *hd),
            wkv_s[None], shift_s[None])
```

```python
import functools

import jax
import jax.numpy as jnp
from jax import lax
from jax.experimental import pallas as pl
from jax.experimental.pallas import tpu as pltpu

F32 = jnp.float32
BF16 = jnp.bfloat16

D_MODEL = 1024
HEAD_DIM = 64
N_HEADS = 8
WIDTH = N_HEADS * HEAD_DIM
MOBA_BLOCK = 256
MOBA_TOP_K = 3
DECAY_LORA = 64
AAA_LORA = 64
GATE_LORA = 128
ATTN_COLS = 3 * WIDTH
RWKV_COLS = 3 * WIDTH + DECAY_LORA + AAA_LORA + GATE_LORA
RMS_EPS = 1e-6
GN_EPS = 64e-5
NEG_INF = -1e30
PAGE_SIZE = 128
RWKV_CHUNK = 64
VMEM_LIMIT = 48 * 1024 * 1024

_O_R, _O_K, _O_V = 0, WIDTH, 2 * WIDTH
_O_XW = 3 * WIDTH
_O_XA = _O_XW + DECAY_LORA
_O_XG = _O_XA + AAA_LORA


def _to_internal(t):
    r, xw, kv, rest = (t[..., :WIDTH], t[..., WIDTH:WIDTH + DECAY_LORA],
                       t[..., WIDTH + DECAY_LORA:3 * WIDTH + DECAY_LORA], t[..., 3 * WIDTH + DECAY_LORA:])
    return jnp.concatenate([r, kv, xw, rest], axis=-1)


def _from_internal(t):
    r, kv, xw, rest = (t[..., :WIDTH], t[..., WIDTH:3 * WIDTH],
                       t[..., 3 * WIDTH:3 * WIDTH + DECAY_LORA], t[..., 3 * WIDTH + DECAY_LORA:])
    return jnp.concatenate([r, xw, kv, rest], axis=-1)


def _params(*sem):
    return pltpu.CompilerParams(dimension_semantics=sem, vmem_limit_bytes=VMEM_LIMIT)


def _rms(x, g):
    return x * lax.rsqrt(jnp.mean(x * x, axis=-1, keepdims=True) + RMS_EPS) * g


def _dot(a, b, **kw):
    return jnp.dot(a, b, preferred_element_type=F32, **kw)


def _dot_nt(a, b, **kw):
    return lax.dot_general(a, b, (((1,), (1,)), ((), ())), preferred_element_type=F32, **kw)


def _dot_tn(a, b, **kw):
    return lax.dot_general(a, b, (((0,), (0,)), ((), ())), preferred_element_type=F32, **kw)


MOBA_VROWS = HEAD_DIM + 16


def _inproj_kernel(x_ref, g_ref, w_ref, srow_ref, kt_ref, vt_ref, qt_ref, ka_ref, vtb_ref, prw_ref, km_ref):
    tm = x_ref.shape[0]
    hd = (N_HEADS, HEAD_DIM, tm)
    xn = _rms(x_ref[...], g_ref[...]).astype(BF16)
    proj = _dot(xn, w_ref[...])
    q = proj[:, 0:WIDTH] * (HEAD_DIM ** -0.5)
    k = proj[:, WIDTH:2 * WIDTH]
    v = proj[:, 2 * WIDTH:3 * WIDTH]
    prw_ref[...] = proj[:, ATTN_COLS:]
    km_ref[0] = jnp.mean(k, axis=0, keepdims=True)
    k_t = k.T.reshape(hd)
    v_t = v.T.reshape(hd)
    kt_ref[0] = k_t
    vt_ref[0] = v_t
    vtb_ref[0, :, 0, 0:HEAD_DIM, :] = v_t.astype(BF16)
    ones_row = lax.broadcasted_iota(jnp.int32, (N_HEADS, MOBA_VROWS - HEAD_DIM, tm), 1) == 0
    vtb_ref[0, :, 0, HEAD_DIM:, :] = jnp.where(ones_row, 1.0, 0.0).astype(BF16)
    qt_ref[0, :, 0:HEAD_DIM, :] = q.T.reshape(hd).astype(BF16)
    qt_ref[0, :, HEAD_DIM:, :] = jnp.broadcast_to(srow_ref[...], hd).astype(BF16)
    lane = lax.broadcasted_iota(jnp.int32, (tm, 128 - HEAD_DIM), 1)
    pos = lax.broadcasted_iota(jnp.int32, (tm, 128 - HEAD_DIM), 0)
    pos_cols = jnp.where(lane == 0, pos, 0).astype(F32).astype(BF16)
    for h in range(N_HEADS):
        ka_ref[0, h, :, 0:HEAD_DIM] = k[:, h * HEAD_DIM:(h + 1) * HEAD_DIM].astype(BF16)
        ka_ref[0, h, :, HEAD_DIM:] = pos_cols


def _inproj_prompt(x, g, w_bf16, slopes, batch, seq):
    tm = MOBA_BLOCK
    m = x.shape[0]
    nb = seq // tm
    ncol = w_bf16.shape[1]
    row = lambda i: (i, 0)
    const = lambda i: (0, 0)
    tok = lambda i: (i // nb, 0, 0, i % nb)
    srow = jnp.zeros((N_HEADS, HEAD_DIM, 1), F32).at[:, 0, 0].set(slopes)
    return pl.pallas_call(
        _inproj_kernel,
        grid=(m // tm,),
        in_specs=[pl.BlockSpec((tm, D_MODEL), row),
                  pl.BlockSpec((1, D_MODEL), const),
                  pl.BlockSpec((D_MODEL, ncol), const),
                  pl.BlockSpec((N_HEADS, HEAD_DIM, 1), lambda i: (0, 0, 0))],
        out_specs=[pl.BlockSpec((1, N_HEADS, HEAD_DIM, tm), tok),
                   pl.BlockSpec((1, N_HEADS, HEAD_DIM, tm), tok),
                   pl.BlockSpec((1, N_HEADS, 128, tm), tok),
                   pl.BlockSpec((1, N_HEADS, tm, 128), lambda i: (i // nb, 0, i % nb, 0)),
                   pl.BlockSpec((1, N_HEADS, 1, MOBA_VROWS, tm), lambda i: (i // nb, 0, i % nb, 0, 0)),
                   pl.BlockSpec((tm, RWKV_COLS), row),
                   pl.BlockSpec((1, 1, WIDTH), lambda i: (i, 0, 0))],
        out_shape=[jax.ShapeDtypeStruct((batch, N_HEADS, HEAD_DIM, seq), F32),
                   jax.ShapeDtypeStruct((batch, N_HEADS, HEAD_DIM, seq), F32),
                   jax.ShapeDtypeStruct((batch, N_HEADS, 128, seq), BF16),
                   jax.ShapeDtypeStruct((batch, N_HEADS, seq, 128), BF16),
                   jax.ShapeDtypeStruct((batch, N_HEADS, nb, MOBA_VROWS, tm), BF16),
                   jax.ShapeDtypeStruct((m, RWKV_COLS), F32),
                   jax.ShapeDtypeStruct((m // tm, 1, WIDTH), F32)],
        compiler_params=_params("parallel"),
        name="inproj_prompt",
    )(x, g, w_bf16, srow)


def _inproj_small_kernel(x_ref, g_ref, w_ref, o_ref):
    xn = _rms(x_ref[...], g_ref[...]).astype(BF16)
    o_ref[...] = _dot(xn, w_ref[...])


def _inproj_small(x, g, w, tn=256):
    m = x.shape[0]
    ncol = w.shape[1]
    return pl.pallas_call(
        _inproj_small_kernel,
        grid=(ncol // tn,),
        in_specs=[pl.BlockSpec((m, D_MODEL), lambda j: (0, 0)),
                  pl.BlockSpec((1, D_MODEL), lambda j: (0, 0)),
                  pl.BlockSpec((D_MODEL, tn), lambda j: (0, j))],
        out_specs=pl.BlockSpec((m, tn), lambda j: (0, j)),
        out_shape=jax.ShapeDtypeStruct((m, ncol), F32),
        compiler_params=_params("parallel"),
        name="inproj_sample",
    )(x, g, w)


_STREAM_PAGES = 16
_PAGES_PER_BLOCK = MOBA_BLOCK // PAGE_SIZE


def _ffn_kernel(pt_ref, x_ref, attn_ref, rw_ref, woa_ref, wor_ref, gf_ref, wup_ref, wdn_ref, gfin_ref, *rest,
                n_stream):
    page_refs = rest[:n_stream]
    y_ref = rest[n_stream]
    km_ref = rest[n_stream + 1] if n_stream else None
    h_sc, hn_sc, acc_sc = rest[-3:]
    j = pl.program_id(1)

    @pl.when(j == 0)
    def _():
        h = x_ref[...] + _dot(attn_ref[...], woa_ref[...]) + _dot(rw_ref[...], wor_ref[...])
        h_sc[...] = h
        hn_sc[...] = _rms(h, gf_ref[...]).astype(BF16)
        acc_sc[...] = jnp.zeros_like(acc_sc)

    u = jnp.maximum(_dot(hn_sc[...], wup_ref[...]), 0.0)
    acc_sc[...] += _dot((u * u).astype(BF16), wdn_ref[...])

    if n_stream:
        ppb = _PAGES_PER_BLOCK
        bps = n_stream // ppb
        nb = km_ref.shape[1]
        g = (pl.program_id(0) * pl.num_programs(1) + j) % (nb // bps)
        for b in range(bps):
            tot = page_refs[b * ppb][0]
            for i in range(1, ppb):
                tot = tot + page_refs[b * ppb + i][0]
            tot_t = tot.reshape(WIDTH, PAGE_SIZE).T
            km_ref[0, pl.ds(g * bps + b, 1), :] = jnp.sum(tot_t, axis=0, keepdims=True) * (1.0 / MOBA_BLOCK)

    @pl.when(j == pl.num_programs(1) - 1)
    def _():
        y_ref[...] = _rms(h_sc[...] + acc_sc[...], gfin_ref[...])


def _out_ffn(x, attn, rw, woa, wor, gf, wup, wdn, gfin, tm, th=1024, stream=None):
    m = x.shape[0]
    hid = wup.shape[1]
    nj = hid // th
    row = lambda i, j, pt: (i, 0)
    const = lambda i, j, pt: (0, 0)
    in_specs = [pl.BlockSpec((tm, D_MODEL), row),
                pl.BlockSpec((tm, WIDTH), row),
                pl.BlockSpec((tm, WIDTH), row),
                pl.BlockSpec((WIDTH, D_MODEL), const),
                pl.BlockSpec((WIDTH, D_MODEL), const),
                pl.BlockSpec((1, D_MODEL), const),
                pl.BlockSpec((D_MODEL, th), lambda i, j, pt: (0, j)),
                pl.BlockSpec((th, D_MODEL), lambda i, j, pt: (j, 0)),
                pl.BlockSpec((1, D_MODEL), const)]
    out_specs = [pl.BlockSpec((tm, D_MODEL), row)]
    out_shape = [jax.ShapeDtypeStruct((m, D_MODEL), F32)]
    operands = [x, attn, rw, woa, wor, gf, wup, wdn, gfin]
    n_stream = 0
    pt_flat = jnp.zeros((1,), jnp.int32)
    if stream is not None:
        page_table, cache_kt = stream
        n_seq, n_pages = page_table.shape
        n_stream = _STREAM_PAGES
        spp = n_pages // n_stream
        assert (m // tm) * nj == n_seq * spp, "page streaming needs one grid step per 16 pages"
        nb = n_pages // _PAGES_PER_BLOCK
        pt_flat = page_table.reshape(-1)

        def page_spec(k):
            def imap(i, j, pt, k=k):
                t = i * nj + j
                return (pt[(t // spp) * n_pages + (t % spp) * n_stream + k], 0, 0, 0)
            return pl.BlockSpec((1, N_HEADS, HEAD_DIM, PAGE_SIZE), imap)

        in_specs += [page_spec(k) for k in range(n_stream)]
        operands += [cache_kt] * n_stream
        out_specs.append(pl.BlockSpec((1, nb, WIDTH), lambda i, j, pt: ((i * nj + j) // spp, 0, 0)))
        out_shape.append(jax.ShapeDtypeStruct((n_seq, nb, WIDTH), F32))
    grid_spec = pltpu.PrefetchScalarGridSpec(
        num_scalar_prefetch=1,
        grid=(m // tm, nj),
        in_specs=in_specs,
        out_specs=out_specs,
        scratch_shapes=[pltpu.VMEM((tm, D_MODEL), F32),
                        pltpu.VMEM((tm, D_MODEL), BF16),
                        pltpu.VMEM((tm, D_MODEL), F32)],
    )
    outs = pl.pallas_call(
        functools.partial(_ffn_kernel, n_stream=n_stream),
        grid_spec=grid_spec,
        out_shape=out_shape,
        compiler_params=_params("arbitrary", "arbitrary"),
        name="out_ffn",
    )(pt_flat, *operands)
    return outs if stream is not None else outs[0]


def _block_rank(gm, axis):
    nb = gm.shape[axis]
    idx = lax.broadcasted_iota(jnp.int32, gm.shape, axis)
    beats = []
    for m in range(nb):
        gmm = lax.slice_in_dim(gm, m, m + 1, axis=axis)
        beats.append(((gmm > gm) | ((gmm == gm) & (m < idx))).astype(jnp.int32))
    while len(beats) > 1:
        beats = [a + b for a, b in zip(beats[0::2], beats[1::2])] + ([beats[-1]] if len(beats) % 2 else [])
    return beats[0]


MOBA_HEADS = 8
BIG = 1e30


def _moba_kernel(slopes_ref, qt_ref, ka_ref, vt_ref, km_ref, o_ref, sel_sc, m_sc, acc_sc, *, nb):
    hg = pl.program_id(1)
    qi = pl.program_id(2)
    blk = MOBA_BLOCK
    n_top = min(MOBA_TOP_K, nb)
    keyi = lax.broadcasted_iota(jnp.int32, (blk, blk), 0)
    qryi = lax.broadcasted_iota(jnp.int32, (blk, blk), 1)
    causal = keyi <= qryi
    bidx = lax.broadcasted_iota(jnp.int32, (nb, blk), 0)
    q0 = pl.multiple_of(qi * blk, blk)

    hs = range(MOBA_HEADS)
    qts = [qt_ref[0, hh] for hh in hs]
    gates = [_dot(km_ref[0, hh], qts[hh]) for hh in hs]
    valid = bidx < qi
    ranks = [_block_rank(jnp.where(valid, g, NEG_INF), 0) for g in gates]
    for hh in hs:
        sel_sc[hh] = ((ranks[hh] < n_top) & valid).astype(F32)
    s0 = [jnp.where(causal, _dot(ka_ref[0, hh, pl.ds(q0, blk), :], qts[hh]), NEG_INF) for hh in hs]
    m0 = [jnp.max(t, axis=0, keepdims=True) for t in s0]
    p0 = [jnp.exp(s0[hh] - m0[hh]) for hh in hs]
    for hh in hs:
        m_sc[hh] = m0[hh]
    pv0 = [_dot(vt_ref[0, hh, qi], p0[hh].astype(BF16)) for hh in hs]
    for hh in hs:
        acc_sc[hh] = pv0[hh]

    def past_block(j, carry):
        k0 = pl.multiple_of(j * blk, blk)
        s = [_dot(ka_ref[0, hh, pl.ds(k0, blk), :], qt_ref[0, hh]) for hh in hs]
        ps, alphas = [], []
        for hh in hs:
            cj = -slopes_ref[hg * MOBA_HEADS + hh] * ((qi - j) * blk).astype(F32)
            picked = sel_sc[hh, pl.ds(j, 1), :] > 0.0
            m_old = m_sc[hh]
            m_new = jnp.maximum(m_old, jnp.where(picked, jnp.max(s[hh], axis=0, keepdims=True) + cj, NEG_INF))
            alpha = jnp.exp(m_old - m_new)
            p = jnp.exp(s[hh] - jnp.where(picked, m_new - cj, BIG))
            m_sc[hh] = m_new
            ps.append(p.astype(BF16))
            alphas.append(alpha)
        pv = [_dot(vt_ref[0, hh, j], ps[hh]) for hh in hs]
        for hh in hs:
            acc_sc[hh] = alphas[hh] * acc_sc[hh] + pv[hh]
        return carry

    lax.fori_loop(0, qi, past_block, 0)

    outs = [acc_sc[hh, 0:HEAD_DIM, :] / acc_sc[hh, HEAD_DIM:HEAD_DIM + 1, :] for hh in hs]
    o_ref[...] = jnp.concatenate([t.T for t in outs], axis=1).astype(o_ref.dtype)


def _moba_prompt(slopes, qt, ka, vt, kmean, batch, seq):
    nb = seq // MOBA_BLOCK
    blk = MOBA_BLOCK
    km = kmean.reshape(batch, nb, N_HEADS, HEAD_DIM).transpose(0, 2, 1, 3).astype(BF16)
    km = jnp.concatenate([km, jnp.zeros((batch, N_HEADS, nb, 128 - HEAD_DIM), BF16)], axis=3)

    hgn = MOBA_HEADS
    grid_spec = pltpu.PrefetchScalarGridSpec(
        num_scalar_prefetch=1,
        grid=(batch, N_HEADS // hgn, nb),
        in_specs=[pl.BlockSpec((1, hgn, 128, blk), lambda b, g, qi, s: (b, g, 0, qi)),
                  pl.BlockSpec((1, hgn, seq, 128), lambda b, g, qi, s: (b, g, 0, 0)),
                  pl.BlockSpec((1, hgn, nb, MOBA_VROWS, blk), lambda b, g, qi, s: (b, g, 0, 0, 0)),
                  pl.BlockSpec((1, hgn, nb, 128), lambda b, g, qi, s: (b, g, 0, 0))],
        out_specs=pl.BlockSpec((blk, hgn * HEAD_DIM), lambda b, g, qi, s: (b * nb + qi, g)),
        scratch_shapes=[pltpu.VMEM((hgn, nb, blk), F32),
                        pltpu.VMEM((hgn, 1, blk), F32),
                        pltpu.VMEM((hgn, MOBA_VROWS, blk), F32)],
    )
    return pl.pallas_call(
        functools.partial(_moba_kernel, nb=nb),
        grid_spec=grid_spec,
        out_shape=jax.ShapeDtypeStruct((batch * seq, WIDTH), BF16),
        compiler_params=_params("parallel", "parallel", "arbitrary"),
        name="moba_prompt",
    )(slopes, qt, ka, vt, km)


def _b(t):
    return t.astype(BF16)


def _split3(x):
    x1 = _b(x)
    r1 = x - x1.astype(F32)
    x2 = _b(r1)
    return x1, x2, _b(r1 - x2.astype(F32))


def _rwkv_pointwise(p, pprev, mu, w0, decay_up, a0, iclr_up, gate_up, k_k, k_a):
    xs = p + mu * (pprev - p)
    r = xs[:, _O_R:_O_R + WIDTH]
    k = xs[:, _O_K:_O_K + WIDTH]
    v = xs[:, _O_V:_O_V + WIDTH]
    xw = xs[:, _O_XW:_O_XW + DECAY_LORA]
    xa = xs[:, _O_XA:_O_XA + AAA_LORA]
    xg = xs[:, _O_XG:_O_XG + GATE_LORA]
    w = w0 + _dot(_b(jnp.tanh(xw)), _b(decay_up))
    w = -jax.nn.softplus(-w) - 0.5
    logdecay = -jnp.exp(w)
    a = jax.nn.sigmoid(a0 + _dot(_b(xa), _b(iclr_up)))
    g = _dot(_b(jax.nn.sigmoid(xg)), _b(gate_up))
    kk = k * k_k
    k2 = k * (1.0 + (a - 1.0) * k_a)
    return r, k2, v, kk, a, g, logdecay


def _head_sum(x):
    row = lax.broadcasted_iota(jnp.int32, (128, 128), 0)
    col = lax.broadcasted_iota(jnp.int32, (128, 128), 1)
    seg = ((row // HEAD_DIM) == (col // HEAD_DIM)).astype(BF16)
    hi = _b(x)
    lo = _b(x - hi.astype(F32))
    cols = [slice(g * 128, (g + 1) * 128) for g in range(x.shape[1] // 128)]
    return jnp.concatenate([_dot(hi[:, c], seg) + _dot(lo[:, c], seg) for c in cols], axis=1)


def _head_norm(kk_h):
    return kk_h * lax.rsqrt(jnp.maximum(jnp.sum(kk_h * kk_h, axis=-1, keepdims=True), 1e-24))


def _group_norm_out(y_h, r_h, k_h, v_h, g_h, rk_h, lnw_h, lnb_h):
    mean = jnp.mean(y_h, axis=-1, keepdims=True)
    var = jnp.mean(jnp.square(y_h - mean), axis=-1, keepdims=True)
    yn = (y_h - mean) * lax.rsqrt(var + GN_EPS) * lnw_h + lnb_h
    yn = yn + jnp.sum(r_h * k_h * rk_h, axis=-1, keepdims=True) * v_h
    return yn * g_h


def _unit_lower_inverse(mats):
    n = mats[0].shape[0]
    row = lax.broadcasted_iota(jnp.int32, (n, n), 0)
    col = lax.broadcasted_iota(jnp.int32, (n, n), 1)
    eye = (row == col).astype(F32)
    size = 16
    same = (row // size) == (col // size)
    pws = [jnp.where(same, a, 0.0) for a in mats]
    xs = [eye - pw for pw in pws]
    for _ in range(3):
        pwb = [_b(pw) for pw in pws]
        pws = [_dot(t, t) for t in pwb]
        xs = [x + _dot(_b(x), _b(pw)) for x, pw in zip(xs, pws)]
        yield
    while size < n:
        size2 = size * 2
        same2 = (row // size2) == (col // size2)
        keep = same2 & jnp.logical_not(same)
        xb = [_b(x) for x in xs]
        ox = [_b(_dot(_b(jnp.where(keep, a, 0.0)), t)) for a, t in zip(mats, xb)]
        xs = [x - _dot(t, o) for x, t, o in zip(xs, xb, ox)]
        same = same2
        size = size2
        yield
    return xs


def _cumsum_rows(x, seg):
    n = x.shape[0]
    row = lax.broadcasted_iota(jnp.int32, (n, n), 0)
    col = lax.broadcasted_iota(jnp.int32, (n, n), 1)
    tri = ((row >= col) & ((row // seg) == (col // seg))).astype(BF16)
    x1, x2, x3 = _split3(x)
    return _dot(tri, x1) + _dot(tri, x2) + _dot(tri, x3)


RWKV_STEP_CHUNKS = 2
_RWKV_BLOCK = RWKV_CHUNK * RWKV_STEP_CHUNKS
_KAPH, _RHAT, _KHAT, _KBAR, _BHAT, _BBAR, _VB = range(7)
_RHAT32, _BONUS, _GATE = range(3)


def _interleave(*gens):
    live = list(gens)
    while live:
        for gen in list(live):
            try:
                next(gen)
            except StopIteration:
                live.remove(gen)


def _rwkv_pointwise_stage(p, last_sc, w, ob_sc, of_sc, we_sc):
    mu, w0, dup, a0, iup, gup, k_k, k_a, r_k = w
    L = RWKV_CHUNK
    nck = RWKV_STEP_CHUNKS
    ts = _RWKV_BLOCK
    rowi = lax.broadcasted_iota(jnp.int32, p.shape, 0)
    pprev = jnp.where(rowi == 0, last_sc[0:1, :], pltpu.roll(p, 1, 0))
    last_sc[0:1, :] = p[ts - 1:ts, :]
    yield
    r, k2, v, kk, a, g, logdecay = _rwkv_pointwise(p, pprev, mu, w0, dup, a0, iup, gup, k_k, k_a)
    yield
    cum = _cumsum_rows(logdecay, L)
    ends = [cum[c * L + L - 1:c * L + L, :] for c in range(nck)]
    cum_end = jnp.concatenate([jnp.broadcast_to(e, (L, WIDTH)) for e in ends], axis=0)
    w_inc = jnp.exp(cum)
    w_exc = jnp.exp(cum - logdecay)
    w_inv = jnp.exp(-cum)
    w_tail = jnp.exp(cum_end - cum)
    for c in range(nck):
        we_sc[c:c + 1, :] = jnp.exp(ends[c])
    yield
    kap_all = kk * lax.rsqrt(jnp.maximum(_head_sum(kk * kk), 1e-24))
    bb_all = kap_all * a
    r_hat_all = r * w_inc
    ob_sc[_KAPH] = _b(kap_all * w_exc)
    ob_sc[_RHAT] = _b(r_hat_all)
    ob_sc[_KHAT] = _b(k2 * w_inv)
    ob_sc[_KBAR] = _b(k2 * w_tail)
    yield
    ob_sc[_BHAT] = _b(bb_all * w_inv)
    ob_sc[_BBAR] = _b(bb_all * w_tail)
    ob_sc[_VB] = _b(v)
    of_sc[_RHAT32] = r_hat_all
    of_sc[_BONUS] = _head_sum(r * k2 * r_k) * v
    of_sc[_GATE] = g


def _rwkv_matmul_stage(ob_sc, of_sc, we_sc, s_sc, y_sc, lnw, lnb, o_ref, rows_out):
    L = RWKV_CHUNK
    nck = RWKV_STEP_CHUNKS
    trow = lax.broadcasted_iota(jnp.int32, (L, L), 0)
    tcol = lax.broadcasted_iota(jnp.int32, (L, L), 1)
    lower_incl = trow >= tcol
    lower_strict = trow > tcol
    heads = range(N_HEADS)
    items = [(slice(c * L, (c + 1) * L), slice(h * HEAD_DIM, (h + 1) * HEAD_DIM))
             for c in range(nck) for h in heads]
    n = range(len(items))
    ld = lambda slot, it: ob_sc[slot, it[0], it[1]]
    lhs = [jnp.concatenate([ld(_KAPH, it), ld(_RHAT, it)], axis=0) for it in items]
    ak = [_dot_nt(lhs[i], ld(_KHAT, items[i])) for i in n]
    ab = [_dot_nt(lhs[i], ld(_BHAT, items[i])) for i in n]
    yield
    a_kr = [_b(jnp.concatenate([jnp.where(lower_strict, t[:L], 0.0), jnp.where(lower_incl, t[L:], 0.0)], axis=0))
            for t in ak]
    a_rb = [_b(jnp.where(lower_incl, t[L:], 0.0)) for t in ab]
    t_inv = yield from _unit_lower_inverse([jnp.where(lower_strict, t[:L], 0.0) for t in ab])
    t_inv = [_b(t) for t in t_inv]
    av = [_dot(a_kr[i], ld(_VB, items[i])) for i in n]
    pm = [_b(_dot(t_inv[i], ld(_KAPH, items[i]))) for i in n]
    yield
    qm = [_b(_dot(t_inv[i], _b(av[i][:L]))) for i in n]
    r_eff = [_b(of_sc[_RHAT32, items[i][0], items[i][1]] - _dot(a_rb[i], pm[i])) for i in n]
    ptb = [_b(_dot_tn(pm[i], ld(_BBAR, items[i]))) for i in n]
    yield
    y0 = [av[i][L:] - _dot(a_rb[i], qm[i]) for i in n]
    cm = [_dot_tn(ld(_VB, items[i]), ld(_KBAR, items[i])) - _dot_tn(qm[i], ld(_BBAR, items[i])) for i in n]
    yield
    state = [s_sc[h] for h in heads]
    for c in range(nck):
        w_end = we_sc[c:c + 1, :]
        sb = [_b(t) for t in state]
        ys = [_dot_nt(r_eff[c * N_HEADS + h], sb[h]) + y0[c * N_HEADS + h] for h in heads]
        state = [state[h] * w_end[:, items[h][1]] - _dot(sb[h], ptb[c * N_HEADS + h]) + cm[c * N_HEADS + h]
                 for h in heads]
        for h in heads:
            y_sc[items[c * N_HEADS + h]] = ys[h]
    for h in heads:
        s_sc[h] = state[h]
    yield
    y = y_sc[...]
    dev = y - _head_sum(y) * (1.0 / HEAD_DIM)
    var = _head_sum(dev * dev) * (1.0 / HEAD_DIM)
    yn = dev * lax.rsqrt(var + GN_EPS) * lnw + lnb
    o_ref[0, rows_out, :] = ((yn + of_sc[_BONUS]) * of_sc[_GATE]).astype(o_ref.dtype)


def _rwkv_chunk_kernel(p_ref, mu_ref, w0_ref, dup_ref, a0_ref, iup_ref, gup_ref, kk_ref, ka_ref,
                       rk_ref, lnw_ref, lnb_ref, o_ref, s_out_ref,
                       s_sc, last_sc, y_sc, xb_sc, xf_sc, xw_sc, yb_sc, yf_sc, yw_sc):
    step = pl.program_id(1)
    last = pl.num_programs(1) - 1
    ts = _RWKV_BLOCK

    @pl.when(step == 0)
    def _():
        s_sc[...] = jnp.zeros_like(s_sc)
        last_sc[...] = jnp.zeros_like(last_sc)
        yb_sc[...] = jnp.zeros_like(yb_sc)
        yf_sc[...] = jnp.zeros_like(yf_sc)
        yw_sc[...] = jnp.zeros_like(yw_sc)

    w = (mu_ref[...], w0_ref[...], dup_ref[...], a0_ref[...], iup_ref[...], gup_ref[...],
         kk_ref[...], ka_ref[...], rk_ref[...])
    lnw, lnb = lnw_ref[...], lnb_ref[...]
    _interleave(_rwkv_matmul_stage(yb_sc, yf_sc, yw_sc, s_sc, y_sc, lnw, lnb, o_ref, slice(0, ts)),
                _rwkv_pointwise_stage(p_ref[0:ts, :], last_sc, w, xb_sc, xf_sc, xw_sc))

    @pl.when(step == last)
    def _():
        s_out_ref[0] = s_sc[...]

    _interleave(_rwkv_matmul_stage(xb_sc, xf_sc, xw_sc, s_sc, y_sc, lnw, lnb, o_ref, slice(ts, 2 * ts)),
                _rwkv_pointwise_stage(p_ref[ts:2 * ts, :], last_sc, w, yb_sc, yf_sc, yw_sc))


def _rwkv_prompt(p_rw, weights, batch, seq):
    ts = _RWKV_BLOCK
    ns = seq // (2 * ts)
    const = lambda b, c: (0, 0)
    w_specs = [pl.BlockSpec(w.shape, const) for w in weights]
    operand_scratch = [pltpu.VMEM((7, ts, WIDTH), BF16), pltpu.VMEM((3, ts, WIDTH), F32), pltpu.VMEM((8, WIDTH), F32)]
    out, state = pl.pallas_call(
        _rwkv_chunk_kernel,
        grid=(batch, ns + 1),
        in_specs=[pl.BlockSpec((2 * ts, RWKV_COLS), lambda b, c: (b * ns + jnp.minimum(c, ns - 1), 0))] + w_specs,
        out_specs=[pl.BlockSpec((1, 2 * ts, WIDTH), lambda b, c: (b, c, 0)),
                   pl.BlockSpec((1, N_HEADS, HEAD_DIM, HEAD_DIM), lambda b, c: (b, 0, 0, 0))],
        out_shape=[jax.ShapeDtypeStruct((batch, seq + 2 * ts, WIDTH), BF16),
                   jax.ShapeDtypeStruct((batch, N_HEADS, HEAD_DIM, HEAD_DIM), F32)],
        scratch_shapes=[pltpu.VMEM((N_HEADS, HEAD_DIM, HEAD_DIM), F32),
                        pltpu.VMEM((8, RWKV_COLS), F32),
                        pltpu.VMEM((ts, WIDTH), F32)] + operand_scratch + operand_scratch,
        compiler_params=_params("parallel", "arbitrary"),
        name="rwkv_prompt",
    )(p_rw, *weights)
    return out[:, ts:ts + seq].reshape(batch * seq, WIDTH), state


def _rwkv_step_kernel(p_ref, sh_ref, s_ref, mu_ref, w0_ref, dup_ref, a0_ref, iup_ref, gup_ref, kk_ref,
                      ka_ref, rk_ref, lnw_ref, lnb_ref, o_ref, s_out_ref):
    p = jnp.broadcast_to(p_ref[0], (8, RWKV_COLS))
    pprev = jnp.broadcast_to(sh_ref[0], (8, RWKV_COLS))
    r, k2, v, kk, a, g, logdecay = (t[0:1] for t in _rwkv_pointwise(
        p, pprev, mu_ref[...], w0_ref[...], dup_ref[...], a0_ref[...], iup_ref[...], gup_ref[...],
        kk_ref[...], ka_ref[...]))
    decay = jnp.exp(logdecay)
    n = HEAD_DIM
    eye = lax.broadcasted_iota(jnp.int32, (n, n), 0) == lax.broadcasted_iota(jnp.int32, (n, n), 1)

    def to_col(row_vec):
        return jnp.sum(jnp.where(eye, row_vec, 0.0), axis=-1, keepdims=True)

    def to_row(col_vec):
        return jnp.sum(jnp.where(eye, col_vec, 0.0), axis=0, keepdims=True)

    for h in range(N_HEADS):
        ln = slice(h * n, (h + 1) * n)
        s0 = s_ref[0, h]
        kap = _head_norm(kk[:, ln])
        sa = jnp.sum(s0 * (-kap), axis=-1, keepdims=True)
        s_new = s0 * decay[:, ln] + sa * (kap * a[:, ln]) + to_col(v[:, ln]) * k2[:, ln]
        s_out_ref[0, h] = s_new
        y = to_row(jnp.sum(s_new * r[:, ln], axis=-1, keepdims=True))
        o_ref[0, :, ln] = _group_norm_out(y, r[:, ln], k2[:, ln], v[:, ln], g[:, ln], rk_ref[:, ln],
                                          lnw_ref[:, ln], lnb_ref[:, ln]).astype(o_ref.dtype)


def _rwkv_sample(p_rw, shift, state, weights):
    n = p_rw.shape[0]
    const = lambda s: (0, 0)
    vec = pl.BlockSpec((1, 1, RWKV_COLS), lambda s: (s, 0, 0))
    st = pl.BlockSpec((1, N_HEADS, HEAD_DIM, HEAD_DIM), lambda s: (s, 0, 0, 0))
    rw, s_new = pl.pallas_call(
        _rwkv_step_kernel,
        grid=(n,),
        in_specs=[vec, vec, st] + [pl.BlockSpec(w.shape, const) for w in weights],
        out_specs=[pl.BlockSpec((1, 1, WIDTH), lambda s: (s, 0, 0)), st],
        out_shape=[jax.ShapeDtypeStruct((n, 1, WIDTH), BF16),
                   jax.ShapeDtypeStruct(state.shape, F32)],
        compiler_params=_params("parallel"),
        name="rwkv_sample",
    )(p_rw.reshape(n, 1, RWKV_COLS), shift.reshape(n, 1, RWKV_COLS), state, *weights)
    return rw.reshape(n, WIDTH), s_new


def _sample_gate_kernel(q_ref, km_ref, idx_ref):
    rnd = lambda t: t.astype(BF16).astype(F32)
    prod = rnd(km_ref[0]) * rnd(q_ref[0])
    lane = lax.broadcasted_iota(jnp.int32, (WIDTH, 128), 0)
    hcol = lax.broadcasted_iota(jnp.int32, (WIDTH, 128), 1)
    head_sum = ((lane // HEAD_DIM) == hcol).astype(BF16)
    pieces = _split3(prod)
    gate = _dot(pieces[0], head_sum) + _dot(pieces[1], head_sum) + _dot(pieces[2], head_sum)
    rank = _block_rank(gate, 0)
    bidx = lax.broadcasted_iota(jnp.int32, gate.shape, 0)
    rows = [jnp.sum(jnp.where(rank == i, bidx, 0), axis=0, keepdims=True) for i in range(MOBA_TOP_K)]
    rows += [jnp.zeros((1, 128), jnp.int32)] * (8 - MOBA_TOP_K)
    idx_ref[0] = jnp.concatenate(rows, axis=0)


def _sample_gate(q, kmean):
    n, nb, _ = kmean.shape
    return pl.pallas_call(
        _sample_gate_kernel,
        grid=(n,),
        in_specs=[pl.BlockSpec((1, 1, WIDTH), lambda s: (s, 0, 0)),
                  pl.BlockSpec((1, nb, WIDTH), lambda s: (s, 0, 0))],
        out_specs=pl.BlockSpec((1, 8, 128), lambda s: (s, 0, 0)),
        out_shape=jax.ShapeDtypeStruct((n, 8, 128), jnp.int32),
        compiler_params=_params("parallel"),
        name="sample_gate",
    )(q.reshape(n, 1, WIDTH), kmean)


def _sample_attn_kernel(pt_ref, top_ref, q_ref, kn_ref, vn_ref, topv_ref, slope_ref, ck_hbm, cv_hbm, o_ref,
                        kbuf, vbuf, sems, *, past_len, n_pages):
    ppb = _PAGES_PER_BLOCK
    npg = MOBA_TOP_K * ppb
    seq = pl.program_id(0)
    nseq = pl.num_programs(0)
    slot = seq % 2

    def slab_copies(sq, sl, h, i):
        blk_id = top_ref[(sq * N_HEADS + h) * MOBA_TOP_K + i // ppb]
        page = pt_ref[sq * n_pages + blk_id * ppb + i % ppb]
        return (pltpu.make_async_copy(ck_hbm.at[page, h], kbuf.at[sl, h, i], sems.at[sl]),
                pltpu.make_async_copy(cv_hbm.at[page, h], vbuf.at[sl, h, i], sems.at[sl]))

    def start_all(sq, sl):
        def per_head(h, carry):
            for i in range(npg):
                for cp in slab_copies(sq, sl, h, i):
                    cp.start()
            return carry
        lax.fori_loop(0, N_HEADS, per_head, 0)

    @pl.when(seq == 0)
    def _():
        start_all(0, 0)

    @pl.when(seq + 1 < nseq)
    def _():
        start_all(seq + 1, 1 - slot)

    def wait_head(h, carry):
        for i in range(npg):
            for cp in slab_copies(seq, slot, h, i):
                cp.wait()
        return carry
    lax.fori_loop(0, N_HEADS, wait_head, 0)

    q = q_ref[0] * (HEAD_DIM ** -0.5)
    slope = slope_ref[...]
    lane = lax.broadcasted_iota(jnp.int32, (1, 1, PAGE_SIZE), 2)
    scores = []
    for i in range(npg):
        blk_id = topv_ref[0, i // ppb][:, :, None]
        dist = (past_len - blk_id * MOBA_BLOCK - (i % ppb) * PAGE_SIZE - lane).astype(F32)
        s = jnp.sum(kbuf[slot, :, i] * q, axis=1, keepdims=True)
        scores.append(s - slope * dist)
    s_self = jnp.sum(q * kn_ref[0], axis=1, keepdims=True)
    m = s_self
    for s in scores:
        m = jnp.maximum(m, jnp.max(s, axis=2, keepdims=True))
    p_self = jnp.exp(s_self - m)
    den = p_self
    accv = jnp.zeros((N_HEADS, HEAD_DIM, PAGE_SIZE), F32)
    for i, s in enumerate(scores):
        p = jnp.exp(s - m)
        den = den + jnp.sum(p, axis=2, keepdims=True)
        accv = accv + vbuf[slot, :, i] * p
    acc = jnp.sum(accv, axis=2, keepdims=True) + p_self * vn_ref[0]
    o_ref[0] = acc / den


def _sample_attn(page_table, top, slopes, q, k_new, v_new, cache_kt, cache_vt):
    n, n_pages = page_table.shape
    npg = MOBA_TOP_K * _PAGES_PER_BLOCK
    top_flat = top.transpose(0, 2, 1).reshape(-1)
    vec = pl.BlockSpec((1, N_HEADS, HEAD_DIM, 1), lambda s, pt, tp: (s, 0, 0, 0))
    grid_spec = pltpu.PrefetchScalarGridSpec(
        num_scalar_prefetch=2,
        grid=(n,),
        in_specs=[vec, vec, vec,
                  pl.BlockSpec((1, MOBA_TOP_K, N_HEADS, 1), lambda s, pt, tp: (s, 0, 0, 0)),
                  pl.BlockSpec((N_HEADS, 1, 1), lambda s, pt, tp: (0, 0, 0)),
                  pl.BlockSpec(memory_space=pl.ANY),
                  pl.BlockSpec(memory_space=pl.ANY)],
        out_specs=vec,
        scratch_shapes=[pltpu.VMEM((2, N_HEADS, npg, HEAD_DIM, PAGE_SIZE), F32),
                        pltpu.VMEM((2, N_HEADS, npg, HEAD_DIM, PAGE_SIZE), F32),
                        pltpu.SemaphoreType.DMA((2,))],
    )
    return pl.pallas_call(
        functools.partial(_sample_attn_kernel, past_len=n_pages * PAGE_SIZE, n_pages=n_pages),
        grid_spec=grid_spec,
        out_shape=jax.ShapeDtypeStruct((n, N_HEADS, HEAD_DIM, 1), F32),
        compiler_params=_params("arbitrary"),
        name="sample_attn",
    )(page_table.reshape(-1), top_flat, q, k_new, v_new, top[..., None], slopes.reshape(N_HEADS, 1, 1),
      cache_kt, cache_vt)


def kernel(x_prompt, x_sample, cache_k, cache_v, page_table, state_wkv, state_shift,
           norm_mix_g, w_in, mu_shift, decay_w0, decay_up, iclr_a0, iclr_up, gate_up,
           k_k, k_a, r_k, ln_x_w, ln_x_b, w_out, norm_ffn_g, w_ffn_up, w_ffn_down, norm_final_g):
    depth = w_in.shape[0]
    assert depth == 1
    batch, seq, _ = x_prompt.shape
    n_seq, n_pages = page_table.shape
    slopes = jnp.exp2(-8.0 * jnp.arange(1, N_HEADS + 1, dtype=F32) / N_HEADS)

    l = 0
    row = lambda t: t.reshape(1, -1)
    w_in_l = jnp.concatenate([w_in[l][:, :ATTN_COLS], _to_internal(w_in[l][:, ATTN_COLS:])], axis=1)
    g_mix = row(norm_mix_g[l])
    rw_weights = (row(_to_internal(mu_shift[l])), row(decay_w0[l]), decay_up[l], row(iclr_a0[l]), iclr_up[l],
                  gate_up[l], row(k_k[l]), row(k_a[l]), row(r_k[l]), row(ln_x_w[l]), row(ln_x_b[l]))
    wo = w_out[l].astype(BF16)
    ffn_weights = (wo[:WIDTH], wo[WIDTH:], row(norm_ffn_g[l]), w_ffn_up[l].astype(BF16),
                   w_ffn_down[l].astype(BF16), row(norm_final_g))

    xp = x_prompt.reshape(batch * seq, D_MODEL)
    w_in_b = w_in_l.astype(BF16)
    kt, vt, qt, ka, vtb, prw, kmean = _inproj_prompt(xp, g_mix, w_in_b, slopes, batch, seq)
    nb = seq // MOBA_BLOCK
    attn_p = _moba_prompt(slopes, qt, ka, vtb, kmean.reshape(batch, nb, WIDTH), batch, seq)
    rw_p, wkv_p = _rwkv_prompt(prw, rw_weights, batch, seq)
    ckt = cache_k.transpose(0, 1, 3, 4, 2).reshape(-1, N_HEADS, HEAD_DIM, PAGE_SIZE)
    cvt = cache_v.transpose(0, 1, 3, 4, 2).reshape(-1, N_HEADS, HEAD_DIM, PAGE_SIZE)
    y_prompt, kmean_s = _out_ffn(xp, attn_p, rw_p, *ffn_weights, tm=1024, th=512, stream=(page_table, ckt))
    y_prompt = y_prompt.reshape(batch, seq, D_MODEL)
    shift_p = _from_internal(prw.reshape(batch, seq, RWKV_COLS)[:, -1])

    hd = (N_HEADS, HEAD_DIM)
    xs = x_sample.reshape(n_seq, D_MODEL)
    proj_s = _inproj_small(xs, g_mix, w_in_b)
    q_s, k_s, v_s = proj_s[:, :WIDTH], proj_s[:, WIDTH:2 * WIDTH], proj_s[:, 2 * WIDTH:ATTN_COLS]
    prw_s = proj_s[:, ATTN_COLS:]
    top = _sample_gate(q_s, kmean_s)[:, :MOBA_TOP_K, :N_HEADS]
    col = lambda t: t.reshape(n_seq, N_HEADS, HEAD_DIM, 1)
    attn_s = _sample_attn(page_table, top, slopes, col(q_s), col(k_s), col(v_s), ckt, cvt).reshape(n_seq, WIDTH)
    rw_s, wkv_s = _rwkv_sample(prw_s, _to_internal(state_shift[l]), state_wkv[l], rw_weights)
    y_sample = _out_ffn(xs, attn_s.astype(BF16), rw_s, *ffn_weights, tm=n_seq).reshape(n_seq, 1, D_MODEL)
    shift_s = _from_internal(prw_s)

    return (y_prompt, y_sample,
            kt.transpose(0, 3, 1, 2)[None], vt.transpose(0, 3, 1, 2)[None],
            wkv_p[None], shift_p[None],
            k_s.reshape(1, n_seq, 1, *hd), v_s.reshape(1, n_seq, 1, *hd),
            wkv_s[None], shift_s[None])
```

```python
import functools

import jax
import jax.numpy as jnp
from jax import lax
from jax.experimental import pallas as pl
from jax.experimental.pallas import tpu as pltpu

F32 = jnp.float32
BF16 = jnp.bfloat16

D_MODEL = 1024
HEAD_DIM = 64
N_HEADS = 8
WIDTH = N_HEADS * HEAD_DIM
MOBA_BLOCK = 256
MOBA_TOP_K = 3
DECAY_LORA = 64
AAA_LORA = 64
GATE_LORA = 128
ATTN_COLS = 3 * WIDTH
RWKV_COLS = 3 * WIDTH + DECAY_LORA + AAA_LORA + GATE_LORA
RMS_EPS = 1e-6
GN_EPS = 64e-5
NEG_INF = -1e30
PAGE_SIZE = 128
RWKV_CHUNK = 64
VMEM_LIMIT = 48 * 1024 * 1024

_O_R, _O_K, _O_V = 0, WIDTH, 2 * WIDTH
_O_XW = 3 * WIDTH
_O_XA = _O_XW + DECAY_LORA
_O_XG = _O_XA + AAA_LORA


def _to_internal(t):
    r, xw, kv, rest = (t[..., :WIDTH], t[..., WIDTH:WIDTH + DECAY_LORA],
                       t[..., WIDTH + DECAY_LORA:3 * WIDTH + DECAY_LORA], t[..., 3 * WIDTH + DECAY_LORA:])
    return jnp.concatenate([r, kv, xw, rest], axis=-1)


def _from_internal(t):
    r, kv, xw, rest = (t[..., :WIDTH], t[..., WIDTH:3 * WIDTH],
                       t[..., 3 * WIDTH:3 * WIDTH + DECAY_LORA], t[..., 3 * WIDTH + DECAY_LORA:])
    return jnp.concatenate([r, xw, kv, rest], axis=-1)


def _params(*sem):
    return pltpu.CompilerParams(dimension_semantics=sem, vmem_limit_bytes=VMEM_LIMIT)


def _rms(x, g):
    return x * lax.rsqrt(jnp.mean(x * x, axis=-1, keepdims=True) + RMS_EPS) * g


def _dot(a, b, **kw):
    return jnp.dot(a, b, preferred_element_type=F32, **kw)


def _dot_nt(a, b, **kw):
    return lax.dot_general(a, b, (((1,), (1,)), ((), ())), preferred_element_type=F32, **kw)


def _dot_tn(a, b, **kw):
    return lax.dot_general(a, b, (((0,), (0,)), ((), ())), preferred_element_type=F32, **kw)


MOBA_VROWS = HEAD_DIM + 16


def _inproj_kernel(x_ref, g_ref, w_ref, srow_ref, kt_ref, vt_ref, qt_ref, ka_ref, vtb_ref, prw_ref, km_ref):
    tm = x_ref.shape[0]
    hd = (N_HEADS, HEAD_DIM, tm)
    xn = _rms(x_ref[...], g_ref[...]).astype(BF16)
    proj = _dot(xn, w_ref[...])
    q = proj[:, 0:WIDTH] * (HEAD_DIM ** -0.5)
    k = proj[:, WIDTH:2 * WIDTH]
    v = proj[:, 2 * WIDTH:3 * WIDTH]
    prw_ref[...] = proj[:, ATTN_COLS:]
    km_ref[0] = jnp.mean(k, axis=0, keepdims=True)
    k_t = k.T.reshape(hd)
    v_t = v.T.reshape(hd)
    kt_ref[0] = k_t
    vt_ref[0] = v_t
    vtb_ref[0, :, 0, 0:HEAD_DIM, :] = v_t.astype(BF16)
    ones_row = lax.broadcasted_iota(jnp.int32, (N_HEADS, MOBA_VROWS - HEAD_DIM, tm), 1) == 0
    vtb_ref[0, :, 0, HEAD_DIM:, :] = jnp.where(ones_row, 1.0, 0.0).astype(BF16)
    qt_ref[0, :, 0:HEAD_DIM, :] = q.T.reshape(hd).astype(BF16)
    qt_ref[0, :, HEAD_DIM:, :] = jnp.broadcast_to(srow_ref[...], hd).astype(BF16)
    lane = lax.broadcasted_iota(jnp.int32, (tm, 128 - HEAD_DIM), 1)
    pos = lax.broadcasted_iota(jnp.int32, (tm, 128 - HEAD_DIM), 0)
    pos_cols = jnp.where(lane == 0, pos, 0).astype(F32).astype(BF16)
    for h in range(N_HEADS):
        ka_ref[0, h, :, 0:HEAD_DIM] = k[:, h * HEAD_DIM:(h + 1) * HEAD_DIM].astype(BF16)
        ka_ref[0, h, :, HEAD_DIM:] = pos_cols


def _inproj_prompt(x, g, w_bf16, slopes, batch, seq):
    tm = MOBA_BLOCK
    m = x.shape[0]
    nb = seq // tm
    ncol = w_bf16.shape[1]
    row = lambda i: (i, 0)
    const = lambda i: (0, 0)
    tok = lambda i: (i // nb, 0, 0, i % nb)
    srow = jnp.zeros((N_HEADS, HEAD_DIM, 1), F32).at[:, 0, 0].set(slopes)
    return pl.pallas_call(
        _inproj_kernel,
        grid=(m // tm,),
        in_specs=[pl.BlockSpec((tm, D_MODEL), row),
                  pl.BlockSpec((1, D_MODEL), const),
                  pl.BlockSpec((D_MODEL, ncol), const),
                  pl.BlockSpec((N_HEADS, HEAD_DIM, 1), lambda i: (0, 0, 0))],
        out_specs=[pl.BlockSpec((1, N_HEADS, HEAD_DIM, tm), tok),
                   pl.BlockSpec((1, N_HEADS, HEAD_DIM, tm), tok),
                   pl.BlockSpec((1, N_HEADS, 128, tm), tok),
                   pl.BlockSpec((1, N_HEADS, tm, 128), lambda i: (i // nb, 0, i % nb, 0)),
                   pl.BlockSpec((1, N_HEADS, 1, MOBA_VROWS, tm), lambda i: (i // nb, 0, i % nb, 0, 0)),
                   pl.BlockSpec((tm, RWKV_COLS), row),
                   pl.BlockSpec((1, 1, WIDTH), lambda i: (i, 0, 0))],
        out_shape=[jax.ShapeDtypeStruct((batch, N_HEADS, HEAD_DIM, seq), F32),
                   jax.ShapeDtypeStruct((batch, N_HEADS, HEAD_DIM, seq), F32),
                   jax.ShapeDtypeStruct((batch, N_HEADS, 128, seq), BF16),
                   jax.ShapeDtypeStruct((batch, N_HEADS, seq, 128), BF16),
                   jax.ShapeDtypeStruct((batch, N_HEADS, nb, MOBA_VROWS, tm), BF16),
                   jax.ShapeDtypeStruct((m, RWKV_COLS), F32),
                   jax.ShapeDtypeStruct((m // tm, 1, WIDTH), F32)],
        compiler_params=_params("parallel"),
        name="inproj_prompt",
    )(x, g, w_bf16, srow)


def _inproj_small_kernel(x_ref, g_ref, w_ref, o_ref):
    xn = _rms(x_ref[...], g_ref[...]).astype(BF16)
    o_ref[...] = _dot(xn, w_ref[...])


def _inproj_small(x, g, w, tn=256):
    m = x.shape[0]
    ncol = w.shape[1]
    return pl.pallas_call(
        _inproj_small_kernel,
        grid=(ncol // tn,),
        in_specs=[pl.BlockSpec((m, D_MODEL), lambda j: (0, 0)),
                  pl.BlockSpec((1, D_MODEL), lambda j: (0, 0)),
                  pl.BlockSpec((D_MODEL, tn), lambda j: (0, j))],
        out_specs=pl.BlockSpec((m, tn), lambda j: (0, j)),
        out_shape=jax.ShapeDtypeStruct((m, ncol), F32),
        compiler_params=_params("parallel"),
        name="inproj_sample",
    )(x, g, w)


_STREAM_PAGES = 16
_PAGES_PER_BLOCK = MOBA_BLOCK // PAGE_SIZE


def _ffn_kernel(pt_ref, x_ref, attn_ref, rw_ref, woa_ref, wor_ref, gf_ref, wup_ref, wdn_ref, gfin_ref, *rest,
                n_stream):
    page_refs = rest[:n_stream]
    y_ref = rest[n_stream]
    km_ref = rest[n_stream + 1] if n_stream else None
    h_sc, hn_sc, acc_sc = rest[-3:]
    j = pl.program_id(1)

    @pl.when(j == 0)
    def _():
        h = x_ref[...] + _dot(attn_ref[...], woa_ref[...]) + _dot(rw_ref[...], wor_ref[...])
        h_sc[...] = h
        hn_sc[...] = _rms(h, gf_ref[...]).astype(BF16)
        acc_sc[...] = jnp.zeros_like(acc_sc)

    u = jnp.maximum(_dot(hn_sc[...], wup_ref[...]), 0.0)
    acc_sc[...] += _dot((u * u).astype(BF16), wdn_ref[...])

    if n_stream:
        ppb = _PAGES_PER_BLOCK
        bps = n_stream // ppb
        nb = km_ref.shape[1]
        g = (pl.program_id(0) * pl.num_programs(1) + j) % (nb // bps)
        for b in range(bps):
            tot = page_refs[b * ppb][0]
            for i in range(1, ppb):
                tot = tot + page_refs[b * ppb + i][0]
            tot_t = tot.reshape(WIDTH, PAGE_SIZE).T
            km_ref[0, pl.ds(g * bps + b, 1), :] = jnp.sum(tot_t, axis=0, keepdims=True) * (1.0 / MOBA_BLOCK)

    @pl.when(j == pl.num_programs(1) - 1)
    def _():
        y_ref[...] = _rms(h_sc[...] + acc_sc[...], gfin_ref[...])


def _out_ffn(x, attn, rw, woa, wor, gf, wup, wdn, gfin, tm, th=1024, stream=None):
    m = x.shape[0]
    hid = wup.shape[1]
    nj = hid // th
    row = lambda i, j, pt: (i, 0)
    const = lambda i, j, pt: (0, 0)
    in_specs = [pl.BlockSpec((tm, D_MODEL), row),
                pl.BlockSpec((tm, WIDTH), row),
                pl.BlockSpec((tm, WIDTH), row),
                pl.BlockSpec((WIDTH, D_MODEL), const),
                pl.BlockSpec((WIDTH, D_MODEL), const),
                pl.BlockSpec((1, D_MODEL), const),
                pl.BlockSpec((D_MODEL, th), lambda i, j, pt: (0, j)),
                pl.BlockSpec((th, D_MODEL), lambda i, j, pt: (j, 0)),
                pl.BlockSpec((1, D_MODEL), const)]
    out_specs = [pl.BlockSpec((tm, D_MODEL), row)]
    out_shape = [jax.ShapeDtypeStruct((m, D_MODEL), F32)]
    operands = [x, attn, rw, woa, wor, gf, wup, wdn, gfin]
    n_stream = 0
    pt_flat = jnp.zeros((1,), jnp.int32)
    if stream is not None:
        page_table, cache_kt = stream
        n_seq, n_pages = page_table.shape
        n_stream = _STREAM_PAGES
        spp = n_pages // n_stream
        assert (m // tm) * nj == n_seq * spp, "page streaming needs one grid step per 16 pages"
        nb = n_pages // _PAGES_PER_BLOCK
        pt_flat = page_table.reshape(-1)

        def page_spec(k):
            def imap(i, j, pt, k=k):
                t = i * nj + j
                return (pt[(t // spp) * n_pages + (t % spp) * n_stream + k], 0, 0, 0)
            return pl.BlockSpec((1, N_HEADS, HEAD_DIM, PAGE_SIZE), imap)

        in_specs += [page_spec(k) for k in range(n_stream)]
        operands += [cache_kt] * n_stream
        out_specs.append(pl.BlockSpec((1, nb, WIDTH), lambda i, j, pt: ((i * nj + j) // spp, 0, 0)))
        out_shape.append(jax.ShapeDtypeStruct((n_seq, nb, WIDTH), F32))
    grid_spec = pltpu.PrefetchScalarGridSpec(
        num_scalar_prefetch=1,
        grid=(m // tm, nj),
        in_specs=in_specs,
        out_specs=out_specs,
        scratch_shapes=[pltpu.VMEM((tm, D_MODEL), F32),
                        pltpu.VMEM((tm, D_MODEL), BF16),
                        pltpu.VMEM((tm, D_MODEL), F32)],
    )
    outs = pl.pallas_call(
        functools.partial(_ffn_kernel, n_stream=n_stream),
        grid_spec=grid_spec,
        out_shape=out_shape,
        compiler_params=_params("arbitrary", "arbitrary"),
        name="out_ffn",
    )(pt_flat, *operands)
    return outs if stream is not None else outs[0]


def _block_rank(gm, axis):
    nb = gm.shape[axis]
    idx = lax.broadcasted_iota(jnp.int32, gm.shape, axis)
    beats = []
    for m in range(nb):
        gmm = lax.slice_in_dim(gm, m, m + 1, axis=axis)
        beats.append(((gmm > gm) | ((gmm == gm) & (m < idx))).astype(jnp.int32))
    while len(beats) > 1:
        beats = [a + b for a, b in zip(beats[0::2], beats[1::2])] + ([beats[-1]] if len(beats) % 2 else [])
    return beats[0]


MOBA_HEADS = 8
BIG = 1e30


def _moba_kernel(slopes_ref, qt_ref, ka_ref, vt_ref, km_ref, o_ref, sel_sc, m_sc, acc_sc, *, nb):
    hg = pl.program_id(1)
    qi = pl.program_id(2)
    blk = MOBA_BLOCK
    n_top = min(MOBA_TOP_K, nb)
    keyi = lax.broadcasted_iota(jnp.int32, (blk, blk), 0)
    qryi = lax.broadcasted_iota(jnp.int32, (blk, blk), 1)
    causal = keyi <= qryi
    bidx = lax.broadcasted_iota(jnp.int32, (nb, blk), 0)
    q0 = pl.multiple_of(qi * blk, blk)

    hs = range(MOBA_HEADS)
    qts = [qt_ref[0, hh] for hh in hs]
    gates = [_dot(km_ref[0, hh], qts[hh]) for hh in hs]
    valid = bidx < qi
    ranks = [_block_rank(jnp.where(valid, g, NEG_INF), 0) for g in gates]
    for hh in hs:
        sel_sc[hh] = ((ranks[hh] < n_top) & valid).astype(F32)
    s0 = [jnp.where(causal, _dot(ka_ref[0, hh, pl.ds(q0, blk), :], qts[hh]), NEG_INF) for hh in hs]
    m0 = [jnp.max(t, axis=0, keepdims=True) for t in s0]
    p0 = [jnp.exp(s0[hh] - m0[hh]) for hh in hs]
    for hh in hs:
        m_sc[hh] = m0[hh]
    pv0 = [_dot(vt_ref[0, hh, qi], p0[hh].astype(BF16)) for hh in hs]
    for hh in hs:
        acc_sc[hh] = pv0[hh]

    def past_block(j, carry):
        k0 = pl.multiple_of(j * blk, blk)
        s = [_dot(ka_ref[0, hh, pl.ds(k0, blk), :], qt_ref[0, hh]) for hh in hs]
        ps, alphas = [], []
        for hh in hs:
            cj = -slopes_ref[hg * MOBA_HEADS + hh] * ((qi - j) * blk).astype(F32)
            picked = sel_sc[hh, pl.ds(j, 1), :] > 0.0
            m_old = m_sc[hh]
            m_new = jnp.maximum(m_old, jnp.where(picked, jnp.max(s[hh], axis=0, keepdims=True) + cj, NEG_INF))
            alpha = jnp.exp(m_old - m_new)
            p = jnp.exp(s[hh] - jnp.where(picked, m_new - cj, BIG))
            m_sc[hh] = m_new
            ps.append(p.astype(BF16))
            alphas.append(alpha)
        pv = [_dot(vt_ref[0, hh, j], ps[hh]) for hh in hs]
        for hh in hs:
            acc_sc[hh] = alphas[hh] * acc_sc[hh] + pv[hh]
        return carry

    lax.fori_loop(0, qi, past_block, 0)

    outs = [acc_sc[hh, 0:HEAD_DIM, :] / acc_sc[hh, HEAD_DIM:HEAD_DIM + 1, :] for hh in hs]
    o_ref[...] = jnp.concatenate([t.T for t in outs], axis=1).astype(o_ref.dtype)


def _moba_prompt(slopes, qt, ka, vt, kmean, batch, seq):
    nb = seq // MOBA_BLOCK
    blk = MOBA_BLOCK
    km = kmean.reshape(batch, nb, N_HEADS, HEAD_DIM).transpose(0, 2, 1, 3).astype(BF16)
    km = jnp.concatenate([km, jnp.zeros((batch, N_HEADS, nb, 128 - HEAD_DIM), BF16)], axis=3)

    hgn = MOBA_HEADS
    grid_spec = pltpu.PrefetchScalarGridSpec(
        num_scalar_prefetch=1,
        grid=(batch, N_HEADS // hgn, nb),
        in_specs=[pl.BlockSpec((1, hgn, 128, blk), lambda b, g, qi, s: (b, g, 0, qi)),
                  pl.BlockSpec((1, hgn, seq, 128), lambda b, g, qi, s: (b, g, 0, 0)),
                  pl.BlockSpec((1, hgn, nb, MOBA_VROWS, blk), lambda b, g, qi, s: (b, g, 0, 0, 0)),
                  pl.BlockSpec((1, hgn, nb, 128), lambda b, g, qi, s: (b, g, 0, 0))],
        out_specs=pl.BlockSpec((blk, hgn * HEAD_DIM), lambda b, g, qi, s: (b * nb + qi, g)),
        scratch_shapes=[pltpu.VMEM((hgn, nb, blk), F32),
                        pltpu.VMEM((hgn, 1, blk), F32),
                        pltpu.VMEM((hgn, MOBA_VROWS, blk), F32)],
    )
    return pl.pallas_call(
        functools.partial(_moba_kernel, nb=nb),
        grid_spec=grid_spec,
        out_shape=jax.ShapeDtypeStruct((batch * seq, WIDTH), BF16),
        compiler_params=_params("parallel", "parallel", "arbitrary"),
        name="moba_prompt",
    )(slopes, qt, ka, vt, km)


def _b(t):
    return t.astype(BF16)


def _split3(x):
    x1 = _b(x)
    r1 = x - x1.astype(F32)
    x2 = _b(r1)
    return x1, x2, _b(r1 - x2.astype(F32))


def _rwkv_pointwise(p, pprev, mu, w0, decay_up, a0, iclr_up, gate_up, k_k, k_a):
    xs = p + mu * (pprev - p)
    r = xs[:, _O_R:_O_R + WIDTH]
    k = xs[:, _O_K:_O_K + WIDTH]
    v = xs[:, _O_V:_O_V + WIDTH]
    xw = xs[:, _O_XW:_O_XW + DECAY_LORA]
    xa = xs[:, _O_XA:_O_XA + AAA_LORA]
    xg = xs[:, _O_XG:_O_XG + GATE_LORA]
    w = w0 + _dot(_b(jnp.tanh(xw)), _b(decay_up))
    w = -jax.nn.softplus(-w) - 0.5
    logdecay = -jnp.exp(w)
    a = jax.nn.sigmoid(a0 + _dot(_b(xa), _b(iclr_up)))
    g = _dot(_b(jax.nn.sigmoid(xg)), _b(gate_up))
    kk = k * k_k
    k2 = k * (1.0 + (a - 1.0) * k_a)
    return r, k2, v, kk, a, g, logdecay


def _head_sum(x):
    row = lax.broadcasted_iota(jnp.int32, (128, 128), 0)
    col = lax.broadcasted_iota(jnp.int32, (128, 128), 1)
    seg = ((row // HEAD_DIM) == (col // HEAD_DIM)).astype(BF16)
    hi = _b(x)
    lo = _b(x - hi.astype(F32))
    cols = [slice(g * 128, (g + 1) * 128) for g in range(x.shape[1] // 128)]
    return jnp.concatenate([_dot(hi[:, c], seg) + _dot(lo[:, c], seg) for c in cols], axis=1)


def _head_norm(kk_h):
    return kk_h * lax.rsqrt(jnp.maximum(jnp.sum(kk_h * kk_h, axis=-1, keepdims=True), 1e-24))


def _group_norm_out(y_h, r_h, k_h, v_h, g_h, rk_h, lnw_h, lnb_h):
    mean = jnp.mean(y_h, axis=-1, keepdims=True)
    var = jnp.mean(jnp.square(y_h - mean), axis=-1, keepdims=True)
    yn = (y_h - mean) * lax.rsqrt(var + GN_EPS) * lnw_h + lnb_h
    yn = yn + jnp.sum(r_h * k_h * rk_h, axis=-1, keepdims=True) * v_h
    return yn * g_h


def _unit_lower_inverse(mats):
    n = mats[0].shape[0]
    row = lax.broadcasted_iota(jnp.int32, (n, n), 0)
    col = lax.broadcasted_iota(jnp.int32, (n, n), 1)
    eye = (row == col).astype(F32)
    size = 16
    same = (row // size) == (col // size)
    pws = [jnp.where(same, a, 0.0) for a in mats]
    xs = [eye - pw for pw in pws]
    for _ in range(3):
        pwb = [_b(pw) for pw in pws]
        pws = [_dot(t, t) for t in pwb]
        xs = [x + _dot(_b(x), _b(pw)) for x, pw in zip(xs, pws)]
        yield
    while size < n:
        size2 = size * 2
        same2 = (row // size2) == (col // size2)
        keep = same2 & jnp.logical_not(same)
        xb = [_b(x) for x in xs]
        ox = [_b(_dot(_b(jnp.where(keep, a, 0.0)), t)) for a, t in zip(mats, xb)]
        xs = [x - _dot(t, o) for x, t, o in zip(xs, xb, ox)]
        same = same2
        size = size2
        yield
    return xs


def _cumsum_rows(x, seg):
    n = x.shape[0]
    row = lax.broadcasted_iota(jnp.int32, (n, n), 0)
    col = lax.broadcasted_iota(jnp.int32, (n, n), 1)
    tri = ((row >= col) & ((row // seg) == (col // seg))).astype(BF16)
    x1, x2, x3 = _split3(x)
    return _dot(tri, x1) + _dot(tri, x2) + _dot(tri, x3)


RWKV_STEP_CHUNKS = 4
_RWKV_BLOCK = RWKV_CHUNK * RWKV_STEP_CHUNKS
_KAPH, _RHAT, _KHAT, _KBAR, _BHAT, _BBAR, _VB = range(7)
_RHAT32, _BONUS, _GATE = range(3)


def _interleave(*gens):
    live = list(gens)
    while live:
        for gen in list(live):
            try:
                next(gen)
            except StopIteration:
                live.remove(gen)


def _rwkv_pointwise_stage(p, last_sc, w, ob_sc, of_sc, we_sc):
    mu, w0, dup, a0, iup, gup, k_k, k_a, r_k = w
    L = RWKV_CHUNK
    nck = RWKV_STEP_CHUNKS
    ts = _RWKV_BLOCK
    rowi = lax.broadcasted_iota(jnp.int32, p.shape, 0)
    pprev = jnp.where(rowi == 0, last_sc[0:1, :], pltpu.roll(p, 1, 0))
    last_sc[0:1, :] = p[ts - 1:ts, :]
    yield
    r, k2, v, kk, a, g, logdecay = _rwkv_pointwise(p, pprev, mu, w0, dup, a0, iup, gup, k_k, k_a)
    yield
    cum = _cumsum_rows(logdecay, L)
    ends = [cum[c * L + L - 1:c * L + L, :] for c in range(nck)]
    cum_end = jnp.concatenate([jnp.broadcast_to(e, (L, WIDTH)) for e in ends], axis=0)
    w_inc = jnp.exp(cum)
    w_exc = jnp.exp(cum - logdecay)
    w_inv = jnp.exp(-cum)
    w_tail = jnp.exp(cum_end - cum)
    for c in range(nck):
        we_sc[c:c + 1, :] = jnp.exp(ends[c])
    yield
    kap_all = kk * lax.rsqrt(jnp.maximum(_head_sum(kk * kk), 1e-24))
    bb_all = kap_all * a
    r_hat_all = r * w_inc
    ob_sc[_KAPH] = _b(kap_all * w_exc)
    ob_sc[_RHAT] = _b(r_hat_all)
    ob_sc[_KHAT] = _b(k2 * w_inv)
    ob_sc[_KBAR] = _b(k2 * w_tail)
    yield
    ob_sc[_BHAT] = _b(bb_all * w_inv)
    ob_sc[_BBAR] = _b(bb_all * w_tail)
    ob_sc[_VB] = _b(v)
    of_sc[_RHAT32] = r_hat_all
    of_sc[_BONUS] = _head_sum(r * k2 * r_k) * v
    of_sc[_GATE] = g


def _rwkv_matmul_stage(ob_sc, of_sc, we_sc, s_sc, y_sc, lnw, lnb, o_ref, rows_out):
    L = RWKV_CHUNK
    nck = RWKV_STEP_CHUNKS
    trow = lax.broadcasted_iota(jnp.int32, (L, L), 0)
    tcol = lax.broadcasted_iota(jnp.int32, (L, L), 1)
    lower_incl = trow >= tcol
    lower_strict = trow > tcol
    heads = range(N_HEADS)
    items = [(slice(c * L, (c + 1) * L), slice(h * HEAD_DIM, (h + 1) * HEAD_DIM))
             for c in range(nck) for h in heads]
    n = range(len(items))
    ld = lambda slot, it: ob_sc[slot, it[0], it[1]]
    lhs = [jnp.concatenate([ld(_KAPH, it), ld(_RHAT, it)], axis=0) for it in items]
    ak = [_dot_nt(lhs[i], ld(_KHAT, items[i])) for i in n]
    ab = [_dot_nt(lhs[i], ld(_BHAT, items[i])) for i in n]
    yield
    a_kr = [_b(jnp.concatenate([jnp.where(lower_strict, t[:L], 0.0), jnp.where(lower_incl, t[L:], 0.0)], axis=0))
            for t in ak]
    a_rb = [_b(jnp.where(lower_incl, t[L:], 0.0)) for t in ab]
    t_inv = yield from _unit_lower_inverse([jnp.where(lower_strict, t[:L], 0.0) for t in ab])
    t_inv = [_b(t) for t in t_inv]
    av = [_dot(a_kr[i], ld(_VB, items[i])) for i in n]
    pm = [_b(_dot(t_inv[i], ld(_KAPH, items[i]))) for i in n]
    yield
    qm = [_b(_dot(t_inv[i], _b(av[i][:L]))) for i in n]
    r_eff = [_b(of_sc[_RHAT32, items[i][0], items[i][1]] - _dot(a_rb[i], pm[i])) for i in n]
    ptb = [_b(_dot_tn(pm[i], ld(_BBAR, items[i]))) for i in n]
    yield
    y0 = [av[i][L:] - _dot(a_rb[i], qm[i]) for i in n]
    cm = [_dot_tn(ld(_VB, items[i]), ld(_KBAR, items[i])) - _dot_tn(qm[i], ld(_BBAR, items[i])) for i in n]
    yield
    state = [s_sc[h] for h in heads]
    for c in range(nck):
        w_end = we_sc[c:c + 1, :]
        sb = [_b(t) for t in state]
        ys = [_dot_nt(r_eff[c * N_HEADS + h], sb[h]) + y0[c * N_HEADS + h] for h in heads]
        state = [state[h] * w_end[:, items[h][1]] - _dot(sb[h], ptb[c * N_HEADS + h]) + cm[c * N_HEADS + h]
                 for h in heads]
        for h in heads:
            y_sc[items[c * N_HEADS + h]] = ys[h]
    for h in heads:
        s_sc[h] = state[h]
    yield
    y = y_sc[...]
    dev = y - _head_sum(y) * (1.0 / HEAD_DIM)
    var = _head_sum(dev * dev) * (1.0 / HEAD_DIM)
    yn = dev * lax.rsqrt(var + GN_EPS) * lnw + lnb
    o_ref[0, rows_out, :] = ((yn + of_sc[_BONUS]) * of_sc[_GATE]).astype(o_ref.dtype)


def _rwkv_chunk_kernel(p_ref, mu_ref, w0_ref, dup_ref, a0_ref, iup_ref, gup_ref, kk_ref, ka_ref,
                       rk_ref, lnw_ref, lnb_ref, o_ref, s_out_ref,
                       s_sc, last_sc, y_sc, xb_sc, xf_sc, xw_sc, yb_sc, yf_sc, yw_sc):
    step = pl.program_id(1)
    last = pl.num_programs(1) - 1
    ts = _RWKV_BLOCK

    @pl.when(step == 0)
    def _():
        s_sc[...] = jnp.zeros_like(s_sc)
        last_sc[...] = jnp.zeros_like(last_sc)
        yb_sc[...] = jnp.zeros_like(yb_sc)
        yf_sc[...] = jnp.zeros_like(yf_sc)
        yw_sc[...] = jnp.zeros_like(yw_sc)

    w = (mu_ref[...], w0_ref[...], dup_ref[...], a0_ref[...], iup_ref[...], gup_ref[...],
         kk_ref[...], ka_ref[...], rk_ref[...])
    lnw, lnb = lnw_ref[...], lnb_ref[...]
    _interleave(_rwkv_matmul_stage(yb_sc, yf_sc, yw_sc, s_sc, y_sc, lnw, lnb, o_ref, slice(0, ts)),
                _rwkv_pointwise_stage(p_ref[0:ts, :], last_sc, w, xb_sc, xf_sc, xw_sc))

    @pl.when(step == last)
    def _():
        s_out_ref[0] = s_sc[...]

    _interleave(_rwkv_matmul_stage(xb_sc, xf_sc, xw_sc, s_sc, y_sc, lnw, lnb, o_ref, slice(ts, 2 * ts)),
                _rwkv_pointwise_stage(p_ref[ts:2 * ts, :], last_sc, w, yb_sc, yf_sc, yw_sc))


def _rwkv_prompt(p_rw, weights, batch, seq):
    ts = _RWKV_BLOCK
    ns = seq // (2 * ts)
    const = lambda b, c: (0, 0)
    w_specs = [pl.BlockSpec(w.shape, const) for w in weights]
    operand_scratch = [pltpu.VMEM((7, ts, WIDTH), BF16), pltpu.VMEM((3, ts, WIDTH), F32), pltpu.VMEM((8, WIDTH), F32)]
    out, state = pl.pallas_call(
        _rwkv_chunk_kernel,
        grid=(batch, ns + 1),
        in_specs=[pl.BlockSpec((2 * ts, RWKV_COLS), lambda b, c: (b * ns + jnp.minimum(c, ns - 1), 0))] + w_specs,
        out_specs=[pl.BlockSpec((1, 2 * ts, WIDTH), lambda b, c: (b, c, 0)),
                   pl.BlockSpec((1, N_HEADS, HEAD_DIM, HEAD_DIM), lambda b, c: (b, 0, 0, 0))],
        out_shape=[jax.ShapeDtypeStruct((batch, seq + 2 * ts, WIDTH), BF16),
                   jax.ShapeDtypeStruct((batch, N_HEADS, HEAD_DIM, HEAD_DIM), F32)],
        scratch_shapes=[pltpu.VMEM((N_HEADS, HEAD_DIM, HEAD_DIM), F32),
                        pltpu.VMEM((8, RWKV_COLS), F32),
                        pltpu.VMEM((ts, WIDTH), F32)] + operand_scratch + operand_scratch,
        compiler_params=_params("parallel", "arbitrary"),
        name="rwkv_prompt",
    )(p_rw, *weights)
    return out[:, ts:ts + seq].reshape(batch * seq, WIDTH), state


def _rwkv_step_kernel(p_ref, sh_ref, s_ref, mu_ref, w0_ref, dup_ref, a0_ref, iup_ref, gup_ref, kk_ref,
                      ka_ref, rk_ref, lnw_ref, lnb_ref, o_ref, s_out_ref):
    p = jnp.broadcast_to(p_ref[0], (8, RWKV_COLS))
    pprev = jnp.broadcast_to(sh_ref[0], (8, RWKV_COLS))
    r, k2, v, kk, a, g, logdecay = (t[0:1] for t in _rwkv_pointwise(
        p, pprev, mu_ref[...], w0_ref[...], dup_ref[...], a0_ref[...], iup_ref[...], gup_ref[...],
        kk_ref[...], ka_ref[...]))
    decay = jnp.exp(logdecay)
    n = HEAD_DIM
    eye = lax.broadcasted_iota(jnp.int32, (n, n), 0) == lax.broadcasted_iota(jnp.int32, (n, n), 1)

    def to_col(row_vec):
        return jnp.sum(jnp.where(eye, row_vec, 0.0), axis=-1, keepdims=True)

    def to_row(col_vec):
        return jnp.sum(jnp.where(eye, col_vec, 0.0), axis=0, keepdims=True)

    for h in range(N_HEADS):
        ln = slice(h * n, (h + 1) * n)
        s0 = s_ref[0, h]
        kap = _head_norm(kk[:, ln])
        sa = jnp.sum(s0 * (-kap), axis=-1, keepdims=True)
        s_new = s0 * decay[:, ln] + sa * (kap * a[:, ln]) + to_col(v[:, ln]) * k2[:, ln]
        s_out_ref[0, h] = s_new
        y = to_row(jnp.sum(s_new * r[:, ln], axis=-1, keepdims=True))
        o_ref[0, :, ln] = _group_norm_out(y, r[:, ln], k2[:, ln], v[:, ln], g[:, ln], rk_ref[:, ln],
                                          lnw_ref[:, ln], lnb_ref[:, ln]).astype(o_ref.dtype)


def _rwkv_sample(p_rw, shift, state, weights):
    n = p_rw.shape[0]
    const = lambda s: (0, 0)
    vec = pl.BlockSpec((1, 1, RWKV_COLS), lambda s: (s, 0, 0))
    st = pl.BlockSpec((1, N_HEADS, HEAD_DIM, HEAD_DIM), lambda s: (s, 0, 0, 0))
    rw, s_new = pl.pallas_call(
        _rwkv_step_kernel,
        grid=(n,),
        in_specs=[vec, vec, st] + [pl.BlockSpec(w.shape, const) for w in weights],
        out_specs=[pl.BlockSpec((1, 1, WIDTH), lambda s: (s, 0, 0)), st],
        out_shape=[jax.ShapeDtypeStruct((n, 1, WIDTH), BF16),
                   jax.ShapeDtypeStruct(state.shape, F32)],
        compiler_params=_params("parallel"),
        name="rwkv_sample",
    )(p_rw.reshape(n, 1, RWKV_COLS), shift.reshape(n, 1, RWKV_COLS), state, *weights)
    return rw.reshape(n, WIDTH), s_new


def _sample_gate_kernel(q_ref, km_ref, idx_ref):
    rnd = lambda t: t.astype(BF16).astype(F32)
    prod = rnd(km_ref[0]) * rnd(q_ref[0])
    lane = lax.broadcasted_iota(jnp.int32, (WIDTH, 128), 0)
    hcol = lax.broadcasted_iota(jnp.int32, (WIDTH, 128), 1)
    head_sum = ((lane // HEAD_DIM) == hcol).astype(BF16)
    pieces = _split3(prod)
    gate = _dot(pieces[0], head_sum) + _dot(pieces[1], head_sum) + _dot(pieces[2], head_sum)
    rank = _block_rank(gate, 0)
    bidx = lax.broadcasted_iota(jnp.int32, gate.shape, 0)
    rows = [jnp.sum(jnp.where(rank == i, bidx, 0), axis=0, keepdims=True) for i in range(MOBA_TOP_K)]
    rows += [jnp.zeros((1, 128), jnp.int32)] * (8 - MOBA_TOP_K)
    idx_ref[0] = jnp.concatenate(rows, axis=0)


def _sample_gate(q, kmean):
    n, nb, _ = kmean.shape
    return pl.pallas_call(
        _sample_gate_kernel,
        grid=(n,),
        in_specs=[pl.BlockSpec((1, 1, WIDTH), lambda s: (s, 0, 0)),
                  pl.BlockSpec((1, nb, WIDTH), lambda s: (s, 0, 0))],
        out_specs=pl.BlockSpec((1, 8, 128), lambda s: (s, 0, 0)),
        out_shape=jax.ShapeDtypeStruct((n, 8, 128), jnp.int32),
        compiler_params=_params("parallel"),
        name="sample_gate",
    )(q.reshape(n, 1, WIDTH), kmean)


def _sample_attn_kernel(pt_ref, top_ref, q_ref, kn_ref, vn_ref, topv_ref, slope_ref, ck_hbm, cv_hbm, o_ref,
                        kbuf, vbuf, sems, *, past_len, n_pages):
    ppb = _PAGES_PER_BLOCK
    npg = MOBA_TOP_K * ppb
    seq = pl.program_id(0)
    nseq = pl.num_programs(0)
    slot = seq % 2

    def slab_copies(sq, sl, h, i):
        blk_id = top_ref[(sq * N_HEADS + h) * MOBA_TOP_K + i // ppb]
        page = pt_ref[sq * n_pages + blk_id * ppb + i % ppb]
        return (pltpu.make_async_copy(ck_hbm.at[page, h], kbuf.at[sl, h, i], sems.at[sl]),
                pltpu.make_async_copy(cv_hbm.at[page, h], vbuf.at[sl, h, i], sems.at[sl]))

    def start_all(sq, sl):
        def per_head(h, carry):
            for i in range(npg):
                for cp in slab_copies(sq, sl, h, i):
                    cp.start()
            return carry
        lax.fori_loop(0, N_HEADS, per_head, 0)

    @pl.when(seq == 0)
    def _():
        start_all(0, 0)

    @pl.when(seq + 1 < nseq)
    def _():
        start_all(seq + 1, 1 - slot)

    def wait_head(h, carry):
        for i in range(npg):
            for cp in slab_copies(seq, slot, h, i):
                cp.wait()
        return carry
    lax.fori_loop(0, N_HEADS, wait_head, 0)

    q = q_ref[0] * (HEAD_DIM ** -0.5)
    slope = slope_ref[...]
    lane = lax.broadcasted_iota(jnp.int32, (1, 1, PAGE_SIZE), 2)
    scores = []
    for i in range(npg):
        blk_id = topv_ref[0, i // ppb][:, :, None]
        dist = (past_len - blk_id * MOBA_BLOCK - (i % ppb) * PAGE_SIZE - lane).astype(F32)
        s = jnp.sum(kbuf[slot, :, i] * q, axis=1, keepdims=True)
        scores.append(s - slope * dist)
    s_self = jnp.sum(q * kn_ref[0], axis=1, keepdims=True)
    m = s_self
    for s in scores:
        m = jnp.maximum(m, jnp.max(s, axis=2, keepdims=True))
    p_self = jnp.exp(s_self - m)
    den = p_self
    accv = jnp.zeros((N_HEADS, HEAD_DIM, PAGE_SIZE), F32)
    for i, s in enumerate(scores):
        p = jnp.exp(s - m)
        den = den + jnp.sum(p, axis=2, keepdims=True)
        accv = accv + vbuf[slot, :, i] * p
    acc = jnp.sum(accv, axis=2, keepdims=True) + p_self * vn_ref[0]
    o_ref[0] = acc / den


def _sample_attn(page_table, top, slopes, q, k_new, v_new, cache_kt, cache_vt):
    n, n_pages = page_table.shape
    npg = MOBA_TOP_K * _PAGES_PER_BLOCK
    top_flat = top.transpose(0, 2, 1).reshape(-1)
    vec = pl.BlockSpec((1, N_HEADS, HEAD_DIM, 1), lambda s, pt, tp: (s, 0, 0, 0))
    grid_spec = pltpu.PrefetchScalarGridSpec(
        num_scalar_prefetch=2,
        grid=(n,),
        in_specs=[vec, vec, vec,
                  pl.BlockSpec((1, MOBA_TOP_K, N_HEADS, 1), lambda s, pt, tp: (s, 0, 0, 0)),
                  pl.BlockSpec((N_HEADS, 1, 1), lambda s, pt, tp: (0, 0, 0)),
                  pl.BlockSpec(memory_space=pl.ANY),
                  pl.BlockSpec(memory_space=pl.ANY)],
        out_specs=vec,
        scratch_shapes=[pltpu.VMEM((2, N_HEADS, npg, HEAD_DIM, PAGE_SIZE), F32),
                        pltpu.VMEM((2, N_HEADS, npg, HEAD_DIM, PAGE_SIZE), F32),
                        pltpu.SemaphoreType.DMA((2,))],
    )
    return pl.pallas_call(
        functools.partial(_sample_attn_kernel, past_len=n_pages * PAGE_SIZE, n_pages=n_pages),
        grid_spec=grid_spec,
        out_shape=jax.ShapeDtypeStruct((n, N_HEADS, HEAD_DIM, 1), F32),
        compiler_params=_params("arbitrary"),
        name="sample_attn",
    )(page_table.reshape(-1), top_flat, q, k_new, v_new, top[..., None], slopes.reshape(N_HEADS, 1, 1),
      cache_kt, cache_vt)


def kernel(x_prompt, x_sample, cache_k, cache_v, page_table, state_wkv, state_shift,
           norm_mix_g, w_in, mu_shift, decay_w0, decay_up, iclr_a0, iclr_up, gate_up,
           k_k, k_a, r_k, ln_x_w, ln_x_b, w_out, norm_ffn_g, w_ffn_up, w_ffn_down, norm_final_g):
    depth = w_in.shape[0]
    assert depth == 1
    batch, seq, _ = x_prompt.shape
    n_seq, n_pages = page_table.shape
    slopes = jnp.exp2(-8.0 * jnp.arange(1, N_HEADS + 1, dtype=F32) / N_HEADS)

    l = 0
    row = lambda t: t.reshape(1, -1)
    w_in_16 = w_in[l].astype(BF16)
    w_in_b = jnp.concatenate([w_in_16[:, :ATTN_COLS], _to_internal(w_in_16[:, ATTN_COLS:])], axis=1)
    g_mix = row(norm_mix_g[l])
    rw_weights = (row(_to_internal(mu_shift[l])), row(decay_w0[l]), decay_up[l], row(iclr_a0[l]), iclr_up[l],
                  gate_up[l], row(k_k[l]), row(k_a[l]), row(r_k[l]), row(ln_x_w[l]), row(ln_x_b[l]))
    wo = w_out[l].astype(BF16)
    ffn_weights = (wo[:WIDTH], wo[WIDTH:], row(norm_ffn_g[l]), w_ffn_up[l].astype(BF16),
                   w_ffn_down[l].astype(BF16), row(norm_final_g))

    xp = x_prompt.reshape(batch * seq, D_MODEL)
    kt, vt, qt, ka, vtb, prw, kmean = _inproj_prompt(xp, g_mix, w_in_b, slopes, batch, seq)
    nb = seq // MOBA_BLOCK
    attn_p = _moba_prompt(slopes, qt, ka, vtb, kmean.reshape(batch, nb, WIDTH), batch, seq)
    rw_p, wkv_p = _rwkv_prompt(prw, rw_weights, batch, seq)
    ckt = cache_k.transpose(0, 1, 3, 4, 2).reshape(-1, N_HEADS, HEAD_DIM, PAGE_SIZE)
    cvt = cache_v.transpose(0, 1, 3, 4, 2).reshape(-1, N_HEADS, HEAD_DIM, PAGE_SIZE)
    y_prompt, kmean_s = _out_ffn(xp, attn_p, rw_p, *ffn_weights, tm=1024, th=512, stream=(page_table, ckt))
    y_prompt = y_prompt.reshape(batch, seq, D_MODEL)
    shift_p = _from_internal(prw.reshape(batch, seq, RWKV_COLS)[:, -1])

    hd = (N_HEADS, HEAD_DIM)
    xs = x_sample.reshape(n_seq, D_MODEL)
    proj_s = _inproj_small(xs, g_mix, w_in_b)
    q_s, k_s, v_s = proj_s[:, :WIDTH], proj_s[:, WIDTH:2 * WIDTH], proj_s[:, 2 * WIDTH:ATTN_COLS]
    prw_s = proj_s[:, ATTN_COLS:]
    top = _sample_gate(q_s, kmean_s)[:, :MOBA_TOP_K, :N_HEADS]
    col = lambda t: t.reshape(n_seq, N_HEADS, HEAD_DIM, 1)
    attn_s = _sample_attn(page_table, top, slopes, col(q_s), col(k_s), col(v_s), ckt, cvt).reshape(n_seq, WIDTH)
    rw_s, wkv_s = _rwkv_sample(prw_s, _to_internal(state_shift[l]), state_wkv[l], rw_weights)
    y_sample = _out_ffn(xs, attn_s.astype(BF16), rw_s, *ffn_weights, tm=n_seq).reshape(n_seq, 1, D_MODEL)
    shift_s = _from_internal(prw_s)

    return (y_prompt, y_sample,
            kt.transpose(0, 3, 1, 2)[None], vt.transpose(0, 3, 1, 2)[None],
            wkv_p[None], shift_p[None],
            k_s.reshape(1, n_seq, 1, *hd), v_s.reshape(1, n_seq, 1, *hd),
            wkv_s[None], shift_s[None])
```

```python
import functools

import jax
import jax.numpy as jnp
from jax import lax
from jax.experimental import pallas as pl
from jax.experimental.pallas import tpu as pltpu

F32 = jnp.float32
BF16 = jnp.bfloat16

D_MODEL = 1024
HEAD_DIM = 64
N_HEADS = 8
WIDTH = N_HEADS * HEAD_DIM
MOBA_BLOCK = 256
MOBA_TOP_K = 3
DECAY_LORA = 64
AAA_LORA = 64
GATE_LORA = 128
ATTN_COLS = 3 * WIDTH
RWKV_COLS = 3 * WIDTH + DECAY_LORA + AAA_LORA + GATE_LORA
RMS_EPS = 1e-6
GN_EPS = 64e-5
NEG_INF = -1e30
PAGE_SIZE = 128
RWKV_CHUNK = 64
VMEM_LIMIT = 48 * 1024 * 1024

_O_R, _O_K, _O_V = 0, WIDTH, 2 * WIDTH
_O_XW = 3 * WIDTH
_O_XA = _O_XW + DECAY_LORA
_O_XG = _O_XA + AAA_LORA


def _to_internal(t):
    r, xw, kv, rest = (t[..., :WIDTH], t[..., WIDTH:WIDTH + DECAY_LORA],
                       t[..., WIDTH + DECAY_LORA:3 * WIDTH + DECAY_LORA], t[..., 3 * WIDTH + DECAY_LORA:])
    return jnp.concatenate([r, kv, xw, rest], axis=-1)


def _from_internal(t):
    r, kv, xw, rest = (t[..., :WIDTH], t[..., WIDTH:3 * WIDTH],
                       t[..., 3 * WIDTH:3 * WIDTH + DECAY_LORA], t[..., 3 * WIDTH + DECAY_LORA:])
    return jnp.concatenate([r, xw, kv, rest], axis=-1)


def _params(*sem):
    return pltpu.CompilerParams(dimension_semantics=sem, vmem_limit_bytes=VMEM_LIMIT)


def _rms(x, g):
    return x * lax.rsqrt(jnp.mean(x * x, axis=-1, keepdims=True) + RMS_EPS) * g


def _dot(a, b, **kw):
    return jnp.dot(a, b, preferred_element_type=F32, **kw)


def _dot_nt(a, b, **kw):
    return lax.dot_general(a, b, (((1,), (1,)), ((), ())), preferred_element_type=F32, **kw)


def _dot_tn(a, b, **kw):
    return lax.dot_general(a, b, (((0,), (0,)), ((), ())), preferred_element_type=F32, **kw)


MOBA_VROWS = HEAD_DIM + 16


def _inproj_kernel(x_ref, g_ref, w_ref, srow_ref, kt_ref, vt_ref, qt_ref, ka_ref, vtb_ref, prw_ref, km_ref):
    tm = x_ref.shape[0]
    hd = (N_HEADS, HEAD_DIM, tm)
    xn = _rms(x_ref[...], g_ref[...]).astype(BF16)
    proj = _dot(xn, w_ref[...])
    q = proj[:, 0:WIDTH] * (HEAD_DIM ** -0.5)
    k = proj[:, WIDTH:2 * WIDTH]
    v = proj[:, 2 * WIDTH:3 * WIDTH]
    prw_ref[...] = proj[:, ATTN_COLS:]
    km_ref[0] = jnp.mean(k, axis=0, keepdims=True)
    k_t = k.T.reshape(hd)
    v_t = v.T.reshape(hd)
    kt_ref[0] = k_t
    vt_ref[0] = v_t
    vtb_ref[0, :, 0, 0:HEAD_DIM, :] = v_t.astype(BF16)
    ones_row = lax.broadcasted_iota(jnp.int32, (N_HEADS, MOBA_VROWS - HEAD_DIM, tm), 1) == 0
    vtb_ref[0, :, 0, HEAD_DIM:, :] = jnp.where(ones_row, 1.0, 0.0).astype(BF16)
    qt_ref[0, :, 0:HEAD_DIM, :] = q.T.reshape(hd).astype(BF16)
    qt_ref[0, :, HEAD_DIM:, :] = jnp.broadcast_to(srow_ref[...], hd).astype(BF16)
    lane = lax.broadcasted_iota(jnp.int32, (tm, 128 - HEAD_DIM), 1)
    pos = lax.broadcasted_iota(jnp.int32, (tm, 128 - HEAD_DIM), 0)
    pos_cols = jnp.where(lane == 0, pos, 0).astype(F32).astype(BF16)
    for h in range(N_HEADS):
        ka_ref[0, h, :, 0:HEAD_DIM] = k[:, h * HEAD_DIM:(h + 1) * HEAD_DIM].astype(BF16)
        ka_ref[0, h, :, HEAD_DIM:] = pos_cols


def _inproj_prompt(x, g, w_bf16, slopes, batch, seq):
    tm = MOBA_BLOCK
    m = x.shape[0]
    nb = seq // tm
    ncol = w_bf16.shape[1]
    row = lambda i: (i, 0)
    const = lambda i: (0, 0)
    tok = lambda i: (i // nb, 0, 0, i % nb)
    srow = jnp.zeros((N_HEADS, HEAD_DIM, 1), F32).at[:, 0, 0].set(slopes)
    return pl.pallas_call(
        _inproj_kernel,
        grid=(m // tm,),
        in_specs=[pl.BlockSpec((tm, D_MODEL), row),
                  pl.BlockSpec((1, D_MODEL), const),
                  pl.BlockSpec((D_MODEL, ncol), const),
                  pl.BlockSpec((N_HEADS, HEAD_DIM, 1), lambda i: (0, 0, 0))],
        out_specs=[pl.BlockSpec((1, N_HEADS, HEAD_DIM, tm), tok),
                   pl.BlockSpec((1, N_HEADS, HEAD_DIM, tm), tok),
                   pl.BlockSpec((1, N_HEADS, 128, tm), tok),
                   pl.BlockSpec((1, N_HEADS, tm, 128), lambda i: (i // nb, 0, i % nb, 0)),
                   pl.BlockSpec((1, N_HEADS, 1, MOBA_VROWS, tm), lambda i: (i // nb, 0, i % nb, 0, 0)),
                   pl.BlockSpec((tm, RWKV_COLS), row),
                   pl.BlockSpec((1, 1, WIDTH), lambda i: (i, 0, 0))],
        out_shape=[jax.ShapeDtypeStruct((batch, N_HEADS, HEAD_DIM, seq), F32),
                   jax.ShapeDtypeStruct((batch, N_HEADS, HEAD_DIM, seq), F32),
                   jax.ShapeDtypeStruct((batch, N_HEADS, 128, seq), BF16),
                   jax.ShapeDtypeStruct((batch, N_HEADS, seq, 128), BF16),
                   jax.ShapeDtypeStruct((batch, N_HEADS, nb, MOBA_VROWS, tm), BF16),
                   jax.ShapeDtypeStruct((m, RWKV_COLS), F32),
                   jax.ShapeDtypeStruct((m // tm, 1, WIDTH), F32)],
        compiler_params=_params("parallel"),
        name="inproj_prompt",
    )(x, g, w_bf16, srow)


def _inproj_small_kernel(x_ref, g_ref, w_ref, o_ref):
    xn = _rms(x_ref[...], g_ref[...]).astype(BF16)
    o_ref[...] = _dot(xn, w_ref[...])


def _inproj_small(x, g, w, tn=256):
    m = x.shape[0]
    ncol = w.shape[1]
    return pl.pallas_call(
        _inproj_small_kernel,
        grid=(ncol // tn,),
        in_specs=[pl.BlockSpec((m, D_MODEL), lambda j: (0, 0)),
                  pl.BlockSpec((1, D_MODEL), lambda j: (0, 0)),
                  pl.BlockSpec((D_MODEL, tn), lambda j: (0, j))],
        out_specs=pl.BlockSpec((m, tn), lambda j: (0, j)),
        out_shape=jax.ShapeDtypeStruct((m, ncol), F32),
        compiler_params=_params("parallel"),
        name="inproj_sample",
    )(x, g, w)


_STREAM_PAGES = 16
_PAGES_PER_BLOCK = MOBA_BLOCK // PAGE_SIZE


def _ffn_kernel(pt_ref, x_ref, attn_ref, rw_ref, woa_ref, wor_ref, gf_ref, wup_ref, wdn_ref, gfin_ref, *rest,
                n_stream):
    page_refs = rest[:n_stream]
    y_ref = rest[n_stream]
    km_ref = rest[n_stream + 1] if n_stream else None
    h_sc, hn_sc, acc_sc = rest[-3:]
    j = pl.program_id(1)

    @pl.when(j == 0)
    def _():
        h = x_ref[...] + _dot(attn_ref[...], woa_ref[...]) + _dot(rw_ref[...], wor_ref[...])
        h_sc[...] = h
        hn_sc[...] = _rms(h, gf_ref[...]).astype(BF16)
        acc_sc[...] = jnp.zeros_like(acc_sc)

    u = jnp.maximum(_dot(hn_sc[...], wup_ref[...]), 0.0)
    acc_sc[...] += _dot((u * u).astype(BF16), wdn_ref[...])

    if n_stream:
        ppb = _PAGES_PER_BLOCK
        bps = n_stream // ppb
        nb = km_ref.shape[1]
        g = (pl.program_id(0) * pl.num_programs(1) + j) % (nb // bps)
        for b in range(bps):
            tot = page_refs[b * ppb][0]
            for i in range(1, ppb):
                tot = tot + page_refs[b * ppb + i][0]
            tot_t = tot.reshape(WIDTH, PAGE_SIZE).T
            km_ref[0, pl.ds(g * bps + b, 1), :] = jnp.sum(tot_t, axis=0, keepdims=True) * (1.0 / MOBA_BLOCK)

    @pl.when(j == pl.num_programs(1) - 1)
    def _():
        y_ref[...] = _rms(h_sc[...] + acc_sc[...], gfin_ref[...])


def _out_ffn(x, attn, rw, woa, wor, gf, wup, wdn, gfin, tm, th=1024, stream=None):
    m = x.shape[0]
    hid = wup.shape[1]
    nj = hid // th
    row = lambda i, j, pt: (i, 0)
    const = lambda i, j, pt: (0, 0)
    in_specs = [pl.BlockSpec((tm, D_MODEL), row),
                pl.BlockSpec((tm, WIDTH), row),
                pl.BlockSpec((tm, WIDTH), row),
                pl.BlockSpec((WIDTH, D_MODEL), const),
                pl.BlockSpec((WIDTH, D_MODEL), const),
                pl.BlockSpec((1, D_MODEL), const),
                pl.BlockSpec((D_MODEL, th), lambda i, j, pt: (0, j)),
                pl.BlockSpec((th, D_MODEL), lambda i, j, pt: (j, 0)),
                pl.BlockSpec((1, D_MODEL), const)]
    out_specs = [pl.BlockSpec((tm, D_MODEL), row)]
    out_shape = [jax.ShapeDtypeStruct((m, D_MODEL), F32)]
    operands = [x, attn, rw, woa, wor, gf, wup, wdn, gfin]
    n_stream = 0
    pt_flat = jnp.zeros((1,), jnp.int32)
    if stream is not None:
        page_table, cache_kt = stream
        n_seq, n_pages = page_table.shape
        n_stream = _STREAM_PAGES
        spp = n_pages // n_stream
        assert (m // tm) * nj == n_seq * spp, "page streaming needs one grid step per 16 pages"
        nb = n_pages // _PAGES_PER_BLOCK
        pt_flat = page_table.reshape(-1)

        def page_spec(k):
            def imap(i, j, pt, k=k):
                t = i * nj + j
                return (pt[(t // spp) * n_pages + (t % spp) * n_stream + k], 0, 0, 0)
            return pl.BlockSpec((1, N_HEADS, HEAD_DIM, PAGE_SIZE), imap)

        in_specs += [page_spec(k) for k in range(n_stream)]
        operands += [cache_kt] * n_stream
        out_specs.append(pl.BlockSpec((1, nb, WIDTH), lambda i, j, pt: ((i * nj + j) // spp, 0, 0)))
        out_shape.append(jax.ShapeDtypeStruct((n_seq, nb, WIDTH), F32))
    grid_spec = pltpu.PrefetchScalarGridSpec(
        num_scalar_prefetch=1,
        grid=(m // tm, nj),
        in_specs=in_specs,
        out_specs=out_specs,
        scratch_shapes=[pltpu.VMEM((tm, D_MODEL), F32),
                        pltpu.VMEM((tm, D_MODEL), BF16),
                        pltpu.VMEM((tm, D_MODEL), F32)],
    )
    outs = pl.pallas_call(
        functools.partial(_ffn_kernel, n_stream=n_stream),
        grid_spec=grid_spec,
        out_shape=out_shape,
        compiler_params=_params("arbitrary", "arbitrary"),
        name="out_ffn",
    )(pt_flat, *operands)
    return outs if stream is not None else outs[0]


def _block_rank(gm, axis):
    nb = gm.shape[axis]
    idx = lax.broadcasted_iota(jnp.int32, gm.shape, axis)
    beats = []
    for m in range(nb):
        gmm = lax.slice_in_dim(gm, m, m + 1, axis=axis)
        beats.append(((gmm > gm) | ((gmm == gm) & (m < idx))).astype(jnp.int32))
    while len(beats) > 1:
        beats = [a + b for a, b in zip(beats[0::2], beats[1::2])] + ([beats[-1]] if len(beats) % 2 else [])
    return beats[0]


MOBA_HEADS = 8
BIG = 1e30


def _moba_kernel(slopes_ref, qt_ref, ka_ref, vt_ref, km_ref, o_ref, sel_sc, m_sc, acc_sc, *, nb):
    hg = pl.program_id(1)
    qi = pl.program_id(2)
    blk = MOBA_BLOCK
    n_top = min(MOBA_TOP_K, nb)
    keyi = lax.broadcasted_iota(jnp.int32, (blk, blk), 0)
    qryi = lax.broadcasted_iota(jnp.int32, (blk, blk), 1)
    causal = keyi <= qryi
    bidx = lax.broadcasted_iota(jnp.int32, (nb, blk), 0)
    q0 = pl.multiple_of(qi * blk, blk)

    hs = range(MOBA_HEADS)
    qts = [qt_ref[0, hh] for hh in hs]
    gates = [_dot(km_ref[0, hh], qts[hh]) for hh in hs]
    valid = bidx < qi
    ranks = [_block_rank(jnp.where(valid, g, NEG_INF), 0) for g in gates]
    for hh in hs:
        sel_sc[hh] = ((ranks[hh] < n_top) & valid).astype(F32)
    s0 = [jnp.where(causal, _dot(ka_ref[0, hh, pl.ds(q0, blk), :], qts[hh]), NEG_INF) for hh in hs]
    m0 = [jnp.max(t, axis=0, keepdims=True) for t in s0]
    p0 = [jnp.exp(s0[hh] - m0[hh]) for hh in hs]
    for hh in hs:
        m_sc[hh] = m0[hh]
    pv0 = [_dot(vt_ref[0, hh, qi], p0[hh].astype(BF16)) for hh in hs]
    for hh in hs:
        acc_sc[hh] = pv0[hh]

    def past_block(j, carry):
        k0 = pl.multiple_of(j * blk, blk)
        s = [_dot(ka_ref[0, hh, pl.ds(k0, blk), :], qt_ref[0, hh]) for hh in hs]
        ps, alphas = [], []
        for hh in hs:
            cj = -slopes_ref[hg * MOBA_HEADS + hh] * ((qi - j) * blk).astype(F32)
            picked = sel_sc[hh, pl.ds(j, 1), :] > 0.0
            m_old = m_sc[hh]
            m_new = jnp.maximum(m_old, jnp.where(picked, jnp.max(s[hh], axis=0, keepdims=True) + cj, NEG_INF))
            alpha = jnp.exp(m_old - m_new)
            p = jnp.exp(s[hh] - jnp.where(picked, m_new - cj, BIG))
            m_sc[hh] = m_new
            ps.append(p.astype(BF16))
            alphas.append(alpha)
        pv = [_dot(vt_ref[0, hh, j], ps[hh]) for hh in hs]
        for hh in hs:
            acc_sc[hh] = alphas[hh] * acc_sc[hh] + pv[hh]
        return carry

    lax.fori_loop(0, qi, past_block, 0)

    outs = [acc_sc[hh, 0:HEAD_DIM, :] / acc_sc[hh, HEAD_DIM:HEAD_DIM + 1, :] for hh in hs]
    o_ref[...] = jnp.concatenate([t.T for t in outs], axis=1).astype(o_ref.dtype)


def _moba_prompt(slopes, qt, ka, vt, kmean, batch, seq):
    nb = seq // MOBA_BLOCK
    blk = MOBA_BLOCK
    km = kmean.reshape(batch, nb, N_HEADS, HEAD_DIM).transpose(0, 2, 1, 3).astype(BF16)
    km = jnp.concatenate([km, jnp.zeros((batch, N_HEADS, nb, 128 - HEAD_DIM), BF16)], axis=3)

    hgn = MOBA_HEADS
    grid_spec = pltpu.PrefetchScalarGridSpec(
        num_scalar_prefetch=1,
        grid=(batch, N_HEADS // hgn, nb),
        in_specs=[pl.BlockSpec((1, hgn, 128, blk), lambda b, g, qi, s: (b, g, 0, qi)),
                  pl.BlockSpec((1, hgn, seq, 128), lambda b, g, qi, s: (b, g, 0, 0)),
                  pl.BlockSpec((1, hgn, nb, MOBA_VROWS, blk), lambda b, g, qi, s: (b, g, 0, 0, 0)),
                  pl.BlockSpec((1, hgn, nb, 128), lambda b, g, qi, s: (b, g, 0, 0))],
        out_specs=pl.BlockSpec((blk, hgn * HEAD_DIM), lambda b, g, qi, s: (b * nb + qi, g)),
        scratch_shapes=[pltpu.VMEM((hgn, nb, blk), F32),
                        pltpu.VMEM((hgn, 1, blk), F32),
                        pltpu.VMEM((hgn, MOBA_VROWS, blk), F32)],
    )
    return pl.pallas_call(
        functools.partial(_moba_kernel, nb=nb),
        grid_spec=grid_spec,
        out_shape=jax.ShapeDtypeStruct((batch * seq, WIDTH), BF16),
        compiler_params=_params("parallel", "parallel", "arbitrary"),
        name="moba_prompt",
    )(slopes, qt, ka, vt, km)


def _b(t):
    return t.astype(BF16)


def _split3(x):
    x1 = _b(x)
    r1 = x - x1.astype(F32)
    x2 = _b(r1)
    return x1, x2, _b(r1 - x2.astype(F32))


def _rwkv_pointwise(p, pprev, mu, w0, decay_up, a0, iclr_up, gate_up, k_k, k_a):
    xs = p + mu * (pprev - p)
    r = xs[:, _O_R:_O_R + WIDTH]
    k = xs[:, _O_K:_O_K + WIDTH]
    v = xs[:, _O_V:_O_V + WIDTH]
    xw = xs[:, _O_XW:_O_XW + DECAY_LORA]
    xa = xs[:, _O_XA:_O_XA + AAA_LORA]
    xg = xs[:, _O_XG:_O_XG + GATE_LORA]
    w = w0 + _dot(_b(jnp.tanh(xw)), _b(decay_up))
    w = -jax.nn.softplus(-w) - 0.5
    logdecay = -jnp.exp(w)
    a = jax.nn.sigmoid(a0 + _dot(_b(xa), _b(iclr_up)))
    g = _dot(_b(jax.nn.sigmoid(xg)), _b(gate_up))
    kk = k * k_k
    k2 = k * (1.0 + (a - 1.0) * k_a)
    return r, k2, v, kk, a, g, logdecay


def _head_sum(x):
    row = lax.broadcasted_iota(jnp.int32, (128, 128), 0)
    col = lax.broadcasted_iota(jnp.int32, (128, 128), 1)
    seg = ((row // HEAD_DIM) == (col // HEAD_DIM)).astype(BF16)
    hi = _b(x)
    lo = _b(x - hi.astype(F32))
    cols = [slice(g * 128, (g + 1) * 128) for g in range(x.shape[1] // 128)]
    return jnp.concatenate([_dot(hi[:, c], seg) + _dot(lo[:, c], seg) for c in cols], axis=1)


def _head_norm(kk_h):
    return kk_h * lax.rsqrt(jnp.maximum(jnp.sum(kk_h * kk_h, axis=-1, keepdims=True), 1e-24))


def _group_norm_out(y_h, r_h, k_h, v_h, g_h, rk_h, lnw_h, lnb_h):
    mean = jnp.mean(y_h, axis=-1, keepdims=True)
    var = jnp.mean(jnp.square(y_h - mean), axis=-1, keepdims=True)
    yn = (y_h - mean) * lax.rsqrt(var + GN_EPS) * lnw_h + lnb_h
    yn = yn + jnp.sum(r_h * k_h * rk_h, axis=-1, keepdims=True) * v_h
    return yn * g_h


def _unit_lower_inverse(mats):
    n = mats[0].shape[0]
    row = lax.broadcasted_iota(jnp.int32, (n, n), 0)
    col = lax.broadcasted_iota(jnp.int32, (n, n), 1)
    eye = (row == col).astype(F32)
    size = 16
    same = (row // size) == (col // size)
    pws = [jnp.where(same, a, 0.0) for a in mats]
    xs = [eye - pw for pw in pws]
    for _ in range(3):
        pwb = [_b(pw) for pw in pws]
        pws = [_dot(t, t) for t in pwb]
        xs = [x + _dot(_b(x), _b(pw)) for x, pw in zip(xs, pws)]
        yield
    while size < n:
        size2 = size * 2
        same2 = (row // size2) == (col // size2)
        keep = same2 & jnp.logical_not(same)
        xb = [_b(x) for x in xs]
        ox = [_b(_dot(_b(jnp.where(keep, a, 0.0)), t)) for a, t in zip(mats, xb)]
        xs = [x - _dot(t, o) for x, t, o in zip(xs, xb, ox)]
        same = same2
        size = size2
        yield
    return xs


def _cumsum_rows(x, seg):
    n = x.shape[0]
    row = lax.broadcasted_iota(jnp.int32, (n, n), 0)
    col = lax.broadcasted_iota(jnp.int32, (n, n), 1)
    tri = ((row >= col) & ((row // seg) == (col // seg))).astype(BF16)
    x1, x2, x3 = _split3(x)
    return _dot(tri, x1) + _dot(tri, x2) + _dot(tri, x3)


RWKV_STEP_CHUNKS = 4
_RWKV_BLOCK = RWKV_CHUNK * RWKV_STEP_CHUNKS
_KAPH, _RHAT, _KHAT, _KBAR, _BHAT, _BBAR, _VB = range(7)
_RHAT32, _BONUS, _GATE = range(3)


def _interleave(*gens):
    live = list(gens)
    while live:
        for gen in list(live):
            try:
                next(gen)
            except StopIteration:
                live.remove(gen)


def _rwkv_pointwise_stage(p, last_sc, w, ob_sc, of_sc, we_sc, keep):
    mu, w0, dup, a0, iup, gup, k_k, k_a, r_k = w
    L = RWKV_CHUNK
    nck = RWKV_STEP_CHUNKS
    ts = _RWKV_BLOCK
    rowi = lax.broadcasted_iota(jnp.int32, p.shape, 0)
    carry = last_sc[0:1, :] if keep is None else last_sc[0:1, :] * keep
    pprev = jnp.where(rowi == 0, carry, pltpu.roll(p, 1, 0))
    last_sc[0:1, :] = p[ts - 1:ts, :]
    yield
    r, k2, v, kk, a, g, logdecay = _rwkv_pointwise(p, pprev, mu, w0, dup, a0, iup, gup, k_k, k_a)
    yield
    cum = _cumsum_rows(logdecay, L)
    ends = [cum[c * L + L - 1:c * L + L, :] for c in range(nck)]
    cum_end = jnp.concatenate([jnp.broadcast_to(e, (L, WIDTH)) for e in ends], axis=0)
    w_inc = jnp.exp(cum)
    w_exc = jnp.exp(cum - logdecay)
    w_inv = jnp.exp(-cum)
    w_tail = jnp.exp(cum_end - cum)
    for c in range(nck):
        we_sc[c:c + 1, :] = jnp.exp(ends[c])
    yield
    kap_all = kk * lax.rsqrt(jnp.maximum(_head_sum(kk * kk), 1e-24))
    bb_all = kap_all * a
    r_hat_all = r * w_inc
    ob_sc[_KAPH] = _b(kap_all * w_exc)
    ob_sc[_RHAT] = _b(r_hat_all)
    ob_sc[_KHAT] = _b(k2 * w_inv)
    ob_sc[_KBAR] = _b(k2 * w_tail)
    yield
    ob_sc[_BHAT] = _b(bb_all * w_inv)
    ob_sc[_BBAR] = _b(bb_all * w_tail)
    ob_sc[_VB] = _b(v)
    of_sc[_RHAT32] = r_hat_all
    of_sc[_BONUS] = _head_sum(r * k2 * r_k) * v
    of_sc[_GATE] = g


def _rwkv_matmul_stage(ob_sc, of_sc, we_sc, s_sc, y_sc, lnw, lnb, o_ref, rows_out, keep):
    L = RWKV_CHUNK
    nck = RWKV_STEP_CHUNKS
    trow = lax.broadcasted_iota(jnp.int32, (L, L), 0)
    tcol = lax.broadcasted_iota(jnp.int32, (L, L), 1)
    lower_incl = trow >= tcol
    lower_strict = trow > tcol
    heads = range(N_HEADS)
    items = [(slice(c * L, (c + 1) * L), slice(h * HEAD_DIM, (h + 1) * HEAD_DIM))
             for c in range(nck) for h in heads]
    n = range(len(items))
    ld = lambda slot, it: ob_sc[slot, it[0], it[1]]
    lhs = [jnp.concatenate([ld(_KAPH, it), ld(_RHAT, it)], axis=0) for it in items]
    ak = [_dot_nt(lhs[i], ld(_KHAT, items[i])) for i in n]
    ab = [_dot_nt(lhs[i], ld(_BHAT, items[i])) for i in n]
    yield
    a_kr = [_b(jnp.concatenate([jnp.where(lower_strict, t[:L], 0.0), jnp.where(lower_incl, t[L:], 0.0)], axis=0))
            for t in ak]
    a_rb = [_b(jnp.where(lower_incl, t[L:], 0.0)) for t in ab]
    t_inv = yield from _unit_lower_inverse([jnp.where(lower_strict, t[:L], 0.0) for t in ab])
    t_inv = [_b(t) for t in t_inv]
    av = [_dot(a_kr[i], ld(_VB, items[i])) for i in n]
    pm = [_b(_dot(t_inv[i], ld(_KAPH, items[i]))) for i in n]
    yield
    qm = [_b(_dot(t_inv[i], _b(av[i][:L]))) for i in n]
    r_eff = [_b(of_sc[_RHAT32, items[i][0], items[i][1]] - _dot(a_rb[i], pm[i])) for i in n]
    ptb = [_b(_dot_tn(pm[i], ld(_BBAR, items[i]))) for i in n]
    yield
    y0 = [av[i][L:] - _dot(a_rb[i], qm[i]) for i in n]
    cm = [_dot_tn(ld(_VB, items[i]), ld(_KBAR, items[i])) - _dot_tn(qm[i], ld(_BBAR, items[i])) for i in n]
    yield
    state = [s_sc[h] if keep is None else s_sc[h] * keep for h in heads]
    for c in range(nck):
        w_end = we_sc[c:c + 1, :]
        sb = [_b(t) for t in state]
        ys = [_dot_nt(r_eff[c * N_HEADS + h], sb[h]) + y0[c * N_HEADS + h] for h in heads]
        state = [state[h] * w_end[:, items[h][1]] - _dot(sb[h], ptb[c * N_HEADS + h]) + cm[c * N_HEADS + h]
                 for h in heads]
        for h in heads:
            y_sc[items[c * N_HEADS + h]] = ys[h]
    for h in heads:
        s_sc[h] = state[h]
    yield
    y = y_sc[...]
    dev = y - _head_sum(y) * (1.0 / HEAD_DIM)
    var = _head_sum(dev * dev) * (1.0 / HEAD_DIM)
    yn = dev * lax.rsqrt(var + GN_EPS) * lnw + lnb
    o_ref[rows_out, :] = ((yn + of_sc[_BONUS]) * of_sc[_GATE]).astype(o_ref.dtype)


def _rwkv_chunk_kernel(p_ref, mu_ref, w0_ref, dup_ref, a0_ref, iup_ref, gup_ref, kk_ref, ka_ref,
                       rk_ref, lnw_ref, lnb_ref, o_ref, s_out_ref,
                       s_sc, last_sc, y_sc, xb_sc, xf_sc, xw_sc, yb_sc, yf_sc, yw_sc, *, seq_blocks):
    step = pl.program_id(0)
    ts = _RWKV_BLOCK

    @pl.when(step == 0)
    def _():
        s_sc[...] = jnp.zeros_like(s_sc)
        last_sc[...] = jnp.zeros_like(last_sc)
        yb_sc[...] = jnp.zeros_like(yb_sc)
        yf_sc[...] = jnp.zeros_like(yf_sc)
        yw_sc[...] = jnp.zeros_like(yw_sc)

    keep = jnp.where((2 * step) % seq_blocks == 0, 0.0, 1.0).astype(F32)
    w = (mu_ref[...], w0_ref[...], dup_ref[...], a0_ref[...], iup_ref[...], gup_ref[...],
         kk_ref[...], ka_ref[...], rk_ref[...])
    lnw, lnb = lnw_ref[...], lnb_ref[...]
    _interleave(_rwkv_matmul_stage(yb_sc, yf_sc, yw_sc, s_sc, y_sc, lnw, lnb, o_ref, slice(0, ts), None),
                _rwkv_pointwise_stage(p_ref[0:ts, :], last_sc, w, xb_sc, xf_sc, xw_sc, keep))
    s_out_ref[0] = s_sc[...]
    _interleave(_rwkv_matmul_stage(xb_sc, xf_sc, xw_sc, s_sc, y_sc, lnw, lnb, o_ref, slice(ts, 2 * ts), keep),
                _rwkv_pointwise_stage(p_ref[ts:2 * ts, :], last_sc, w, yb_sc, yf_sc, yw_sc, None))


def _rwkv_prompt(p_rw, weights, batch, seq):
    ts = _RWKV_BLOCK
    m = batch * seq
    seq_blocks = seq // ts
    assert seq_blocks % 2 == 0
    ns = m // (2 * ts)
    const = lambda c: (0, 0)
    w_specs = [pl.BlockSpec(w.shape, const) for w in weights]
    operand_scratch = [pltpu.VMEM((7, ts, WIDTH), BF16), pltpu.VMEM((3, ts, WIDTH), F32), pltpu.VMEM((8, WIDTH), F32)]
    out, state = pl.pallas_call(
        functools.partial(_rwkv_chunk_kernel, seq_blocks=seq_blocks),
        grid=(ns + 1,),
        in_specs=[pl.BlockSpec((2 * ts, RWKV_COLS), lambda c: (jnp.minimum(c, ns - 1), 0))] + w_specs,
        out_specs=[pl.BlockSpec((2 * ts, WIDTH), lambda c: (c, 0)),
                   pl.BlockSpec((1, N_HEADS, HEAD_DIM, HEAD_DIM),
                                lambda c: (jnp.maximum(2 * c - 1, 0) // seq_blocks, 0, 0, 0))],
        out_shape=[jax.ShapeDtypeStruct((m + 2 * ts, WIDTH), BF16),
                   jax.ShapeDtypeStruct((batch, N_HEADS, HEAD_DIM, HEAD_DIM), F32)],
        scratch_shapes=[pltpu.VMEM((N_HEADS, HEAD_DIM, HEAD_DIM), F32),
                        pltpu.VMEM((8, RWKV_COLS), F32),
                        pltpu.VMEM((ts, WIDTH), F32)] + operand_scratch + operand_scratch,
        compiler_params=_params("arbitrary"),
        name="rwkv_prompt",
    )(p_rw, *weights)
    return out[ts:ts + m], state


def _rwkv_step_kernel(p_ref, sh_ref, s_ref, mu_ref, w0_ref, dup_ref, a0_ref, iup_ref, gup_ref, kk_ref,
                      ka_ref, rk_ref, lnw_ref, lnb_ref, o_ref, s_out_ref):
    p = jnp.broadcast_to(p_ref[0], (8, RWKV_COLS))
    pprev = jnp.broadcast_to(sh_ref[0], (8, RWKV_COLS))
    r, k2, v, kk, a, g, logdecay = (t[0:1] for t in _rwkv_pointwise(
        p, pprev, mu_ref[...], w0_ref[...], dup_ref[...], a0_ref[...], iup_ref[...], gup_ref[...],
        kk_ref[...], ka_ref[...]))
    decay = jnp.exp(logdecay)
    n = HEAD_DIM
    eye = lax.broadcasted_iota(jnp.int32, (n, n), 0) == lax.broadcasted_iota(jnp.int32, (n, n), 1)

    def to_col(row_vec):
        return jnp.sum(jnp.where(eye, row_vec, 0.0), axis=-1, keepdims=True)

    def to_row(col_vec):
        return jnp.sum(jnp.where(eye, col_vec, 0.0), axis=0, keepdims=True)

    for h in range(N_HEADS):
        ln = slice(h * n, (h + 1) * n)
        s0 = s_ref[0, h]
        kap = _head_norm(kk[:, ln])
        sa = jnp.sum(s0 * (-kap), axis=-1, keepdims=True)
        s_new = s0 * decay[:, ln] + sa * (kap * a[:, ln]) + to_col(v[:, ln]) * k2[:, ln]
        s_out_ref[0, h] = s_new
        y = to_row(jnp.sum(s_new * r[:, ln], axis=-1, keepdims=True))
        o_ref[0, :, ln] = _group_norm_out(y, r[:, ln], k2[:, ln], v[:, ln], g[:, ln], rk_ref[:, ln],
                                          lnw_ref[:, ln], lnb_ref[:, ln]).astype(o_ref.dtype)


def _rwkv_sample(p_rw, shift, state, weights):
    n = p_rw.shape[0]
    const = lambda s: (0, 0)
    vec = pl.BlockSpec((1, 1, RWKV_COLS), lambda s: (s, 0, 0))
    st = pl.BlockSpec((1, N_HEADS, HEAD_DIM, HEAD_DIM), lambda s: (s, 0, 0, 0))
    rw, s_new = pl.pallas_call(
        _rwkv_step_kernel,
        grid=(n,),
        in_specs=[vec, vec, st] + [pl.BlockSpec(w.shape, const) for w in weights],
        out_specs=[pl.BlockSpec((1, 1, WIDTH), lambda s: (s, 0, 0)), st],
        out_shape=[jax.ShapeDtypeStruct((n, 1, WIDTH), BF16),
                   jax.ShapeDtypeStruct(state.shape, F32)],
        compiler_params=_params("parallel"),
        name="rwkv_sample",
    )(p_rw.reshape(n, 1, RWKV_COLS), shift.reshape(n, 1, RWKV_COLS), state, *weights)
    return rw.reshape(n, WIDTH), s_new


def _sample_gate_kernel(q_ref, km_ref, idx_ref):
    rnd = lambda t: t.astype(BF16).astype(F32)
    prod = rnd(km_ref[0]) * rnd(q_ref[0])
    lane = lax.broadcasted_iota(jnp.int32, (WIDTH, 128), 0)
    hcol = lax.broadcasted_iota(jnp.int32, (WIDTH, 128), 1)
    head_sum = ((lane // HEAD_DIM) == hcol).astype(BF16)
    pieces = _split3(prod)
    gate = _dot(pieces[0], head_sum) + _dot(pieces[1], head_sum) + _dot(pieces[2], head_sum)
    rank = _block_rank(gate, 0)
    bidx = lax.broadcasted_iota(jnp.int32, gate.shape, 0)
    rows = [jnp.sum(jnp.where(rank == i, bidx, 0), axis=0, keepdims=True) for i in range(MOBA_TOP_K)]
    rows += [jnp.zeros((1, 128), jnp.int32)] * (8 - MOBA_TOP_K)
    idx_ref[0] = jnp.concatenate(rows, axis=0)


def _sample_gate(q, kmean):
    n, nb, _ = kmean.shape
    return pl.pallas_call(
        _sample_gate_kernel,
        grid=(n,),
        in_specs=[pl.BlockSpec((1, 1, WIDTH), lambda s: (s, 0, 0)),
                  pl.BlockSpec((1, nb, WIDTH), lambda s: (s, 0, 0))],
        out_specs=pl.BlockSpec((1, 8, 128), lambda s: (s, 0, 0)),
        out_shape=jax.ShapeDtypeStruct((n, 8, 128), jnp.int32),
        compiler_params=_params("parallel"),
        name="sample_gate",
    )(q.reshape(n, 1, WIDTH), kmean)


def _sample_attn_kernel(pt_ref, top_ref, q_ref, kn_ref, vn_ref, topv_ref, slope_ref, ck_hbm, cv_hbm, o_ref,
                        kbuf, vbuf, sems, *, past_len, n_pages):
    ppb = _PAGES_PER_BLOCK
    npg = MOBA_TOP_K * ppb
    seq = pl.program_id(0)
    nseq = pl.num_programs(0)
    slot = seq % 2

    def slab_copies(sq, sl, h, i):
        blk_id = top_ref[(sq * N_HEADS + h) * MOBA_TOP_K + i // ppb]
        page = pt_ref[sq * n_pages + blk_id * ppb + i % ppb]
        return (pltpu.make_async_copy(ck_hbm.at[page, h], kbuf.at[sl, h, i], sems.at[sl]),
                pltpu.make_async_copy(cv_hbm.at[page, h], vbuf.at[sl, h, i], sems.at[sl]))

    def start_all(sq, sl):
        def per_head(h, carry):
            for i in range(npg):
                for cp in slab_copies(sq, sl, h, i):
                    cp.start()
            return carry
        lax.fori_loop(0, N_HEADS, per_head, 0)

    @pl.when(seq == 0)
    def _():
        start_all(0, 0)

    @pl.when(seq + 1 < nseq)
    def _():
        start_all(seq + 1, 1 - slot)

    def wait_head(h, carry):
        for i in range(npg):
            for cp in slab_copies(seq, slot, h, i):
                cp.wait()
        return carry
    lax.fori_loop(0, N_HEADS, wait_head, 0)

    q = q_ref[0] * (HEAD_DIM ** -0.5)
    slope = slope_ref[...]
    lane = lax.broadcasted_iota(jnp.int32, (1, 1, PAGE_SIZE), 2)
    scores = []
    for i in range(npg):
        blk_id = topv_ref[0, i // ppb][:, :, None]
        dist = (past_len - blk_id * MOBA_BLOCK - (i % ppb) * PAGE_SIZE - lane).astype(F32)
        s = jnp.sum(kbuf[slot, :, i] * q, axis=1, keepdims=True)
        scores.append(s - slope * dist)
    s_self = jnp.sum(q * kn_ref[0], axis=1, keepdims=True)
    m = s_self
    for s in scores:
        m = jnp.maximum(m, jnp.max(s, axis=2, keepdims=True))
    p_self = jnp.exp(s_self - m)
    den = p_self
    accv = jnp.zeros((N_HEADS, HEAD_DIM, PAGE_SIZE), F32)
    for i, s in enumerate(scores):
        p = jnp.exp(s - m)
        den = den + jnp.sum(p, axis=2, keepdims=True)
        accv = accv + vbuf[slot, :, i] * p
    acc = jnp.sum(accv, axis=2, keepdims=True) + p_self * vn_ref[0]
    o_ref[0] = acc / den


def _sample_attn(page_table, top, slopes, q, k_new, v_new, cache_kt, cache_vt):
    n, n_pages = page_table.shape
    npg = MOBA_TOP_K * _PAGES_PER_BLOCK
    top_flat = top.transpose(0, 2, 1).reshape(-1)
    vec = pl.BlockSpec((1, N_HEADS, HEAD_DIM, 1), lambda s, pt, tp: (s, 0, 0, 0))
    grid_spec = pltpu.PrefetchScalarGridSpec(
        num_scalar_prefetch=2,
        grid=(n,),
        in_specs=[vec, vec, vec,
                  pl.BlockSpec((1, MOBA_TOP_K, N_HEADS, 1), lambda s, pt, tp: (s, 0, 0, 0)),
                  pl.BlockSpec((N_HEADS, 1, 1), lambda s, pt, tp: (0, 0, 0)),
                  pl.BlockSpec(memory_space=pl.ANY),
                  pl.BlockSpec(memory_space=pl.ANY)],
        out_specs=vec,
        scratch_shapes=[pltpu.VMEM((2, N_HEADS, npg, HEAD_DIM, PAGE_SIZE), F32),
                        pltpu.VMEM((2, N_HEADS, npg, HEAD_DIM, PAGE_SIZE), F32),
                        pltpu.SemaphoreType.DMA((2,))],
    )
    return pl.pallas_call(
        functools.partial(_sample_attn_kernel, past_len=n_pages * PAGE_SIZE, n_pages=n_pages),
        grid_spec=grid_spec,
        out_shape=jax.ShapeDtypeStruct((n, N_HEADS, HEAD_DIM, 1), F32),
        compiler_params=_params("arbitrary"),
        name="sample_attn",
    )(page_table.reshape(-1), top_flat, q, k_new, v_new, top[..., None], slopes.reshape(N_HEADS, 1, 1),
      cache_kt, cache_vt)


def kernel(x_prompt, x_sample, cache_k, cache_v, page_table, state_wkv, state_shift,
           norm_mix_g, w_in, mu_shift, decay_w0, decay_up, iclr_a0, iclr_up, gate_up,
           k_k, k_a, r_k, ln_x_w, ln_x_b, w_out, norm_ffn_g, w_ffn_up, w_ffn_down, norm_final_g):
    depth = w_in.shape[0]
    assert depth == 1
    batch, seq, _ = x_prompt.shape
    n_seq, n_pages = page_table.shape
    slopes = jnp.exp2(-8.0 * jnp.arange(1, N_HEADS + 1, dtype=F32) / N_HEADS)

    l = 0
    row = lambda t: t.reshape(1, -1)
    w_in_16 = w_in[l].astype(BF16)
    w_in_b = jnp.concatenate([w_in_16[:, :ATTN_COLS], _to_internal(w_in_16[:, ATTN_COLS:])], axis=1)
    g_mix = row(norm_mix_g[l])
    rw_weights = (row(_to_internal(mu_shift[l])), row(decay_w0[l]), decay_up[l], row(iclr_a0[l]), iclr_up[l],
                  gate_up[l], row(k_k[l]), row(k_a[l]), row(r_k[l]), row(ln_x_w[l]), row(ln_x_b[l]))
    wo = w_out[l].astype(BF16)
    ffn_weights = (wo[:WIDTH], wo[WIDTH:], row(norm_ffn_g[l]), w_ffn_up[l].astype(BF16),
                   w_ffn_down[l].astype(BF16), row(norm_final_g))

    xp = x_prompt.reshape(batch * seq, D_MODEL)
    kt, vt, qt, ka, vtb, prw, kmean = _inproj_prompt(xp, g_mix, w_in_b, slopes, batch, seq)
    nb = seq // MOBA_BLOCK
    attn_p = _moba_prompt(slopes, qt, ka, vtb, kmean.reshape(batch, nb, WIDTH), batch, seq)
    rw_p, wkv_p = _rwkv_prompt(prw, rw_weights, batch, seq)
    ckt = cache_k.transpose(0, 1, 3, 4, 2).reshape(-1, N_HEADS, HEAD_DIM, PAGE_SIZE)
    cvt = cache_v.transpose(0, 1, 3, 4, 2).reshape(-1, N_HEADS, HEAD_DIM, PAGE_SIZE)
    y_prompt, kmean_s = _out_ffn(xp, attn_p, rw_p, *ffn_weights, tm=1024, th=512, stream=(page_table, ckt))
    y_prompt = y_prompt.reshape(batch, seq, D_MODEL)
    shift_p = _from_internal(prw.reshape(batch, seq, RWKV_COLS)[:, -1])

    hd = (N_HEADS, HEAD_DIM)
    xs = x_sample.reshape(n_seq, D_MODEL)
    proj_s = _inproj_small(xs, g_mix, w_in_b)
    q_s, k_s, v_s = proj_s[:, :WIDTH], proj_s[:, WIDTH:2 * WIDTH], proj_s[:, 2 * WIDTH:ATTN_COLS]
    prw_s = proj_s[:, ATTN_COLS:]
    top = _sample_gate(q_s, kmean_s)[:, :MOBA_TOP_K, :N_HEADS]
    col = lambda t: t.reshape(n_seq, N_HEADS, HEAD_DIM, 1)
    attn_s = _sample_attn(page_table, top, slopes, col(q_s), col(k_s), col(v_s), ckt, cvt).reshape(n_seq, WIDTH)
    rw_s, wkv_s = _rwkv_sample(prw_s, _to_internal(state_shift[l]), state_wkv[l], rw_weights)
    y_sample = _out_ffn(xs, attn_s.astype(BF16), rw_s, *ffn_weights, tm=n_seq).reshape(n_seq, 1, D_MODEL)
    shift_s = _from_internal(prw_s)

    return (y_prompt, y_sample,
            kt.transpose(0, 3, 1, 2)[None], vt.transpose(0, 3, 1, 2)[None],
            wkv_p[None], shift_p[None],
            k_s.reshape(1, n_seq, 1, *hd), v_s.reshape(1, n_seq, 1, *hd),
            wkv_s[None], shift_s[None])
```

```python
import functools

import jax
import jax.numpy as jnp
from jax import lax
from jax.experimental import pallas as pl
from jax.experimental.pallas import tpu as pltpu

F32 = jnp.float32
BF16 = jnp.bfloat16

D_MODEL = 1024
HEAD_DIM = 64
N_HEADS = 8
WIDTH = N_HEADS * HEAD_DIM
MOBA_BLOCK = 256
MOBA_TOP_K = 3
DECAY_LORA = 64
AAA_LORA = 64
GATE_LORA = 128
ATTN_COLS = 3 * WIDTH
RWKV_COLS = 3 * WIDTH + DECAY_LORA + AAA_LORA + GATE_LORA
RMS_EPS = 1e-6
GN_EPS = 64e-5
NEG_INF = -1e30
PAGE_SIZE = 128
RWKV_CHUNK = 64
VMEM_LIMIT = 48 * 1024 * 1024

_O_R, _O_K, _O_V = 0, WIDTH, 2 * WIDTH
_O_XW = 3 * WIDTH
_O_XA = _O_XW + DECAY_LORA
_O_XG = _O_XA + AAA_LORA


def _to_internal(t):
    r, xw, kv, rest = (t[..., :WIDTH], t[..., WIDTH:WIDTH + DECAY_LORA],
                       t[..., WIDTH + DECAY_LORA:3 * WIDTH + DECAY_LORA], t[..., 3 * WIDTH + DECAY_LORA:])
    return jnp.concatenate([r, kv, xw, rest], axis=-1)


def _from_internal(t):
    r, kv, xw, rest = (t[..., :WIDTH], t[..., WIDTH:3 * WIDTH],
                       t[..., 3 * WIDTH:3 * WIDTH + DECAY_LORA], t[..., 3 * WIDTH + DECAY_LORA:])
    return jnp.concatenate([r, xw, kv, rest], axis=-1)


def _params(*sem):
    return pltpu.CompilerParams(dimension_semantics=sem, vmem_limit_bytes=VMEM_LIMIT)


def _rms(x, g):
    return x * lax.rsqrt(jnp.mean(x * x, axis=-1, keepdims=True) + RMS_EPS) * g


def _dot(a, b, **kw):
    return jnp.dot(a, b, preferred_element_type=F32, **kw)


def _dot_nt(a, b, **kw):
    return lax.dot_general(a, b, (((1,), (1,)), ((), ())), preferred_element_type=F32, **kw)


def _dot_tn(a, b, **kw):
    return lax.dot_general(a, b, (((0,), (0,)), ((), ())), preferred_element_type=F32, **kw)


MOBA_VROWS = HEAD_DIM + 16


def _inproj_kernel(x_ref, g_ref, w_ref, srow_ref, kt_ref, vt_ref, qt_ref, ka_ref, vtb_ref, prw_ref, km_ref):
    tm = x_ref.shape[0]
    hd = (N_HEADS, HEAD_DIM, tm)
    xn = _rms(x_ref[...], g_ref[...]).astype(BF16)
    proj = _dot(xn, w_ref[...])
    q = proj[:, 0:WIDTH] * (HEAD_DIM ** -0.5)
    k = proj[:, WIDTH:2 * WIDTH]
    v = proj[:, 2 * WIDTH:3 * WIDTH]
    prw_ref[...] = proj[:, ATTN_COLS:]
    km_ref[0] = jnp.mean(k, axis=0, keepdims=True)
    k_t = k.T.reshape(hd)
    v_t = v.T.reshape(hd)
    kt_ref[0] = k_t
    vt_ref[0] = v_t
    vtb_ref[0, :, 0, 0:HEAD_DIM, :] = v_t.astype(BF16)
    ones_row = lax.broadcasted_iota(jnp.int32, (N_HEADS, MOBA_VROWS - HEAD_DIM, tm), 1) == 0
    vtb_ref[0, :, 0, HEAD_DIM:, :] = jnp.where(ones_row, 1.0, 0.0).astype(BF16)
    qt_ref[0, :, 0:HEAD_DIM, :] = q.T.reshape(hd).astype(BF16)
    qt_ref[0, :, HEAD_DIM:, :] = jnp.broadcast_to(srow_ref[...], hd).astype(BF16)
    lane = lax.broadcasted_iota(jnp.int32, (tm, 128 - HEAD_DIM), 1)
    pos = lax.broadcasted_iota(jnp.int32, (tm, 128 - HEAD_DIM), 0)
    pos_cols = jnp.where(lane == 0, pos, 0).astype(F32).astype(BF16)
    for h in range(N_HEADS):
        ka_ref[0, h, :, 0:HEAD_DIM] = k[:, h * HEAD_DIM:(h + 1) * HEAD_DIM].astype(BF16)
        ka_ref[0, h, :, HEAD_DIM:] = pos_cols


def _inproj_prompt(x, g, w_bf16, slopes, batch, seq):
    tm = MOBA_BLOCK
    m = x.shape[0]
    nb = seq // tm
    ncol = w_bf16.shape[1]
    row = lambda i: (i, 0)
    const = lambda i: (0, 0)
    tok = lambda i: (i // nb, 0, 0, i % nb)
    srow = jnp.zeros((N_HEADS, HEAD_DIM, 1), F32).at[:, 0, 0].set(slopes)
    return pl.pallas_call(
        _inproj_kernel,
        grid=(m // tm,),
        in_specs=[pl.BlockSpec((tm, D_MODEL), row),
                  pl.BlockSpec((1, D_MODEL), const),
                  pl.BlockSpec((D_MODEL, ncol), const),
                  pl.BlockSpec((N_HEADS, HEAD_DIM, 1), lambda i: (0, 0, 0))],
        out_specs=[pl.BlockSpec((1, N_HEADS, HEAD_DIM, tm), tok),
                   pl.BlockSpec((1, N_HEADS, HEAD_DIM, tm), tok),
                   pl.BlockSpec((1, N_HEADS, 128, tm), tok),
                   pl.BlockSpec((1, N_HEADS, tm, 128), lambda i: (i // nb, 0, i % nb, 0)),
                   pl.BlockSpec((1, N_HEADS, 1, MOBA_VROWS, tm), lambda i: (i // nb, 0, i % nb, 0, 0)),
                   pl.BlockSpec((tm, RWKV_COLS), row),
                   pl.BlockSpec((1, 1, WIDTH), lambda i: (i, 0, 0))],
        out_shape=[jax.ShapeDtypeStruct((batch, N_HEADS, HEAD_DIM, seq), F32),
                   jax.ShapeDtypeStruct((batch, N_HEADS, HEAD_DIM, seq), F32),
                   jax.ShapeDtypeStruct((batch, N_HEADS, 128, seq), BF16),
                   jax.ShapeDtypeStruct((batch, N_HEADS, seq, 128), BF16),
                   jax.ShapeDtypeStruct((batch, N_HEADS, nb, MOBA_VROWS, tm), BF16),
                   jax.ShapeDtypeStruct((m, RWKV_COLS), F32),
                   jax.ShapeDtypeStruct((m // tm, 1, WIDTH), F32)],
        compiler_params=_params("parallel"),
        name="inproj_prompt",
    )(x, g, w_bf16, srow)


def _inproj_small_kernel(x_ref, g_ref, w_ref, o_ref):
    xn = _rms(x_ref[...], g_ref[...]).astype(BF16)
    o_ref[...] = _dot(xn, w_ref[...])


def _inproj_small(x, g, w, tn=256):
    m = x.shape[0]
    ncol = w.shape[1]
    return pl.pallas_call(
        _inproj_small_kernel,
        grid=(ncol // tn,),
        in_specs=[pl.BlockSpec((m, D_MODEL), lambda j: (0, 0)),
                  pl.BlockSpec((1, D_MODEL), lambda j: (0, 0)),
                  pl.BlockSpec((D_MODEL, tn), lambda j: (0, j))],
        out_specs=pl.BlockSpec((m, tn), lambda j: (0, j)),
        out_shape=jax.ShapeDtypeStruct((m, ncol), F32),
        compiler_params=_params("parallel"),
        name="inproj_sample",
    )(x, g, w)


_STREAM_PAGES = 16
_PAGES_PER_BLOCK = MOBA_BLOCK // PAGE_SIZE


def _ffn_kernel(pt_ref, x_ref, attn_ref, rw_ref, woa_ref, wor_ref, gf_ref, wup_ref, wdn_ref, gfin_ref, *rest,
                n_stream):
    page_refs = rest[:n_stream]
    y_ref = rest[n_stream]
    km_ref = rest[n_stream + 1] if n_stream else None
    h_sc, hn_sc, acc_sc = rest[-3:]
    j = pl.program_id(1)

    @pl.when(j == 0)
    def _():
        h = x_ref[...] + _dot(attn_ref[...], woa_ref[...]) + _dot(rw_ref[...], wor_ref[...])
        h_sc[...] = h
        hn_sc[...] = _rms(h, gf_ref[...]).astype(BF16)
        acc_sc[...] = jnp.zeros_like(acc_sc)

    u = jnp.maximum(_dot(hn_sc[...], wup_ref[j]), 0.0)
    acc_sc[...] += _dot((u * u).astype(BF16), wdn_ref[j])

    if n_stream:
        ppb = _PAGES_PER_BLOCK
        bps = n_stream // ppb
        nb = km_ref.shape[1]
        g = (pl.program_id(0) * pl.num_programs(1) + j) % (nb // bps)
        for b in range(bps):
            tot = page_refs[b * ppb][0]
            for i in range(1, ppb):
                tot = tot + page_refs[b * ppb + i][0]
            tot_t = tot.reshape(WIDTH, PAGE_SIZE).T
            km_ref[0, pl.ds(g * bps + b, 1), :] = jnp.sum(tot_t, axis=0, keepdims=True) * (1.0 / MOBA_BLOCK)

    @pl.when(j == pl.num_programs(1) - 1)
    def _():
        y_ref[...] = _rms(h_sc[...] + acc_sc[...], gfin_ref[...])


def _out_ffn(x, attn, rw, woa, wor, gf, wup, wdn, gfin, tm, th=1024, stream=None):
    m = x.shape[0]
    hid = wup.shape[1]
    nj = hid // th
    row = lambda i, j, pt: (i, 0)
    const = lambda i, j, pt: (0, 0)
    in_specs = [pl.BlockSpec((tm, D_MODEL), row),
                pl.BlockSpec((tm, WIDTH), row),
                pl.BlockSpec((tm, WIDTH), row),
                pl.BlockSpec((WIDTH, D_MODEL), const),
                pl.BlockSpec((WIDTH, D_MODEL), const),
                pl.BlockSpec((1, D_MODEL), const),
                pl.BlockSpec((nj, D_MODEL, th), lambda i, j, pt: (0, 0, 0), pipeline_mode=pl.Buffered(1)),
                pl.BlockSpec((nj, th, D_MODEL), lambda i, j, pt: (0, 0, 0), pipeline_mode=pl.Buffered(1)),
                pl.BlockSpec((1, D_MODEL), const)]
    out_specs = [pl.BlockSpec((tm, D_MODEL), row)]
    out_shape = [jax.ShapeDtypeStruct((m, D_MODEL), F32)]
    wup3 = wup.reshape(D_MODEL, nj, th).transpose(1, 0, 2)
    wdn3 = wdn.reshape(nj, th, D_MODEL)
    operands = [x, attn, rw, woa, wor, gf, wup3, wdn3, gfin]
    n_stream = 0
    pt_flat = jnp.zeros((1,), jnp.int32)
    if stream is not None:
        page_table, cache_kt = stream
        n_seq, n_pages = page_table.shape
        n_stream = _STREAM_PAGES
        spp = n_pages // n_stream
        assert (m // tm) * nj == n_seq * spp, "page streaming needs one grid step per 16 pages"
        nb = n_pages // _PAGES_PER_BLOCK
        pt_flat = page_table.reshape(-1)

        def page_spec(k):
            def imap(i, j, pt, k=k):
                t = i * nj + j
                return (pt[(t // spp) * n_pages + (t % spp) * n_stream + k], 0, 0, 0)
            return pl.BlockSpec((1, N_HEADS, HEAD_DIM, PAGE_SIZE), imap)

        in_specs += [page_spec(k) for k in range(n_stream)]
        operands += [cache_kt] * n_stream
        out_specs.append(pl.BlockSpec((1, nb, WIDTH), lambda i, j, pt: ((i * nj + j) // spp, 0, 0)))
        out_shape.append(jax.ShapeDtypeStruct((n_seq, nb, WIDTH), F32))
    grid_spec = pltpu.PrefetchScalarGridSpec(
        num_scalar_prefetch=1,
        grid=(m // tm, nj),
        in_specs=in_specs,
        out_specs=out_specs,
        scratch_shapes=[pltpu.VMEM((tm, D_MODEL), F32),
                        pltpu.VMEM((tm, D_MODEL), BF16),
                        pltpu.VMEM((tm, D_MODEL), F32)],
    )
    outs = pl.pallas_call(
        functools.partial(_ffn_kernel, n_stream=n_stream),
        grid_spec=grid_spec,
        out_shape=out_shape,
        compiler_params=_params("arbitrary", "arbitrary"),
        name="out_ffn",
    )(pt_flat, *operands)
    return outs if stream is not None else outs[0]


def _block_rank(gm, axis):
    nb = gm.shape[axis]
    idx = lax.broadcasted_iota(jnp.int32, gm.shape, axis)
    beats = []
    for m in range(nb):
        gmm = lax.slice_in_dim(gm, m, m + 1, axis=axis)
        beats.append(((gmm > gm) | ((gmm == gm) & (m < idx))).astype(jnp.int32))
    while len(beats) > 1:
        beats = [a + b for a, b in zip(beats[0::2], beats[1::2])] + ([beats[-1]] if len(beats) % 2 else [])
    return beats[0]


MOBA_HEADS = 8
BIG = 1e30


def _moba_kernel(slopes_ref, qt_ref, ka_ref, vt_ref, km_ref, o_ref, sel_sc, m_sc, acc_sc, *, nb):
    hg = pl.program_id(1)
    qi = pl.program_id(2)
    blk = MOBA_BLOCK
    n_top = min(MOBA_TOP_K, nb)
    keyi = lax.broadcasted_iota(jnp.int32, (blk, blk), 0)
    qryi = lax.broadcasted_iota(jnp.int32, (blk, blk), 1)
    causal = keyi <= qryi
    bidx = lax.broadcasted_iota(jnp.int32, (nb, blk), 0)
    q0 = pl.multiple_of(qi * blk, blk)

    hs = range(MOBA_HEADS)
    qts = [qt_ref[0, hh] for hh in hs]
    gates = [_dot(km_ref[0, hh], qts[hh]) for hh in hs]
    valid = bidx < qi
    ranks = [_block_rank(jnp.where(valid, g, NEG_INF), 0) for g in gates]
    for hh in hs:
        sel_sc[hh] = ((ranks[hh] < n_top) & valid).astype(F32)
    s0 = [jnp.where(causal, _dot(ka_ref[0, hh, pl.ds(q0, blk), :], qts[hh]), NEG_INF) for hh in hs]
    m0 = [jnp.max(t, axis=0, keepdims=True) for t in s0]
    p0 = [jnp.exp(s0[hh] - m0[hh]) for hh in hs]
    for hh in hs:
        m_sc[hh] = m0[hh]
    pv0 = [_dot(vt_ref[0, hh, qi], p0[hh].astype(BF16)) for hh in hs]
    for hh in hs:
        acc_sc[hh] = pv0[hh]

    def past_block(j, carry):
        k0 = pl.multiple_of(j * blk, blk)
        s = [_dot(ka_ref[0, hh, pl.ds(k0, blk), :], qt_ref[0, hh]) for hh in hs]
        ps, alphas = [], []
        for hh in hs:
            cj = -slopes_ref[hg * MOBA_HEADS + hh] * ((qi - j) * blk).astype(F32)
            picked = sel_sc[hh, pl.ds(j, 1), :] > 0.0
            m_old = m_sc[hh]
            m_new = jnp.maximum(m_old, jnp.where(picked, jnp.max(s[hh], axis=0, keepdims=True) + cj, NEG_INF))
            alpha = jnp.exp(m_old - m_new)
            p = jnp.exp(s[hh] - jnp.where(picked, m_new - cj, BIG))
            m_sc[hh] = m_new
            ps.append(p.astype(BF16))
            alphas.append(alpha)
        pv = [_dot(vt_ref[0, hh, j], ps[hh]) for hh in hs]
        for hh in hs:
            acc_sc[hh] = alphas[hh] * acc_sc[hh] + pv[hh]
        return carry

    lax.fori_loop(0, qi, past_block, 0)

    outs = [acc_sc[hh, 0:HEAD_DIM, :] / acc_sc[hh, HEAD_DIM:HEAD_DIM + 1, :] for hh in hs]
    o_ref[...] = jnp.concatenate([t.T for t in outs], axis=1).astype(o_ref.dtype)


def _moba_prompt(slopes, qt, ka, vt, kmean, batch, seq):
    nb = seq // MOBA_BLOCK
    blk = MOBA_BLOCK
    km = kmean.reshape(batch, nb, N_HEADS, HEAD_DIM).transpose(0, 2, 1, 3).astype(BF16)
    km = jnp.concatenate([km, jnp.zeros((batch, N_HEADS, nb, 128 - HEAD_DIM), BF16)], axis=3)

    hgn = MOBA_HEADS
    grid_spec = pltpu.PrefetchScalarGridSpec(
        num_scalar_prefetch=1,
        grid=(batch, N_HEADS // hgn, nb),
        in_specs=[pl.BlockSpec((1, hgn, 128, blk), lambda b, g, qi, s: (b, g, 0, qi)),
                  pl.BlockSpec((1, hgn, seq, 128), lambda b, g, qi, s: (b, g, 0, 0)),
                  pl.BlockSpec((1, hgn, nb, MOBA_VROWS, blk), lambda b, g, qi, s: (b, g, 0, 0, 0)),
                  pl.BlockSpec((1, hgn, nb, 128), lambda b, g, qi, s: (b, g, 0, 0))],
        out_specs=pl.BlockSpec((blk, hgn * HEAD_DIM), lambda b, g, qi, s: (b * nb + qi, g)),
        scratch_shapes=[pltpu.VMEM((hgn, nb, blk), F32),
                        pltpu.VMEM((hgn, 1, blk), F32),
                        pltpu.VMEM((hgn, MOBA_VROWS, blk), F32)],
    )
    return pl.pallas_call(
        functools.partial(_moba_kernel, nb=nb),
        grid_spec=grid_spec,
        out_shape=jax.ShapeDtypeStruct((batch * seq, WIDTH), BF16),
        compiler_params=_params("parallel", "parallel", "arbitrary"),
        name="moba_prompt",
    )(slopes, qt, ka, vt, km)


def _b(t):
    return t.astype(BF16)


def _split3(x):
    x1 = _b(x)
    r1 = x - x1.astype(F32)
    x2 = _b(r1)
    return x1, x2, _b(r1 - x2.astype(F32))


def _rwkv_pointwise(p, pprev, mu, w0, decay_up, a0, iclr_up, gate_up, k_k, k_a):
    xs = p + mu * (pprev - p)
    r = xs[:, _O_R:_O_R + WIDTH]
    k = xs[:, _O_K:_O_K + WIDTH]
    v = xs[:, _O_V:_O_V + WIDTH]
    xw = xs[:, _O_XW:_O_XW + DECAY_LORA]
    xa = xs[:, _O_XA:_O_XA + AAA_LORA]
    xg = xs[:, _O_XG:_O_XG + GATE_LORA]
    w = w0 + _dot(_b(jnp.tanh(xw)), _b(decay_up))
    w = -jax.nn.softplus(-w) - 0.5
    logdecay = -jnp.exp(w)
    a = jax.nn.sigmoid(a0 + _dot(_b(xa), _b(iclr_up)))
    g = _dot(_b(jax.nn.sigmoid(xg)), _b(gate_up))
    kk = k * k_k
    k2 = k * (1.0 + (a - 1.0) * k_a)
    return r, k2, v, kk, a, g, logdecay


def _head_sum(x):
    row = lax.broadcasted_iota(jnp.int32, (128, 128), 0)
    col = lax.broadcasted_iota(jnp.int32, (128, 128), 1)
    seg = ((row // HEAD_DIM) == (col // HEAD_DIM)).astype(BF16)
    hi = _b(x)
    lo = _b(x - hi.astype(F32))
    cols = [slice(g * 128, (g + 1) * 128) for g in range(x.shape[1] // 128)]
    return jnp.concatenate([_dot(hi[:, c], seg) + _dot(lo[:, c], seg) for c in cols], axis=1)


def _head_norm(kk_h):
    return kk_h * lax.rsqrt(jnp.maximum(jnp.sum(kk_h * kk_h, axis=-1, keepdims=True), 1e-24))


def _group_norm_out(y_h, r_h, k_h, v_h, g_h, rk_h, lnw_h, lnb_h):
    mean = jnp.mean(y_h, axis=-1, keepdims=True)
    var = jnp.mean(jnp.square(y_h - mean), axis=-1, keepdims=True)
    yn = (y_h - mean) * lax.rsqrt(var + GN_EPS) * lnw_h + lnb_h
    yn = yn + jnp.sum(r_h * k_h * rk_h, axis=-1, keepdims=True) * v_h
    return yn * g_h


def _unit_lower_inverse(mats):
    n = mats[0].shape[0]
    row = lax.broadcasted_iota(jnp.int32, (n, n), 0)
    col = lax.broadcasted_iota(jnp.int32, (n, n), 1)
    eye = (row == col).astype(F32)
    size = 16
    same = (row // size) == (col // size)
    pws = [jnp.where(same, a, 0.0) for a in mats]
    xs = [eye - pw for pw in pws]
    for _ in range(3):
        pwb = [_b(pw) for pw in pws]
        pws = [_dot(t, t) for t in pwb]
        xs = [x + _dot(_b(x), _b(pw)) for x, pw in zip(xs, pws)]
        yield
    while size < n:
        size2 = size * 2
        same2 = (row // size2) == (col // size2)
        keep = same2 & jnp.logical_not(same)
        xb = [_b(x) for x in xs]
        ox = [_b(_dot(_b(jnp.where(keep, a, 0.0)), t)) for a, t in zip(mats, xb)]
        xs = [x - _dot(t, o) for x, t, o in zip(xs, xb, ox)]
        same = same2
        size = size2
        yield
    return xs


def _cumsum_rows(x, seg):
    n = x.shape[0]
    row = lax.broadcasted_iota(jnp.int32, (n, n), 0)
    col = lax.broadcasted_iota(jnp.int32, (n, n), 1)
    tri = ((row >= col) & ((row // seg) == (col // seg))).astype(BF16)
    x1, x2, x3 = _split3(x)
    return _dot(tri, x1) + _dot(tri, x2) + _dot(tri, x3)


RWKV_STEP_CHUNKS = 4
_RWKV_BLOCK = RWKV_CHUNK * RWKV_STEP_CHUNKS
_KAPH, _RHAT, _KHAT, _KBAR, _BHAT, _BBAR, _VB = range(7)
_RHAT32, _BONUS, _GATE = range(3)


def _interleave(*gens):
    live = list(gens)
    while live:
        for gen in list(live):
            try:
                next(gen)
            except StopIteration:
                live.remove(gen)


def _rwkv_pointwise_stage(p, last_sc, w, ob_sc, of_sc, we_sc, keep):
    mu, w0, dup, a0, iup, gup, k_k, k_a, r_k = w
    L = RWKV_CHUNK
    nck = RWKV_STEP_CHUNKS
    ts = _RWKV_BLOCK
    rowi = lax.broadcasted_iota(jnp.int32, p.shape, 0)
    carry = last_sc[0:1, :] if keep is None else last_sc[0:1, :] * keep
    pprev = jnp.where(rowi == 0, carry, pltpu.roll(p, 1, 0))
    last_sc[0:1, :] = p[ts - 1:ts, :]
    yield
    r, k2, v, kk, a, g, logdecay = _rwkv_pointwise(p, pprev, mu, w0, dup, a0, iup, gup, k_k, k_a)
    yield
    cum = _cumsum_rows(logdecay, L)
    ends = [cum[c * L + L - 1:c * L + L, :] for c in range(nck)]
    cum_end = jnp.concatenate([jnp.broadcast_to(e, (L, WIDTH)) for e in ends], axis=0)
    w_inc = jnp.exp(cum)
    w_exc = jnp.exp(cum - logdecay)
    w_inv = jnp.exp(-cum)
    w_tail = jnp.exp(cum_end - cum)
    for c in range(nck):
        we_sc[c:c + 1, :] = jnp.exp(ends[c])
    yield
    kap_all = kk * lax.rsqrt(jnp.maximum(_head_sum(kk * kk), 1e-24))
    bb_all = kap_all * a
    r_hat_all = r * w_inc
    ob_sc[_KAPH] = _b(kap_all * w_exc)
    ob_sc[_RHAT] = _b(r_hat_all)
    ob_sc[_KHAT] = _b(k2 * w_inv)
    ob_sc[_KBAR] = _b(k2 * w_tail)
    yield
    ob_sc[_BHAT] = _b(bb_all * w_inv)
    ob_sc[_BBAR] = _b(bb_all * w_tail)
    ob_sc[_VB] = _b(v)
    of_sc[_RHAT32] = r_hat_all
    of_sc[_BONUS] = _head_sum(r * k2 * r_k) * v
    of_sc[_GATE] = g


def _rwkv_matmul_stage(ob_sc, of_sc, we_sc, s_sc, y_sc, lnw, lnb, o_ref, rows_out, keep):
    L = RWKV_CHUNK
    nck = RWKV_STEP_CHUNKS
    trow = lax.broadcasted_iota(jnp.int32, (L, L), 0)
    tcol = lax.broadcasted_iota(jnp.int32, (L, L), 1)
    lower_incl = trow >= tcol
    lower_strict = trow > tcol
    heads = range(N_HEADS)
    items = [(slice(c * L, (c + 1) * L), slice(h * HEAD_DIM, (h + 1) * HEAD_DIM))
             for c in range(nck) for h in heads]
    n = range(len(items))
    ld = lambda slot, it: ob_sc[slot, it[0], it[1]]
    lhs = [jnp.concatenate([ld(_KAPH, it), ld(_RHAT, it)], axis=0) for it in items]
    ak = [_dot_nt(lhs[i], ld(_KHAT, items[i])) for i in n]
    ab = [_dot_nt(lhs[i], ld(_BHAT, items[i])) for i in n]
    yield
    a_kr = [_b(jnp.concatenate([jnp.where(lower_strict, t[:L], 0.0), jnp.where(lower_incl, t[L:], 0.0)], axis=0))
            for t in ak]
    a_rb = [_b(jnp.where(lower_incl, t[L:], 0.0)) for t in ab]
    t_inv = yield from _unit_lower_inverse([jnp.where(lower_strict, t[:L], 0.0) for t in ab])
    t_inv = [_b(t) for t in t_inv]
    av = [_dot(a_kr[i], ld(_VB, items[i])) for i in n]
    pm = [_b(_dot(t_inv[i], ld(_KAPH, items[i]))) for i in n]
    yield
    qm = [_b(_dot(t_inv[i], _b(av[i][:L]))) for i in n]
    r_eff = [_b(of_sc[_RHAT32, items[i][0], items[i][1]] - _dot(a_rb[i], pm[i])) for i in n]
    ptb = [_b(_dot_tn(pm[i], ld(_BBAR, items[i]))) for i in n]
    yield
    y0 = [av[i][L:] - _dot(a_rb[i], qm[i]) for i in n]
    cm = [_dot_tn(ld(_VB, items[i]), ld(_KBAR, items[i])) - _dot_tn(qm[i], ld(_BBAR, items[i])) for i in n]
    yield
    state = [s_sc[h] if keep is None else s_sc[h] * keep for h in heads]
    for c in range(nck):
        w_end = we_sc[c:c + 1, :]
        sb = [_b(t) for t in state]
        ys = [_dot_nt(r_eff[c * N_HEADS + h], sb[h]) + y0[c * N_HEADS + h] for h in heads]
        state = [state[h] * w_end[:, items[h][1]] - _dot(sb[h], ptb[c * N_HEADS + h]) + cm[c * N_HEADS + h]
                 for h in heads]
        for h in heads:
            y_sc[items[c * N_HEADS + h]] = ys[h]
    for h in heads:
        s_sc[h] = state[h]
    yield
    y = y_sc[...]
    dev = y - _head_sum(y) * (1.0 / HEAD_DIM)
    var = _head_sum(dev * dev) * (1.0 / HEAD_DIM)
    yn = dev * lax.rsqrt(var + GN_EPS) * lnw + lnb
    o_ref[rows_out, :] = ((yn + of_sc[_BONUS]) * of_sc[_GATE]).astype(o_ref.dtype)


def _rwkv_chunk_kernel(p_ref, mu_ref, w0_ref, dup_ref, a0_ref, iup_ref, gup_ref, kk_ref, ka_ref,
                       rk_ref, lnw_ref, lnb_ref, o_ref, s_out_ref,
                       s_sc, last_sc, y_sc, xb_sc, xf_sc, xw_sc, yb_sc, yf_sc, yw_sc, *, seq_blocks):
    step = pl.program_id(0)
    ts = _RWKV_BLOCK

    @pl.when(step == 0)
    def _():
        s_sc[...] = jnp.zeros_like(s_sc)
        last_sc[...] = jnp.zeros_like(last_sc)
        yb_sc[...] = jnp.zeros_like(yb_sc)
        yf_sc[...] = jnp.zeros_like(yf_sc)
        yw_sc[...] = jnp.zeros_like(yw_sc)

    keep = jnp.where((2 * step) % seq_blocks == 0, 0.0, 1.0).astype(F32)
    w = (mu_ref[...], w0_ref[...], dup_ref[...], a0_ref[...], iup_ref[...], gup_ref[...],
         kk_ref[...], ka_ref[...], rk_ref[...])
    lnw, lnb = lnw_ref[...], lnb_ref[...]
    _interleave(_rwkv_matmul_stage(yb_sc, yf_sc, yw_sc, s_sc, y_sc, lnw, lnb, o_ref, slice(0, ts), None),
                _rwkv_pointwise_stage(p_ref[0:ts, :], last_sc, w, xb_sc, xf_sc, xw_sc, keep))
    s_out_ref[0] = s_sc[...]
    _interleave(_rwkv_matmul_stage(xb_sc, xf_sc, xw_sc, s_sc, y_sc, lnw, lnb, o_ref, slice(ts, 2 * ts), keep),
                _rwkv_pointwise_stage(p_ref[ts:2 * ts, :], last_sc, w, yb_sc, yf_sc, yw_sc, None))


def _rwkv_prompt(p_rw, weights, batch, seq):
    ts = _RWKV_BLOCK
    m = batch * seq
    seq_blocks = seq // ts
    assert seq_blocks % 2 == 0
    ns = m // (2 * ts)
    const = lambda c: (0, 0)
    w_specs = [pl.BlockSpec(w.shape, const) for w in weights]
    operand_scratch = [pltpu.VMEM((7, ts, WIDTH), BF16), pltpu.VMEM((3, ts, WIDTH), F32), pltpu.VMEM((8, WIDTH), F32)]
    out, state = pl.pallas_call(
        functools.partial(_rwkv_chunk_kernel, seq_blocks=seq_blocks),
        grid=(ns + 1,),
        in_specs=[pl.BlockSpec((2 * ts, RWKV_COLS), lambda c: (jnp.minimum(c, ns - 1), 0))] + w_specs,
        out_specs=[pl.BlockSpec((2 * ts, WIDTH), lambda c: (c, 0)),
                   pl.BlockSpec((1, N_HEADS, HEAD_DIM, HEAD_DIM),
                                lambda c: (jnp.maximum(2 * c - 1, 0) // seq_blocks, 0, 0, 0))],
        out_shape=[jax.ShapeDtypeStruct((m + 2 * ts, WIDTH), BF16),
                   jax.ShapeDtypeStruct((batch, N_HEADS, HEAD_DIM, HEAD_DIM), F32)],
        scratch_shapes=[pltpu.VMEM((N_HEADS, HEAD_DIM, HEAD_DIM), F32),
                        pltpu.VMEM((8, RWKV_COLS), F32),
                        pltpu.VMEM((ts, WIDTH), F32)] + operand_scratch + operand_scratch,
        compiler_params=_params("arbitrary"),
        name="rwkv_prompt",
    )(p_rw, *weights)
    return out[ts:ts + m], state


def _rwkv_step_kernel(p_ref, sh_ref, s_ref, mu_ref, w0_ref, dup_ref, a0_ref, iup_ref, gup_ref, kk_ref,
                      ka_ref, rk_ref, lnw_ref, lnb_ref, o_ref, s_out_ref):
    p = jnp.broadcast_to(p_ref[0], (8, RWKV_COLS))
    pprev = jnp.broadcast_to(sh_ref[0], (8, RWKV_COLS))
    r, k2, v, kk, a, g, logdecay = (t[0:1] for t in _rwkv_pointwise(
        p, pprev, mu_ref[...], w0_ref[...], dup_ref[...], a0_ref[...], iup_ref[...], gup_ref[...],
        kk_ref[...], ka_ref[...]))
    decay = jnp.exp(logdecay)
    n = HEAD_DIM
    eye = lax.broadcasted_iota(jnp.int32, (n, n), 0) == lax.broadcasted_iota(jnp.int32, (n, n), 1)

    def to_col(row_vec):
        return jnp.sum(jnp.where(eye, row_vec, 0.0), axis=-1, keepdims=True)

    def to_row(col_vec):
        return jnp.sum(jnp.where(eye, col_vec, 0.0), axis=0, keepdims=True)

    for h in range(N_HEADS):
        ln = slice(h * n, (h + 1) * n)
        s0 = s_ref[0, h]
        kap = _head_norm(kk[:, ln])
        sa = jnp.sum(s0 * (-kap), axis=-1, keepdims=True)
        s_new = s0 * decay[:, ln] + sa * (kap * a[:, ln]) + to_col(v[:, ln]) * k2[:, ln]
        s_out_ref[0, h] = s_new
        y = to_row(jnp.sum(s_new * r[:, ln], axis=-1, keepdims=True))
        o_ref[0, :, ln] = _group_norm_out(y, r[:, ln], k2[:, ln], v[:, ln], g[:, ln], rk_ref[:, ln],
                                          lnw_ref[:, ln], lnb_ref[:, ln]).astype(o_ref.dtype)


def _rwkv_sample(p_rw, shift, state, weights):
    n = p_rw.shape[0]
    const = lambda s: (0, 0)
    vec = pl.BlockSpec((1, 1, RWKV_COLS), lambda s: (s, 0, 0))
    st = pl.BlockSpec((1, N_HEADS, HEAD_DIM, HEAD_DIM), lambda s: (s, 0, 0, 0))
    rw, s_new = pl.pallas_call(
        _rwkv_step_kernel,
        grid=(n,),
        in_specs=[vec, vec, st] + [pl.BlockSpec(w.shape, const) for w in weights],
        out_specs=[pl.BlockSpec((1, 1, WIDTH), lambda s: (s, 0, 0)), st],
        out_shape=[jax.ShapeDtypeStruct((n, 1, WIDTH), BF16),
                   jax.ShapeDtypeStruct(state.shape, F32)],
        compiler_params=_params("parallel"),
        name="rwkv_sample",
    )(p_rw.reshape(n, 1, RWKV_COLS), shift.reshape(n, 1, RWKV_COLS), state, *weights)
    return rw.reshape(n, WIDTH), s_new


def _sample_gate_kernel(q_ref, km_ref, idx_ref):
    rnd = lambda t: t.astype(BF16).astype(F32)
    prod = rnd(km_ref[0]) * rnd(q_ref[0])
    lane = lax.broadcasted_iota(jnp.int32, (WIDTH, 128), 0)
    hcol = lax.broadcasted_iota(jnp.int32, (WIDTH, 128), 1)
    head_sum = ((lane // HEAD_DIM) == hcol).astype(BF16)
    pieces = _split3(prod)
    gate = _dot(pieces[0], head_sum) + _dot(pieces[1], head_sum) + _dot(pieces[2], head_sum)
    rank = _block_rank(gate, 0)
    bidx = lax.broadcasted_iota(jnp.int32, gate.shape, 0)
    rows = [jnp.sum(jnp.where(rank == i, bidx, 0), axis=0, keepdims=True) for i in range(MOBA_TOP_K)]
    rows += [jnp.zeros((1, 128), jnp.int32)] * (8 - MOBA_TOP_K)
    idx_ref[0] = jnp.concatenate(rows, axis=0)


def _sample_gate(q, kmean):
    n, nb, _ = kmean.shape
    return pl.pallas_call(
        _sample_gate_kernel,
        grid=(n,),
        in_specs=[pl.BlockSpec((1, 1, WIDTH), lambda s: (s, 0, 0)),
                  pl.BlockSpec((1, nb, WIDTH), lambda s: (s, 0, 0))],
        out_specs=pl.BlockSpec((1, 8, 128), lambda s: (s, 0, 0)),
        out_shape=jax.ShapeDtypeStruct((n, 8, 128), jnp.int32),
        compiler_params=_params("parallel"),
        name="sample_gate",
    )(q.reshape(n, 1, WIDTH), kmean)


def _sample_attn_kernel(pt_ref, top_ref, q_ref, kn_ref, vn_ref, topv_ref, slope_ref, ck_hbm, cv_hbm, o_ref,
                        kbuf, vbuf, sems, *, past_len, n_pages):
    ppb = _PAGES_PER_BLOCK
    npg = MOBA_TOP_K * ppb
    seq = pl.program_id(0)
    nseq = pl.num_programs(0)
    slot = seq % 2

    def slab_copies(sq, sl, h, i):
        blk_id = top_ref[(sq * N_HEADS + h) * MOBA_TOP_K + i // ppb]
        page = pt_ref[sq * n_pages + blk_id * ppb + i % ppb]
        return (pltpu.make_async_copy(ck_hbm.at[page, h], kbuf.at[sl, h, i], sems.at[sl]),
                pltpu.make_async_copy(cv_hbm.at[page, h], vbuf.at[sl, h, i], sems.at[sl]))

    def start_all(sq, sl):
        def per_head(h, carry):
            for i in range(npg):
                for cp in slab_copies(sq, sl, h, i):
                    cp.start()
            return carry
        lax.fori_loop(0, N_HEADS, per_head, 0)

    @pl.when(seq == 0)
    def _():
        start_all(0, 0)

    @pl.when(seq + 1 < nseq)
    def _():
        start_all(seq + 1, 1 - slot)

    def wait_head(h, carry):
        for i in range(npg):
            for cp in slab_copies(seq, slot, h, i):
                cp.wait()
        return carry
    lax.fori_loop(0, N_HEADS, wait_head, 0)

    q = q_ref[0] * (HEAD_DIM ** -0.5)
    slope = slope_ref[...]
    lane = lax.broadcasted_iota(jnp.int32, (1, 1, PAGE_SIZE), 2)
    scores = []
    for i in range(npg):
        blk_id = topv_ref[0, i // ppb][:, :, None]
        dist = (past_len - blk_id * MOBA_BLOCK - (i % ppb) * PAGE_SIZE - lane).astype(F32)
        s = jnp.sum(kbuf[slot, :, i] * q, axis=1, keepdims=True)
        scores.append(s - slope * dist)
    s_self = jnp.sum(q * kn_ref[0], axis=1, keepdims=True)
    m = s_self
    for s in scores:
        m = jnp.maximum(m, jnp.max(s, axis=2, keepdims=True))
    p_self = jnp.exp(s_self - m)
    den = p_self
    accv = jnp.zeros((N_HEADS, HEAD_DIM, PAGE_SIZE), F32)
    for i, s in enumerate(scores):
        p = jnp.exp(s - m)
        den = den + jnp.sum(p, axis=2, keepdims=True)
        accv = accv + vbuf[slot, :, i] * p
    acc = jnp.sum(accv, axis=2, keepdims=True) + p_self * vn_ref[0]
    o_ref[0] = acc / den


def _sample_attn(page_table, top, slopes, q, k_new, v_new, cache_kt, cache_vt):
    n, n_pages = page_table.shape
    npg = MOBA_TOP_K * _PAGES_PER_BLOCK
    top_flat = top.transpose(0, 2, 1).reshape(-1)
    vec = pl.BlockSpec((1, N_HEADS, HEAD_DIM, 1), lambda s, pt, tp: (s, 0, 0, 0))
    grid_spec = pltpu.PrefetchScalarGridSpec(
        num_scalar_prefetch=2,
        grid=(n,),
        in_specs=[vec, vec, vec,
                  pl.BlockSpec((1, MOBA_TOP_K, N_HEADS, 1), lambda s, pt, tp: (s, 0, 0, 0)),
                  pl.BlockSpec((N_HEADS, 1, 1), lambda s, pt, tp: (0, 0, 0)),
                  pl.BlockSpec(memory_space=pl.ANY),
                  pl.BlockSpec(memory_space=pl.ANY)],
        out_specs=vec,
        scratch_shapes=[pltpu.VMEM((2, N_HEADS, npg, HEAD_DIM, PAGE_SIZE), F32),
                        pltpu.VMEM((2, N_HEADS, npg, HEAD_DIM, PAGE_SIZE), F32),
                        pltpu.SemaphoreType.DMA((2,))],
    )
    return pl.pallas_call(
        functools.partial(_sample_attn_kernel, past_len=n_pages * PAGE_SIZE, n_pages=n_pages),
        grid_spec=grid_spec,
        out_shape=jax.ShapeDtypeStruct((n, N_HEADS, HEAD_DIM, 1), F32),
        compiler_params=_params("arbitrary"),
        name="sample_attn",
    )(page_table.reshape(-1), top_flat, q, k_new, v_new, top[..., None], slopes.reshape(N_HEADS, 1, 1),
      cache_kt, cache_vt)


def kernel(x_prompt, x_sample, cache_k, cache_v, page_table, state_wkv, state_shift,
           norm_mix_g, w_in, mu_shift, decay_w0, decay_up, iclr_a0, iclr_up, gate_up,
           k_k, k_a, r_k, ln_x_w, ln_x_b, w_out, norm_ffn_g, w_ffn_up, w_ffn_down, norm_final_g):
    depth = w_in.shape[0]
    assert depth == 1
    batch, seq, _ = x_prompt.shape
    n_seq, n_pages = page_table.shape
    slopes = jnp.exp2(-8.0 * jnp.arange(1, N_HEADS + 1, dtype=F32) / N_HEADS)

    l = 0
    row = lambda t: t.reshape(1, -1)
    w_in_16 = w_in[l].astype(BF16)
    w_in_b = jnp.concatenate([w_in_16[:, :ATTN_COLS], _to_internal(w_in_16[:, ATTN_COLS:])], axis=1)
    g_mix = row(norm_mix_g[l])
    rw_weights = (row(_to_internal(mu_shift[l])), row(decay_w0[l]), decay_up[l], row(iclr_a0[l]), iclr_up[l],
                  gate_up[l], row(k_k[l]), row(k_a[l]), row(r_k[l]), row(ln_x_w[l]), row(ln_x_b[l]))
    wo = w_out[l].astype(BF16)
    ffn_weights = (wo[:WIDTH], wo[WIDTH:], row(norm_ffn_g[l]), w_ffn_up[l].astype(BF16),
                   w_ffn_down[l].astype(BF16), row(norm_final_g))

    xp = x_prompt.reshape(batch * seq, D_MODEL)
    kt, vt, qt, ka, vtb, prw, kmean = _inproj_prompt(xp, g_mix, w_in_b, slopes, batch, seq)
    nb = seq // MOBA_BLOCK
    attn_p = _moba_prompt(slopes, qt, ka, vtb, kmean.reshape(batch, nb, WIDTH), batch, seq)
    rw_p, wkv_p = _rwkv_prompt(prw, rw_weights, batch, seq)
    ckt = cache_k.transpose(0, 1, 3, 4, 2).reshape(-1, N_HEADS, HEAD_DIM, PAGE_SIZE)
    cvt = cache_v.transpose(0, 1, 3, 4, 2).reshape(-1, N_HEADS, HEAD_DIM, PAGE_SIZE)
    y_prompt, kmean_s = _out_ffn(xp, attn_p, rw_p, *ffn_weights, tm=512, th=1024, stream=(page_table, ckt))
    y_prompt = y_prompt.reshape(batch, seq, D_MODEL)
    shift_p = _from_internal(prw.reshape(batch, seq, RWKV_COLS)[:, -1])

    hd = (N_HEADS, HEAD_DIM)
    xs = x_sample.reshape(n_seq, D_MODEL)
    proj_s = _inproj_small(xs, g_mix, w_in_b)
    q_s, k_s, v_s = proj_s[:, :WIDTH], proj_s[:, WIDTH:2 * WIDTH], proj_s[:, 2 * WIDTH:ATTN_COLS]
    prw_s = proj_s[:, ATTN_COLS:]
    top = _sample_gate(q_s, kmean_s)[:, :MOBA_TOP_K, :N_HEADS]
    col = lambda t: t.reshape(n_seq, N_HEADS, HEAD_DIM, 1)
    attn_s = _sample_attn(page_table, top, slopes, col(q_s), col(k_s), col(v_s), ckt, cvt).reshape(n_seq, WIDTH)
    rw_s, wkv_s = _rwkv_sample(prw_s, _to_internal(state_shift[l]), state_wkv[l], rw_weights)
    y_sample = _out_ffn(xs, attn_s.astype(BF16), rw_s, *ffn_weights, tm=n_seq).reshape(n_seq, 1, D_MODEL)
    shift_s = _from_internal(prw_s)

    return (y_prompt, y_sample,
            kt.transpose(0, 3, 1, 2)[None], vt.transpose(0, 3, 1, 2)[None],
            wkv_p[None], shift_p[None],
            k_s.reshape(1, n_seq, 1, *hd), v_s.reshape(1, n_seq, 1, *hd),
            wkv_s[None], shift_s[None])
```

```python
import functools

import jax
import jax.numpy as jnp
from jax import lax
from jax.experimental import pallas as pl
from jax.experimental.pallas import tpu as pltpu

F32 = jnp.float32
BF16 = jnp.bfloat16

D_MODEL = 1024
HEAD_DIM = 64
N_HEADS = 8
WIDTH = N_HEADS * HEAD_DIM
MOBA_BLOCK = 256
MOBA_TOP_K = 3
DECAY_LORA = 64
AAA_LORA = 64
GATE_LORA = 128
ATTN_COLS = 3 * WIDTH
RWKV_COLS = 3 * WIDTH + DECAY_LORA + AAA_LORA + GATE_LORA
RMS_EPS = 1e-6
GN_EPS = 64e-5
NEG_INF = -1e30
PAGE_SIZE = 128
RWKV_CHUNK = 64
VMEM_LIMIT = 48 * 1024 * 1024

_O_R, _O_K, _O_V = 0, WIDTH, 2 * WIDTH
_O_XW = 3 * WIDTH
_O_XA = _O_XW + DECAY_LORA
_O_XG = _O_XA + AAA_LORA


def _to_internal(t):
    r, xw, kv, rest = (t[..., :WIDTH], t[..., WIDTH:WIDTH + DECAY_LORA],
                       t[..., WIDTH + DECAY_LORA:3 * WIDTH + DECAY_LORA], t[..., 3 * WIDTH + DECAY_LORA:])
    return jnp.concatenate([r, kv, xw, rest], axis=-1)


def _from_internal(t):
    r, kv, xw, rest = (t[..., :WIDTH], t[..., WIDTH:3 * WIDTH],
                       t[..., 3 * WIDTH:3 * WIDTH + DECAY_LORA], t[..., 3 * WIDTH + DECAY_LORA:])
    return jnp.concatenate([r, xw, kv, rest], axis=-1)


def _params(*sem):
    return pltpu.CompilerParams(dimension_semantics=sem, vmem_limit_bytes=VMEM_LIMIT)


def _rms(x, g):
    return x * lax.rsqrt(jnp.mean(x * x, axis=-1, keepdims=True) + RMS_EPS) * g


def _dot(a, b, **kw):
    return jnp.dot(a, b, preferred_element_type=F32, **kw)


def _dot_nt(a, b, **kw):
    return lax.dot_general(a, b, (((1,), (1,)), ((), ())), preferred_element_type=F32, **kw)


def _dot_tn(a, b, **kw):
    return lax.dot_general(a, b, (((0,), (0,)), ((), ())), preferred_element_type=F32, **kw)


MOBA_VROWS = HEAD_DIM + 16


def _inproj_kernel(x_ref, g_ref, w_ref, srow_ref, kt_ref, vt_ref, qt_ref, ka_ref, vtb_ref, prw_ref, km_ref):
    tm = x_ref.shape[0]
    hd = (N_HEADS, HEAD_DIM, tm)
    xn = _rms(x_ref[...], g_ref[...]).astype(BF16)
    proj = _dot(xn, w_ref[...])
    q = proj[:, 0:WIDTH] * (HEAD_DIM ** -0.5)
    k = proj[:, WIDTH:2 * WIDTH]
    v = proj[:, 2 * WIDTH:3 * WIDTH]
    prw_ref[...] = proj[:, ATTN_COLS:]
    km_ref[0] = jnp.mean(k, axis=0, keepdims=True)
    k_t = k.T.reshape(hd)
    v_t = v.T.reshape(hd)
    kt_ref[0] = k_t
    vt_ref[0] = v_t
    vtb_ref[0, :, 0, 0:HEAD_DIM, :] = v_t.astype(BF16)
    ones_row = lax.broadcasted_iota(jnp.int32, (N_HEADS, MOBA_VROWS - HEAD_DIM, tm), 1) == 0
    vtb_ref[0, :, 0, HEAD_DIM:, :] = jnp.where(ones_row, 1.0, 0.0).astype(BF16)
    qt_ref[0, :, 0:HEAD_DIM, :] = q.T.reshape(hd).astype(BF16)
    qt_ref[0, :, HEAD_DIM:, :] = jnp.broadcast_to(srow_ref[...], hd).astype(BF16)
    lane = lax.broadcasted_iota(jnp.int32, (tm, 128 - HEAD_DIM), 1)
    pos = lax.broadcasted_iota(jnp.int32, (tm, 128 - HEAD_DIM), 0)
    pos_cols = jnp.where(lane == 0, pos, 0).astype(F32).astype(BF16)
    for h in range(N_HEADS):
        ka_ref[0, h, :, 0:HEAD_DIM] = k[:, h * HEAD_DIM:(h + 1) * HEAD_DIM].astype(BF16)
        ka_ref[0, h, :, HEAD_DIM:] = pos_cols


def _inproj_prompt(x, g, w_bf16, slopes, batch, seq):
    tm = MOBA_BLOCK
    m = x.shape[0]
    nb = seq // tm
    ncol = w_bf16.shape[1]
    row = lambda i: (i, 0)
    const = lambda i: (0, 0)
    tok = lambda i: (i // nb, 0, 0, i % nb)
    srow = jnp.zeros((N_HEADS, HEAD_DIM, 1), F32).at[:, 0, 0].set(slopes)
    return pl.pallas_call(
        _inproj_kernel,
        grid=(m // tm,),
        in_specs=[pl.BlockSpec((tm, D_MODEL), row),
                  pl.BlockSpec((1, D_MODEL), const),
                  pl.BlockSpec((D_MODEL, ncol), const),
                  pl.BlockSpec((N_HEADS, HEAD_DIM, 1), lambda i: (0, 0, 0))],
        out_specs=[pl.BlockSpec((1, N_HEADS, HEAD_DIM, tm), tok),
                   pl.BlockSpec((1, N_HEADS, HEAD_DIM, tm), tok),
                   pl.BlockSpec((1, N_HEADS, 128, tm), tok),
                   pl.BlockSpec((1, N_HEADS, tm, 128), lambda i: (i // nb, 0, i % nb, 0)),
                   pl.BlockSpec((1, N_HEADS, 1, MOBA_VROWS, tm), lambda i: (i // nb, 0, i % nb, 0, 0)),
                   pl.BlockSpec((tm, RWKV_COLS), row),
                   pl.BlockSpec((1, 1, WIDTH), lambda i: (i, 0, 0))],
        out_shape=[jax.ShapeDtypeStruct((batch, N_HEADS, HEAD_DIM, seq), F32),
                   jax.ShapeDtypeStruct((batch, N_HEADS, HEAD_DIM, seq), F32),
                   jax.ShapeDtypeStruct((batch, N_HEADS, 128, seq), BF16),
                   jax.ShapeDtypeStruct((batch, N_HEADS, seq, 128), BF16),
                   jax.ShapeDtypeStruct((batch, N_HEADS, nb, MOBA_VROWS, tm), BF16),
                   jax.ShapeDtypeStruct((m, RWKV_COLS), F32),
                   jax.ShapeDtypeStruct((m // tm, 1, WIDTH), F32)],
        compiler_params=_params("parallel"),
        name="inproj_prompt",
    )(x, g, w_bf16, srow)


def _inproj_small_kernel(x_ref, g_ref, w_ref, o_ref):
    xn = _rms(x_ref[...], g_ref[...]).astype(BF16)
    o_ref[...] = _dot(xn, w_ref[...])


def _inproj_small(x, g, w, tn=256):
    m = x.shape[0]
    ncol = w.shape[1]
    return pl.pallas_call(
        _inproj_small_kernel,
        grid=(ncol // tn,),
        in_specs=[pl.BlockSpec((m, D_MODEL), lambda j: (0, 0)),
                  pl.BlockSpec((1, D_MODEL), lambda j: (0, 0)),
                  pl.BlockSpec((D_MODEL, tn), lambda j: (0, j))],
        out_specs=pl.BlockSpec((m, tn), lambda j: (0, j)),
        out_shape=jax.ShapeDtypeStruct((m, ncol), F32),
        compiler_params=_params("parallel"),
        name="inproj_sample",
    )(x, g, w)


_STREAM_PAGES = 16
_PAGES_PER_BLOCK = MOBA_BLOCK // PAGE_SIZE


def _ffn_kernel(pt_ref, x_ref, attn_ref, rw_ref, woa_ref, wor_ref, gf_ref, wup_ref, wdn_ref, gfin_ref, *rest,
                n_stream):
    page_refs = rest[:n_stream]
    y_ref = rest[n_stream]
    km_ref = rest[n_stream + 1] if n_stream else None
    h_sc, hn_sc, acc_sc = rest[-3:]
    j = pl.program_id(1)

    @pl.when(j == 0)
    def _():
        h = x_ref[...] + _dot(attn_ref[...], woa_ref[...]) + _dot(rw_ref[...], wor_ref[...])
        h_sc[...] = h
        hn_sc[...] = _rms(h, gf_ref[...]).astype(BF16)
        acc_sc[...] = jnp.zeros_like(acc_sc)

    u = jnp.maximum(_dot(hn_sc[...], wup_ref[...]), 0.0)
    acc_sc[...] += _dot((u * u).astype(BF16), wdn_ref[...])

    if n_stream:
        ppb = _PAGES_PER_BLOCK
        bps = n_stream // ppb
        nb = km_ref.shape[1]
        g = (pl.program_id(0) * pl.num_programs(1) + j) % (nb // bps)
        for b in range(bps):
            tot = page_refs[b * ppb][0]
            for i in range(1, ppb):
                tot = tot + page_refs[b * ppb + i][0]
            tot_t = tot.reshape(WIDTH, PAGE_SIZE).T
            km_ref[0, pl.ds(g * bps + b, 1), :] = jnp.sum(tot_t, axis=0, keepdims=True) * (1.0 / MOBA_BLOCK)

    @pl.when(j == pl.num_programs(1) - 1)
    def _():
        y_ref[...] = _rms(h_sc[...] + acc_sc[...], gfin_ref[...])


def _out_ffn(x, attn, rw, woa, wor, gf, wup, wdn, gfin, tm, th=1024, stream=None):
    m = x.shape[0]
    hid = wup.shape[1]
    nj = hid // th
    row = lambda i, j, pt: (i, 0)
    const = lambda i, j, pt: (0, 0)
    in_specs = [pl.BlockSpec((tm, D_MODEL), row),
                pl.BlockSpec((tm, WIDTH), row),
                pl.BlockSpec((tm, WIDTH), row),
                pl.BlockSpec((WIDTH, D_MODEL), const),
                pl.BlockSpec((WIDTH, D_MODEL), const),
                pl.BlockSpec((1, D_MODEL), const),
                pl.BlockSpec((D_MODEL, th), lambda i, j, pt: (0, j)),
                pl.BlockSpec((th, D_MODEL), lambda i, j, pt: (j, 0)),
                pl.BlockSpec((1, D_MODEL), const)]
    out_specs = [pl.BlockSpec((tm, D_MODEL), row)]
    out_shape = [jax.ShapeDtypeStruct((m, D_MODEL), F32)]
    operands = [x, attn, rw, woa, wor, gf, wup, wdn, gfin]
    n_stream = 0
    pt_flat = jnp.zeros((1,), jnp.int32)
    if stream is not None:
        page_table, cache_kt = stream
        n_seq, n_pages = page_table.shape
        n_stream = _STREAM_PAGES
        spp = n_pages // n_stream
        assert (m // tm) * nj == n_seq * spp, "page streaming needs one grid step per 16 pages"
        nb = n_pages // _PAGES_PER_BLOCK
        pt_flat = page_table.reshape(-1)

        def page_spec(k):
            def imap(i, j, pt, k=k):
                t = i * nj + j
                return (pt[(t // spp) * n_pages + (t % spp) * n_stream + k], 0, 0, 0)
            return pl.BlockSpec((1, N_HEADS, HEAD_DIM, PAGE_SIZE), imap)

        in_specs += [page_spec(k) for k in range(n_stream)]
        operands += [cache_kt] * n_stream
        out_specs.append(pl.BlockSpec((1, nb, WIDTH), lambda i, j, pt: ((i * nj + j) // spp, 0, 0)))
        out_shape.append(jax.ShapeDtypeStruct((n_seq, nb, WIDTH), F32))
    grid_spec = pltpu.PrefetchScalarGridSpec(
        num_scalar_prefetch=1,
        grid=(m // tm, nj),
        in_specs=in_specs,
        out_specs=out_specs,
        scratch_shapes=[pltpu.VMEM((tm, D_MODEL), F32),
                        pltpu.VMEM((tm, D_MODEL), BF16),
                        pltpu.VMEM((tm, D_MODEL), F32)],
    )
    outs = pl.pallas_call(
        functools.partial(_ffn_kernel, n_stream=n_stream),
        grid_spec=grid_spec,
        out_shape=out_shape,
        compiler_params=_params("arbitrary", "arbitrary"),
        name="out_ffn",
    )(pt_flat, *operands)
    return outs if stream is not None else outs[0]


def _block_rank(gm, axis):
    nb = gm.shape[axis]
    idx = lax.broadcasted_iota(jnp.int32, gm.shape, axis)
    beats = []
    for m in range(nb):
        gmm = lax.slice_in_dim(gm, m, m + 1, axis=axis)
        beats.append(((gmm > gm) | ((gmm == gm) & (m < idx))).astype(jnp.int32))
    while len(beats) > 1:
        beats = [a + b for a, b in zip(beats[0::2], beats[1::2])] + ([beats[-1]] if len(beats) % 2 else [])
    return beats[0]


MOBA_HEADS = 8
MOBA_UNROLL = 4
BIG = 1e30


def _moba_kernel(slopes_ref, qt_ref, ka_ref, vt_ref, km_ref, o_ref, sel_sc, m_sc, acc_sc, *, nb):
    hg = pl.program_id(1)
    qi = pl.program_id(2)
    blk = MOBA_BLOCK
    n_top = min(MOBA_TOP_K, nb)
    keyi = lax.broadcasted_iota(jnp.int32, (blk, blk), 0)
    qryi = lax.broadcasted_iota(jnp.int32, (blk, blk), 1)
    causal = keyi <= qryi
    bidx = lax.broadcasted_iota(jnp.int32, (nb, blk), 0)
    q0 = pl.multiple_of(qi * blk, blk)

    hs = range(MOBA_HEADS)
    qts = [qt_ref[0, hh] for hh in hs]
    gates = [_dot(km_ref[0, hh], qts[hh]) for hh in hs]
    valid = bidx < qi
    ranks = [_block_rank(jnp.where(valid, g, NEG_INF), 0) for g in gates]
    for hh in hs:
        sel_sc[hh] = ((ranks[hh] < n_top) & valid).astype(F32)
    s0 = [jnp.where(causal, _dot(ka_ref[0, hh, pl.ds(q0, blk), :], qts[hh]), NEG_INF) for hh in hs]
    m0 = [jnp.max(t, axis=0, keepdims=True) for t in s0]
    p0 = [jnp.exp(s0[hh] - m0[hh]) for hh in hs]
    for hh in hs:
        m_sc[hh] = m0[hh]
    pv0 = [_dot(vt_ref[0, hh, qi], p0[hh].astype(BF16)) for hh in hs]
    for hh in hs:
        acc_sc[hh] = pv0[hh]

    def past_block(j, carry):
        k0 = pl.multiple_of(j * blk, blk)
        s = [_dot(ka_ref[0, hh, pl.ds(k0, blk), :], qt_ref[0, hh]) for hh in hs]
        ps, alphas = [], []
        for hh in hs:
            cj = -slopes_ref[hg * MOBA_HEADS + hh] * ((qi - j) * blk).astype(F32)
            picked = sel_sc[hh, pl.ds(j, 1), :] > 0.0
            m_old = m_sc[hh]
            m_new = jnp.maximum(m_old, jnp.where(picked, jnp.max(s[hh], axis=0, keepdims=True) + cj, NEG_INF))
            alpha = jnp.exp(m_old - m_new)
            p = jnp.exp(s[hh] - jnp.where(picked, m_new - cj, BIG))
            m_sc[hh] = m_new
            ps.append(p.astype(BF16))
            alphas.append(alpha)
        pv = [_dot(vt_ref[0, hh, j], ps[hh]) for hh in hs]
        for hh in hs:
            acc_sc[hh] = alphas[hh] * acc_sc[hh] + pv[hh]
        return carry

    def unrolled_body(jj, carry):
        for t in range(MOBA_UNROLL):
            past_block(MOBA_UNROLL * jj + t, carry)
        return carry

    lax.fori_loop(0, qi // MOBA_UNROLL, unrolled_body, 0)
    lax.fori_loop((qi // MOBA_UNROLL) * MOBA_UNROLL, qi, past_block, 0)

    outs = [acc_sc[hh, 0:HEAD_DIM, :] / acc_sc[hh, HEAD_DIM:HEAD_DIM + 1, :] for hh in hs]
    o_ref[...] = jnp.concatenate([t.T for t in outs], axis=1).astype(o_ref.dtype)


def _moba_prompt(slopes, qt, ka, vt, kmean, batch, seq):
    nb = seq // MOBA_BLOCK
    blk = MOBA_BLOCK
    km = kmean.reshape(batch, nb, N_HEADS, HEAD_DIM).transpose(0, 2, 1, 3).astype(BF16)
    km = jnp.concatenate([km, jnp.zeros((batch, N_HEADS, nb, 128 - HEAD_DIM), BF16)], axis=3)

    hgn = MOBA_HEADS
    grid_spec = pltpu.PrefetchScalarGridSpec(
        num_scalar_prefetch=1,
        grid=(batch, N_HEADS // hgn, nb),
        in_specs=[pl.BlockSpec((1, hgn, 128, blk), lambda b, g, qi, s: (b, g, 0, qi)),
                  pl.BlockSpec((1, hgn, seq, 128), lambda b, g, qi, s: (b, g, 0, 0)),
                  pl.BlockSpec((1, hgn, nb, MOBA_VROWS, blk), lambda b, g, qi, s: (b, g, 0, 0, 0)),
                  pl.BlockSpec((1, hgn, nb, 128), lambda b, g, qi, s: (b, g, 0, 0))],
        out_specs=pl.BlockSpec((blk, hgn * HEAD_DIM), lambda b, g, qi, s: (b * nb + qi, g)),
        scratch_shapes=[pltpu.VMEM((hgn, nb, blk), F32),
                        pltpu.VMEM((hgn, 1, blk), F32),
                        pltpu.VMEM((hgn, MOBA_VROWS, blk), F32)],
    )
    return pl.pallas_call(
        functools.partial(_moba_kernel, nb=nb),
        grid_spec=grid_spec,
        out_shape=jax.ShapeDtypeStruct((batch * seq, WIDTH), BF16),
        compiler_params=_params("parallel", "parallel", "arbitrary"),
        name="moba_prompt",
    )(slopes, qt, ka, vt, km)


def _b(t):
    return t.astype(BF16)


def _split3(x):
    x1 = _b(x)
    r1 = x - x1.astype(F32)
    x2 = _b(r1)
    return x1, x2, _b(r1 - x2.astype(F32))


def _rwkv_pointwise(p, pprev, mu, w0, decay_up, a0, iclr_up, gate_up, k_k, k_a):
    xs = p + mu * (pprev - p)
    r = xs[:, _O_R:_O_R + WIDTH]
    k = xs[:, _O_K:_O_K + WIDTH]
    v = xs[:, _O_V:_O_V + WIDTH]
    xw = xs[:, _O_XW:_O_XW + DECAY_LORA]
    xa = xs[:, _O_XA:_O_XA + AAA_LORA]
    xg = xs[:, _O_XG:_O_XG + GATE_LORA]
    w = w0 + _dot(_b(jnp.tanh(xw)), _b(decay_up))
    w = -jax.nn.softplus(-w) - 0.5
    logdecay = -jnp.exp(w)
    a = jax.nn.sigmoid(a0 + _dot(_b(xa), _b(iclr_up)))
    g = _dot(_b(jax.nn.sigmoid(xg)), _b(gate_up))
    kk = k * k_k
    k2 = k * (1.0 + (a - 1.0) * k_a)
    return r, k2, v, kk, a, g, logdecay


def _head_sum(x):
    row = lax.broadcasted_iota(jnp.int32, (128, 128), 0)
    col = lax.broadcasted_iota(jnp.int32, (128, 128), 1)
    seg = ((row // HEAD_DIM) == (col // HEAD_DIM)).astype(BF16)
    hi = _b(x)
    lo = _b(x - hi.astype(F32))
    cols = [slice(g * 128, (g + 1) * 128) for g in range(x.shape[1] // 128)]
    return jnp.concatenate([_dot(hi[:, c], seg) + _dot(lo[:, c], seg) for c in cols], axis=1)


def _head_norm(kk_h):
    return kk_h * lax.rsqrt(jnp.maximum(jnp.sum(kk_h * kk_h, axis=-1, keepdims=True), 1e-24))


def _group_norm_out(y_h, r_h, k_h, v_h, g_h, rk_h, lnw_h, lnb_h):
    mean = jnp.mean(y_h, axis=-1, keepdims=True)
    var = jnp.mean(jnp.square(y_h - mean), axis=-1, keepdims=True)
    yn = (y_h - mean) * lax.rsqrt(var + GN_EPS) * lnw_h + lnb_h
    yn = yn + jnp.sum(r_h * k_h * rk_h, axis=-1, keepdims=True) * v_h
    return yn * g_h


def _unit_lower_inverse(mats):
    n = mats[0].shape[0]
    row = lax.broadcasted_iota(jnp.int32, (n, n), 0)
    col = lax.broadcasted_iota(jnp.int32, (n, n), 1)
    eye = (row == col).astype(F32)
    size = 16
    same = (row // size) == (col // size)
    pws = [jnp.where(same, a, 0.0) for a in mats]
    xs = [eye - pw for pw in pws]
    for _ in range(3):
        pwb = [_b(pw) for pw in pws]
        pws = [_dot(t, t) for t in pwb]
        xs = [x + _dot(_b(x), _b(pw)) for x, pw in zip(xs, pws)]
        yield
    while size < n:
        size2 = size * 2
        same2 = (row // size2) == (col // size2)
        keep = same2 & jnp.logical_not(same)
        xb = [_b(x) for x in xs]
        ox = [_b(_dot(_b(jnp.where(keep, a, 0.0)), t)) for a, t in zip(mats, xb)]
        xs = [x - _dot(t, o) for x, t, o in zip(xs, xb, ox)]
        same = same2
        size = size2
        yield
    return xs


def _cumsum_rows(x, seg):
    n = x.shape[0]
    row = lax.broadcasted_iota(jnp.int32, (n, n), 0)
    col = lax.broadcasted_iota(jnp.int32, (n, n), 1)
    tri = ((row >= col) & ((row // seg) == (col // seg))).astype(BF16)
    x1, x2, x3 = _split3(x)
    return _dot(tri, x1) + _dot(tri, x2) + _dot(tri, x3)


RWKV_STEP_CHUNKS = 4
_RWKV_BLOCK = RWKV_CHUNK * RWKV_STEP_CHUNKS
_KAPH, _RHAT, _KHAT, _KBAR, _BHAT, _BBAR, _VB = range(7)
_RHAT32, _BONUS, _GATE = range(3)


def _interleave(*gens):
    live = list(gens)
    while live:
        for gen in list(live):
            try:
                next(gen)
            except StopIteration:
                live.remove(gen)


def _rwkv_pointwise_stage(p, last_sc, w, ob_sc, of_sc, we_sc, keep):
    mu, w0, dup, a0, iup, gup, k_k, k_a, r_k = w
    L = RWKV_CHUNK
    nck = RWKV_STEP_CHUNKS
    ts = _RWKV_BLOCK
    rowi = lax.broadcasted_iota(jnp.int32, p.shape, 0)
    carry = last_sc[0:1, :] if keep is None else last_sc[0:1, :] * keep
    pprev = jnp.where(rowi == 0, carry, pltpu.roll(p, 1, 0))
    last_sc[0:1, :] = p[ts - 1:ts, :]
    yield
    r, k2, v, kk, a, g, logdecay = _rwkv_pointwise(p, pprev, mu, w0, dup, a0, iup, gup, k_k, k_a)
    yield
    cum = _cumsum_rows(logdecay, L)
    ends = [cum[c * L + L - 1:c * L + L, :] for c in range(nck)]
    cum_end = jnp.concatenate([jnp.broadcast_to(e, (L, WIDTH)) for e in ends], axis=0)
    w_inc = jnp.exp(cum)
    w_exc = jnp.exp(cum - logdecay)
    w_inv = jnp.exp(-cum)
    w_tail = jnp.exp(cum_end - cum)
    for c in range(nck):
        we_sc[c:c + 1, :] = jnp.exp(ends[c])
    yield
    kap_all = kk * lax.rsqrt(jnp.maximum(_head_sum(kk * kk), 1e-24))
    bb_all = kap_all * a
    r_hat_all = r * w_inc
    ob_sc[_KAPH] = _b(kap_all * w_exc)
    ob_sc[_RHAT] = _b(r_hat_all)
    ob_sc[_KHAT] = _b(k2 * w_inv)
    ob_sc[_KBAR] = _b(k2 * w_tail)
    yield
    ob_sc[_BHAT] = _b(bb_all * w_inv)
    ob_sc[_BBAR] = _b(bb_all * w_tail)
    ob_sc[_VB] = _b(v)
    of_sc[_RHAT32] = r_hat_all
    of_sc[_BONUS] = _head_sum(r * k2 * r_k) * v
    of_sc[_GATE] = g


def _rwkv_matmul_stage(ob_sc, of_sc, we_sc, s_sc, y_sc, lnw, lnb, o_ref, rows_out, keep):
    L = RWKV_CHUNK
    nck = RWKV_STEP_CHUNKS
    trow = lax.broadcasted_iota(jnp.int32, (L, L), 0)
    tcol = lax.broadcasted_iota(jnp.int32, (L, L), 1)
    lower_incl = trow >= tcol
    lower_strict = trow > tcol
    heads = range(N_HEADS)
    items = [(slice(c * L, (c + 1) * L), slice(h * HEAD_DIM, (h + 1) * HEAD_DIM))
             for c in range(nck) for h in heads]
    n = range(len(items))
    ld = lambda slot, it: ob_sc[slot, it[0], it[1]]
    lhs = [jnp.concatenate([ld(_KAPH, it), ld(_RHAT, it)], axis=0) for it in items]
    ak = [_dot_nt(lhs[i], ld(_KHAT, items[i])) for i in n]
    ab = [_dot_nt(lhs[i], ld(_BHAT, items[i])) for i in n]
    yield
    a_kr = [_b(jnp.concatenate([jnp.where(lower_strict, t[:L], 0.0), jnp.where(lower_incl, t[L:], 0.0)], axis=0))
            for t in ak]
    a_rb = [_b(jnp.where(lower_incl, t[L:], 0.0)) for t in ab]
    t_inv = yield from _unit_lower_inverse([jnp.where(lower_strict, t[:L], 0.0) for t in ab])
    t_inv = [_b(t) for t in t_inv]
    av = [_dot(a_kr[i], ld(_VB, items[i])) for i in n]
    pm = [_b(_dot(t_inv[i], ld(_KAPH, items[i]))) for i in n]
    yield
    qm = [_b(_dot(t_inv[i], _b(av[i][:L]))) for i in n]
    r_eff = [_b(of_sc[_RHAT32, items[i][0], items[i][1]] - _dot(a_rb[i], pm[i])) for i in n]
    ptb = [_b(_dot_tn(pm[i], ld(_BBAR, items[i]))) for i in n]
    yield
    y0 = [av[i][L:] - _dot(a_rb[i], qm[i]) for i in n]
    cm = [_dot_tn(ld(_VB, items[i]), ld(_KBAR, items[i])) - _dot_tn(qm[i], ld(_BBAR, items[i])) for i in n]
    yield
    state = [s_sc[h] if keep is None else s_sc[h] * keep for h in heads]
    for c in range(nck):
        w_end = we_sc[c:c + 1, :]
        sb = [_b(t) for t in state]
        ys = [_dot_nt(r_eff[c * N_HEADS + h], sb[h]) + y0[c * N_HEADS + h] for h in heads]
        state = [state[h] * w_end[:, items[h][1]] - _dot(sb[h], ptb[c * N_HEADS + h]) + cm[c * N_HEADS + h]
                 for h in heads]
        for h in heads:
            y_sc[items[c * N_HEADS + h]] = ys[h]
    for h in heads:
        s_sc[h] = state[h]
    yield
    y = y_sc[...]
    dev = y - _head_sum(y) * (1.0 / HEAD_DIM)
    var = _head_sum(dev * dev) * (1.0 / HEAD_DIM)
    yn = dev * lax.rsqrt(var + GN_EPS) * lnw + lnb
    o_ref[rows_out, :] = ((yn + of_sc[_BONUS]) * of_sc[_GATE]).astype(o_ref.dtype)


def _rwkv_chunk_kernel(p_ref, mu_ref, w0_ref, dup_ref, a0_ref, iup_ref, gup_ref, kk_ref, ka_ref,
                       rk_ref, lnw_ref, lnb_ref, o_ref, s_out_ref,
                       s_sc, last_sc, y_sc, xb_sc, xf_sc, xw_sc, yb_sc, yf_sc, yw_sc, *, seq_blocks):
    step = pl.program_id(0)
    ts = _RWKV_BLOCK

    @pl.when(step == 0)
    def _():
        s_sc[...] = jnp.zeros_like(s_sc)
        last_sc[...] = jnp.zeros_like(last_sc)
        yb_sc[...] = jnp.zeros_like(yb_sc)
        yf_sc[...] = jnp.zeros_like(yf_sc)
        yw_sc[...] = jnp.zeros_like(yw_sc)

    keep = jnp.where((2 * step) % seq_blocks == 0, 0.0, 1.0).astype(F32)
    w = (mu_ref[...], w0_ref[...], dup_ref[...], a0_ref[...], iup_ref[...], gup_ref[...],
         kk_ref[...], ka_ref[...], rk_ref[...])
    lnw, lnb = lnw_ref[...], lnb_ref[...]
    _interleave(_rwkv_matmul_stage(yb_sc, yf_sc, yw_sc, s_sc, y_sc, lnw, lnb, o_ref, slice(0, ts), None),
                _rwkv_pointwise_stage(p_ref[0:ts, :], last_sc, w, xb_sc, xf_sc, xw_sc, keep))
    s_out_ref[0] = s_sc[...]
    _interleave(_rwkv_matmul_stage(xb_sc, xf_sc, xw_sc, s_sc, y_sc, lnw, lnb, o_ref, slice(ts, 2 * ts), keep),
                _rwkv_pointwise_stage(p_ref[ts:2 * ts, :], last_sc, w, yb_sc, yf_sc, yw_sc, None))


def _rwkv_prompt(p_rw, weights, batch, seq):
    ts = _RWKV_BLOCK
    m = batch * seq
    seq_blocks = seq // ts
    assert seq_blocks % 2 == 0
    ns = m // (2 * ts)
    const = lambda c: (0, 0)
    w_specs = [pl.BlockSpec(w.shape, const) for w in weights]
    operand_scratch = [pltpu.VMEM((7, ts, WIDTH), BF16), pltpu.VMEM((3, ts, WIDTH), F32), pltpu.VMEM((8, WIDTH), F32)]
    out, state = pl.pallas_call(
        functools.partial(_rwkv_chunk_kernel, seq_blocks=seq_blocks),
        grid=(ns + 1,),
        in_specs=[pl.BlockSpec((2 * ts, RWKV_COLS), lambda c: (jnp.minimum(c, ns - 1), 0))] + w_specs,
        out_specs=[pl.BlockSpec((2 * ts, WIDTH), lambda c: (c, 0)),
                   pl.BlockSpec((1, N_HEADS, HEAD_DIM, HEAD_DIM),
                                lambda c: (jnp.maximum(2 * c - 1, 0) // seq_blocks, 0, 0, 0))],
        out_shape=[jax.ShapeDtypeStruct((m + 2 * ts, WIDTH), BF16),
                   jax.ShapeDtypeStruct((batch, N_HEADS, HEAD_DIM, HEAD_DIM), F32)],
        scratch_shapes=[pltpu.VMEM((N_HEADS, HEAD_DIM, HEAD_DIM), F32),
                        pltpu.VMEM((8, RWKV_COLS), F32),
                        pltpu.VMEM((ts, WIDTH), F32)] + operand_scratch + operand_scratch,
        compiler_params=_params("arbitrary"),
        name="rwkv_prompt",
    )(p_rw, *weights)
    return out[ts:ts + m], state


def _rwkv_step_kernel(p_ref, sh_ref, s_ref, mu_ref, w0_ref, dup_ref, a0_ref, iup_ref, gup_ref, kk_ref,
                      ka_ref, rk_ref, lnw_ref, lnb_ref, o_ref, s_out_ref):
    p = jnp.broadcast_to(p_ref[0], (8, RWKV_COLS))
    pprev = jnp.broadcast_to(sh_ref[0], (8, RWKV_COLS))
    r, k2, v, kk, a, g, logdecay = (t[0:1] for t in _rwkv_pointwise(
        p, pprev, mu_ref[...], w0_ref[...], dup_ref[...], a0_ref[...], iup_ref[...], gup_ref[...],
        kk_ref[...], ka_ref[...]))
    decay = jnp.exp(logdecay)
    n = HEAD_DIM
    eye = lax.broadcasted_iota(jnp.int32, (n, n), 0) == lax.broadcasted_iota(jnp.int32, (n, n), 1)

    def to_col(row_vec):
        return jnp.sum(jnp.where(eye, row_vec, 0.0), axis=-1, keepdims=True)

    def to_row(col_vec):
        return jnp.sum(jnp.where(eye, col_vec, 0.0), axis=0, keepdims=True)

    for h in range(N_HEADS):
        ln = slice(h * n, (h + 1) * n)
        s0 = s_ref[0, h]
        kap = _head_norm(kk[:, ln])
        sa = jnp.sum(s0 * (-kap), axis=-1, keepdims=True)
        s_new = s0 * decay[:, ln] + sa * (kap * a[:, ln]) + to_col(v[:, ln]) * k2[:, ln]
        s_out_ref[0, h] = s_new
        y = to_row(jnp.sum(s_new * r[:, ln], axis=-1, keepdims=True))
        o_ref[0, :, ln] = _group_norm_out(y, r[:, ln], k2[:, ln], v[:, ln], g[:, ln], rk_ref[:, ln],
                                          lnw_ref[:, ln], lnb_ref[:, ln]).astype(o_ref.dtype)


def _rwkv_sample(p_rw, shift, state, weights):
    n = p_rw.shape[0]
    const = lambda s: (0, 0)
    vec = pl.BlockSpec((1, 1, RWKV_COLS), lambda s: (s, 0, 0))
    st = pl.BlockSpec((1, N_HEADS, HEAD_DIM, HEAD_DIM), lambda s: (s, 0, 0, 0))
    rw, s_new = pl.pallas_call(
        _rwkv_step_kernel,
        grid=(n,),
        in_specs=[vec, vec, st] + [pl.BlockSpec(w.shape, const) for w in weights],
        out_specs=[pl.BlockSpec((1, 1, WIDTH), lambda s: (s, 0, 0)), st],
        out_shape=[jax.ShapeDtypeStruct((n, 1, WIDTH), BF16),
                   jax.ShapeDtypeStruct(state.shape, F32)],
        compiler_params=_params("parallel"),
        name="rwkv_sample",
    )(p_rw.reshape(n, 1, RWKV_COLS), shift.reshape(n, 1, RWKV_COLS), state, *weights)
    return rw.reshape(n, WIDTH), s_new


def _sample_gate_kernel(q_ref, km_ref, idx_ref):
    rnd = lambda t: t.astype(BF16).astype(F32)
    prod = rnd(km_ref[0]) * rnd(q_ref[0])
    lane = lax.broadcasted_iota(jnp.int32, (WIDTH, 128), 0)
    hcol = lax.broadcasted_iota(jnp.int32, (WIDTH, 128), 1)
    head_sum = ((lane // HEAD_DIM) == hcol).astype(BF16)
    pieces = _split3(prod)
    gate = _dot(pieces[0], head_sum) + _dot(pieces[1], head_sum) + _dot(pieces[2], head_sum)
    rank = _block_rank(gate, 0)
    bidx = lax.broadcasted_iota(jnp.int32, gate.shape, 0)
    rows = [jnp.sum(jnp.where(rank == i, bidx, 0), axis=0, keepdims=True) for i in range(MOBA_TOP_K)]
    rows += [jnp.zeros((1, 128), jnp.int32)] * (8 - MOBA_TOP_K)
    idx_ref[0] = jnp.concatenate(rows, axis=0)


def _sample_gate(q, kmean):
    n, nb, _ = kmean.shape
    return pl.pallas_call(
        _sample_gate_kernel,
        grid=(n,),
        in_specs=[pl.BlockSpec((1, 1, WIDTH), lambda s: (s, 0, 0)),
                  pl.BlockSpec((1, nb, WIDTH), lambda s: (s, 0, 0))],
        out_specs=pl.BlockSpec((1, 8, 128), lambda s: (s, 0, 0)),
        out_shape=jax.ShapeDtypeStruct((n, 8, 128), jnp.int32),
        compiler_params=_params("parallel"),
        name="sample_gate",
    )(q.reshape(n, 1, WIDTH), kmean)


def _sample_attn_kernel(pt_ref, top_ref, q_ref, kn_ref, vn_ref, topv_ref, slope_ref, ck_hbm, cv_hbm, o_ref,
                        kbuf, vbuf, sems, *, past_len, n_pages):
    ppb = _PAGES_PER_BLOCK
    npg = MOBA_TOP_K * ppb
    seq = pl.program_id(0)
    nseq = pl.num_programs(0)
    slot = seq % 2

    def slab_copies(sq, sl, h, i):
        blk_id = top_ref[(sq * N_HEADS + h) * MOBA_TOP_K + i // ppb]
        page = pt_ref[sq * n_pages + blk_id * ppb + i % ppb]
        return (pltpu.make_async_copy(ck_hbm.at[page, h], kbuf.at[sl, h, i], sems.at[sl]),
                pltpu.make_async_copy(cv_hbm.at[page, h], vbuf.at[sl, h, i], sems.at[sl]))

    def start_all(sq, sl):
        def per_head(h, carry):
            for i in range(npg):
                for cp in slab_copies(sq, sl, h, i):
                    cp.start()
            return carry
        lax.fori_loop(0, N_HEADS, per_head, 0)

    @pl.when(seq == 0)
    def _():
        start_all(0, 0)

    @pl.when(seq + 1 < nseq)
    def _():
        start_all(seq + 1, 1 - slot)

    def wait_head(h, carry):
        for i in range(npg):
            for cp in slab_copies(seq, slot, h, i):
                cp.wait()
        return carry
    lax.fori_loop(0, N_HEADS, wait_head, 0)

    q = q_ref[0] * (HEAD_DIM ** -0.5)
    slope = slope_ref[...]
    lane = lax.broadcasted_iota(jnp.int32, (1, 1, PAGE_SIZE), 2)
    scores = []
    for i in range(npg):
        blk_id = topv_ref[0, i // ppb][:, :, None]
        dist = (past_len - blk_id * MOBA_BLOCK - (i % ppb) * PAGE_SIZE - lane).astype(F32)
        s = jnp.sum(kbuf[slot, :, i] * q, axis=1, keepdims=True)
        scores.append(s - slope * dist)
    s_self = jnp.sum(q * kn_ref[0], axis=1, keepdims=True)
    m = s_self
    for s in scores:
        m = jnp.maximum(m, jnp.max(s, axis=2, keepdims=True))
    p_self = jnp.exp(s_self - m)
    den = p_self
    accv = jnp.zeros((N_HEADS, HEAD_DIM, PAGE_SIZE), F32)
    for i, s in enumerate(scores):
        p = jnp.exp(s - m)
        den = den + jnp.sum(p, axis=2, keepdims=True)
        accv = accv + vbuf[slot, :, i] * p
    acc = jnp.sum(accv, axis=2, keepdims=True) + p_self * vn_ref[0]
    o_ref[0] = acc / den


def _sample_attn(page_table, top, slopes, q, k_new, v_new, cache_kt, cache_vt):
    n, n_pages = page_table.shape
    npg = MOBA_TOP_K * _PAGES_PER_BLOCK
    top_flat = top.transpose(0, 2, 1).reshape(-1)
    vec = pl.BlockSpec((1, N_HEADS, HEAD_DIM, 1), lambda s, pt, tp: (s, 0, 0, 0))
    grid_spec = pltpu.PrefetchScalarGridSpec(
        num_scalar_prefetch=2,
        grid=(n,),
        in_specs=[vec, vec, vec,
                  pl.BlockSpec((1, MOBA_TOP_K, N_HEADS, 1), lambda s, pt, tp: (s, 0, 0, 0)),
                  pl.BlockSpec((N_HEADS, 1, 1), lambda s, pt, tp: (0, 0, 0)),
                  pl.BlockSpec(memory_space=pl.ANY),
                  pl.BlockSpec(memory_space=pl.ANY)],
        out_specs=vec,
        scratch_shapes=[pltpu.VMEM((2, N_HEADS, npg, HEAD_DIM, PAGE_SIZE), F32),
                        pltpu.VMEM((2, N_HEADS, npg, HEAD_DIM, PAGE_SIZE), F32),
                        pltpu.SemaphoreType.DMA((2,))],
    )
    return pl.pallas_call(
        functools.partial(_sample_attn_kernel, past_len=n_pages * PAGE_SIZE, n_pages=n_pages),
        grid_spec=grid_spec,
        out_shape=jax.ShapeDtypeStruct((n, N_HEADS, HEAD_DIM, 1), F32),
        compiler_params=_params("arbitrary"),
        name="sample_attn",
    )(page_table.reshape(-1), top_flat, q, k_new, v_new, top[..., None], slopes.reshape(N_HEADS, 1, 1),
      cache_kt, cache_vt)


def kernel(x_prompt, x_sample, cache_k, cache_v, page_table, state_wkv, state_shift,
           norm_mix_g, w_in, mu_shift, decay_w0, decay_up, iclr_a0, iclr_up, gate_up,
           k_k, k_a, r_k, ln_x_w, ln_x_b, w_out, norm_ffn_g, w_ffn_up, w_ffn_down, norm_final_g):
    depth = w_in.shape[0]
    assert depth == 1
    batch, seq, _ = x_prompt.shape
    n_seq, n_pages = page_table.shape
    slopes = jnp.exp2(-8.0 * jnp.arange(1, N_HEADS + 1, dtype=F32) / N_HEADS)

    l = 0
    row = lambda t: t.reshape(1, -1)
    w_in_16 = w_in[l].astype(BF16)
    w_in_b = jnp.concatenate([w_in_16[:, :ATTN_COLS], _to_internal(w_in_16[:, ATTN_COLS:])], axis=1)
    g_mix = row(norm_mix_g[l])
    rw_weights = (row(_to_internal(mu_shift[l])), row(decay_w0[l]), decay_up[l], row(iclr_a0[l]), iclr_up[l],
                  gate_up[l], row(k_k[l]), row(k_a[l]), row(r_k[l]), row(ln_x_w[l]), row(ln_x_b[l]))
    wo = w_out[l].astype(BF16)
    ffn_weights = (wo[:WIDTH], wo[WIDTH:], row(norm_ffn_g[l]), w_ffn_up[l].astype(BF16),
                   w_ffn_down[l].astype(BF16), row(norm_final_g))

    xp = x_prompt.reshape(batch * seq, D_MODEL)
    kt, vt, qt, ka, vtb, prw, kmean = _inproj_prompt(xp, g_mix, w_in_b, slopes, batch, seq)
    nb = seq // MOBA_BLOCK
    attn_p = _moba_prompt(slopes, qt, ka, vtb, kmean.reshape(batch, nb, WIDTH), batch, seq)
    rw_p, wkv_p = _rwkv_prompt(prw, rw_weights, batch, seq)
    ckt = cache_k.transpose(0, 1, 3, 4, 2).reshape(-1, N_HEADS, HEAD_DIM, PAGE_SIZE)
    cvt = cache_v.transpose(0, 1, 3, 4, 2).reshape(-1, N_HEADS, HEAD_DIM, PAGE_SIZE)
    y_prompt, kmean_s = _out_ffn(xp, attn_p, rw_p, *ffn_weights, tm=1024, th=512, stream=(page_table, ckt))
    y_prompt = y_prompt.reshape(batch, seq, D_MODEL)
    shift_p = _from_internal(prw.reshape(batch, seq, RWKV_COLS)[:, -1])

    hd = (N_HEADS, HEAD_DIM)
    xs = x_sample.reshape(n_seq, D_MODEL)
    proj_s = _inproj_small(xs, g_mix, w_in_b)
    q_s, k_s, v_s = proj_s[:, :WIDTH], proj_s[:, WIDTH:2 * WIDTH], proj_s[:, 2 * WIDTH:ATTN_COLS]
    prw_s = proj_s[:, ATTN_COLS:]
    top = _sample_gate(q_s, kmean_s)[:, :MOBA_TOP_K, :N_HEADS]
    col = lambda t: t.reshape(n_seq, N_HEADS, HEAD_DIM, 1)
    attn_s = _sample_attn(page_table, top, slopes, col(q_s), col(k_s), col(v_s), ckt, cvt).reshape(n_seq, WIDTH)
    rw_s, wkv_s = _rwkv_sample(prw_s, _to_internal(state_shift[l]), state_wkv[l], rw_weights)
    y_sample = _out_ffn(xs, attn_s.astype(BF16), rw_s, *ffn_weights, tm=n_seq).reshape(n_seq, 1, D_MODEL)
    shift_s = _from_internal(prw_s)

    return (y_prompt, y_sample,
            kt.transpose(0, 3, 1, 2)[None], vt.transpose(0, 3, 1, 2)[None],
            wkv_p[None], shift_p[None],
            k_s.reshape(1, n_seq, 1, *hd), v_s.reshape(1, n_seq, 1, *hd),
            wkv_s[None], shift_s[None])
```

```python
import functools

import jax
import jax.numpy as jnp
from jax import lax
from jax.experimental import pallas as pl
from jax.experimental.pallas import tpu as pltpu

F32 = jnp.float32
BF16 = jnp.bfloat16

D_MODEL = 1024
HEAD_DIM = 64
N_HEADS = 8
WIDTH = N_HEADS * HEAD_DIM
MOBA_BLOCK = 256
MOBA_TOP_K = 3
DECAY_LORA = 64
AAA_LORA = 64
GATE_LORA = 128
ATTN_COLS = 3 * WIDTH
RWKV_COLS = 3 * WIDTH + DECAY_LORA + AAA_LORA + GATE_LORA
RMS_EPS = 1e-6
GN_EPS = 64e-5
NEG_INF = -1e30
PAGE_SIZE = 128
RWKV_CHUNK = 64
VMEM_LIMIT = 48 * 1024 * 1024

_O_R, _O_K, _O_V = 0, WIDTH, 2 * WIDTH
_O_XW = 3 * WIDTH
_O_XA = _O_XW + DECAY_LORA
_O_XG = _O_XA + AAA_LORA


def _to_internal(t):
    r, xw, kv, rest = (t[..., :WIDTH], t[..., WIDTH:WIDTH + DECAY_LORA],
                       t[..., WIDTH + DECAY_LORA:3 * WIDTH + DECAY_LORA], t[..., 3 * WIDTH + DECAY_LORA:])
    return jnp.concatenate([r, kv, xw, rest], axis=-1)


def _from_internal(t):
    r, kv, xw, rest = (t[..., :WIDTH], t[..., WIDTH:3 * WIDTH],
                       t[..., 3 * WIDTH:3 * WIDTH + DECAY_LORA], t[..., 3 * WIDTH + DECAY_LORA:])
    return jnp.concatenate([r, xw, kv, rest], axis=-1)


def _params(*sem):
    return pltpu.CompilerParams(dimension_semantics=sem, vmem_limit_bytes=VMEM_LIMIT)


def _rms(x, g):
    return x * lax.rsqrt(jnp.mean(x * x, axis=-1, keepdims=True) + RMS_EPS) * g


def _dot(a, b, **kw):
    return jnp.dot(a, b, preferred_element_type=F32, **kw)


def _dot_nt(a, b, **kw):
    return lax.dot_general(a, b, (((1,), (1,)), ((), ())), preferred_element_type=F32, **kw)


def _dot_tn(a, b, **kw):
    return lax.dot_general(a, b, (((0,), (0,)), ((), ())), preferred_element_type=F32, **kw)


MOBA_VROWS = HEAD_DIM + 16
INPROJ_BLOCKS = 2


def _inproj_kernel(x_ref, g_ref, w_ref, srow_ref, kt_ref, vt_ref, qt_ref, ka_ref, vtb_ref, prw_ref, km_ref):
    tm = x_ref.shape[0]
    hd = (N_HEADS, HEAD_DIM, tm)
    xn = _rms(x_ref[...], g_ref[...]).astype(BF16)
    proj = _dot(xn, w_ref[...])
    q = proj[:, 0:WIDTH] * (HEAD_DIM ** -0.5)
    k = proj[:, WIDTH:2 * WIDTH]
    v = proj[:, 2 * WIDTH:3 * WIDTH]
    prw_ref[...] = proj[:, ATTN_COLS:]
    nblk = tm // MOBA_BLOCK
    km_ref[0] = jnp.mean(k.reshape(nblk, MOBA_BLOCK, WIDTH), axis=1)
    k_t = k.T.reshape(hd)
    v_t = v.T.reshape(hd)
    kt_ref[0] = k_t
    vt_ref[0] = v_t
    ones_row = lax.broadcasted_iota(jnp.int32, (N_HEADS, MOBA_VROWS - HEAD_DIM, MOBA_BLOCK), 1) == 0
    ones_rows = jnp.where(ones_row, 1.0, 0.0).astype(BF16)
    v_tb = v_t.astype(BF16)
    for blk in range(nblk):
        vtb_ref[0, :, blk, 0:HEAD_DIM, :] = v_tb[:, :, blk * MOBA_BLOCK:(blk + 1) * MOBA_BLOCK]
        vtb_ref[0, :, blk, HEAD_DIM:, :] = ones_rows
    qt_ref[0, :, 0:HEAD_DIM, :] = q.T.reshape(hd).astype(BF16)
    qt_ref[0, :, HEAD_DIM:, :] = jnp.broadcast_to(srow_ref[...], hd).astype(BF16)
    lane = lax.broadcasted_iota(jnp.int32, (tm, 128 - HEAD_DIM), 1)
    pos = lax.broadcasted_iota(jnp.int32, (tm, 128 - HEAD_DIM), 0)
    pos_cols = jnp.where(lane == 0, pos % MOBA_BLOCK, 0).astype(F32).astype(BF16)
    for h in range(N_HEADS):
        ka_ref[0, h, :, 0:HEAD_DIM] = k[:, h * HEAD_DIM:(h + 1) * HEAD_DIM].astype(BF16)
        ka_ref[0, h, :, HEAD_DIM:] = pos_cols


def _inproj_prompt(x, g, w_bf16, slopes, batch, seq):
    tm = INPROJ_BLOCKS * MOBA_BLOCK
    m = x.shape[0]
    nb = seq // tm
    nblk = INPROJ_BLOCKS
    ncol = w_bf16.shape[1]
    row = lambda i: (i, 0)
    const = lambda i: (0, 0)
    tok = lambda i: (i // nb, 0, 0, i % nb)
    srow = jnp.zeros((N_HEADS, HEAD_DIM, 1), F32).at[:, 0, 0].set(slopes)
    return pl.pallas_call(
        _inproj_kernel,
        grid=(m // tm,),
        in_specs=[pl.BlockSpec((tm, D_MODEL), row),
                  pl.BlockSpec((1, D_MODEL), const),
                  pl.BlockSpec((D_MODEL, ncol), const),
                  pl.BlockSpec((N_HEADS, HEAD_DIM, 1), lambda i: (0, 0, 0))],
        out_specs=[pl.BlockSpec((1, N_HEADS, HEAD_DIM, tm), tok),
                   pl.BlockSpec((1, N_HEADS, HEAD_DIM, tm), tok),
                   pl.BlockSpec((1, N_HEADS, 128, tm), tok),
                   pl.BlockSpec((1, N_HEADS, tm, 128), lambda i: (i // nb, 0, i % nb, 0)),
                   pl.BlockSpec((1, N_HEADS, nblk, MOBA_VROWS, MOBA_BLOCK), lambda i: (i // nb, 0, i % nb, 0, 0)),
                   pl.BlockSpec((tm, RWKV_COLS), row),
                   pl.BlockSpec((1, nblk, WIDTH), lambda i: (i, 0, 0))],
        out_shape=[jax.ShapeDtypeStruct((batch, N_HEADS, HEAD_DIM, seq), F32),
                   jax.ShapeDtypeStruct((batch, N_HEADS, HEAD_DIM, seq), F32),
                   jax.ShapeDtypeStruct((batch, N_HEADS, 128, seq), BF16),
                   jax.ShapeDtypeStruct((batch, N_HEADS, seq, 128), BF16),
                   jax.ShapeDtypeStruct((batch, N_HEADS, seq // MOBA_BLOCK, MOBA_VROWS, MOBA_BLOCK), BF16),
                   jax.ShapeDtypeStruct((m, RWKV_COLS), F32),
                   jax.ShapeDtypeStruct((m // tm, nblk, WIDTH), F32)],
        compiler_params=_params("parallel"),
        name="inproj_prompt",
    )(x, g, w_bf16, srow)


def _inproj_small_kernel(x_ref, g_ref, w_ref, o_ref):
    xn = _rms(x_ref[...], g_ref[...]).astype(BF16)
    o_ref[...] = _dot(xn, w_ref[...])


def _inproj_small(x, g, w, tn=256):
    m = x.shape[0]
    ncol = w.shape[1]
    return pl.pallas_call(
        _inproj_small_kernel,
        grid=(ncol // tn,),
        in_specs=[pl.BlockSpec((m, D_MODEL), lambda j: (0, 0)),
                  pl.BlockSpec((1, D_MODEL), lambda j: (0, 0)),
                  pl.BlockSpec((D_MODEL, tn), lambda j: (0, j))],
        out_specs=pl.BlockSpec((m, tn), lambda j: (0, j)),
        out_shape=jax.ShapeDtypeStruct((m, ncol), F32),
        compiler_params=_params("parallel"),
        name="inproj_sample",
    )(x, g, w)


_STREAM_PAGES = 16
_PAGES_PER_BLOCK = MOBA_BLOCK // PAGE_SIZE


def _ffn_kernel(pt_ref, x_ref, attn_ref, rw_ref, woa_ref, wor_ref, gf_ref, wup_ref, wdn_ref, gfin_ref, *rest,
                n_stream):
    page_refs = rest[:n_stream]
    y_ref = rest[n_stream]
    km_ref = rest[n_stream + 1] if n_stream else None
    h_sc, hn_sc, acc_sc = rest[-3:]
    j = pl.program_id(1)

    @pl.when(j == 0)
    def _():
        h = x_ref[...] + _dot(attn_ref[...], woa_ref[...]) + _dot(rw_ref[...], wor_ref[...])
        h_sc[...] = h
        hn_sc[...] = _rms(h, gf_ref[...]).astype(BF16)
        acc_sc[...] = jnp.zeros_like(acc_sc)

    u = jnp.maximum(_dot(hn_sc[...], wup_ref[...]), 0.0)
    acc_sc[...] += _dot((u * u).astype(BF16), wdn_ref[...])

    if n_stream:
        ppb = _PAGES_PER_BLOCK
        bps = n_stream // ppb
        nb = km_ref.shape[1]
        g = (pl.program_id(0) * pl.num_programs(1) + j) % (nb // bps)
        for b in range(bps):
            tot = page_refs[b * ppb][0]
            for i in range(1, ppb):
                tot = tot + page_refs[b * ppb + i][0]
            tot_t = tot.reshape(WIDTH, PAGE_SIZE).T
            km_ref[0, pl.ds(g * bps + b, 1), :] = jnp.sum(tot_t, axis=0, keepdims=True) * (1.0 / MOBA_BLOCK)

    @pl.when(j == pl.num_programs(1) - 1)
    def _():
        y_ref[...] = _rms(h_sc[...] + acc_sc[...], gfin_ref[...])


def _out_ffn(x, attn, rw, woa, wor, gf, wup, wdn, gfin, tm, th=1024, stream=None):
    m = x.shape[0]
    hid = wup.shape[1]
    nj = hid // th
    row = lambda i, j, pt: (i, 0)
    const = lambda i, j, pt: (0, 0)
    in_specs = [pl.BlockSpec((tm, D_MODEL), row),
                pl.BlockSpec((tm, WIDTH), row),
                pl.BlockSpec((tm, WIDTH), row),
                pl.BlockSpec((WIDTH, D_MODEL), const),
                pl.BlockSpec((WIDTH, D_MODEL), const),
                pl.BlockSpec((1, D_MODEL), const),
                pl.BlockSpec((D_MODEL, th), lambda i, j, pt: (0, j)),
                pl.BlockSpec((th, D_MODEL), lambda i, j, pt: (j, 0)),
                pl.BlockSpec((1, D_MODEL), const)]
    out_specs = [pl.BlockSpec((tm, D_MODEL), row)]
    out_shape = [jax.ShapeDtypeStruct((m, D_MODEL), F32)]
    operands = [x, attn, rw, woa, wor, gf, wup, wdn, gfin]
    n_stream = 0
    pt_flat = jnp.zeros((1,), jnp.int32)
    if stream is not None:
        page_table, cache_kt = stream
        n_seq, n_pages = page_table.shape
        n_stream = _STREAM_PAGES
        spp = n_pages // n_stream
        assert (m // tm) * nj == n_seq * spp, "page streaming needs one grid step per 16 pages"
        nb = n_pages // _PAGES_PER_BLOCK
        pt_flat = page_table.reshape(-1)

        def page_spec(k):
            def imap(i, j, pt, k=k):
                t = i * nj + j
                return (pt[(t // spp) * n_pages + (t % spp) * n_stream + k], 0, 0, 0)
            return pl.BlockSpec((1, N_HEADS, HEAD_DIM, PAGE_SIZE), imap)

        in_specs += [page_spec(k) for k in range(n_stream)]
        operands += [cache_kt] * n_stream
        out_specs.append(pl.BlockSpec((1, nb, WIDTH), lambda i, j, pt: ((i * nj + j) // spp, 0, 0)))
        out_shape.append(jax.ShapeDtypeStruct((n_seq, nb, WIDTH), F32))
    grid_spec = pltpu.PrefetchScalarGridSpec(
        num_scalar_prefetch=1,
        grid=(m // tm, nj),
        in_specs=in_specs,
        out_specs=out_specs,
        scratch_shapes=[pltpu.VMEM((tm, D_MODEL), F32),
                        pltpu.VMEM((tm, D_MODEL), BF16),
                        pltpu.VMEM((tm, D_MODEL), F32)],
    )
    outs = pl.pallas_call(
        functools.partial(_ffn_kernel, n_stream=n_stream),
        grid_spec=grid_spec,
        out_shape=out_shape,
        compiler_params=_params("arbitrary", "arbitrary"),
        name="out_ffn",
    )(pt_flat, *operands)
    return outs if stream is not None else outs[0]


def _block_rank(gm, axis):
    nb = gm.shape[axis]
    idx = lax.broadcasted_iota(jnp.int32, gm.shape, axis)
    beats = []
    for m in range(nb):
        gmm = lax.slice_in_dim(gm, m, m + 1, axis=axis)
        beats.append(((gmm > gm) | ((gmm == gm) & (m < idx))).astype(jnp.int32))
    while len(beats) > 1:
        beats = [a + b for a, b in zip(beats[0::2], beats[1::2])] + ([beats[-1]] if len(beats) % 2 else [])
    return beats[0]


MOBA_HEADS = 8
MOBA_UNROLL = 4
BIG = 1e30


def _moba_kernel(slopes_ref, qt_ref, ka_ref, vt_ref, km_ref, o_ref, sel_sc, m_sc, acc_sc, *, nb):
    hg = pl.program_id(1)
    qi = pl.program_id(2)
    blk = MOBA_BLOCK
    n_top = min(MOBA_TOP_K, nb)
    keyi = lax.broadcasted_iota(jnp.int32, (blk, blk), 0)
    qryi = lax.broadcasted_iota(jnp.int32, (blk, blk), 1)
    causal = keyi <= qryi
    bidx = lax.broadcasted_iota(jnp.int32, (nb, blk), 0)
    q0 = pl.multiple_of(qi * blk, blk)

    hs = range(MOBA_HEADS)
    qts = [qt_ref[0, hh] for hh in hs]
    gates = [_dot(km_ref[0, hh], qts[hh]) for hh in hs]
    valid = bidx < qi
    ranks = [_block_rank(jnp.where(valid, g, NEG_INF), 0) for g in gates]
    for hh in hs:
        sel_sc[hh] = ((ranks[hh] < n_top) & valid).astype(F32)
    s0 = [jnp.where(causal, _dot(ka_ref[0, hh, pl.ds(q0, blk), :], qts[hh]), NEG_INF) for hh in hs]
    m0 = [jnp.max(t, axis=0, keepdims=True) for t in s0]
    p0 = [jnp.exp(s0[hh] - m0[hh]) for hh in hs]
    for hh in hs:
        m_sc[hh] = m0[hh]
    pv0 = [_dot(vt_ref[0, hh, qi], p0[hh].astype(BF16)) for hh in hs]
    for hh in hs:
        acc_sc[hh] = pv0[hh]

    def past_block(j, carry):
        k0 = pl.multiple_of(j * blk, blk)
        s = [_dot(ka_ref[0, hh, pl.ds(k0, blk), :], qt_ref[0, hh]) for hh in hs]
        ps, alphas = [], []
        for hh in hs:
            cj = -slopes_ref[hg * MOBA_HEADS + hh] * ((qi - j) * blk).astype(F32)
            picked = sel_sc[hh, pl.ds(j, 1), :] > 0.0
            m_old = m_sc[hh]
            m_new = jnp.maximum(m_old, jnp.where(picked, jnp.max(s[hh], axis=0, keepdims=True) + cj, NEG_INF))
            alpha = jnp.exp(m_old - m_new)
            p = jnp.exp(s[hh] - jnp.where(picked, m_new - cj, BIG))
            m_sc[hh] = m_new
            ps.append(p.astype(BF16))
            alphas.append(alpha)
        pv = [_dot(vt_ref[0, hh, j], ps[hh]) for hh in hs]
        for hh in hs:
            acc_sc[hh] = alphas[hh] * acc_sc[hh] + pv[hh]
        return carry

    def unrolled_body(jj, carry):
        for t in range(MOBA_UNROLL):
            past_block(MOBA_UNROLL * jj + t, carry)
        return carry

    lax.fori_loop(0, qi // MOBA_UNROLL, unrolled_body, 0)
    lax.fori_loop((qi // MOBA_UNROLL) * MOBA_UNROLL, qi, past_block, 0)

    outs = [acc_sc[hh, 0:HEAD_DIM, :] / acc_sc[hh, HEAD_DIM:HEAD_DIM + 1, :] for hh in hs]
    o_ref[...] = jnp.concatenate([t.T for t in outs], axis=1).astype(o_ref.dtype)


def _moba_prompt(slopes, qt, ka, vt, kmean, batch, seq):
    nb = seq // MOBA_BLOCK
    blk = MOBA_BLOCK
    km = kmean.reshape(batch, nb, N_HEADS, HEAD_DIM).transpose(0, 2, 1, 3).astype(BF16)
    km = jnp.concatenate([km, jnp.zeros((batch, N_HEADS, nb, 128 - HEAD_DIM), BF16)], axis=3)

    hgn = MOBA_HEADS
    grid_spec = pltpu.PrefetchScalarGridSpec(
        num_scalar_prefetch=1,
        grid=(batch, N_HEADS // hgn, nb),
        in_specs=[pl.BlockSpec((1, hgn, 128, blk), lambda b, g, qi, s: (b, g, 0, qi)),
                  pl.BlockSpec((1, hgn, seq, 128), lambda b, g, qi, s: (b, g, 0, 0)),
                  pl.BlockSpec((1, hgn, nb, MOBA_VROWS, blk), lambda b, g, qi, s: (b, g, 0, 0, 0)),
                  pl.BlockSpec((1, hgn, nb, 128), lambda b, g, qi, s: (b, g, 0, 0))],
        out_specs=pl.BlockSpec((blk, hgn * HEAD_DIM), lambda b, g, qi, s: (b * nb + qi, g)),
        scratch_shapes=[pltpu.VMEM((hgn, nb, blk), F32),
                        pltpu.VMEM((hgn, 1, blk), F32),
                        pltpu.VMEM((hgn, MOBA_VROWS, blk), F32)],
    )
    return pl.pallas_call(
        functools.partial(_moba_kernel, nb=nb),
        grid_spec=grid_spec,
        out_shape=jax.ShapeDtypeStruct((batch * seq, WIDTH), BF16),
        compiler_params=_params("parallel", "parallel", "arbitrary"),
        name="moba_prompt",
    )(slopes, qt, ka, vt, km)


def _b(t):
    return t.astype(BF16)


def _split3(x):
    x1 = _b(x)
    r1 = x - x1.astype(F32)
    x2 = _b(r1)
    return x1, x2, _b(r1 - x2.astype(F32))


def _rwkv_pointwise(p, pprev, mu, w0, decay_up, a0, iclr_up, gate_up, k_k, k_a):
    xs = p + mu * (pprev - p)
    r = xs[:, _O_R:_O_R + WIDTH]
    k = xs[:, _O_K:_O_K + WIDTH]
    v = xs[:, _O_V:_O_V + WIDTH]
    xw = xs[:, _O_XW:_O_XW + DECAY_LORA]
    xa = xs[:, _O_XA:_O_XA + AAA_LORA]
    xg = xs[:, _O_XG:_O_XG + GATE_LORA]
    w = w0 + _dot(_b(jnp.tanh(xw)), _b(decay_up))
    w = -jax.nn.softplus(-w) - 0.5
    logdecay = -jnp.exp(w)
    a = jax.nn.sigmoid(a0 + _dot(_b(xa), _b(iclr_up)))
    g = _dot(_b(jax.nn.sigmoid(xg)), _b(gate_up))
    kk = k * k_k
    k2 = k * (1.0 + (a - 1.0) * k_a)
    return r, k2, v, kk, a, g, logdecay


def _head_sum(x):
    row = lax.broadcasted_iota(jnp.int32, (128, 128), 0)
    col = lax.broadcasted_iota(jnp.int32, (128, 128), 1)
    seg = ((row // HEAD_DIM) == (col // HEAD_DIM)).astype(BF16)
    xb = _b(x)
    cols = [slice(g * 128, (g + 1) * 128) for g in range(x.shape[1] // 128)]
    return jnp.concatenate([_dot(xb[:, c], seg) for c in cols], axis=1)


def _head_norm(kk_h):
    return kk_h * lax.rsqrt(jnp.maximum(jnp.sum(kk_h * kk_h, axis=-1, keepdims=True), 1e-24))


def _group_norm_out(y_h, r_h, k_h, v_h, g_h, rk_h, lnw_h, lnb_h):
    mean = jnp.mean(y_h, axis=-1, keepdims=True)
    var = jnp.mean(jnp.square(y_h - mean), axis=-1, keepdims=True)
    yn = (y_h - mean) * lax.rsqrt(var + GN_EPS) * lnw_h + lnb_h
    yn = yn + jnp.sum(r_h * k_h * rk_h, axis=-1, keepdims=True) * v_h
    return yn * g_h


def _unit_lower_inverse(mats):
    n = mats[0].shape[0]
    row = lax.broadcasted_iota(jnp.int32, (n, n), 0)
    col = lax.broadcasted_iota(jnp.int32, (n, n), 1)
    eye = (row == col).astype(F32)
    size = 16
    same = (row // size) == (col // size)
    pws = [jnp.where(same, a, 0.0) for a in mats]
    xs = [eye - pw for pw in pws]
    for _ in range(3):
        pwb = [_b(pw) for pw in pws]
        pws = [_dot(t, t) for t in pwb]
        xs = [x + _dot(_b(x), _b(pw)) for x, pw in zip(xs, pws)]
        yield
    while size < n:
        size2 = size * 2
        same2 = (row // size2) == (col // size2)
        keep = same2 & jnp.logical_not(same)
        xb = [_b(x) for x in xs]
        ox = [_b(_dot(_b(jnp.where(keep, a, 0.0)), t)) for a, t in zip(mats, xb)]
        xs = [x - _dot(t, o) for x, t, o in zip(xs, xb, ox)]
        same = same2
        size = size2
        yield
    return xs


def _cumsum_rows(x, seg):
    n = x.shape[0]
    row = lax.broadcasted_iota(jnp.int32, (n, n), 0)
    col = lax.broadcasted_iota(jnp.int32, (n, n), 1)
    tri = ((row >= col) & ((row // seg) == (col // seg))).astype(BF16)
    x1, x2, x3 = _split3(x)
    return _dot(tri, x1) + _dot(tri, x2) + _dot(tri, x3)


RWKV_STEP_CHUNKS = 4
_RWKV_BLOCK = RWKV_CHUNK * RWKV_STEP_CHUNKS
_KAPH, _RHAT, _KHAT, _KBAR, _BHAT, _BBAR, _VB = range(7)
_RHAT32, _BONUS, _GATE = range(3)


def _interleave(*gens):
    live = list(gens)
    while live:
        for gen in list(live):
            try:
                next(gen)
            except StopIteration:
                live.remove(gen)


def _rwkv_pointwise_stage(p, last_sc, w, ob_sc, of_sc, we_sc, keep):
    mu, w0, dup, a0, iup, gup, k_k, k_a, r_k = w
    L = RWKV_CHUNK
    nck = RWKV_STEP_CHUNKS
    ts = _RWKV_BLOCK
    rowi = lax.broadcasted_iota(jnp.int32, p.shape, 0)
    carry = last_sc[0:1, :] if keep is None else last_sc[0:1, :] * keep
    pprev = jnp.where(rowi == 0, carry, pltpu.roll(p, 1, 0))
    last_sc[0:1, :] = p[ts - 1:ts, :]
    yield
    r, k2, v, kk, a, g, logdecay = _rwkv_pointwise(p, pprev, mu, w0, dup, a0, iup, gup, k_k, k_a)
    yield
    cum = _cumsum_rows(logdecay, L)
    ends = [cum[c * L + L - 1:c * L + L, :] for c in range(nck)]
    cum_end = jnp.concatenate([jnp.broadcast_to(e, (L, WIDTH)) for e in ends], axis=0)
    w_inc = jnp.exp(cum)
    w_exc = jnp.exp(cum - logdecay)
    w_inv = jnp.exp(-cum)
    w_tail = jnp.exp(cum_end - cum)
    for c in range(nck):
        we_sc[c:c + 1, :] = jnp.exp(ends[c])
    yield
    kap_all = kk * lax.rsqrt(jnp.maximum(_head_sum(kk * kk), 1e-24))
    bb_all = kap_all * a
    r_hat_all = r * w_inc
    ob_sc[_KAPH] = _b(kap_all * w_exc)
    ob_sc[_RHAT] = _b(r_hat_all)
    ob_sc[_KHAT] = _b(k2 * w_inv)
    ob_sc[_KBAR] = _b(k2 * w_tail)
    yield
    ob_sc[_BHAT] = _b(bb_all * w_inv)
    ob_sc[_BBAR] = _b(bb_all * w_tail)
    ob_sc[_VB] = _b(v)
    of_sc[_RHAT32] = r_hat_all
    of_sc[_BONUS] = _head_sum(r * k2 * r_k) * v
    of_sc[_GATE] = g


def _rwkv_matmul_stage(ob_sc, of_sc, we_sc, s_sc, y_sc, lnw, lnb, o_ref, rows_out, keep):
    L = RWKV_CHUNK
    nck = RWKV_STEP_CHUNKS
    trow = lax.broadcasted_iota(jnp.int32, (L, L), 0)
    tcol = lax.broadcasted_iota(jnp.int32, (L, L), 1)
    lower_incl = trow >= tcol
    lower_strict = trow > tcol
    heads = range(N_HEADS)
    items = [(slice(c * L, (c + 1) * L), slice(h * HEAD_DIM, (h + 1) * HEAD_DIM))
             for c in range(nck) for h in heads]
    n = range(len(items))
    ld = lambda slot, it: ob_sc[slot, it[0], it[1]]
    lhs = [jnp.concatenate([ld(_KAPH, it), ld(_RHAT, it)], axis=0) for it in items]
    ak = [_dot_nt(lhs[i], ld(_KHAT, items[i])) for i in n]
    ab = [_dot_nt(lhs[i], ld(_BHAT, items[i])) for i in n]
    yield
    a_kr = [_b(jnp.concatenate([jnp.where(lower_strict, t[:L], 0.0), jnp.where(lower_incl, t[L:], 0.0)], axis=0))
            for t in ak]
    a_rb = [_b(jnp.where(lower_incl, t[L:], 0.0)) for t in ab]
    t_inv = yield from _unit_lower_inverse([jnp.where(lower_strict, t[:L], 0.0) for t in ab])
    t_inv = [_b(t) for t in t_inv]
    av = [_dot(a_kr[i], ld(_VB, items[i])) for i in n]
    pm = [_b(_dot(t_inv[i], ld(_KAPH, items[i]))) for i in n]
    yield
    qm = [_b(_dot(t_inv[i], _b(av[i][:L]))) for i in n]
    r_eff = [_b(of_sc[_RHAT32, items[i][0], items[i][1]] - _dot(a_rb[i], pm[i])) for i in n]
    ptb = [_b(_dot_tn(pm[i], ld(_BBAR, items[i]))) for i in n]
    yield
    y0 = [av[i][L:] - _dot(a_rb[i], qm[i]) for i in n]
    cm = [_dot_tn(ld(_VB, items[i]), ld(_KBAR, items[i])) - _dot_tn(qm[i], ld(_BBAR, items[i])) for i in n]
    yield
    state = [s_sc[h] if keep is None else s_sc[h] * keep for h in heads]
    for c in range(nck):
        w_end = we_sc[c:c + 1, :]
        sb = [_b(t) for t in state]
        ys = [_dot_nt(r_eff[c * N_HEADS + h], sb[h]) + y0[c * N_HEADS + h] for h in heads]
        state = [state[h] * w_end[:, items[h][1]] - _dot(sb[h], ptb[c * N_HEADS + h]) + cm[c * N_HEADS + h]
                 for h in heads]
        for h in heads:
            y_sc[items[c * N_HEADS + h]] = ys[h]
    for h in heads:
        s_sc[h] = state[h]
    yield
    y = y_sc[...]
    dev = y - _head_sum(y) * (1.0 / HEAD_DIM)
    var = _head_sum(dev * dev) * (1.0 / HEAD_DIM)
    yn = dev * lax.rsqrt(var + GN_EPS) * lnw + lnb
    o_ref[rows_out, :] = ((yn + of_sc[_BONUS]) * of_sc[_GATE]).astype(o_ref.dtype)


def _rwkv_chunk_kernel(p_ref, mu_ref, w0_ref, dup_ref, a0_ref, iup_ref, gup_ref, kk_ref, ka_ref,
                       rk_ref, lnw_ref, lnb_ref, o_ref, s_out_ref,
                       s_sc, last_sc, y_sc, xb_sc, xf_sc, xw_sc, yb_sc, yf_sc, yw_sc, *, seq_blocks):
    step = pl.program_id(0)
    ts = _RWKV_BLOCK

    @pl.when(step == 0)
    def _():
        s_sc[...] = jnp.zeros_like(s_sc)
        last_sc[...] = jnp.zeros_like(last_sc)
        yb_sc[...] = jnp.zeros_like(yb_sc)
        yf_sc[...] = jnp.zeros_like(yf_sc)
        yw_sc[...] = jnp.zeros_like(yw_sc)

    keep = jnp.where((2 * step) % seq_blocks == 0, 0.0, 1.0).astype(F32)
    w = (mu_ref[...], w0_ref[...], dup_ref[...], a0_ref[...], iup_ref[...], gup_ref[...],
         kk_ref[...], ka_ref[...], rk_ref[...])
    lnw, lnb = lnw_ref[...], lnb_ref[...]
    _interleave(_rwkv_matmul_stage(yb_sc, yf_sc, yw_sc, s_sc, y_sc, lnw, lnb, o_ref, slice(0, ts), None),
                _rwkv_pointwise_stage(p_ref[0:ts, :], last_sc, w, xb_sc, xf_sc, xw_sc, keep))
    s_out_ref[0] = s_sc[...]
    _interleave(_rwkv_matmul_stage(xb_sc, xf_sc, xw_sc, s_sc, y_sc, lnw, lnb, o_ref, slice(ts, 2 * ts), keep),
                _rwkv_pointwise_stage(p_ref[ts:2 * ts, :], last_sc, w, yb_sc, yf_sc, yw_sc, None))


def _rwkv_prompt(p_rw, weights, batch, seq):
    ts = _RWKV_BLOCK
    m = batch * seq
    seq_blocks = seq // ts
    assert seq_blocks % 2 == 0
    ns = m // (2 * ts)
    const = lambda c: (0, 0)
    w_specs = [pl.BlockSpec(w.shape, const) for w in weights]
    operand_scratch = [pltpu.VMEM((7, ts, WIDTH), BF16), pltpu.VMEM((3, ts, WIDTH), F32), pltpu.VMEM((8, WIDTH), F32)]
    out, state = pl.pallas_call(
        functools.partial(_rwkv_chunk_kernel, seq_blocks=seq_blocks),
        grid=(ns + 1,),
        in_specs=[pl.BlockSpec((2 * ts, RWKV_COLS), lambda c: (jnp.minimum(c, ns - 1), 0))] + w_specs,
        out_specs=[pl.BlockSpec((2 * ts, WIDTH), lambda c: (c, 0)),
                   pl.BlockSpec((1, N_HEADS, HEAD_DIM, HEAD_DIM),
                                lambda c: (jnp.maximum(2 * c - 1, 0) // seq_blocks, 0, 0, 0))],
        out_shape=[jax.ShapeDtypeStruct((m + 2 * ts, WIDTH), BF16),
                   jax.ShapeDtypeStruct((batch, N_HEADS, HEAD_DIM, HEAD_DIM), F32)],
        scratch_shapes=[pltpu.VMEM((N_HEADS, HEAD_DIM, HEAD_DIM), F32),
                        pltpu.VMEM((8, RWKV_COLS), F32),
                        pltpu.VMEM((ts, WIDTH), F32)] + operand_scratch + operand_scratch,
        compiler_params=_params("arbitrary"),
        name="rwkv_prompt",
    )(p_rw, *weights)
    return out[ts:ts + m], state


def _rwkv_step_kernel(p_ref, sh_ref, s_ref, mu_ref, w0_ref, dup_ref, a0_ref, iup_ref, gup_ref, kk_ref,
                      ka_ref, rk_ref, lnw_ref, lnb_ref, o_ref, s_out_ref):
    p = jnp.broadcast_to(p_ref[0], (8, RWKV_COLS))
    pprev = jnp.broadcast_to(sh_ref[0], (8, RWKV_COLS))
    r, k2, v, kk, a, g, logdecay = (t[0:1] for t in _rwkv_pointwise(
        p, pprev, mu_ref[...], w0_ref[...], dup_ref[...], a0_ref[...], iup_ref[...], gup_ref[...],
        kk_ref[...], ka_ref[...]))
    decay = jnp.exp(logdecay)
    n = HEAD_DIM
    eye = lax.broadcasted_iota(jnp.int32, (n, n), 0) == lax.broadcasted_iota(jnp.int32, (n, n), 1)

    def to_col(row_vec):
        return jnp.sum(jnp.where(eye, row_vec, 0.0), axis=-1, keepdims=True)

    def to_row(col_vec):
        return jnp.sum(jnp.where(eye, col_vec, 0.0), axis=0, keepdims=True)

    for h in range(N_HEADS):
        ln = slice(h * n, (h + 1) * n)
        s0 = s_ref[0, h]
        kap = _head_norm(kk[:, ln])
        sa = jnp.sum(s0 * (-kap), axis=-1, keepdims=True)
        s_new = s0 * decay[:, ln] + sa * (kap * a[:, ln]) + to_col(v[:, ln]) * k2[:, ln]
        s_out_ref[0, h] = s_new
        y = to_row(jnp.sum(s_new * r[:, ln], axis=-1, keepdims=True))
        o_ref[0, :, ln] = _group_norm_out(y, r[:, ln], k2[:, ln], v[:, ln], g[:, ln], rk_ref[:, ln],
                                          lnw_ref[:, ln], lnb_ref[:, ln]).astype(o_ref.dtype)


def _rwkv_sample(p_rw, shift, state, weights):
    n = p_rw.shape[0]
    const = lambda s: (0, 0)
    vec = pl.BlockSpec((1, 1, RWKV_COLS), lambda s: (s, 0, 0))
    st = pl.BlockSpec((1, N_HEADS, HEAD_DIM, HEAD_DIM), lambda s: (s, 0, 0, 0))
    rw, s_new = pl.pallas_call(
        _rwkv_step_kernel,
        grid=(n,),
        in_specs=[vec, vec, st] + [pl.BlockSpec(w.shape, const) for w in weights],
        out_specs=[pl.BlockSpec((1, 1, WIDTH), lambda s: (s, 0, 0)), st],
        out_shape=[jax.ShapeDtypeStruct((n, 1, WIDTH), BF16),
                   jax.ShapeDtypeStruct(state.shape, F32)],
        compiler_params=_params("parallel"),
        name="rwkv_sample",
    )(p_rw.reshape(n, 1, RWKV_COLS), shift.reshape(n, 1, RWKV_COLS), state, *weights)
    return rw.reshape(n, WIDTH), s_new


def _sample_gate_kernel(q_ref, km_ref, idx_ref):
    rnd = lambda t: t.astype(BF16).astype(F32)
    prod = rnd(km_ref[0]) * rnd(q_ref[0])
    lane = lax.broadcasted_iota(jnp.int32, (WIDTH, 128), 0)
    hcol = lax.broadcasted_iota(jnp.int32, (WIDTH, 128), 1)
    head_sum = ((lane // HEAD_DIM) == hcol).astype(BF16)
    pieces = _split3(prod)
    gate = _dot(pieces[0], head_sum) + _dot(pieces[1], head_sum) + _dot(pieces[2], head_sum)
    rank = _block_rank(gate, 0)
    bidx = lax.broadcasted_iota(jnp.int32, gate.shape, 0)
    rows = [jnp.sum(jnp.where(rank == i, bidx, 0), axis=0, keepdims=True) for i in range(MOBA_TOP_K)]
    rows += [jnp.zeros((1, 128), jnp.int32)] * (8 - MOBA_TOP_K)
    idx_ref[0] = jnp.concatenate(rows, axis=0)


def _sample_gate(q, kmean):
    n, nb, _ = kmean.shape
    return pl.pallas_call(
        _sample_gate_kernel,
        grid=(n,),
        in_specs=[pl.BlockSpec((1, 1, WIDTH), lambda s: (s, 0, 0)),
                  pl.BlockSpec((1, nb, WIDTH), lambda s: (s, 0, 0))],
        out_specs=pl.BlockSpec((1, 8, 128), lambda s: (s, 0, 0)),
        out_shape=jax.ShapeDtypeStruct((n, 8, 128), jnp.int32),
        compiler_params=_params("parallel"),
        name="sample_gate",
    )(q.reshape(n, 1, WIDTH), kmean)


def _sample_attn_kernel(pt_ref, top_ref, q_ref, kn_ref, vn_ref, topv_ref, slope_ref, ck_hbm, cv_hbm, o_ref,
                        kbuf, vbuf, sems, *, past_len, n_pages):
    ppb = _PAGES_PER_BLOCK
    npg = MOBA_TOP_K * ppb
    seq = pl.program_id(0)
    nseq = pl.num_programs(0)
    slot = seq % 2

    def slab_copies(sq, sl, h, i):
        blk_id = top_ref[(sq * N_HEADS + h) * MOBA_TOP_K + i // ppb]
        page = pt_ref[sq * n_pages + blk_id * ppb + i % ppb]
        return (pltpu.make_async_copy(ck_hbm.at[page, h], kbuf.at[sl, h, i], sems.at[sl]),
                pltpu.make_async_copy(cv_hbm.at[page, h], vbuf.at[sl, h, i], sems.at[sl]))

    def start_all(sq, sl):
        def per_head(h, carry):
            for i in range(npg):
                for cp in slab_copies(sq, sl, h, i):
                    cp.start()
            return carry
        lax.fori_loop(0, N_HEADS, per_head, 0)

    @pl.when(seq == 0)
    def _():
        start_all(0, 0)

    @pl.when(seq + 1 < nseq)
    def _():
        start_all(seq + 1, 1 - slot)

    def wait_head(h, carry):
        for i in range(npg):
            for cp in slab_copies(seq, slot, h, i):
                cp.wait()
        return carry
    lax.fori_loop(0, N_HEADS, wait_head, 0)

    q = q_ref[0] * (HEAD_DIM ** -0.5)
    slope = slope_ref[...]
    lane = lax.broadcasted_iota(jnp.int32, (1, 1, PAGE_SIZE), 2)
    scores = []
    for i in range(npg):
        blk_id = topv_ref[0, i // ppb][:, :, None]
        dist = (past_len - blk_id * MOBA_BLOCK - (i % ppb) * PAGE_SIZE - lane).astype(F32)
        s = jnp.sum(kbuf[slot, :, i] * q, axis=1, keepdims=True)
        scores.append(s - slope * dist)
    s_self = jnp.sum(q * kn_ref[0], axis=1, keepdims=True)
    m = s_self
    for s in scores:
        m = jnp.maximum(m, jnp.max(s, axis=2, keepdims=True))
    p_self = jnp.exp(s_self - m)
    den = p_self
    accv = jnp.zeros((N_HEADS, HEAD_DIM, PAGE_SIZE), F32)
    for i, s in enumerate(scores):
        p = jnp.exp(s - m)
        den = den + jnp.sum(p, axis=2, keepdims=True)
        accv = accv + vbuf[slot, :, i] * p
    acc = jnp.sum(accv, axis=2, keepdims=True) + p_self * vn_ref[0]
    o_ref[0] = acc / den


def _sample_attn(page_table, top, slopes, q, k_new, v_new, cache_kt, cache_vt):
    n, n_pages = page_table.shape
    npg = MOBA_TOP_K * _PAGES_PER_BLOCK
    top_flat = top.transpose(0, 2, 1).reshape(-1)
    vec = pl.BlockSpec((1, N_HEADS, HEAD_DIM, 1), lambda s, pt, tp: (s, 0, 0, 0))
    grid_spec = pltpu.PrefetchScalarGridSpec(
        num_scalar_prefetch=2,
        grid=(n,),
        in_specs=[vec, vec, vec,
                  pl.BlockSpec((1, MOBA_TOP_K, N_HEADS, 1), lambda s, pt, tp: (s, 0, 0, 0)),
                  pl.BlockSpec((N_HEADS, 1, 1), lambda s, pt, tp: (0, 0, 0)),
                  pl.BlockSpec(memory_space=pl.ANY),
                  pl.BlockSpec(memory_space=pl.ANY)],
        out_specs=vec,
        scratch_shapes=[pltpu.VMEM((2, N_HEADS, npg, HEAD_DIM, PAGE_SIZE), F32),
                        pltpu.VMEM((2, N_HEADS, npg, HEAD_DIM, PAGE_SIZE), F32),
                        pltpu.SemaphoreType.DMA((2,))],
    )
    return pl.pallas_call(
        functools.partial(_sample_attn_kernel, past_len=n_pages * PAGE_SIZE, n_pages=n_pages),
        grid_spec=grid_spec,
        out_shape=jax.ShapeDtypeStruct((n, N_HEADS, HEAD_DIM, 1), F32),
        compiler_params=_params("arbitrary"),
        name="sample_attn",
    )(page_table.reshape(-1), top_flat, q, k_new, v_new, top[..., None], slopes.reshape(N_HEADS, 1, 1),
      cache_kt, cache_vt)


def kernel(x_prompt, x_sample, cache_k, cache_v, page_table, state_wkv, state_shift,
           norm_mix_g, w_in, mu_shift, decay_w0, decay_up, iclr_a0, iclr_up, gate_up,
           k_k, k_a, r_k, ln_x_w, ln_x_b, w_out, norm_ffn_g, w_ffn_up, w_ffn_down, norm_final_g):
    depth = w_in.shape[0]
    assert depth == 1
    batch, seq, _ = x_prompt.shape
    n_seq, n_pages = page_table.shape
    slopes = jnp.exp2(-8.0 * jnp.arange(1, N_HEADS + 1, dtype=F32) / N_HEADS)

    l = 0
    row = lambda t: t.reshape(1, -1)
    w_in_16 = w_in[l].astype(BF16)
    w_in_b = jnp.concatenate([w_in_16[:, :ATTN_COLS], _to_internal(w_in_16[:, ATTN_COLS:])], axis=1)
    g_mix = row(norm_mix_g[l])
    rw_weights = (row(_to_internal(mu_shift[l])), row(decay_w0[l]), decay_up[l], row(iclr_a0[l]), iclr_up[l],
                  gate_up[l], row(k_k[l]), row(k_a[l]), row(r_k[l]), row(ln_x_w[l]), row(ln_x_b[l]))
    wo = w_out[l].astype(BF16)
    ffn_weights = (wo[:WIDTH], wo[WIDTH:], row(norm_ffn_g[l]), w_ffn_up[l].astype(BF16),
                   w_ffn_down[l].astype(BF16), row(norm_final_g))

    xp = x_prompt.reshape(batch * seq, D_MODEL)
    kt, vt, qt, ka, vtb, prw, kmean = _inproj_prompt(xp, g_mix, w_in_b, slopes, batch, seq)
    nb = seq // MOBA_BLOCK
    attn_p = _moba_prompt(slopes, qt, ka, vtb, kmean.reshape(batch, nb, WIDTH), batch, seq)
    rw_p, wkv_p = _rwkv_prompt(prw, rw_weights, batch, seq)
    ckt = cache_k.transpose(0, 1, 3, 4, 2).reshape(-1, N_HEADS, HEAD_DIM, PAGE_SIZE)
    cvt = cache_v.transpose(0, 1, 3, 4, 2).reshape(-1, N_HEADS, HEAD_DIM, PAGE_SIZE)
    y_prompt, kmean_s = _out_ffn(xp, attn_p, rw_p, *ffn_weights, tm=1024, th=512, stream=(page_table, ckt))
    y_prompt = y_prompt.reshape(batch, seq, D_MODEL)
    shift_p = _from_internal(prw.reshape(batch, seq, RWKV_COLS)[:, -1])

    hd = (N_HEADS, HEAD_DIM)
    xs = x_sample.reshape(n_seq, D_MODEL)
    proj_s = _inproj_small(xs, g_mix, w_in_b)
    q_s, k_s, v_s = proj_s[:, :WIDTH], proj_s[:, WIDTH:2 * WIDTH], proj_s[:, 2 * WIDTH:ATTN_COLS]
    prw_s = proj_s[:, ATTN_COLS:]
    top = _sample_gate(q_s, kmean_s)[:, :MOBA_TOP_K, :N_HEADS]
    col = lambda t: t.reshape(n_seq, N_HEADS, HEAD_DIM, 1)
    attn_s = _sample_attn(page_table, top, slopes, col(q_s), col(k_s), col(v_s), ckt, cvt).reshape(n_seq, WIDTH)
    rw_s, wkv_s = _rwkv_sample(prw_s, _to_internal(state_shift[l]), state_wkv[l], rw_weights)
    y_sample = _out_ffn(xs, attn_s.astype(BF16), rw_s, *ffn_weights, tm=n_seq).reshape(n_seq, 1, D_MODEL)
    shift_s = _from_internal(prw_s)

    return (y_prompt, y_sample,
            kt.transpose(0, 3, 1, 2)[None], vt.transpose(0, 3, 1, 2)[None],
            wkv_p[None], shift_p[None],
            k_s.reshape(1, n_seq, 1, *hd), v_s.reshape(1, n_seq, 1, *hd),
            wkv_s[None], shift_s[None])
```

```python
import functools

import jax
import jax.numpy as jnp
from jax import lax
from jax.experimental import pallas as pl
from jax.experimental.pallas import tpu as pltpu

F32 = jnp.float32
BF16 = jnp.bfloat16

D_MODEL = 1024
HEAD_DIM = 64
N_HEADS = 8
WIDTH = N_HEADS * HEAD_DIM
MOBA_BLOCK = 256
MOBA_TOP_K = 3
DECAY_LORA = 64
AAA_LORA = 64
GATE_LORA = 128
ATTN_COLS = 3 * WIDTH
RWKV_COLS = 3 * WIDTH + DECAY_LORA + AAA_LORA + GATE_LORA
RMS_EPS = 1e-6
GN_EPS = 64e-5
NEG_INF = -1e30
PAGE_SIZE = 128
RWKV_CHUNK = 64
VMEM_LIMIT = 48 * 1024 * 1024

_O_R, _O_K, _O_V = 0, WIDTH, 2 * WIDTH
_O_XW = 3 * WIDTH
_O_XA = _O_XW + DECAY_LORA
_O_XG = _O_XA + AAA_LORA


def _to_internal(t):
    r, xw, kv, rest = (t[..., :WIDTH], t[..., WIDTH:WIDTH + DECAY_LORA],
                       t[..., WIDTH + DECAY_LORA:3 * WIDTH + DECAY_LORA], t[..., 3 * WIDTH + DECAY_LORA:])
    return jnp.concatenate([r, kv, xw, rest], axis=-1)


def _from_internal(t):
    r, kv, xw, rest = (t[..., :WIDTH], t[..., WIDTH:3 * WIDTH],
                       t[..., 3 * WIDTH:3 * WIDTH + DECAY_LORA], t[..., 3 * WIDTH + DECAY_LORA:])
    return jnp.concatenate([r, xw, kv, rest], axis=-1)


def _params(*sem):
    return pltpu.CompilerParams(dimension_semantics=sem, vmem_limit_bytes=VMEM_LIMIT)


def _rms(x, g):
    return x * lax.rsqrt(jnp.mean(x * x, axis=-1, keepdims=True) + RMS_EPS) * g


def _dot(a, b, **kw):
    return jnp.dot(a, b, preferred_element_type=F32, **kw)


def _dot_nt(a, b, **kw):
    return lax.dot_general(a, b, (((1,), (1,)), ((), ())), preferred_element_type=F32, **kw)


def _dot_tn(a, b, **kw):
    return lax.dot_general(a, b, (((0,), (0,)), ((), ())), preferred_element_type=F32, **kw)


MOBA_VROWS = HEAD_DIM + 16
INPROJ_BLOCKS = 2


def _inproj_kernel(x_ref, g_ref, w_ref, srow_ref, kt_ref, vt_ref, qt_ref, ka_ref, vtb_ref, prw_ref, km_ref):
    tm = x_ref.shape[0]
    hd = (N_HEADS, HEAD_DIM, tm)
    xn = _rms(x_ref[...], g_ref[...]).astype(BF16)
    proj = _dot(xn, w_ref[...])
    q = proj[:, 0:WIDTH] * (HEAD_DIM ** -0.5)
    k = proj[:, WIDTH:2 * WIDTH]
    v = proj[:, 2 * WIDTH:3 * WIDTH]
    prw_ref[...] = proj[:, ATTN_COLS:]
    nblk = tm // MOBA_BLOCK
    km_ref[0] = jnp.mean(k.reshape(nblk, MOBA_BLOCK, WIDTH), axis=1)
    k_t = k.T.reshape(hd)
    v_t = v.T.reshape(hd)
    kt_ref[0] = k_t
    vt_ref[0] = v_t
    ones_row = lax.broadcasted_iota(jnp.int32, (N_HEADS, MOBA_VROWS - HEAD_DIM, MOBA_BLOCK), 1) == 0
    ones_rows = jnp.where(ones_row, 1.0, 0.0).astype(BF16)
    v_tb = v_t.astype(BF16)
    for blk in range(nblk):
        vtb_ref[0, :, blk, 0:HEAD_DIM, :] = v_tb[:, :, blk * MOBA_BLOCK:(blk + 1) * MOBA_BLOCK]
        vtb_ref[0, :, blk, HEAD_DIM:, :] = ones_rows
    qt_ref[0, :, 0:HEAD_DIM, :] = q.T.reshape(hd).astype(BF16)
    qt_ref[0, :, HEAD_DIM:, :] = jnp.broadcast_to(srow_ref[...], hd).astype(BF16)
    lane = lax.broadcasted_iota(jnp.int32, (tm, 128 - HEAD_DIM), 1)
    pos = lax.broadcasted_iota(jnp.int32, (tm, 128 - HEAD_DIM), 0)
    pos_cols = jnp.where(lane == 0, pos % MOBA_BLOCK, 0).astype(F32).astype(BF16)
    for h in range(N_HEADS):
        ka_ref[0, h, :, 0:HEAD_DIM] = k[:, h * HEAD_DIM:(h + 1) * HEAD_DIM].astype(BF16)
        ka_ref[0, h, :, HEAD_DIM:] = pos_cols


def _inproj_prompt(x, g, w_bf16, slopes, batch, seq):
    tm = INPROJ_BLOCKS * MOBA_BLOCK
    m = x.shape[0]
    nb = seq // tm
    nblk = INPROJ_BLOCKS
    ncol = w_bf16.shape[1]
    row = lambda i: (i, 0)
    const = lambda i: (0, 0)
    tok = lambda i: (i // nb, 0, 0, i % nb)
    srow = jnp.zeros((N_HEADS, HEAD_DIM, 1), F32).at[:, 0, 0].set(slopes)
    return pl.pallas_call(
        _inproj_kernel,
        grid=(m // tm,),
        in_specs=[pl.BlockSpec((tm, D_MODEL), row),
                  pl.BlockSpec((1, D_MODEL), const),
                  pl.BlockSpec((D_MODEL, ncol), const),
                  pl.BlockSpec((N_HEADS, HEAD_DIM, 1), lambda i: (0, 0, 0))],
        out_specs=[pl.BlockSpec((1, N_HEADS, HEAD_DIM, tm), tok),
                   pl.BlockSpec((1, N_HEADS, HEAD_DIM, tm), tok),
                   pl.BlockSpec((1, N_HEADS, 128, tm), tok),
                   pl.BlockSpec((1, N_HEADS, tm, 128), lambda i: (i // nb, 0, i % nb, 0)),
                   pl.BlockSpec((1, N_HEADS, nblk, MOBA_VROWS, MOBA_BLOCK), lambda i: (i // nb, 0, i % nb, 0, 0)),
                   pl.BlockSpec((tm, RWKV_COLS), row),
                   pl.BlockSpec((1, nblk, WIDTH), lambda i: (i, 0, 0))],
        out_shape=[jax.ShapeDtypeStruct((batch, N_HEADS, HEAD_DIM, seq), F32),
                   jax.ShapeDtypeStruct((batch, N_HEADS, HEAD_DIM, seq), F32),
                   jax.ShapeDtypeStruct((batch, N_HEADS, 128, seq), BF16),
                   jax.ShapeDtypeStruct((batch, N_HEADS, seq, 128), BF16),
                   jax.ShapeDtypeStruct((batch, N_HEADS, seq // MOBA_BLOCK, MOBA_VROWS, MOBA_BLOCK), BF16),
                   jax.ShapeDtypeStruct((m, RWKV_COLS), F32),
                   jax.ShapeDtypeStruct((m // tm, nblk, WIDTH), F32)],
        compiler_params=_params("parallel"),
        name="inproj_prompt",
    )(x, g, w_bf16, srow)


def _inproj_small_kernel(x_ref, g_ref, w_ref, o_ref):
    xn = _rms(x_ref[...], g_ref[...]).astype(BF16)
    o_ref[...] = _dot(xn, w_ref[...])


def _inproj_small(x, g, w, tn=256):
    m = x.shape[0]
    ncol = w.shape[1]
    return pl.pallas_call(
        _inproj_small_kernel,
        grid=(ncol // tn,),
        in_specs=[pl.BlockSpec((m, D_MODEL), lambda j: (0, 0)),
                  pl.BlockSpec((1, D_MODEL), lambda j: (0, 0)),
                  pl.BlockSpec((D_MODEL, tn), lambda j: (0, j))],
        out_specs=pl.BlockSpec((m, tn), lambda j: (0, j)),
        out_shape=jax.ShapeDtypeStruct((m, ncol), F32),
        compiler_params=_params("parallel"),
        name="inproj_sample",
    )(x, g, w)


_STREAM_PAGES = 16
_PAGES_PER_BLOCK = MOBA_BLOCK // PAGE_SIZE


def _ffn_kernel(pt_ref, x_ref, attn_ref, rw_ref, woa_ref, wor_ref, gf_ref, wup_ref, wdn_ref, gfin_ref, *rest,
                n_stream):
    page_refs = rest[:n_stream]
    y_ref = rest[n_stream]
    km_ref = rest[n_stream + 1] if n_stream else None
    h_sc, hn_sc, acc_sc = rest[-3:]
    j = pl.program_id(1)

    @pl.when(j == 0)
    def _():
        h = x_ref[...] + _dot(attn_ref[...], woa_ref[...]) + _dot(rw_ref[...], wor_ref[...])
        h_sc[...] = h
        hn_sc[...] = _rms(h, gf_ref[...]).astype(BF16)
        acc_sc[...] = jnp.zeros_like(acc_sc)

    u = jnp.maximum(_dot(hn_sc[...], wup_ref[...]), 0.0)
    acc_sc[...] += _dot((u * u).astype(BF16), wdn_ref[...])

    if n_stream:
        ppb = _PAGES_PER_BLOCK
        bps = n_stream // ppb
        nb = km_ref.shape[1]
        g = (pl.program_id(0) * pl.num_programs(1) + j) % (nb // bps)
        for b in range(bps):
            tot = page_refs[b * ppb][0]
            for i in range(1, ppb):
                tot = tot + page_refs[b * ppb + i][0]
            tot_t = tot.reshape(WIDTH, PAGE_SIZE).T
            km_ref[0, pl.ds(g * bps + b, 1), :] = jnp.sum(tot_t, axis=0, keepdims=True) * (1.0 / MOBA_BLOCK)

    @pl.when(j == pl.num_programs(1) - 1)
    def _():
        y_ref[...] = _rms(h_sc[...] + acc_sc[...], gfin_ref[...])


def _out_ffn(x, attn, rw, woa, wor, gf, wup, wdn, gfin, tm, th=1024, stream=None):
    m = x.shape[0]
    hid = wup.shape[1]
    nj = hid // th
    row = lambda i, j, pt: (i, 0)
    const = lambda i, j, pt: (0, 0)
    in_specs = [pl.BlockSpec((tm, D_MODEL), row),
                pl.BlockSpec((tm, WIDTH), row),
                pl.BlockSpec((tm, WIDTH), row),
                pl.BlockSpec((WIDTH, D_MODEL), const),
                pl.BlockSpec((WIDTH, D_MODEL), const),
                pl.BlockSpec((1, D_MODEL), const),
                pl.BlockSpec((D_MODEL, th), lambda i, j, pt: (0, j)),
                pl.BlockSpec((th, D_MODEL), lambda i, j, pt: (j, 0)),
                pl.BlockSpec((1, D_MODEL), const)]
    out_specs = [pl.BlockSpec((tm, D_MODEL), row)]
    out_shape = [jax.ShapeDtypeStruct((m, D_MODEL), F32)]
    operands = [x, attn, rw, woa, wor, gf, wup, wdn, gfin]
    n_stream = 0
    pt_flat = jnp.zeros((1,), jnp.int32)
    if stream is not None:
        page_table, cache_kt = stream
        n_seq, n_pages = page_table.shape
        n_stream = _STREAM_PAGES
        spp = n_pages // n_stream
        assert (m // tm) * nj == n_seq * spp, "page streaming needs one grid step per 16 pages"
        nb = n_pages // _PAGES_PER_BLOCK
        pt_flat = page_table.reshape(-1)

        def page_spec(k):
            def imap(i, j, pt, k=k):
                t = i * nj + j
                return (pt[(t // spp) * n_pages + (t % spp) * n_stream + k], 0, 0, 0)
            return pl.BlockSpec((1, N_HEADS, HEAD_DIM, PAGE_SIZE), imap)

        in_specs += [page_spec(k) for k in range(n_stream)]
        operands += [cache_kt] * n_stream
        out_specs.append(pl.BlockSpec((1, nb, WIDTH), lambda i, j, pt: ((i * nj + j) // spp, 0, 0)))
        out_shape.append(jax.ShapeDtypeStruct((n_seq, nb, WIDTH), F32))
    grid_spec = pltpu.PrefetchScalarGridSpec(
        num_scalar_prefetch=1,
        grid=(m // tm, nj),
        in_specs=in_specs,
        out_specs=out_specs,
        scratch_shapes=[pltpu.VMEM((tm, D_MODEL), F32),
                        pltpu.VMEM((tm, D_MODEL), BF16),
                        pltpu.VMEM((tm, D_MODEL), F32)],
    )
    outs = pl.pallas_call(
        functools.partial(_ffn_kernel, n_stream=n_stream),
        grid_spec=grid_spec,
        out_shape=out_shape,
        compiler_params=_params("arbitrary", "arbitrary"),
        name="out_ffn",
    )(pt_flat, *operands)
    return outs if stream is not None else outs[0]


def _block_rank(gm, axis):
    nb = gm.shape[axis]
    idx = lax.broadcasted_iota(jnp.int32, gm.shape, axis)
    beats = []
    for m in range(nb):
        gmm = lax.slice_in_dim(gm, m, m + 1, axis=axis)
        beats.append(((gmm > gm) | ((gmm == gm) & (m < idx))).astype(jnp.int32))
    while len(beats) > 1:
        beats = [a + b for a, b in zip(beats[0::2], beats[1::2])] + ([beats[-1]] if len(beats) % 2 else [])
    return beats[0]


MOBA_HEADS = 8
MOBA_UNROLL = 4
BIG = 1e30


def _moba_kernel(slopes_ref, qt_ref, ka_ref, vt_ref, km_ref, o_ref, sel_sc, m_sc, acc_sc, *, nb):
    hg = pl.program_id(1)
    qi = pl.program_id(2)
    blk = MOBA_BLOCK
    n_top = min(MOBA_TOP_K, nb)
    keyi = lax.broadcasted_iota(jnp.int32, (blk, blk), 0)
    qryi = lax.broadcasted_iota(jnp.int32, (blk, blk), 1)
    causal = keyi <= qryi
    bidx = lax.broadcasted_iota(jnp.int32, (nb, blk), 0)
    q0 = pl.multiple_of(qi * blk, blk)

    hs = range(MOBA_HEADS)
    qts = [qt_ref[0, hh] for hh in hs]
    gates = [_dot(km_ref[0, hh], qts[hh]) for hh in hs]
    valid = bidx < qi
    ranks = [_block_rank(jnp.where(valid, g, NEG_INF), 0) for g in gates]
    for hh in hs:
        sel_sc[hh] = ((ranks[hh] < n_top) & valid).astype(F32)
    s0 = [jnp.where(causal, _dot(ka_ref[0, hh, pl.ds(q0, blk), :], qts[hh]), NEG_INF) for hh in hs]
    m0 = [jnp.max(t, axis=0, keepdims=True) for t in s0]
    p0 = [jnp.exp(s0[hh] - m0[hh]) for hh in hs]
    for hh in hs:
        m_sc[hh] = m0[hh]
    pv0 = [_dot(vt_ref[0, hh, qi], p0[hh].astype(BF16)) for hh in hs]
    for hh in hs:
        acc_sc[hh] = pv0[hh]

    def past_block(j, carry):
        k0 = pl.multiple_of(j * blk, blk)
        s = [_dot(ka_ref[0, hh, pl.ds(k0, blk), :], qt_ref[0, hh]) for hh in hs]
        ps, alphas = [], []
        for hh in hs:
            cj = -slopes_ref[hg * MOBA_HEADS + hh] * ((qi - j) * blk).astype(F32)
            picked = sel_sc[hh, pl.ds(j, 1), :] > 0.0
            m_old = m_sc[hh]
            m_new = jnp.maximum(m_old, jnp.where(picked, jnp.max(s[hh], axis=0, keepdims=True) + cj, NEG_INF))
            alpha = jnp.exp(m_old - m_new)
            p = jnp.exp(s[hh] - jnp.where(picked, m_new - cj, BIG))
            m_sc[hh] = m_new
            ps.append(p.astype(BF16))
            alphas.append(alpha)
        pv = [_dot(vt_ref[0, hh, j], ps[hh]) for hh in hs]
        for hh in hs:
            acc_sc[hh] = alphas[hh] * acc_sc[hh] + pv[hh]
        return carry

    def unrolled_body(jj, carry):
        for t in range(MOBA_UNROLL):
            past_block(MOBA_UNROLL * jj + t, carry)
        return carry

    lax.fori_loop(0, qi // MOBA_UNROLL, unrolled_body, 0)
    lax.fori_loop((qi // MOBA_UNROLL) * MOBA_UNROLL, qi, past_block, 0)

    outs = [acc_sc[hh, 0:HEAD_DIM, :] / acc_sc[hh, HEAD_DIM:HEAD_DIM + 1, :] for hh in hs]
    o_ref[...] = jnp.concatenate([t.T for t in outs], axis=1).astype(o_ref.dtype)


def _moba_prompt(slopes, qt, ka, vt, kmean, batch, seq):
    nb = seq // MOBA_BLOCK
    blk = MOBA_BLOCK
    km = kmean.reshape(batch, nb, N_HEADS, HEAD_DIM).transpose(0, 2, 1, 3).astype(BF16)
    km = jnp.concatenate([km, jnp.zeros((batch, N_HEADS, nb, 128 - HEAD_DIM), BF16)], axis=3)

    hgn = MOBA_HEADS
    grid_spec = pltpu.PrefetchScalarGridSpec(
        num_scalar_prefetch=1,
        grid=(batch, N_HEADS // hgn, nb),
        in_specs=[pl.BlockSpec((1, hgn, 128, blk), lambda b, g, qi, s: (b, g, 0, qi)),
                  pl.BlockSpec((1, hgn, seq, 128), lambda b, g, qi, s: (b, g, 0, 0)),
                  pl.BlockSpec((1, hgn, nb, MOBA_VROWS, blk), lambda b, g, qi, s: (b, g, 0, 0, 0)),
                  pl.BlockSpec((1, hgn, nb, 128), lambda b, g, qi, s: (b, g, 0, 0))],
        out_specs=pl.BlockSpec((blk, hgn * HEAD_DIM), lambda b, g, qi, s: (b * nb + qi, g)),
        scratch_shapes=[pltpu.VMEM((hgn, nb, blk), F32),
                        pltpu.VMEM((hgn, 1, blk), F32),
                        pltpu.VMEM((hgn, MOBA_VROWS, blk), F32)],
    )
    return pl.pallas_call(
        functools.partial(_moba_kernel, nb=nb),
        grid_spec=grid_spec,
        out_shape=jax.ShapeDtypeStruct((batch * seq, WIDTH), BF16),
        compiler_params=_params("parallel", "parallel", "arbitrary"),
        name="moba_prompt",
    )(slopes, qt, ka, vt, km)


def _b(t):
    return t.astype(BF16)


def _split3(x):
    x1 = _b(x)
    r1 = x - x1.astype(F32)
    x2 = _b(r1)
    return x1, x2, _b(r1 - x2.astype(F32))


def _rwkv_pointwise(p, pprev, mu, w0, decay_up, a0, iclr_up, gate_up, k_k, k_a):
    xs = p + mu * (pprev - p)
    r = xs[:, _O_R:_O_R + WIDTH]
    k = xs[:, _O_K:_O_K + WIDTH]
    v = xs[:, _O_V:_O_V + WIDTH]
    xw = xs[:, _O_XW:_O_XW + DECAY_LORA]
    xa = xs[:, _O_XA:_O_XA + AAA_LORA]
    xg = xs[:, _O_XG:_O_XG + GATE_LORA]
    w = w0 + _dot(_b(jnp.tanh(xw)), _b(decay_up))
    w = -jax.nn.softplus(-w) - 0.5
    logdecay = -jnp.exp(w)
    a = jax.nn.sigmoid(a0 + _dot(_b(xa), _b(iclr_up)))
    g = _dot(_b(jax.nn.sigmoid(xg)), _b(gate_up))
    kk = k * k_k
    k2 = k * (1.0 + (a - 1.0) * k_a)
    return r, k2, v, kk, a, g, logdecay


def _head_sum(x):
    row = lax.broadcasted_iota(jnp.int32, (128, 128), 0)
    col = lax.broadcasted_iota(jnp.int32, (128, 128), 1)
    seg = ((row // HEAD_DIM) == (col // HEAD_DIM)).astype(BF16)
    xb = _b(x)
    cols = [slice(g * 128, (g + 1) * 128) for g in range(x.shape[1] // 128)]
    return jnp.concatenate([_dot(xb[:, c], seg) for c in cols], axis=1)


def _head_norm(kk_h):
    return kk_h * lax.rsqrt(jnp.maximum(jnp.sum(kk_h * kk_h, axis=-1, keepdims=True), 1e-24))


def _group_norm_out(y_h, r_h, k_h, v_h, g_h, rk_h, lnw_h, lnb_h):
    mean = jnp.mean(y_h, axis=-1, keepdims=True)
    var = jnp.mean(jnp.square(y_h - mean), axis=-1, keepdims=True)
    yn = (y_h - mean) * lax.rsqrt(var + GN_EPS) * lnw_h + lnb_h
    yn = yn + jnp.sum(r_h * k_h * rk_h, axis=-1, keepdims=True) * v_h
    return yn * g_h


def _unit_lower_inverse(mats):
    n = mats[0].shape[0]
    row = lax.broadcasted_iota(jnp.int32, (n, n), 0)
    col = lax.broadcasted_iota(jnp.int32, (n, n), 1)
    eye = (row == col).astype(F32)
    size = 16
    same = (row // size) == (col // size)
    pws = [jnp.where(same, a, 0.0) for a in mats]
    xs = [eye - pw for pw in pws]
    for _ in range(3):
        pwb = [_b(pw) for pw in pws]
        pws = [_dot(t, t) for t in pwb]
        xs = [x + _dot(_b(x), _b(pw)) for x, pw in zip(xs, pws)]
        yield
    while size < n:
        size2 = size * 2
        same2 = (row // size2) == (col // size2)
        keep = same2 & jnp.logical_not(same)
        xb = [_b(x) for x in xs]
        ox = [_b(_dot(_b(jnp.where(keep, a, 0.0)), t)) for a, t in zip(mats, xb)]
        xs = [x - _dot(t, o) for x, t, o in zip(xs, xb, ox)]
        same = same2
        size = size2
        yield
    return xs


def _cumsum_rows(x, seg):
    n = x.shape[0]
    row = lax.broadcasted_iota(jnp.int32, (n, n), 0)
    col = lax.broadcasted_iota(jnp.int32, (n, n), 1)
    tri = ((row >= col) & ((row // seg) == (col // seg))).astype(BF16)
    x1, x2, x3 = _split3(x)
    return _dot(tri, x1) + _dot(tri, x2) + _dot(tri, x3)


RWKV_STEP_CHUNKS = 4
_RWKV_BLOCK = RWKV_CHUNK * RWKV_STEP_CHUNKS
_KAPH, _RHAT, _KHAT, _KBAR, _BHAT, _BBAR, _VB = range(7)
_RHAT32, _BONUS, _GATE = range(3)


def _interleave(*gens):
    live = list(gens)
    while live:
        for gen in list(live):
            try:
                next(gen)
            except StopIteration:
                live.remove(gen)


def _rwkv_pointwise_stage(p, last_sc, w, ob_sc, of_sc, we_sc, keep):
    mu, w0, dup, a0, iup, gup, k_k, k_a, r_k = w
    L = RWKV_CHUNK
    nck = RWKV_STEP_CHUNKS
    ts = _RWKV_BLOCK
    rowi = lax.broadcasted_iota(jnp.int32, p.shape, 0)
    carry = last_sc[0:1, :] if keep is None else last_sc[0:1, :] * keep
    pprev = jnp.where(rowi == 0, carry, pltpu.roll(p, 1, 0))
    last_sc[0:1, :] = p[ts - 1:ts, :]
    yield
    r, k2, v, kk, a, g, logdecay = _rwkv_pointwise(p, pprev, mu, w0, dup, a0, iup, gup, k_k, k_a)
    yield
    cum = _cumsum_rows(logdecay, L)
    ends = [cum[c * L + L - 1:c * L + L, :] for c in range(nck)]
    cum_end = jnp.concatenate([jnp.broadcast_to(e, (L, WIDTH)) for e in ends], axis=0)
    w_inc = jnp.exp(cum)
    w_exc = jnp.exp(cum - logdecay)
    w_inv = jnp.exp(-cum)
    w_tail = jnp.exp(cum_end - cum)
    for c in range(nck):
        we_sc[c:c + 1, :] = jnp.exp(ends[c])
    yield
    kap_all = kk * lax.rsqrt(jnp.maximum(_head_sum(kk * kk), 1e-24))
    bb_all = kap_all * a
    r_hat_all = r * w_inc
    ob_sc[_KAPH] = _b(kap_all * w_exc)
    ob_sc[_RHAT] = _b(r_hat_all)
    ob_sc[_KHAT] = _b(k2 * w_inv)
    ob_sc[_KBAR] = _b(k2 * w_tail)
    yield
    ob_sc[_BHAT] = _b(bb_all * w_inv)
    ob_sc[_BBAR] = _b(bb_all * w_tail)
    ob_sc[_VB] = _b(v)
    of_sc[_RHAT32] = r_hat_all
    of_sc[_BONUS] = _head_sum(r * k2 * r_k) * v
    of_sc[_GATE] = g


def _rwkv_matmul_stage(ob_sc, of_sc, we_sc, s_sc, y_sc, lnw, lnb, o_ref, rows_out, keep):
    L = RWKV_CHUNK
    nck = RWKV_STEP_CHUNKS
    trow = lax.broadcasted_iota(jnp.int32, (L, L), 0)
    tcol = lax.broadcasted_iota(jnp.int32, (L, L), 1)
    lower_incl = trow >= tcol
    lower_strict = trow > tcol
    heads = range(N_HEADS)
    items = [(slice(c * L, (c + 1) * L), slice(h * HEAD_DIM, (h + 1) * HEAD_DIM))
             for c in range(nck) for h in heads]
    n = range(len(items))
    ld = lambda slot, it: ob_sc[slot, it[0], it[1]]
    lhs = [jnp.concatenate([ld(_KAPH, it), ld(_RHAT, it)], axis=0) for it in items]
    ak = [_dot_nt(lhs[i], ld(_KHAT, items[i])) for i in n]
    ab = [_dot_nt(lhs[i], ld(_BHAT, items[i])) for i in n]
    yield
    a_kr = [_b(jnp.concatenate([jnp.where(lower_strict, t[:L], 0.0), jnp.where(lower_incl, t[L:], 0.0)], axis=0))
            for t in ak]
    a_rb = [_b(jnp.where(lower_incl, t[L:], 0.0)) for t in ab]
    t_inv = yield from _unit_lower_inverse([jnp.where(lower_strict, t[:L], 0.0) for t in ab])
    t_inv = [_b(t) for t in t_inv]
    av = [_dot(a_kr[i], ld(_VB, items[i])) for i in n]
    pm = [_b(_dot(t_inv[i], ld(_KAPH, items[i]))) for i in n]
    yield
    qm = [_b(_dot(t_inv[i], _b(av[i][:L]))) for i in n]
    r_eff = [_b(of_sc[_RHAT32, items[i][0], items[i][1]] - _dot(a_rb[i], pm[i])) for i in n]
    ptb = [_b(_dot_tn(pm[i], ld(_BBAR, items[i]))) for i in n]
    yield
    y0 = [av[i][L:] - _dot(a_rb[i], qm[i]) for i in n]
    cm = [_dot_tn(ld(_VB, items[i]), ld(_KBAR, items[i])) - _dot_tn(qm[i], ld(_BBAR, items[i])) for i in n]
    yield
    state = [s_sc[h] if keep is None else s_sc[h] * keep for h in heads]
    for c in range(nck):
        w_end = we_sc[c:c + 1, :]
        sb = [_b(t) for t in state]
        ys = [_dot_nt(r_eff[c * N_HEADS + h], sb[h]) + y0[c * N_HEADS + h] for h in heads]
        state = [state[h] * w_end[:, items[h][1]] - _dot(sb[h], ptb[c * N_HEADS + h]) + cm[c * N_HEADS + h]
                 for h in heads]
        for h in heads:
            y_sc[items[c * N_HEADS + h]] = ys[h]
    for h in heads:
        s_sc[h] = state[h]
    yield
    y = y_sc[...]
    dev = y - _head_sum(y) * (1.0 / HEAD_DIM)
    var = _head_sum(dev * dev) * (1.0 / HEAD_DIM)
    yn = dev * lax.rsqrt(var + GN_EPS) * lnw + lnb
    o_ref[rows_out, :] = ((yn + of_sc[_BONUS]) * of_sc[_GATE]).astype(o_ref.dtype)


def _rwkv_chunk_kernel(p_ref, mu_ref, w0_ref, dup_ref, a0_ref, iup_ref, gup_ref, kk_ref, ka_ref,
                       rk_ref, lnw_ref, lnb_ref, o_ref, s_out_ref,
                       s_sc, last_sc, y_sc, xb_sc, xf_sc, xw_sc, yb_sc, yf_sc, yw_sc, *, seq_blocks):
    step = pl.program_id(0)
    ts = _RWKV_BLOCK

    @pl.when(step == 0)
    def _():
        s_sc[...] = jnp.zeros_like(s_sc)
        last_sc[...] = jnp.zeros_like(last_sc)
        yb_sc[...] = jnp.zeros_like(yb_sc)
        yf_sc[...] = jnp.zeros_like(yf_sc)
        yw_sc[...] = jnp.zeros_like(yw_sc)

    keep = jnp.where((2 * step) % seq_blocks == 0, 0.0, 1.0).astype(F32)
    w = (mu_ref[...], w0_ref[...], dup_ref[...], a0_ref[...], iup_ref[...], gup_ref[...],
         kk_ref[...], ka_ref[...], rk_ref[...])
    lnw, lnb = lnw_ref[...], lnb_ref[...]
    _interleave(_rwkv_matmul_stage(yb_sc, yf_sc, yw_sc, s_sc, y_sc, lnw, lnb, o_ref, slice(0, ts), None),
                _rwkv_pointwise_stage(p_ref[0:ts, :], last_sc, w, xb_sc, xf_sc, xw_sc, keep))
    s_out_ref[0] = s_sc[...]
    _interleave(_rwkv_matmul_stage(xb_sc, xf_sc, xw_sc, s_sc, y_sc, lnw, lnb, o_ref, slice(ts, 2 * ts), keep),
                _rwkv_pointwise_stage(p_ref[ts:2 * ts, :], last_sc, w, yb_sc, yf_sc, yw_sc, None))


def _rwkv_prompt(p_rw, weights, batch, seq):
    ts = _RWKV_BLOCK
    m = batch * seq
    seq_blocks = seq // ts
    assert seq_blocks % 2 == 0
    ns = m // (2 * ts)
    const = lambda c: (0, 0)
    w_specs = [pl.BlockSpec(w.shape, const) for w in weights]
    operand_scratch = [pltpu.VMEM((7, ts, WIDTH), BF16), pltpu.VMEM((3, ts, WIDTH), F32), pltpu.VMEM((8, WIDTH), F32)]
    out, state = pl.pallas_call(
        functools.partial(_rwkv_chunk_kernel, seq_blocks=seq_blocks),
        grid=(ns + 1,),
        in_specs=[pl.BlockSpec((2 * ts, RWKV_COLS), lambda c: (jnp.minimum(c, ns - 1), 0))] + w_specs,
        out_specs=[pl.BlockSpec((2 * ts, WIDTH), lambda c: (c, 0)),
                   pl.BlockSpec((1, N_HEADS, HEAD_DIM, HEAD_DIM),
                                lambda c: (jnp.maximum(2 * c - 1, 0) // seq_blocks, 0, 0, 0))],
        out_shape=[jax.ShapeDtypeStruct((m + 2 * ts, WIDTH), BF16),
                   jax.ShapeDtypeStruct((batch, N_HEADS, HEAD_DIM, HEAD_DIM), F32)],
        scratch_shapes=[pltpu.VMEM((N_HEADS, HEAD_DIM, HEAD_DIM), F32),
                        pltpu.VMEM((8, RWKV_COLS), F32),
                        pltpu.VMEM((ts, WIDTH), F32)] + operand_scratch + operand_scratch,
        compiler_params=_params("arbitrary"),
        name="rwkv_prompt",
    )(p_rw, *weights)
    return out[ts:ts + m], state


def _rwkv_step_kernel(p_ref, sh_ref, s_ref, mu_ref, w0_ref, dup_ref, a0_ref, iup_ref, gup_ref, kk_ref,
                      ka_ref, rk_ref, lnw_ref, lnb_ref, o_ref, s_out_ref):
    p = jnp.broadcast_to(p_ref[0], (8, RWKV_COLS))
    pprev = jnp.broadcast_to(sh_ref[0], (8, RWKV_COLS))
    r, k2, v, kk, a, g, logdecay = (t[0:1] for t in _rwkv_pointwise(
        p, pprev, mu_ref[...], w0_ref[...], dup_ref[...], a0_ref[...], iup_ref[...], gup_ref[...],
        kk_ref[...], ka_ref[...]))
    decay = jnp.exp(logdecay)
    n = HEAD_DIM
    eye = lax.broadcasted_iota(jnp.int32, (n, n), 0) == lax.broadcasted_iota(jnp.int32, (n, n), 1)

    def to_col(row_vec):
        return jnp.sum(jnp.where(eye, row_vec, 0.0), axis=-1, keepdims=True)

    def to_row(col_vec):
        return jnp.sum(jnp.where(eye, col_vec, 0.0), axis=0, keepdims=True)

    for h in range(N_HEADS):
        ln = slice(h * n, (h + 1) * n)
        s0 = s_ref[0, h]
        kap = _head_norm(kk[:, ln])
        sa = jnp.sum(s0 * (-kap), axis=-1, keepdims=True)
        s_new = s0 * decay[:, ln] + sa * (kap * a[:, ln]) + to_col(v[:, ln]) * k2[:, ln]
        s_out_ref[0, h] = s_new
        y = to_row(jnp.sum(s_new * r[:, ln], axis=-1, keepdims=True))
        o_ref[0, :, ln] = _group_norm_out(y, r[:, ln], k2[:, ln], v[:, ln], g[:, ln], rk_ref[:, ln],
                                          lnw_ref[:, ln], lnb_ref[:, ln]).astype(o_ref.dtype)


def _rwkv_sample(p_rw, shift, state, weights):
    n = p_rw.shape[0]
    const = lambda s: (0, 0)
    vec = pl.BlockSpec((1, 1, RWKV_COLS), lambda s: (s, 0, 0))
    st = pl.BlockSpec((1, N_HEADS, HEAD_DIM, HEAD_DIM), lambda s: (s, 0, 0, 0))
    rw, s_new = pl.pallas_call(
        _rwkv_step_kernel,
        grid=(n,),
        in_specs=[vec, vec, st] + [pl.BlockSpec(w.shape, const) for w in weights],
        out_specs=[pl.BlockSpec((1, 1, WIDTH), lambda s: (s, 0, 0)), st],
        out_shape=[jax.ShapeDtypeStruct((n, 1, WIDTH), BF16),
                   jax.ShapeDtypeStruct(state.shape, F32)],
        compiler_params=_params("parallel"),
        name="rwkv_sample",
    )(p_rw.reshape(n, 1, RWKV_COLS), shift.reshape(n, 1, RWKV_COLS), state, *weights)
    return rw.reshape(n, WIDTH), s_new


def _sample_gate_kernel(q_ref, km_ref, idx_ref):
    rnd = lambda t: t.astype(BF16).astype(F32)
    prod = rnd(km_ref[0]) * rnd(q_ref[0])
    lane = lax.broadcasted_iota(jnp.int32, (WIDTH, 128), 0)
    hcol = lax.broadcasted_iota(jnp.int32, (WIDTH, 128), 1)
    head_sum = ((lane // HEAD_DIM) == hcol).astype(BF16)
    pieces = _split3(prod)
    gate = _dot(pieces[0], head_sum) + _dot(pieces[1], head_sum) + _dot(pieces[2], head_sum)
    rank = _block_rank(gate, 0)
    bidx = lax.broadcasted_iota(jnp.int32, gate.shape, 0)
    rows = [jnp.sum(jnp.where(rank == i, bidx, 0), axis=0, keepdims=True) for i in range(MOBA_TOP_K)]
    rows += [jnp.zeros((1, 128), jnp.int32)] * (8 - MOBA_TOP_K)
    idx_ref[0] = jnp.concatenate(rows, axis=0)


def _sample_gate(q, kmean):
    n, nb, _ = kmean.shape
    return pl.pallas_call(
        _sample_gate_kernel,
        grid=(n,),
        in_specs=[pl.BlockSpec((1, 1, WIDTH), lambda s: (s, 0, 0)),
                  pl.BlockSpec((1, nb, WIDTH), lambda s: (s, 0, 0))],
        out_specs=pl.BlockSpec((1, 8, 128), lambda s: (s, 0, 0)),
        out_shape=jax.ShapeDtypeStruct((n, 8, 128), jnp.int32),
        compiler_params=_params("parallel"),
        name="sample_gate",
    )(q.reshape(n, 1, WIDTH), kmean)


def _sample_attn_kernel(pt_ref, top_ref, q_ref, kn_ref, vn_ref, topv_ref, slope_ref, ck_hbm, cv_hbm, o_ref,
                        kbuf, vbuf, sems, *, past_len, n_pages):
    ppb = _PAGES_PER_BLOCK
    npg = MOBA_TOP_K * ppb
    seq = pl.program_id(0)
    nseq = pl.num_programs(0)
    slot = seq % 2

    def slab_copies(sq, sl, h, i):
        blk_id = top_ref[(sq * N_HEADS + h) * MOBA_TOP_K + i // ppb]
        page = pt_ref[sq * n_pages + blk_id * ppb + i % ppb]
        return (pltpu.make_async_copy(ck_hbm.at[page, h], kbuf.at[sl, h, i], sems.at[sl]),
                pltpu.make_async_copy(cv_hbm.at[page, h], vbuf.at[sl, h, i], sems.at[sl]))

    def start_all(sq, sl):
        def per_head(h, carry):
            for i in range(npg):
                for cp in slab_copies(sq, sl, h, i):
                    cp.start()
            return carry
        lax.fori_loop(0, N_HEADS, per_head, 0)

    @pl.when(seq == 0)
    def _():
        start_all(0, 0)

    @pl.when(seq + 1 < nseq)
    def _():
        start_all(seq + 1, 1 - slot)

    def wait_head(h, carry):
        for i in range(npg):
            for cp in slab_copies(seq, slot, h, i):
                cp.wait()
        return carry
    lax.fori_loop(0, N_HEADS, wait_head, 0)

    heads = range(N_HEADS)
    lns = [slice(h * HEAD_DIM, (h + 1) * HEAD_DIM) for h in heads]
    q_row = q_ref[0] * (HEAD_DIM ** -0.5)
    q8 = [_b(jnp.broadcast_to(q_row[:, ln], (8, HEAD_DIM))) for ln in lns]
    lane = lax.broadcasted_iota(jnp.int32, (1, PAGE_SIZE), 1)
    raw = [[_dot(q8[h], _b(kbuf[slot, h, i]))[0:1] for i in range(npg)] for h in heads]
    probs, dens, accs = [], [], []
    for h in heads:
        slope = slope_ref[h]
        scores = []
        for i in range(npg):
            blk_id = topv_ref[0, i // ppb][h:h + 1, :]
            dist = (past_len - blk_id * MOBA_BLOCK - (i % ppb) * PAGE_SIZE - lane).astype(F32)
            scores.append(raw[h][i] - slope * dist)
        s_self = jnp.sum(q_row[:, lns[h]] * kn_ref[0][:, lns[h]], axis=1, keepdims=True)
        m = s_self
        for sc in scores:
            m = jnp.maximum(m, jnp.max(sc, axis=1, keepdims=True))
        p_self = jnp.exp(s_self - m)
        ps = [jnp.exp(sc - m) for sc in scores]
        den = p_self
        for p in ps:
            den = den + jnp.sum(p, axis=1, keepdims=True)
        probs.append([_b(jnp.broadcast_to(p, (8, PAGE_SIZE))) for p in ps])
        dens.append(den)
        accs.append(p_self * vn_ref[0][:, lns[h]])
    pv = [[_dot_nt(probs[h][i], _b(vbuf[slot, h, i]))[0:1] for i in range(npg)] for h in heads]
    outs = []
    for h in heads:
        acc = accs[h]
        for t in pv[h]:
            acc = acc + t
        outs.append(acc / dens[h])
    o_ref[0] = jnp.concatenate(outs, axis=1)


def _sample_attn(page_table, top, slopes, q, k_new, v_new, cache_kt, cache_vt):
    n, n_pages = page_table.shape
    npg = MOBA_TOP_K * _PAGES_PER_BLOCK
    top_flat = top.transpose(0, 2, 1).reshape(-1)
    vec = pl.BlockSpec((1, 1, WIDTH), lambda s, pt, tp: (s, 0, 0))
    grid_spec = pltpu.PrefetchScalarGridSpec(
        num_scalar_prefetch=2,
        grid=(n,),
        in_specs=[vec, vec, vec,
                  pl.BlockSpec((1, MOBA_TOP_K, N_HEADS, 1), lambda s, pt, tp: (s, 0, 0, 0)),
                  pl.BlockSpec((N_HEADS, 1, 1), lambda s, pt, tp: (0, 0, 0)),
                  pl.BlockSpec(memory_space=pl.ANY),
                  pl.BlockSpec(memory_space=pl.ANY)],
        out_specs=vec,
        scratch_shapes=[pltpu.VMEM((2, N_HEADS, npg, HEAD_DIM, PAGE_SIZE), F32),
                        pltpu.VMEM((2, N_HEADS, npg, HEAD_DIM, PAGE_SIZE), F32),
                        pltpu.SemaphoreType.DMA((2,))],
    )
    row = lambda t: t.reshape(n, 1, WIDTH)
    return pl.pallas_call(
        functools.partial(_sample_attn_kernel, past_len=n_pages * PAGE_SIZE, n_pages=n_pages),
        grid_spec=grid_spec,
        out_shape=jax.ShapeDtypeStruct((n, 1, WIDTH), F32),
        compiler_params=_params("arbitrary"),
        name="sample_attn",
    )(page_table.reshape(-1), top_flat, row(q), row(k_new), row(v_new), top[..., None],
      slopes.reshape(N_HEADS, 1, 1), cache_kt, cache_vt).reshape(n, WIDTH)


def kernel(x_prompt, x_sample, cache_k, cache_v, page_table, state_wkv, state_shift,
           norm_mix_g, w_in, mu_shift, decay_w0, decay_up, iclr_a0, iclr_up, gate_up,
           k_k, k_a, r_k, ln_x_w, ln_x_b, w_out, norm_ffn_g, w_ffn_up, w_ffn_down, norm_final_g):
    depth = w_in.shape[0]
    assert depth == 1
    batch, seq, _ = x_prompt.shape
    n_seq, n_pages = page_table.shape
    slopes = jnp.exp2(-8.0 * jnp.arange(1, N_HEADS + 1, dtype=F32) / N_HEADS)

    l = 0
    row = lambda t: t.reshape(1, -1)
    w_in_16 = w_in[l].astype(BF16)
    w_in_b = jnp.concatenate([w_in_16[:, :ATTN_COLS], _to_internal(w_in_16[:, ATTN_COLS:])], axis=1)
    g_mix = row(norm_mix_g[l])
    rw_weights = (row(_to_internal(mu_shift[l])), row(decay_w0[l]), decay_up[l], row(iclr_a0[l]), iclr_up[l],
                  gate_up[l], row(k_k[l]), row(k_a[l]), row(r_k[l]), row(ln_x_w[l]), row(ln_x_b[l]))
    wo = w_out[l].astype(BF16)
    ffn_weights = (wo[:WIDTH], wo[WIDTH:], row(norm_ffn_g[l]), w_ffn_up[l].astype(BF16),
                   w_ffn_down[l].astype(BF16), row(norm_final_g))

    xp = x_prompt.reshape(batch * seq, D_MODEL)
    kt, vt, qt, ka, vtb, prw, kmean = _inproj_prompt(xp, g_mix, w_in_b, slopes, batch, seq)
    nb = seq // MOBA_BLOCK
    attn_p = _moba_prompt(slopes, qt, ka, vtb, kmean.reshape(batch, nb, WIDTH), batch, seq)
    rw_p, wkv_p = _rwkv_prompt(prw, rw_weights, batch, seq)
    ckt = cache_k.transpose(0, 1, 3, 4, 2).reshape(-1, N_HEADS, HEAD_DIM, PAGE_SIZE)
    cvt = cache_v.transpose(0, 1, 3, 4, 2).reshape(-1, N_HEADS, HEAD_DIM, PAGE_SIZE)
    y_prompt, kmean_s = _out_ffn(xp, attn_p, rw_p, *ffn_weights, tm=1024, th=512, stream=(page_table, ckt))
    y_prompt = y_prompt.reshape(batch, seq, D_MODEL)
    shift_p = _from_internal(prw.reshape(batch, seq, RWKV_COLS)[:, -1])

    hd = (N_HEADS, HEAD_DIM)
    xs = x_sample.reshape(n_seq, D_MODEL)
    proj_s = _inproj_small(xs, g_mix, w_in_b)
    q_s, k_s, v_s = proj_s[:, :WIDTH], proj_s[:, WIDTH:2 * WIDTH], proj_s[:, 2 * WIDTH:ATTN_COLS]
    prw_s = proj_s[:, ATTN_COLS:]
    top = _sample_gate(q_s, kmean_s)[:, :MOBA_TOP_K, :N_HEADS]
    attn_s = _sample_attn(page_table, top, slopes, q_s, k_s, v_s, ckt, cvt)
    rw_s, wkv_s = _rwkv_sample(prw_s, _to_internal(state_shift[l]), state_wkv[l], rw_weights)
    y_sample = _out_ffn(xs, attn_s.astype(BF16), rw_s, *ffn_weights, tm=n_seq).reshape(n_seq, 1, D_MODEL)
    shift_s = _from_internal(prw_s)

    return (y_prompt, y_sample,
            kt.transpose(0, 3, 1, 2)[None], vt.transpose(0, 3, 1, 2)[None],
            wkv_p[None], shift_p[None],
            k_s.reshape(1, n_seq, 1, *hd), v_s.reshape(1, n_seq, 1, *hd),
            wkv_s[None], shift_s[None])
```

```python
import functools

import jax
import jax.numpy as jnp
from jax import lax
from jax.experimental import pallas as pl
from jax.experimental.pallas import tpu as pltpu

F32 = jnp.float32
BF16 = jnp.bfloat16

D_MODEL = 1024
HEAD_DIM = 64
N_HEADS = 8
WIDTH = N_HEADS * HEAD_DIM
MOBA_BLOCK = 256
MOBA_TOP_K = 3
DECAY_LORA = 64
AAA_LORA = 64
GATE_LORA = 128
ATTN_COLS = 3 * WIDTH
RWKV_COLS = 3 * WIDTH + DECAY_LORA + AAA_LORA + GATE_LORA
RMS_EPS = 1e-6
GN_EPS = 64e-5
NEG_INF = -1e30
PAGE_SIZE = 128
RWKV_CHUNK = 64
VMEM_LIMIT = 48 * 1024 * 1024

_O_R, _O_K, _O_V = 0, WIDTH, 2 * WIDTH
_O_XW = 3 * WIDTH
_O_XA = _O_XW + DECAY_LORA
_O_XG = _O_XA + AAA_LORA


def _to_internal(t):
    r, xw, kv, rest = (t[..., :WIDTH], t[..., WIDTH:WIDTH + DECAY_LORA],
                       t[..., WIDTH + DECAY_LORA:3 * WIDTH + DECAY_LORA], t[..., 3 * WIDTH + DECAY_LORA:])
    return jnp.concatenate([r, kv, xw, rest], axis=-1)


def _from_internal(t):
    r, kv, xw, rest = (t[..., :WIDTH], t[..., WIDTH:3 * WIDTH],
                       t[..., 3 * WIDTH:3 * WIDTH + DECAY_LORA], t[..., 3 * WIDTH + DECAY_LORA:])
    return jnp.concatenate([r, xw, kv, rest], axis=-1)


def _params(*sem):
    return pltpu.CompilerParams(dimension_semantics=sem, vmem_limit_bytes=VMEM_LIMIT)


def _rms(x, g):
    return x * lax.rsqrt(jnp.mean(x * x, axis=-1, keepdims=True) + RMS_EPS) * g


def _dot(a, b, **kw):
    return jnp.dot(a, b, preferred_element_type=F32, **kw)


def _dot_nt(a, b, **kw):
    return lax.dot_general(a, b, (((1,), (1,)), ((), ())), preferred_element_type=F32, **kw)


def _dot_tn(a, b, **kw):
    return lax.dot_general(a, b, (((0,), (0,)), ((), ())), preferred_element_type=F32, **kw)


MOBA_VROWS = HEAD_DIM + 16
INPROJ_BLOCKS = 2


def _inproj_kernel(x_ref, g_ref, w_ref, srow_ref, kt_ref, vt_ref, qt_ref, ka_ref, vtb_ref, prw_ref, km_ref):
    tm = x_ref.shape[0]
    hd = (N_HEADS, HEAD_DIM, tm)
    xn = _rms(x_ref[...], g_ref[...]).astype(BF16)
    proj = _dot(xn, w_ref[...])
    q = proj[:, 0:WIDTH] * (HEAD_DIM ** -0.5)
    k = proj[:, WIDTH:2 * WIDTH]
    v = proj[:, 2 * WIDTH:3 * WIDTH]
    prw_ref[...] = proj[:, ATTN_COLS:]
    nblk = tm // MOBA_BLOCK
    km_ref[0] = jnp.mean(k.reshape(nblk, MOBA_BLOCK, WIDTH), axis=1)
    k_t = k.T.reshape(hd)
    v_t = v.T.reshape(hd)
    kt_ref[0] = k_t
    vt_ref[0] = v_t
    ones_row = lax.broadcasted_iota(jnp.int32, (N_HEADS, MOBA_VROWS - HEAD_DIM, MOBA_BLOCK), 1) == 0
    ones_rows = jnp.where(ones_row, 1.0, 0.0).astype(BF16)
    v_tb = v_t.astype(BF16)
    for blk in range(nblk):
        vtb_ref[0, :, blk, 0:HEAD_DIM, :] = v_tb[:, :, blk * MOBA_BLOCK:(blk + 1) * MOBA_BLOCK]
        vtb_ref[0, :, blk, HEAD_DIM:, :] = ones_rows
    qt_ref[0, :, 0:HEAD_DIM, :] = q.T.reshape(hd).astype(BF16)
    qt_ref[0, :, HEAD_DIM:, :] = jnp.broadcast_to(srow_ref[...], hd).astype(BF16)
    lane = lax.broadcasted_iota(jnp.int32, (tm, 128 - HEAD_DIM), 1)
    pos = lax.broadcasted_iota(jnp.int32, (tm, 128 - HEAD_DIM), 0)
    pos_cols = jnp.where(lane == 0, pos % MOBA_BLOCK, 0).astype(F32).astype(BF16)
    for h in range(N_HEADS):
        ka_ref[0, h, :, 0:HEAD_DIM] = k[:, h * HEAD_DIM:(h + 1) * HEAD_DIM].astype(BF16)
        ka_ref[0, h, :, HEAD_DIM:] = pos_cols


def _inproj_prompt(x, g, w_bf16, slopes, batch, seq):
    tm = INPROJ_BLOCKS * MOBA_BLOCK
    m = x.shape[0]
    nb = seq // tm
    nblk = INPROJ_BLOCKS
    ncol = w_bf16.shape[1]
    row = lambda i: (i, 0)
    const = lambda i: (0, 0)
    tok = lambda i: (i // nb, 0, 0, i % nb)
    srow = jnp.zeros((N_HEADS, HEAD_DIM, 1), F32).at[:, 0, 0].set(slopes)
    return pl.pallas_call(
        _inproj_kernel,
        grid=(m // tm,),
        in_specs=[pl.BlockSpec((tm, D_MODEL), row),
                  pl.BlockSpec((1, D_MODEL), const),
                  pl.BlockSpec((D_MODEL, ncol), const),
                  pl.BlockSpec((N_HEADS, HEAD_DIM, 1), lambda i: (0, 0, 0))],
        out_specs=[pl.BlockSpec((1, N_HEADS, HEAD_DIM, tm), tok),
                   pl.BlockSpec((1, N_HEADS, HEAD_DIM, tm), tok),
                   pl.BlockSpec((1, N_HEADS, 128, tm), tok),
                   pl.BlockSpec((1, N_HEADS, tm, 128), lambda i: (i // nb, 0, i % nb, 0)),
                   pl.BlockSpec((1, N_HEADS, nblk, MOBA_VROWS, MOBA_BLOCK), lambda i: (i // nb, 0, i % nb, 0, 0)),
                   pl.BlockSpec((tm, RWKV_COLS), row),
                   pl.BlockSpec((1, nblk, WIDTH), lambda i: (i, 0, 0))],
        out_shape=[jax.ShapeDtypeStruct((batch, N_HEADS, HEAD_DIM, seq), F32),
                   jax.ShapeDtypeStruct((batch, N_HEADS, HEAD_DIM, seq), F32),
                   jax.ShapeDtypeStruct((batch, N_HEADS, 128, seq), BF16),
                   jax.ShapeDtypeStruct((batch, N_HEADS, seq, 128), BF16),
                   jax.ShapeDtypeStruct((batch, N_HEADS, seq // MOBA_BLOCK, MOBA_VROWS, MOBA_BLOCK), BF16),
                   jax.ShapeDtypeStruct((m, RWKV_COLS), F32),
                   jax.ShapeDtypeStruct((m // tm, nblk, WIDTH), F32)],
        compiler_params=_params("parallel"),
        name="inproj_prompt",
    )(x, g, w_bf16, srow)


def _inproj_small_kernel(x_ref, g_ref, w_ref, o_ref):
    xn = _rms(x_ref[...], g_ref[...]).astype(BF16)
    o_ref[...] = _dot(xn, w_ref[...])


def _inproj_small(x, g, w, tn=256):
    m = x.shape[0]
    ncol = w.shape[1]
    return pl.pallas_call(
        _inproj_small_kernel,
        grid=(ncol // tn,),
        in_specs=[pl.BlockSpec((m, D_MODEL), lambda j: (0, 0)),
                  pl.BlockSpec((1, D_MODEL), lambda j: (0, 0)),
                  pl.BlockSpec((D_MODEL, tn), lambda j: (0, j))],
        out_specs=pl.BlockSpec((m, tn), lambda j: (0, j)),
        out_shape=jax.ShapeDtypeStruct((m, ncol), F32),
        compiler_params=_params("parallel"),
        name="inproj_sample",
    )(x, g, w)


_STREAM_PAGES = 16
_PAGES_PER_BLOCK = MOBA_BLOCK // PAGE_SIZE


def _ffn_kernel(pt_ref, x_ref, attn_ref, rw_ref, woa_ref, wor_ref, gf_ref, wup_ref, wdn_ref, gfin_ref, *rest,
                n_stream):
    page_refs = rest[:n_stream]
    y_ref = rest[n_stream]
    km_ref = rest[n_stream + 1] if n_stream else None
    h_sc, hn_sc, acc_sc = rest[-3:]
    j = pl.program_id(1)

    @pl.when(j == 0)
    def _():
        h = x_ref[...] + _dot(attn_ref[...], woa_ref[...]) + _dot(rw_ref[...], wor_ref[...])
        h_sc[...] = h
        hn_sc[...] = _rms(h, gf_ref[...]).astype(BF16)
        acc_sc[...] = jnp.zeros_like(acc_sc)

    u = jnp.maximum(_dot(hn_sc[...], wup_ref[...]), 0.0)
    acc_sc[...] += _dot((u * u).astype(BF16), wdn_ref[...])

    if n_stream:
        ppb = _PAGES_PER_BLOCK
        bps = n_stream // ppb
        nb = km_ref.shape[1]
        g = (pl.program_id(0) * pl.num_programs(1) + j) % (nb // bps)
        for b in range(bps):
            tot = page_refs[b * ppb][0]
            for i in range(1, ppb):
                tot = tot + page_refs[b * ppb + i][0]
            tot_t = tot.reshape(WIDTH, PAGE_SIZE).T
            km_ref[0, pl.ds(g * bps + b, 1), :] = jnp.sum(tot_t, axis=0, keepdims=True) * (1.0 / MOBA_BLOCK)

    @pl.when(j == pl.num_programs(1) - 1)
    def _():
        y_ref[...] = _rms(h_sc[...] + acc_sc[...], gfin_ref[...])


def _out_ffn(x, attn, rw, woa, wor, gf, wup, wdn, gfin, tm, th=1024, stream=None):
    m = x.shape[0]
    hid = wup.shape[1]
    nj = hid // th
    row = lambda i, j, pt: (i, 0)
    const = lambda i, j, pt: (0, 0)
    in_specs = [pl.BlockSpec((tm, D_MODEL), row),
                pl.BlockSpec((tm, WIDTH), row),
                pl.BlockSpec((tm, WIDTH), row),
                pl.BlockSpec((WIDTH, D_MODEL), const),
                pl.BlockSpec((WIDTH, D_MODEL), const),
                pl.BlockSpec((1, D_MODEL), const),
                pl.BlockSpec((D_MODEL, th), lambda i, j, pt: (0, j)),
                pl.BlockSpec((th, D_MODEL), lambda i, j, pt: (j, 0)),
                pl.BlockSpec((1, D_MODEL), const)]
    out_specs = [pl.BlockSpec((tm, D_MODEL), row)]
    out_shape = [jax.ShapeDtypeStruct((m, D_MODEL), F32)]
    operands = [x, attn, rw, woa, wor, gf, wup, wdn, gfin]
    n_stream = 0
    pt_flat = jnp.zeros((1,), jnp.int32)
    if stream is not None:
        page_table, cache_kt = stream
        n_seq, n_pages = page_table.shape
        n_stream = _STREAM_PAGES
        spp = n_pages // n_stream
        assert (m // tm) * nj == n_seq * spp, "page streaming needs one grid step per 16 pages"
        nb = n_pages // _PAGES_PER_BLOCK
        pt_flat = page_table.reshape(-1)

        def page_spec(k):
            def imap(i, j, pt, k=k):
                t = i * nj + j
                return (pt[(t // spp) * n_pages + (t % spp) * n_stream + k], 0, 0, 0)
            return pl.BlockSpec((1, N_HEADS, HEAD_DIM, PAGE_SIZE), imap)

        in_specs += [page_spec(k) for k in range(n_stream)]
        operands += [cache_kt] * n_stream
        out_specs.append(pl.BlockSpec((1, nb, WIDTH), lambda i, j, pt: ((i * nj + j) // spp, 0, 0)))
        out_shape.append(jax.ShapeDtypeStruct((n_seq, nb, WIDTH), F32))
    grid_spec = pltpu.PrefetchScalarGridSpec(
        num_scalar_prefetch=1,
        grid=(m // tm, nj),
        in_specs=in_specs,
        out_specs=out_specs,
        scratch_shapes=[pltpu.VMEM((tm, D_MODEL), F32),
                        pltpu.VMEM((tm, D_MODEL), BF16),
                        pltpu.VMEM((tm, D_MODEL), F32)],
    )
    outs = pl.pallas_call(
        functools.partial(_ffn_kernel, n_stream=n_stream),
        grid_spec=grid_spec,
        out_shape=out_shape,
        compiler_params=_params("arbitrary", "arbitrary"),
        name="out_ffn",
    )(pt_flat, *operands)
    return outs if stream is not None else outs[0]


def _block_rank(gm, axis):
    nb = gm.shape[axis]
    idx = lax.broadcasted_iota(jnp.int32, gm.shape, axis)
    beats = []
    for m in range(nb):
        gmm = lax.slice_in_dim(gm, m, m + 1, axis=axis)
        beats.append(((gmm > gm) | ((gmm == gm) & (m < idx))).astype(jnp.int32))
    while len(beats) > 1:
        beats = [a + b for a, b in zip(beats[0::2], beats[1::2])] + ([beats[-1]] if len(beats) % 2 else [])
    return beats[0]


MOBA_HEADS = 8
MOBA_UNROLL = 4
BIG = 1e30


def _moba_kernel(slopes_ref, qt_ref, ka_ref, vt_ref, km_ref, o_ref, sel_sc, m_sc, acc_sc, *, nb):
    hg = pl.program_id(1)
    qi = pl.program_id(2)
    blk = MOBA_BLOCK
    n_top = min(MOBA_TOP_K, nb)
    keyi = lax.broadcasted_iota(jnp.int32, (blk, blk), 0)
    qryi = lax.broadcasted_iota(jnp.int32, (blk, blk), 1)
    causal = keyi <= qryi
    bidx = lax.broadcasted_iota(jnp.int32, (nb, blk), 0)
    q0 = pl.multiple_of(qi * blk, blk)

    hs = range(MOBA_HEADS)
    qts = [qt_ref[0, hh] for hh in hs]
    gates = [_dot(km_ref[0, hh], qts[hh]) for hh in hs]
    valid = bidx < qi
    ranks = [_block_rank(jnp.where(valid, g, NEG_INF), 0) for g in gates]
    for hh in hs:
        sel_sc[hh] = ((ranks[hh] < n_top) & valid).astype(F32)
    s0 = [jnp.where(causal, _dot(ka_ref[0, hh, pl.ds(q0, blk), :], qts[hh]), NEG_INF) for hh in hs]
    m0 = [jnp.max(t, axis=0, keepdims=True) for t in s0]
    p0 = [jnp.exp(s0[hh] - m0[hh]) for hh in hs]
    for hh in hs:
        m_sc[hh] = m0[hh]
    pv0 = [_dot(vt_ref[0, hh, qi], p0[hh].astype(BF16)) for hh in hs]
    for hh in hs:
        acc_sc[hh] = pv0[hh]

    def past_block(j, carry):
        k0 = pl.multiple_of(j * blk, blk)
        s = [_dot(ka_ref[0, hh, pl.ds(k0, blk), :], qt_ref[0, hh]) for hh in hs]
        ps, alphas = [], []
        for hh in hs:
            cj = -slopes_ref[hg * MOBA_HEADS + hh] * ((qi - j) * blk).astype(F32)
            picked = sel_sc[hh, pl.ds(j, 1), :] > 0.0
            m_old = m_sc[hh]
            m_new = jnp.maximum(m_old, jnp.where(picked, jnp.max(s[hh], axis=0, keepdims=True) + cj, NEG_INF))
            alpha = jnp.exp(m_old - m_new)
            p = jnp.exp(s[hh] - jnp.where(picked, m_new - cj, BIG))
            m_sc[hh] = m_new
            ps.append(p.astype(BF16))
            alphas.append(alpha)
        pv = [_dot(vt_ref[0, hh, j], ps[hh]) for hh in hs]
        for hh in hs:
            acc_sc[hh] = alphas[hh] * acc_sc[hh] + pv[hh]
        return carry

    def unrolled_body(jj, carry):
        for t in range(MOBA_UNROLL):
            past_block(MOBA_UNROLL * jj + t, carry)
        return carry

    lax.fori_loop(0, qi // MOBA_UNROLL, unrolled_body, 0)
    lax.fori_loop((qi // MOBA_UNROLL) * MOBA_UNROLL, qi, past_block, 0)

    outs = [acc_sc[hh, 0:HEAD_DIM, :] / acc_sc[hh, HEAD_DIM:HEAD_DIM + 1, :] for hh in hs]
    o_ref[...] = jnp.concatenate([t.T for t in outs], axis=1).astype(o_ref.dtype)


def _moba_prompt(slopes, qt, ka, vt, kmean, batch, seq):
    nb = seq // MOBA_BLOCK
    blk = MOBA_BLOCK
    km = kmean.reshape(batch, nb, N_HEADS, HEAD_DIM).transpose(0, 2, 1, 3).astype(BF16)
    km = jnp.concatenate([km, jnp.zeros((batch, N_HEADS, nb, 128 - HEAD_DIM), BF16)], axis=3)

    hgn = MOBA_HEADS
    grid_spec = pltpu.PrefetchScalarGridSpec(
        num_scalar_prefetch=1,
        grid=(batch, N_HEADS // hgn, nb),
        in_specs=[pl.BlockSpec((1, hgn, 128, blk), lambda b, g, qi, s: (b, g, 0, qi)),
                  pl.BlockSpec((1, hgn, seq, 128), lambda b, g, qi, s: (b, g, 0, 0)),
                  pl.BlockSpec((1, hgn, nb, MOBA_VROWS, blk), lambda b, g, qi, s: (b, g, 0, 0, 0)),
                  pl.BlockSpec((1, hgn, nb, 128), lambda b, g, qi, s: (b, g, 0, 0))],
        out_specs=pl.BlockSpec((blk, hgn * HEAD_DIM), lambda b, g, qi, s: (b * nb + qi, g)),
        scratch_shapes=[pltpu.VMEM((hgn, nb, blk), F32),
                        pltpu.VMEM((hgn, 1, blk), F32),
                        pltpu.VMEM((hgn, MOBA_VROWS, blk), F32)],
    )
    return pl.pallas_call(
        functools.partial(_moba_kernel, nb=nb),
        grid_spec=grid_spec,
        out_shape=jax.ShapeDtypeStruct((batch * seq, WIDTH), BF16),
        compiler_params=_params("parallel", "parallel", "arbitrary"),
        name="moba_prompt",
    )(slopes, qt, ka, vt, km)


def _b(t):
    return t.astype(BF16)


def _split3(x):
    x1 = _b(x)
    r1 = x - x1.astype(F32)
    x2 = _b(r1)
    return x1, x2, _b(r1 - x2.astype(F32))


def _rwkv_pointwise(p, pprev, mu, w0, decay_up, a0, iclr_up, gate_up, k_k, k_a):
    xs = p + mu * (pprev - p)
    r = xs[:, _O_R:_O_R + WIDTH]
    k = xs[:, _O_K:_O_K + WIDTH]
    v = xs[:, _O_V:_O_V + WIDTH]
    xw = xs[:, _O_XW:_O_XW + DECAY_LORA]
    xa = xs[:, _O_XA:_O_XA + AAA_LORA]
    xg = xs[:, _O_XG:_O_XG + GATE_LORA]
    w = w0 + _dot(_b(jnp.tanh(xw)), _b(decay_up))
    w = -jax.nn.softplus(-w) - 0.5
    logdecay = -jnp.exp(w)
    a = jax.nn.sigmoid(a0 + _dot(_b(xa), _b(iclr_up)))
    g = _dot(_b(jax.nn.sigmoid(xg)), _b(gate_up))
    kk = k * k_k
    k2 = k * (1.0 + (a - 1.0) * k_a)
    return r, k2, v, kk, a, g, logdecay


def _head_sum(x):
    row = lax.broadcasted_iota(jnp.int32, (128, 128), 0)
    col = lax.broadcasted_iota(jnp.int32, (128, 128), 1)
    seg = ((row // HEAD_DIM) == (col // HEAD_DIM)).astype(BF16)
    xb = _b(x)
    cols = [slice(g * 128, (g + 1) * 128) for g in range(x.shape[1] // 128)]
    return jnp.concatenate([_dot(xb[:, c], seg) for c in cols], axis=1)


def _head_norm(kk_h):
    return kk_h * lax.rsqrt(jnp.maximum(jnp.sum(kk_h * kk_h, axis=-1, keepdims=True), 1e-24))


def _group_norm_out(y_h, r_h, k_h, v_h, g_h, rk_h, lnw_h, lnb_h):
    mean = jnp.mean(y_h, axis=-1, keepdims=True)
    var = jnp.mean(jnp.square(y_h - mean), axis=-1, keepdims=True)
    yn = (y_h - mean) * lax.rsqrt(var + GN_EPS) * lnw_h + lnb_h
    yn = yn + jnp.sum(r_h * k_h * rk_h, axis=-1, keepdims=True) * v_h
    return yn * g_h


def _unit_lower_inverse(mats):
    n = mats[0].shape[0]
    row = lax.broadcasted_iota(jnp.int32, (n, n), 0)
    col = lax.broadcasted_iota(jnp.int32, (n, n), 1)
    eye = (row == col).astype(F32)
    size = 16
    same = (row // size) == (col // size)
    pws = [jnp.where(same, a, 0.0) for a in mats]
    xs = [eye - pw for pw in pws]
    for _ in range(3):
        pwb = [_b(pw) for pw in pws]
        pws = [_dot(t, t) for t in pwb]
        xs = [x + _dot(_b(x), _b(pw)) for x, pw in zip(xs, pws)]
        yield
    while size < n:
        size2 = size * 2
        same2 = (row // size2) == (col // size2)
        keep = same2 & jnp.logical_not(same)
        xb = [_b(x) for x in xs]
        ox = [_b(_dot(_b(jnp.where(keep, a, 0.0)), t)) for a, t in zip(mats, xb)]
        xs = [x - _dot(t, o) for x, t, o in zip(xs, xb, ox)]
        same = same2
        size = size2
        yield
    return xs


def _cumsum_rows(x, seg):
    n = x.shape[0]
    row = lax.broadcasted_iota(jnp.int32, (n, n), 0)
    col = lax.broadcasted_iota(jnp.int32, (n, n), 1)
    tri = ((row >= col) & ((row // seg) == (col // seg))).astype(BF16)
    x1, x2, x3 = _split3(x)
    return _dot(tri, x1) + _dot(tri, x2) + _dot(tri, x3)


RWKV_STEP_CHUNKS = 4
_RWKV_BLOCK = RWKV_CHUNK * RWKV_STEP_CHUNKS
_KAPH, _RHAT, _KHAT, _KBAR, _BHAT, _BBAR, _VB = range(7)
_RHAT32, _BONUS, _GATE = range(3)


def _interleave(*gens):
    live = list(gens)
    while live:
        for gen in list(live):
            try:
                next(gen)
            except StopIteration:
                live.remove(gen)


def _rwkv_pointwise_stage(p, last_sc, w, ob_sc, of_sc, we_sc, keep):
    mu, w0, dup, a0, iup, gup, k_k, k_a, r_k = w
    L = RWKV_CHUNK
    nck = RWKV_STEP_CHUNKS
    ts = _RWKV_BLOCK
    rowi = lax.broadcasted_iota(jnp.int32, p.shape, 0)
    carry = last_sc[0:1, :] if keep is None else last_sc[0:1, :] * keep
    pprev = jnp.where(rowi == 0, carry, pltpu.roll(p, 1, 0))
    last_sc[0:1, :] = p[ts - 1:ts, :]
    yield
    r, k2, v, kk, a, g, logdecay = _rwkv_pointwise(p, pprev, mu, w0, dup, a0, iup, gup, k_k, k_a)
    yield
    cum = _cumsum_rows(logdecay, L)
    ends = [cum[c * L + L - 1:c * L + L, :] for c in range(nck)]
    cum_end = jnp.concatenate([jnp.broadcast_to(e, (L, WIDTH)) for e in ends], axis=0)
    w_inc = jnp.exp(cum)
    w_exc = jnp.exp(cum - logdecay)
    w_inv = jnp.exp(-cum)
    w_tail = jnp.exp(cum_end - cum)
    for c in range(nck):
        we_sc[c:c + 1, :] = jnp.exp(ends[c])
    yield
    kap_all = kk * lax.rsqrt(jnp.maximum(_head_sum(kk * kk), 1e-24))
    bb_all = kap_all * a
    r_hat_all = r * w_inc
    ob_sc[_KAPH] = _b(kap_all * w_exc)
    ob_sc[_RHAT] = _b(r_hat_all)
    ob_sc[_KHAT] = _b(k2 * w_inv)
    ob_sc[_KBAR] = _b(k2 * w_tail)
    yield
    ob_sc[_BHAT] = _b(bb_all * w_inv)
    ob_sc[_BBAR] = _b(bb_all * w_tail)
    ob_sc[_VB] = _b(v)
    of_sc[_RHAT32] = r_hat_all
    of_sc[_BONUS] = _head_sum(r * k2 * r_k) * v
    of_sc[_GATE] = g


def _rwkv_matmul_stage(ob_sc, of_sc, we_sc, s_sc, y_sc, lnw, lnb, o_ref, rows_out, keep):
    L = RWKV_CHUNK
    nck = RWKV_STEP_CHUNKS
    trow = lax.broadcasted_iota(jnp.int32, (L, L), 0)
    tcol = lax.broadcasted_iota(jnp.int32, (L, L), 1)
    lower_incl = trow >= tcol
    lower_strict = trow > tcol
    heads = range(N_HEADS)
    items = [(slice(c * L, (c + 1) * L), slice(h * HEAD_DIM, (h + 1) * HEAD_DIM))
             for c in range(nck) for h in heads]
    n = range(len(items))
    ld = lambda slot, it: ob_sc[slot, it[0], it[1]]
    lhs = [jnp.concatenate([ld(_KAPH, it), ld(_RHAT, it)], axis=0) for it in items]
    ak = [_dot_nt(lhs[i], ld(_KHAT, items[i])) for i in n]
    ab = [_dot_nt(lhs[i], ld(_BHAT, items[i])) for i in n]
    yield
    a_kr = [_b(jnp.concatenate([jnp.where(lower_strict, t[:L], 0.0), jnp.where(lower_incl, t[L:], 0.0)], axis=0))
            for t in ak]
    a_rb = [_b(jnp.where(lower_incl, t[L:], 0.0)) for t in ab]
    t_inv = yield from _unit_lower_inverse([jnp.where(lower_strict, t[:L], 0.0) for t in ab])
    t_inv = [_b(t) for t in t_inv]
    av = [_dot(a_kr[i], ld(_VB, items[i])) for i in n]
    pm = [_b(_dot(t_inv[i], ld(_KAPH, items[i]))) for i in n]
    yield
    qm = [_b(_dot(t_inv[i], _b(av[i][:L]))) for i in n]
    r_eff = [_b(of_sc[_RHAT32, items[i][0], items[i][1]] - _dot(a_rb[i], pm[i])) for i in n]
    ptb = [_b(_dot_tn(pm[i], ld(_BBAR, items[i]))) for i in n]
    yield
    y0 = [av[i][L:] - _dot(a_rb[i], qm[i]) for i in n]
    cm = [_dot_tn(ld(_VB, items[i]), ld(_KBAR, items[i])) - _dot_tn(qm[i], ld(_BBAR, items[i])) for i in n]
    yield
    state = [s_sc[h] if keep is None else s_sc[h] * keep for h in heads]
    for c in range(nck):
        w_end = we_sc[c:c + 1, :]
        sb = [_b(t) for t in state]
        ys = [_dot_nt(r_eff[c * N_HEADS + h], sb[h]) + y0[c * N_HEADS + h] for h in heads]
        state = [state[h] * w_end[:, items[h][1]] - _dot(sb[h], ptb[c * N_HEADS + h]) + cm[c * N_HEADS + h]
                 for h in heads]
        for h in heads:
            y_sc[items[c * N_HEADS + h]] = ys[h]
    for h in heads:
        s_sc[h] = state[h]
    yield
    y = y_sc[...]
    dev = y - _head_sum(y) * (1.0 / HEAD_DIM)
    var = _head_sum(dev * dev) * (1.0 / HEAD_DIM)
    yn = dev * lax.rsqrt(var + GN_EPS) * lnw + lnb
    o_ref[rows_out, :] = ((yn + of_sc[_BONUS]) * of_sc[_GATE]).astype(o_ref.dtype)


def _rwkv_chunk_kernel(p_ref, mu_ref, w0_ref, dup_ref, a0_ref, iup_ref, gup_ref, kk_ref, ka_ref,
                       rk_ref, lnw_ref, lnb_ref, o_ref, s_out_ref,
                       s_sc, last_sc, y_sc, xb_sc, xf_sc, xw_sc, yb_sc, yf_sc, yw_sc, *, seq_blocks):
    step = pl.program_id(0)
    ts = _RWKV_BLOCK

    @pl.when(step == 0)
    def _():
        s_sc[...] = jnp.zeros_like(s_sc)
        last_sc[...] = jnp.zeros_like(last_sc)
        yb_sc[...] = jnp.zeros_like(yb_sc)
        yf_sc[...] = jnp.zeros_like(yf_sc)
        yw_sc[...] = jnp.zeros_like(yw_sc)

    keep = jnp.where((2 * step) % seq_blocks == 0, 0.0, 1.0).astype(F32)
    w = (mu_ref[...], w0_ref[...], dup_ref[...], a0_ref[...], iup_ref[...], gup_ref[...],
         kk_ref[...], ka_ref[...], rk_ref[...])
    lnw, lnb = lnw_ref[...], lnb_ref[...]
    _interleave(_rwkv_matmul_stage(yb_sc, yf_sc, yw_sc, s_sc, y_sc, lnw, lnb, o_ref, slice(0, ts), None),
                _rwkv_pointwise_stage(p_ref[0:ts, :], last_sc, w, xb_sc, xf_sc, xw_sc, keep))
    s_out_ref[0] = s_sc[...]
    _interleave(_rwkv_matmul_stage(xb_sc, xf_sc, xw_sc, s_sc, y_sc, lnw, lnb, o_ref, slice(ts, 2 * ts), keep),
                _rwkv_pointwise_stage(p_ref[ts:2 * ts, :], last_sc, w, yb_sc, yf_sc, yw_sc, None))


def _rwkv_prompt(p_rw, weights, batch, seq):
    ts = _RWKV_BLOCK
    m = batch * seq
    seq_blocks = seq // ts
    assert seq_blocks % 2 == 0
    ns = m // (2 * ts)
    const = lambda c: (0, 0)
    w_specs = [pl.BlockSpec(w.shape, const) for w in weights]
    operand_scratch = [pltpu.VMEM((7, ts, WIDTH), BF16), pltpu.VMEM((3, ts, WIDTH), F32), pltpu.VMEM((8, WIDTH), F32)]
    out, state = pl.pallas_call(
        functools.partial(_rwkv_chunk_kernel, seq_blocks=seq_blocks),
        grid=(ns + 1,),
        in_specs=[pl.BlockSpec((2 * ts, RWKV_COLS), lambda c: (jnp.minimum(c, ns - 1), 0))] + w_specs,
        out_specs=[pl.BlockSpec((2 * ts, WIDTH), lambda c: (c, 0)),
                   pl.BlockSpec((1, N_HEADS, HEAD_DIM, HEAD_DIM),
                                lambda c: (jnp.maximum(2 * c - 1, 0) // seq_blocks, 0, 0, 0))],
        out_shape=[jax.ShapeDtypeStruct((m + 2 * ts, WIDTH), BF16),
                   jax.ShapeDtypeStruct((batch, N_HEADS, HEAD_DIM, HEAD_DIM), F32)],
        scratch_shapes=[pltpu.VMEM((N_HEADS, HEAD_DIM, HEAD_DIM), F32),
                        pltpu.VMEM((8, RWKV_COLS), F32),
                        pltpu.VMEM((ts, WIDTH), F32)] + operand_scratch + operand_scratch,
        compiler_params=_params("arbitrary"),
        name="rwkv_prompt",
    )(p_rw, *weights)
    return out[ts:ts + m], state


def _rwkv_step_kernel(p_ref, sh_ref, s_ref, mu_ref, w0_ref, dup_ref, a0_ref, iup_ref, gup_ref, kk_ref,
                      ka_ref, rk_ref, lnw_ref, lnb_ref, o_ref, s_out_ref):
    p = jnp.broadcast_to(p_ref[0], (8, RWKV_COLS))
    pprev = jnp.broadcast_to(sh_ref[0], (8, RWKV_COLS))
    r, k2, v, kk, a, g, logdecay = (t[0:1] for t in _rwkv_pointwise(
        p, pprev, mu_ref[...], w0_ref[...], dup_ref[...], a0_ref[...], iup_ref[...], gup_ref[...],
        kk_ref[...], ka_ref[...]))
    decay = jnp.exp(logdecay)
    n = HEAD_DIM
    eye = lax.broadcasted_iota(jnp.int32, (n, n), 0) == lax.broadcasted_iota(jnp.int32, (n, n), 1)

    def to_col(row_vec):
        return jnp.sum(jnp.where(eye, row_vec, 0.0), axis=-1, keepdims=True)

    def to_row(col_vec):
        return jnp.sum(jnp.where(eye, col_vec, 0.0), axis=0, keepdims=True)

    for h in range(N_HEADS):
        ln = slice(h * n, (h + 1) * n)
        s0 = s_ref[0, h]
        kap = _head_norm(kk[:, ln])
        sa = jnp.sum(s0 * (-kap), axis=-1, keepdims=True)
        s_new = s0 * decay[:, ln] + sa * (kap * a[:, ln]) + to_col(v[:, ln]) * k2[:, ln]
        s_out_ref[0, h] = s_new
        y = to_row(jnp.sum(s_new * r[:, ln], axis=-1, keepdims=True))
        o_ref[0, :, ln] = _group_norm_out(y, r[:, ln], k2[:, ln], v[:, ln], g[:, ln], rk_ref[:, ln],
                                          lnw_ref[:, ln], lnb_ref[:, ln]).astype(o_ref.dtype)


def _rwkv_sample(p_rw, shift, state, weights):
    n = p_rw.shape[0]
    const = lambda s: (0, 0)
    vec = pl.BlockSpec((1, 1, RWKV_COLS), lambda s: (s, 0, 0))
    st = pl.BlockSpec((1, N_HEADS, HEAD_DIM, HEAD_DIM), lambda s: (s, 0, 0, 0))
    rw, s_new = pl.pallas_call(
        _rwkv_step_kernel,
        grid=(n,),
        in_specs=[vec, vec, st] + [pl.BlockSpec(w.shape, const) for w in weights],
        out_specs=[pl.BlockSpec((1, 1, WIDTH), lambda s: (s, 0, 0)), st],
        out_shape=[jax.ShapeDtypeStruct((n, 1, WIDTH), BF16),
                   jax.ShapeDtypeStruct(state.shape, F32)],
        compiler_params=_params("parallel"),
        name="rwkv_sample",
    )(p_rw.reshape(n, 1, RWKV_COLS), shift.reshape(n, 1, RWKV_COLS), state, *weights)
    return rw.reshape(n, WIDTH), s_new


def _sample_gate_kernel(q_ref, km_ref, idx_ref):
    rnd = lambda t: t.astype(BF16).astype(F32)
    prod = rnd(km_ref[0]) * rnd(q_ref[0])
    lane = lax.broadcasted_iota(jnp.int32, (WIDTH, 128), 0)
    hcol = lax.broadcasted_iota(jnp.int32, (WIDTH, 128), 1)
    head_sum = ((lane // HEAD_DIM) == hcol).astype(BF16)
    pieces = _split3(prod)
    gate = _dot(pieces[0], head_sum) + _dot(pieces[1], head_sum) + _dot(pieces[2], head_sum)
    rank = _block_rank(gate, 0)
    bidx = lax.broadcasted_iota(jnp.int32, gate.shape, 0)
    rows = [jnp.sum(jnp.where(rank == i, bidx, 0), axis=0, keepdims=True) for i in range(MOBA_TOP_K)]
    rows += [jnp.zeros((1, 128), jnp.int32)] * (8 - MOBA_TOP_K)
    idx_ref[0] = jnp.concatenate(rows, axis=0)


def _sample_gate(q, kmean):
    n, nb, _ = kmean.shape
    return pl.pallas_call(
        _sample_gate_kernel,
        grid=(n,),
        in_specs=[pl.BlockSpec((1, 1, WIDTH), lambda s: (s, 0, 0)),
                  pl.BlockSpec((1, nb, WIDTH), lambda s: (s, 0, 0))],
        out_specs=pl.BlockSpec((1, 8, 128), lambda s: (s, 0, 0)),
        out_shape=jax.ShapeDtypeStruct((n, 8, 128), jnp.int32),
        compiler_params=_params("parallel"),
        name="sample_gate",
    )(q.reshape(n, 1, WIDTH), kmean)


def _sample_attn_kernel(pt_ref, top_ref, q_ref, kn_ref, vn_ref, topv_ref, slope_ref, ck_hbm, cv_hbm, o_ref,
                        kbuf, vbuf, sems, *, past_len, n_pages):
    ppb = _PAGES_PER_BLOCK
    npg = MOBA_TOP_K * ppb
    seq = pl.program_id(0)
    nseq = pl.num_programs(0)
    slot = seq % 2

    def slab_copies(sq, sl, h, i):
        blk_id = top_ref[(sq * N_HEADS + h) * MOBA_TOP_K + i // ppb]
        page = pt_ref[sq * n_pages + blk_id * ppb + i % ppb]
        return (pltpu.make_async_copy(ck_hbm.at[page, h], kbuf.at[sl, h, i], sems.at[sl]),
                pltpu.make_async_copy(cv_hbm.at[page, h], vbuf.at[sl, h, i], sems.at[sl]))

    def start_all(sq, sl):
        def per_head(h, carry):
            for i in range(npg):
                for cp in slab_copies(sq, sl, h, i):
                    cp.start(priority=i % 2)
            return carry
        lax.fori_loop(0, N_HEADS, per_head, 0)

    @pl.when(seq == 0)
    def _():
        start_all(0, 0)

    @pl.when(seq + 1 < nseq)
    def _():
        start_all(seq + 1, 1 - slot)

    def wait_head(h, carry):
        for i in range(npg):
            for cp in slab_copies(seq, slot, h, i):
                cp.wait()
        return carry
    lax.fori_loop(0, N_HEADS, wait_head, 0)

    heads = range(N_HEADS)
    lns = [slice(h * HEAD_DIM, (h + 1) * HEAD_DIM) for h in heads]
    q_row = q_ref[0] * (HEAD_DIM ** -0.5)
    q8 = [_b(jnp.broadcast_to(q_row[:, ln], (8, HEAD_DIM))) for ln in lns]
    lane = lax.broadcasted_iota(jnp.int32, (1, PAGE_SIZE), 1)
    raw = [[_dot(q8[h], _b(kbuf[slot, h, i]))[0:1] for i in range(npg)] for h in heads]
    probs, dens, accs = [], [], []
    for h in heads:
        slope = slope_ref[h]
        scores = []
        for i in range(npg):
            blk_id = topv_ref[0, i // ppb][h:h + 1, :]
            dist = (past_len - blk_id * MOBA_BLOCK - (i % ppb) * PAGE_SIZE - lane).astype(F32)
            scores.append(raw[h][i] - slope * dist)
        s_self = jnp.sum(q_row[:, lns[h]] * kn_ref[0][:, lns[h]], axis=1, keepdims=True)
        m = s_self
        for sc in scores:
            m = jnp.maximum(m, jnp.max(sc, axis=1, keepdims=True))
        p_self = jnp.exp(s_self - m)
        ps = [jnp.exp(sc - m) for sc in scores]
        den = p_self
        for p in ps:
            den = den + jnp.sum(p, axis=1, keepdims=True)
        probs.append([_b(jnp.broadcast_to(p, (8, PAGE_SIZE))) for p in ps])
        dens.append(den)
        accs.append(p_self * vn_ref[0][:, lns[h]])
    pv = [[_dot_nt(probs[h][i], _b(vbuf[slot, h, i]))[0:1] for i in range(npg)] for h in heads]
    outs = []
    for h in heads:
        acc = accs[h]
        for t in pv[h]:
            acc = acc + t
        outs.append(acc / dens[h])
    o_ref[0] = jnp.concatenate(outs, axis=1)


def _sample_attn(page_table, top, slopes, q, k_new, v_new, cache_kt, cache_vt):
    n, n_pages = page_table.shape
    npg = MOBA_TOP_K * _PAGES_PER_BLOCK
    top_flat = top.transpose(0, 2, 1).reshape(-1)
    vec = pl.BlockSpec((1, 1, WIDTH), lambda s, pt, tp: (s, 0, 0))
    grid_spec = pltpu.PrefetchScalarGridSpec(
        num_scalar_prefetch=2,
        grid=(n,),
        in_specs=[vec, vec, vec,
                  pl.BlockSpec((1, MOBA_TOP_K, N_HEADS, 1), lambda s, pt, tp: (s, 0, 0, 0)),
                  pl.BlockSpec((N_HEADS, 1, 1), lambda s, pt, tp: (0, 0, 0)),
                  pl.BlockSpec(memory_space=pl.ANY),
                  pl.BlockSpec(memory_space=pl.ANY)],
        out_specs=vec,
        scratch_shapes=[pltpu.VMEM((2, N_HEADS, npg, HEAD_DIM, PAGE_SIZE), F32),
                        pltpu.VMEM((2, N_HEADS, npg, HEAD_DIM, PAGE_SIZE), F32),
                        pltpu.SemaphoreType.DMA((2,))],
    )
    row = lambda t: t.reshape(n, 1, WIDTH)
    return pl.pallas_call(
        functools.partial(_sample_attn_kernel, past_len=n_pages * PAGE_SIZE, n_pages=n_pages),
        grid_spec=grid_spec,
        out_shape=jax.ShapeDtypeStruct((n, 1, WIDTH), F32),
        compiler_params=_params("arbitrary"),
        name="sample_attn",
    )(page_table.reshape(-1), top_flat, row(q), row(k_new), row(v_new), top[..., None],
      slopes.reshape(N_HEADS, 1, 1), cache_kt, cache_vt).reshape(n, WIDTH)


def kernel(x_prompt, x_sample, cache_k, cache_v, page_table, state_wkv, state_shift,
           norm_mix_g, w_in, mu_shift, decay_w0, decay_up, iclr_a0, iclr_up, gate_up,
           k_k, k_a, r_k, ln_x_w, ln_x_b, w_out, norm_ffn_g, w_ffn_up, w_ffn_down, norm_final_g):
    depth = w_in.shape[0]
    assert depth == 1
    batch, seq, _ = x_prompt.shape
    n_seq, n_pages = page_table.shape
    slopes = jnp.exp2(-8.0 * jnp.arange(1, N_HEADS + 1, dtype=F32) / N_HEADS)

    l = 0
    row = lambda t: t.reshape(1, -1)
    w_in_16 = w_in[l].astype(BF16)
    w_in_b = jnp.concatenate([w_in_16[:, :ATTN_COLS], _to_internal(w_in_16[:, ATTN_COLS:])], axis=1)
    g_mix = row(norm_mix_g[l])
    rw_weights = (row(_to_internal(mu_shift[l])), row(decay_w0[l]), decay_up[l], row(iclr_a0[l]), iclr_up[l],
                  gate_up[l], row(k_k[l]), row(k_a[l]), row(r_k[l]), row(ln_x_w[l]), row(ln_x_b[l]))
    wo = w_out[l].astype(BF16)
    ffn_weights = (wo[:WIDTH], wo[WIDTH:], row(norm_ffn_g[l]), w_ffn_up[l].astype(BF16),
                   w_ffn_down[l].astype(BF16), row(norm_final_g))

    xp = x_prompt.reshape(batch * seq, D_MODEL)
    kt, vt, qt, ka, vtb, prw, kmean = _inproj_prompt(xp, g_mix, w_in_b, slopes, batch, seq)
    nb = seq // MOBA_BLOCK
    attn_p = _moba_prompt(slopes, qt, ka, vtb, kmean.reshape(batch, nb, WIDTH), batch, seq)
    rw_p, wkv_p = _rwkv_prompt(prw, rw_weights, batch, seq)
    ckt = cache_k.transpose(0, 1, 3, 4, 2).reshape(-1, N_HEADS, HEAD_DIM, PAGE_SIZE)
    cvt = cache_v.transpose(0, 1, 3, 4, 2).reshape(-1, N_HEADS, HEAD_DIM, PAGE_SIZE)
    y_prompt, kmean_s = _out_ffn(xp, attn_p, rw_p, *ffn_weights, tm=1024, th=512, stream=(page_table, ckt))
    y_prompt = y_prompt.reshape(batch, seq, D_MODEL)
    shift_p = _from_internal(prw.reshape(batch, seq, RWKV_COLS)[:, -1])

    hd = (N_HEADS, HEAD_DIM)
    xs = x_sample.reshape(n_seq, D_MODEL)
    proj_s = _inproj_small(xs, g_mix, w_in_b)
    q_s, k_s, v_s = proj_s[:, :WIDTH], proj_s[:, WIDTH:2 * WIDTH], proj_s[:, 2 * WIDTH:ATTN_COLS]
    prw_s = proj_s[:, ATTN_COLS:]
    top = _sample_gate(q_s, kmean_s)[:, :MOBA_TOP_K, :N_HEADS]
    attn_s = _sample_attn(page_table, top, slopes, q_s, k_s, v_s, ckt, cvt)
    rw_s, wkv_s = _rwkv_sample(prw_s, _to_internal(state_shift[l]), state_wkv[l], rw_weights)
    y_sample = _out_ffn(xs, attn_s.astype(BF16), rw_s, *ffn_weights, tm=n_seq).reshape(n_seq, 1, D_MODEL)
    shift_s = _from_internal(prw_s)

    return (y_prompt, y_sample,
            kt.transpose(0, 3, 1, 2)[None], vt.transpose(0, 3, 1, 2)[None],
            wkv_p[None], shift_p[None],
            k_s.reshape(1, n_seq, 1, *hd), v_s.reshape(1, n_seq, 1, *hd),
            wkv_s[None], shift_s[None])
```
